```python
import jax, jax.numpy as jnp
from jax import lax
import numpy as np

D_MODEL = 1024
BATCH = 2
SEQ = 16384
DEPTH = 1
DEC_BATCH = 16
DEC_SEQ = 64
PAST_LEN = 2048

CHUNK = 64
HEAD_DIM = 64
N_HEADS_FOX = 8
N_HEADS_BAND = 8
FOX_WIDTH = N_HEADS_FOX * HEAD_DIM
BAND_WIDTH = N_HEADS_BAND * HEAD_DIM
MIX_WIDTH = FOX_WIDTH + BAND_WIDTH
N_LEFT_CHUNKS = 8
BAND_CHUNKS = N_LEFT_CHUNKS + 1
BAND_REACH = N_LEFT_CHUNKS * CHUNK
BAND_KEYS = BAND_CHUNKS * CHUNK
REL_CLIP = 256
N_REL = 2 * REL_CLIP + 1
Q_BLOCK = 128
COL_Q_FOX = 0
COL_K_FOX = COL_Q_FOX + FOX_WIDTH
COL_V_FOX = COL_K_FOX + FOX_WIDTH
COL_F = COL_V_FOX + FOX_WIDTH
COL_Q_BAND = COL_F + N_HEADS_FOX
COL_K_BAND = COL_Q_BAND + BAND_WIDTH
COL_V_BAND = COL_K_BAND + BAND_WIDTH
IN_WIDTH = COL_V_BAND + BAND_WIDTH
N_EXPERTS = 64
EXPERT_FF = 256
SHARED_FF = 256
TOP_K = 8
N_EXPERT_GROUPS = 8
EXPERTS_PER_GROUP = N_EXPERTS // N_EXPERT_GROUPS
TOPK_GROUPS = 4
ROUTED_SCALE = 2.5
MOE_BLOCK = 128
EPS = 1e-6
NEG_INF = -1e30
ATTN_SCALE = HEAD_DIM ** -0.5

kernel_name = 'hymba_fox_chunkband_moe_stream_step'


def rms_norm(x, g):
    xf = x.astype(jnp.float32)
    y = xf * lax.rsqrt(jnp.mean(xf * xf, axis=-1, keepdims=True) + EPS)
    return (y * g.astype(jnp.float32)).astype(x.dtype)


def modulation(c, w_ada, b_ada):
    mod = jnp.dot(jax.nn.silu(c), w_ada) + b_ada
    return jnp.split(mod[:, None, :], 6, axis=-1)


def project_mixers(h, w_in, b_forget, g_q_fox, g_k_fox, g_q_band, g_k_band):
    b, t, _ = h.shape
    p = jnp.dot(h, w_in)

    def heads(lo, n):
        return p[..., lo:lo + n * HEAD_DIM].reshape(b, t, n, HEAD_DIM)

    qf = rms_norm(heads(COL_Q_FOX, N_HEADS_FOX), g_q_fox)
    kf = rms_norm(heads(COL_K_FOX, N_HEADS_FOX), g_k_fox)
    vf = heads(COL_V_FOX, N_HEADS_FOX)
    logf = jax.nn.log_sigmoid((p[..., COL_F:COL_F + N_HEADS_FOX] + b_forget).astype(jnp.float32))
    qb = rms_norm(heads(COL_Q_BAND, N_HEADS_BAND), g_q_band)
    kb = rms_norm(heads(COL_K_BAND, N_HEADS_BAND), g_k_band)
    vb = heads(COL_V_BAND, N_HEADS_BAND)
    return qf, kf, vf, logf, qb, kb, vb


def fox_prompt(q, k, v, logf):
    b, s_len, h, dh = q.shape
    nb = s_len // Q_BLOCK
    cum = jnp.cumsum(logf, axis=1)
    cum_k = jnp.swapaxes(cum, 1, 2)[:, :, None, :]
    qb = jnp.moveaxis(q.reshape(b, nb, Q_BLOCK, h, dh), 1, 0)
    cb = jnp.moveaxis(cum.reshape(b, nb, Q_BLOCK, h), 1, 0)
    kpos = jnp.arange(s_len)

    def block(args):
        i, qi, ci = args
        sc = jnp.einsum('bqhd,bkhd->bhqk', qi, k, preferred_element_type=jnp.float32) * ATTN_SCALE
        sc = sc + jnp.swapaxes(ci, 1, 2)[..., None] - cum_k
        qpos = i * Q_BLOCK + jnp.arange(Q_BLOCK)
        sc = jnp.where(kpos[None, :] <= qpos[:, None], sc, NEG_INF)
        pr = jax.nn.softmax(sc, axis=-1)
        return jnp.einsum('bhqk,bkhd->bqhd', pr.astype(v.dtype), v)

    o = lax.map(block, (jnp.arange(nb), qb, cb))
    return jnp.moveaxis(o, 0, 1).reshape(b, s_len, h, dh)


def fox_sample(q, k_new, v_new, logf_new, k_cache, v_cache, logf_cache):
    past = k_cache.shape[1]
    t = q.shape[1]
    k = jnp.concatenate([k_cache, k_new], axis=1)
    v = jnp.concatenate([v_cache, v_new], axis=1)
    cum = jnp.cumsum(jnp.concatenate([logf_cache.astype(jnp.float32), logf_new], axis=1), axis=1)
    sc = jnp.einsum('bqhd,bkhd->bhqk', q, k, preferred_element_type=jnp.float32) * ATTN_SCALE
    sc = sc + jnp.swapaxes(cum[:, past:], 1, 2)[..., None] - jnp.swapaxes(cum, 1, 2)[:, :, None, :]
    kpos = jnp.arange(past + t)
    qpos = past + jnp.arange(t)
    sc = jnp.where(kpos[None, :] <= qpos[:, None], sc, NEG_INF)
    pr = jax.nn.softmax(sc, axis=-1)
    return jnp.einsum('bhqk,bkhd->bqhd', pr.astype(v.dtype), v)


def rel_bias_block(rel_bias, dist):
    return rel_bias[:, jnp.clip(dist, -REL_CLIP, REL_CLIP) + REL_CLIP]


def band_prompt(q, k, v, rel_bias):
    b, s_len, h, dh = q.shape
    nc = s_len // CHUNK
    pad = ((0, 0), (BAND_REACH, 0), (0, 0), (0, 0))
    kp = jnp.pad(k, pad)
    vp = jnp.pad(v, pad)
    qc = jnp.moveaxis(q.reshape(b, nc, CHUNK, h, dh), 1, 0)
    dist = jnp.arange(CHUNK)[:, None] + BAND_REACH - jnp.arange(BAND_KEYS)[None, :]
    bias = rel_bias_block(rel_bias, dist).astype(jnp.float32)

    def chunk(args):
        c, qi = args
        start = c * CHUNK
        ki = lax.dynamic_slice_in_dim(kp, start, BAND_KEYS, axis=1)
        vi = lax.dynamic_slice_in_dim(vp, start, BAND_KEYS, axis=1)
        sc = jnp.einsum('bqhd,bkhd->bhqk', qi, ki, preferred_element_type=jnp.float32) * ATTN_SCALE + bias
        kpos = start - BAND_REACH + jnp.arange(BAND_KEYS)
        sc = jnp.where(kpos >= 0, sc, NEG_INF)
        pr = jax.nn.softmax(sc, axis=-1)
        return jnp.einsum('bhqk,bkhd->bqhd', pr.astype(vi.dtype), vi)

    o = lax.map(chunk, (jnp.arange(nc), qc))
    return jnp.moveaxis(o, 0, 1).reshape(b, s_len, h, dh)


def band_sample(q, k_new, v_new, k_cache, v_cache, rel_bias):
    n_cache = k_cache.shape[1]
    t = q.shape[1]
    k = jnp.concatenate([k_cache, k_new], axis=1)
    v = jnp.concatenate([v_cache, v_new], axis=1)
    dist = jnp.arange(t)[:, None] + n_cache - jnp.arange(n_cache + t)[None, :]
    bias = rel_bias_block(rel_bias, dist).astype(jnp.float32)
    sc = jnp.einsum('bqhd,bkhd->bhqk', q, k, preferred_element_type=jnp.float32) * ATTN_SCALE + bias
    pr = jax.nn.softmax(sc, axis=-1)
    o = jnp.einsum('bhqk,bkhd->bqhd', pr.astype(v.dtype), v)
    return o, k[:, -n_cache:], v[:, -n_cache:]


def merge_heads(of, ob, out_g_fox, out_g_band, w_out):
    b, t = of.shape[:2]
    of = rms_norm(of.reshape(b, t, FOX_WIDTH), out_g_fox)
    ob = rms_norm(ob.reshape(b, t, BAND_WIDTH), out_g_band)
    return jnp.dot(jnp.concatenate([of, ob], axis=-1), w_out)


def moe_ffn(h, w_router, b_router, w_gate, w_up, w_down, ws_gate, ws_up, ws_down):
    n = h.shape[0]
    scores = jax.nn.sigmoid(jnp.dot(h, w_router, preferred_element_type=jnp.float32))
    choice = scores + b_router.astype(jnp.float32)
    grouped = choice.reshape(n, N_EXPERT_GROUPS, EXPERTS_PER_GROUP)
    group_score = jnp.sum(lax.top_k(grouped, 2)[0], axis=-1)
    _, top_groups = lax.top_k(group_score, TOPK_GROUPS)
    group_mask = jnp.sum(jax.nn.one_hot(top_groups, N_EXPERT_GROUPS, dtype=jnp.float32), axis=1) > 0
    expert_mask = jnp.repeat(group_mask, EXPERTS_PER_GROUP, axis=1)
    _, top_experts = lax.top_k(jnp.where(expert_mask, choice, NEG_INF), TOP_K)
    w = jnp.take_along_axis(scores, top_experts, axis=-1)
    w = w / jnp.sum(w, axis=-1, keepdims=True) * ROUTED_SCALE
    combine = jnp.einsum('nk,nke->ne', w, jax.nn.one_hot(top_experts, N_EXPERTS, dtype=jnp.float32))
    g = jnp.einsum('nd,edf->nef', h, w_gate)
    u = jnp.einsum('nd,edf->nef', h, w_up)
    act = jax.nn.silu(g) * u * combine[..., None].astype(h.dtype)
    routed = jnp.einsum('nef,efd->nd', act, w_down)
    shared = jnp.dot(jax.nn.silu(jnp.dot(h, ws_gate)) * jnp.dot(h, ws_up), ws_down)
    return shared + routed


def moe_prompt(h, *w):
    b, s_len, d = h.shape
    y = lax.map(lambda hb: moe_ffn(hb, *w), h.reshape(-1, MOE_BLOCK, d))
    return y.reshape(b, s_len, d)


def moe_sample(h, *w):
    b, t, d = h.shape
    return moe_ffn(h.reshape(b * t, d), *w).reshape(b, t, d)


def trunk_layer(x, c, p, attend, channel_mix):
    (w_ada, b_ada, norm1_g, norm2_g, w_in, b_forget, g_q_fox, g_k_fox, g_q_band, g_k_band,
     rel_bias, out_g_fox, out_g_band, w_out) = p[:14]
    moe_w = p[14:]
    shift1, scale1, gate1, shift2, scale2, gate2 = modulation(c, w_ada, b_ada)
    h = rms_norm(x, norm1_g) * (1.0 + scale1) + shift1
    qf, kf, vf, logf, qb, kb, vb = project_mixers(h, w_in, b_forget, g_q_fox, g_k_fox, g_q_band, g_k_band)
    of, ob, states = attend(qf, kf, vf, logf, qb, kb, vb, rel_bias)
    x = x + gate1 * merge_heads(of, ob, out_g_fox, out_g_band, w_out)
    h = rms_norm(x, norm2_g) * (1.0 + scale2) + shift2
    x = x + gate2 * channel_mix(h, *moe_w)
    return x, states


def setup_inputs(seed: int = 0) -> dict:
    key = jax.random.key(seed)
    ks = jax.random.split(key, 40)
    f32 = jnp.float32

    def nrm(k, shape, scale):
        return jax.random.normal(k, shape, f32) * scale

    L = DEPTH
    band_len = min(BAND_REACH, PAST_LEN)
    hd = (N_HEADS_FOX, HEAD_DIM)
    return {
        'x_prompt': nrm(ks[0], (BATCH, SEQ, D_MODEL), 1.0),
        'x_sample': nrm(ks[1], (DEC_BATCH, DEC_SEQ, D_MODEL), 1.0),
        'cache_fox_k': nrm(ks[2], (L, DEC_BATCH, PAST_LEN) + hd, 1.0),
        'cache_fox_v': nrm(ks[3], (L, DEC_BATCH, PAST_LEN) + hd, 1.0),
        'cache_fox_logf': jax.nn.log_sigmoid(nrm(ks[4], (L, DEC_BATCH, PAST_LEN, N_HEADS_FOX), 1.0) + 3.0),
        'cache_band_k': nrm(ks[5], (L, DEC_BATCH, band_len, N_HEADS_BAND, HEAD_DIM), 1.0),
        'cache_band_v': nrm(ks[6], (L, DEC_BATCH, band_len, N_HEADS_BAND, HEAD_DIM), 1.0),
        'c_prompt': nrm(ks[7], (BATCH, D_MODEL), 1.0),
        'c_sample': nrm(ks[8], (DEC_BATCH, D_MODEL), 1.0),
        'w_ada': nrm(ks[9], (L, D_MODEL, 6 * D_MODEL), 0.3 * D_MODEL ** -0.5),
        'b_ada': nrm(ks[10], (L, 6 * D_MODEL), 0.02),
        'norm1_g': 1.0 + nrm(ks[11], (L, D_MODEL), 0.02),
        'norm2_g': 1.0 + nrm(ks[12], (L, D_MODEL), 0.02),
        'w_in': nrm(ks[13], (L, D_MODEL, IN_WIDTH), D_MODEL ** -0.5),
        'b_forget': jax.random.uniform(ks[14], (L, N_HEADS_FOX), f32, 1.0, 5.0),
        'g_q_fox': 1.0 + nrm(ks[15], (L, HEAD_DIM), 0.02),
        'g_k_fox': 1.0 + nrm(ks[16], (L, HEAD_DIM), 0.02),
        'g_q_band': 1.0 + nrm(ks[17], (L, HEAD_DIM), 0.02),
        'g_k_band': 1.0 + nrm(ks[18], (L, HEAD_DIM), 0.02),
        'rel_bias': nrm(ks[19], (L, N_HEADS_BAND, N_REL), 0.5),
        'out_g_fox': 1.0 + nrm(ks[20], (L, FOX_WIDTH), 0.02),
        'out_g_band': 1.0 + nrm(ks[21], (L, BAND_WIDTH), 0.02),
        'w_out': nrm(ks[22], (L, MIX_WIDTH, D_MODEL), MIX_WIDTH ** -0.5),
        'w_router': nrm(ks[23], (L, D_MODEL, N_EXPERTS), D_MODEL ** -0.5),
        'b_router': nrm(ks[24], (L, N_EXPERTS), 0.01),
        'w_gate': nrm(ks[25], (L, N_EXPERTS, D_MODEL, EXPERT_FF), D_MODEL ** -0.5),
        'w_up': nrm(ks[26], (L, N_EXPERTS, D_MODEL, EXPERT_FF), D_MODEL ** -0.5),
        'w_down': nrm(ks[27], (L, N_EXPERTS, EXPERT_FF, D_MODEL), EXPERT_FF ** -0.5),
        'ws_gate': nrm(ks[28], (L, D_MODEL, SHARED_FF), D_MODEL ** -0.5),
        'ws_up': nrm(ks[29], (L, D_MODEL, SHARED_FF), D_MODEL ** -0.5),
        'ws_down': nrm(ks[30], (L, SHARED_FF, D_MODEL), SHARED_FF ** -0.5),
    }


def reference(x_prompt, x_sample, cache_fox_k, cache_fox_v, cache_fox_logf, cache_band_k, cache_band_v,
              c_prompt, c_sample, w_ada, b_ada, norm1_g, norm2_g, w_in, b_forget, g_q_fox, g_k_fox,
              g_q_band, g_k_band, rel_bias, out_g_fox, out_g_band, w_out, w_router, b_router,
              w_gate, w_up, w_down, ws_gate, ws_up, ws_down):
    xp, xs = x_prompt, x_sample
    sp_fk, sp_fv, sp_fl, sp_bk, sp_bv = [], [], [], [], []
    ss_fk, ss_fv, ss_fl, ss_bk, ss_bv = [], [], [], [], []
    for l in range(DEPTH):
        p = (w_ada[l], b_ada[l], norm1_g[l], norm2_g[l], w_in[l], b_forget[l], g_q_fox[l], g_k_fox[l],
             g_q_band[l], g_k_band[l], rel_bias[l], out_g_fox[l], out_g_band[l], w_out[l],
             w_router[l], b_router[l], w_gate[l], w_up[l], w_down[l], ws_gate[l], ws_up[l], ws_down[l])

        def attend_prompt(qf, kf, vf, logf, qb, kb, vb, rb):
            of = fox_prompt(qf, kf, vf, logf)
            ob = band_prompt(qb, kb, vb, rb)
            n_keep = min(BAND_REACH, kb.shape[1])
            return of, ob, (kf, vf, logf, kb[:, -n_keep:], vb[:, -n_keep:])

        def attend_sample(qf, kf, vf, logf, qb, kb, vb, rb, l=l):
            of = fox_sample(qf, kf, vf, logf, cache_fox_k[l], cache_fox_v[l], cache_fox_logf[l])
            ob, nbk, nbv = band_sample(qb, kb, vb, cache_band_k[l], cache_band_v[l], rb)
            return of, ob, (kf, vf, logf, nbk, nbv)

        xp, (pk, pv, pl, pbk, pbv) = trunk_layer(xp, c_prompt, p, attend_prompt, moe_prompt)
        xs, (sk, sv, sl, sbk, sbv) = trunk_layer(xs, c_sample, p, attend_sample, moe_sample)
        sp_fk.append(pk); sp_fv.append(pv); sp_fl.append(pl); sp_bk.append(pbk); sp_bv.append(pbv)
        ss_fk.append(sk); ss_fv.append(sv); ss_fl.append(sl); ss_bk.append(sbk); ss_bv.append(sbv)
    return (xp, xs,
            jnp.stack(sp_fk), jnp.stack(sp_fv), jnp.stack(sp_fl), jnp.stack(sp_bk), jnp.stack(sp_bv),
            jnp.stack(ss_fk), jnp.stack(ss_fv), jnp.stack(ss_fl), jnp.stack(ss_bk), jnp.stack(ss_bv))
```

```python
import functools

import jax
import jax.numpy as jnp
from jax import lax
from jax.experimental import pallas as pl
from jax.experimental.pallas import tpu as pltpu

F32 = jnp.float32
BF16 = jnp.bfloat16

HEAD_DIM = 64
N_HEADS = 8
WIDTH = N_HEADS * HEAD_DIM
PAIR = 2 * HEAD_DIM
N_PAIRS = N_HEADS // 2
LANES = 128
CHUNK = 64
BAND_REACH = 512
REL_CLIP = 256
N_EXPERTS = 64
N_GROUPS = 8
GROUP_SIZE = N_EXPERTS // N_GROUPS
TOPK_GROUPS = 4
TOP_K = 8
ROUTED_SCALE = 2.5
EPS = 1e-6
NEG_INF = -1e30
ATTN_SCALE = HEAD_DIM ** -0.5
LOG2E = 1.4426950408889634
VMEM_LIMIT = 56 * 1024 * 1024


def _cparams(n_axes):
    return pltpu.CompilerParams(dimension_semantics=("arbitrary",) * n_axes,
                                vmem_limit_bytes=VMEM_LIMIT)


def _dot(a, b):
    return jnp.dot(a, b, preferred_element_type=F32)


def _dot_nt(a, b):
    return lax.dot_general(a, b, (((1,), (1,)), ((), ())), preferred_element_type=F32)


def _split2(a):
    hi = a.astype(BF16)
    lo = (a - hi.astype(F32)).astype(BF16)
    return hi, lo


def _split3(a):
    hi = a.astype(BF16)
    r = a - hi.astype(F32)
    mid = r.astype(BF16)
    lo = (r - mid.astype(F32)).astype(BF16)
    return hi, mid, lo


def _ada_body(c_ref, w_ref, b_ref, o_ref):
    c = c_ref[...]
    a = c * jax.nn.sigmoid(c)
    a_hi, a_lo = _split2(a)
    w_hi, w_lo = _split2(w_ref[...])
    o_ref[...] = _dot(a_hi, w_hi) + _dot(a_hi, w_lo) + _dot(a_lo, w_hi) + b_ref[...]


def _ada(c_all, w_ada, b_ada):
    rows, d = c_all.shape
    n = w_ada.shape[1]
    tn = 1024
    return pl.pallas_call(
        _ada_body,
        grid=(n // tn,),
        in_specs=[pl.BlockSpec((rows, d), lambda j: (0, 0)),
                  pl.BlockSpec((d, tn), lambda j: (0, j)),
                  pl.BlockSpec((1, tn), lambda j: (0, j))],
        out_specs=pl.BlockSpec((rows, tn), lambda j: (0, j)),
        out_shape=jax.ShapeDtypeStruct((rows, n), F32),
        compiler_params=_cparams(1),
        name="ada",
    )(c_all, w_ada, b_ada)


def _log_sigmoid(z):
    return jnp.minimum(z, 0.0) - jnp.log(1.0 + jnp.exp(-jnp.abs(z)))


def _proj_body(x_ref, sh_ref, sc_ref, g1_ref, w_ref, bd_ref, gqf_ref, gkf_ref, gqb_ref, gkb_ref, bf_ref,
               qf_ref, kf_ref, vf_ref, kf32_ref, vf32_ref, lf_ref, qb_ref, kb_ref, vb_ref, kb32_ref, vb32_ref,
               *, band_last_only):
    x = x_ref[...]
    g, r, d = x.shape
    ms = jnp.mean(x * x, axis=-1, keepdims=True)
    h = x * lax.rsqrt(ms + EPS) * g1_ref[...] * (1.0 + sc_ref[...]) + sh_ref[...]
    hb = h.reshape(g * r, d).astype(BF16)

    def seg(i):
        return _dot(hb, w_ref[:, i * WIDTH:(i + 1) * WIDTH])

    def head_norm(t, gain_ref):
        ssq = _dot((t * t).astype(BF16), bd_ref[...])
        return t * lax.rsqrt(ssq + EPS) * gain_ref[...]

    qf_ref[...] = head_norm(seg(0), gqf_ref).astype(BF16)
    kf = head_norm(seg(1), gkf_ref)
    kf32_ref[...] = kf
    kf_ref[...] = kf.astype(BF16)
    vf = seg(2)
    vf32_ref[...] = vf
    vf_ref[...] = vf.astype(BF16)
    z = _dot(hb, w_ref[:, 6 * WIDTH:6 * WIDTH + LANES]) + bf_ref[...]
    lf_ref[...] = _log_sigmoid(z)
    qb_ref[...] = head_norm(seg(3), gqb_ref).astype(BF16)
    kb = head_norm(seg(4), gkb_ref)
    kb_ref[...] = kb.astype(BF16)
    vb = seg(5)
    vb_ref[...] = vb.astype(BF16)

    if band_last_only:
        @pl.when(pl.program_id(1) == pl.num_programs(1) - 1)
        def _():
            kb32_ref[...] = kb
            vb32_ref[...] = vb
    else:
        kb32_ref[...] = kb
        vb32_ref[...] = vb


def _proj(x, shift, scale, g1, w_all, bd, gqf, gkf, gqb, gkb, bf_row, *, G, R, band_last_only):
    nb, s, d = x.shape
    n = nb * s
    tm = G * R
    nbi, nsi = nb // G, s // R
    grid = (nbi, nsi)
    row = lambda b, i: (b * nsi + i, 0)
    const = lambda b, i: (0, 0)
    mod_spec = pl.BlockSpec((G, 1, d), lambda b, i: (b, 0, 0))
    out_bf = jax.ShapeDtypeStruct((n, WIDTH), BF16)
    out_f32 = jax.ShapeDtypeStruct((n, WIDTH), F32)
    tile = pl.BlockSpec((tm, WIDTH), row)
    if band_last_only:
        assert G == 1 and R == BAND_REACH
        band_shape = jax.ShapeDtypeStruct((nb, BAND_REACH, WIDTH), F32)
        band_spec = pl.BlockSpec((None, BAND_REACH, WIDTH), lambda b, i: (b, 0, 0))
    else:
        band_shape, band_spec = out_f32, tile
    return pl.pallas_call(
        functools.partial(_proj_body, band_last_only=band_last_only),
        grid=grid,
        in_specs=[pl.BlockSpec((G, R, d), lambda b, i: (b, i, 0)), mod_spec, mod_spec,
                  pl.BlockSpec((1, d), const), pl.BlockSpec(w_all.shape, const), pl.BlockSpec(bd.shape, const),
                  pl.BlockSpec((1, WIDTH), const), pl.BlockSpec((1, WIDTH), const),
                  pl.BlockSpec((1, WIDTH), const), pl.BlockSpec((1, WIDTH), const),
                  pl.BlockSpec((1, LANES), const)],
        out_specs=[tile, tile, tile, tile, tile, pl.BlockSpec((tm, LANES), row), tile, tile, tile,
                   band_spec, band_spec],
        out_shape=[out_bf, out_bf, out_bf, out_f32, out_f32, jax.ShapeDtypeStruct((n, LANES), F32),
                   out_bf, out_bf, out_bf, band_shape, band_shape],
        compiler_params=_cparams(2),
        name="proj",
    )(x, shift, scale, g1, w_all, bd, gqf, gkf, gqb, gkb, bf_row)


def _scan_body(lf_ref, cum_ref, cumt_ref, carry_ref):
    @pl.when(pl.program_id(1) == 0)
    def _():
        carry_ref[...] = jnp.zeros_like(carry_ref)

    lf = lf_ref[...]
    ts = lf.shape[0]
    lane = lax.broadcasted_iota(jnp.int32, lf.shape, 1)
    lf = jnp.where(lane < N_HEADS, lf, 0.0)
    hi, mid, lo = _split3(lf)
    rr = lax.broadcasted_iota(jnp.int32, (ts, ts), 0)
    cc = lax.broadcasted_iota(jnp.int32, (ts, ts), 1)
    tri = jnp.where(cc <= rr, 1.0, 0.0).astype(BF16)
    cum = _dot(tri, hi) + _dot(tri, mid) + _dot(tri, lo) + carry_ref[0:1, :]
    carry_ref[...] = jnp.broadcast_to(cum[ts - 1:ts, :], carry_ref.shape)
    cum2 = cum * LOG2E
    cum_ref[...] = cum2
    cumt_ref[...] = cum2.T[0:N_HEADS, :]


def _scan(lf, *, TS, TK):
    b, s, _ = lf.shape
    assert TS == TK
    t_block = (None, None, N_HEADS, TK)
    t_map = lambda bi, i: (bi, i, 0, 0)
    return pl.pallas_call(
        _scan_body,
        grid=(b, s // TS),
        in_specs=[pl.BlockSpec((None, TS, LANES), lambda bi, i: (bi, i, 0))],
        out_specs=[pl.BlockSpec((None, TS, LANES), lambda bi, i: (bi, i, 0)),
                   pl.BlockSpec(t_block, t_map)],
        out_shape=[jax.ShapeDtypeStruct((b, s, LANES), F32),
                   jax.ShapeDtypeStruct((b, s // TK, N_HEADS, TK), F32)],
        scratch_shapes=[pltpu.VMEM((8, LANES), F32)],
        compiler_params=_cparams(2),
        name="scan",
    )(lf)


def _fox_body(q_ref, k_ref, v_ref, cq_ref, ck_ref, o_ref, *, TQ, TK, q_off):
    p = pl.program_id(1)
    i = pl.program_id(2)
    q = q_ref[...]
    cq_blk = cq_ref[...]
    lane = lax.broadcasted_iota(jnp.int32, (TQ, PAIR), 1)
    ones_blk = jnp.where(lax.broadcasted_iota(jnp.int32, (TK, LANES), 1) == 0, 1.0, 0.0).astype(BF16)
    q0 = q_off + i * TQ
    n_full = q0 // TK
    qpos = q0 + lax.broadcasted_iota(jnp.int32, (TQ, TK), 0)
    kcol = lax.broadcasted_iota(jnp.int32, (TQ, TK), 1)

    outs = []
    for par in range(2):
        h = 2 * p + par
        qm = jnp.where((lane >= HEAD_DIM) == (par == 1), q, jnp.zeros_like(q))
        cq_col = jnp.sum(jnp.where(lane == h, cq_blk, 0.0), axis=1, keepdims=True)
        ref0 = cq_col[0:1, :]
        cqr = cq_col - ref0

        def step(j, carry, masked):
            m, l, acc = carry
            k0 = pl.multiple_of(j * TK, TK)
            kb = k_ref[pl.ds(k0, TK), :]
            vb = v_ref[pl.ds(k0, TK), :]
            s = _dot_nt(qm, kb)
            ck = ck_ref[j, pl.ds(h, 1), :]
            u = s - (ck - ref0)
            if masked:
                u = jnp.where(k0 + kcol <= qpos, u, NEG_INF)
            m_new = jnp.maximum(m, jnp.max(u, axis=1, keepdims=True) + cqr)
            alpha = jnp.exp2(m - m_new)
            pexp = jnp.exp2(u + (cqr - m_new))
            pv = _dot(pexp.astype(BF16), jnp.concatenate([vb, ones_blk], axis=1))
            return m_new, alpha * l + pv[:, LANES:LANES + 1], alpha * acc + pv[:, :LANES]

        init = (jnp.full((TQ, 1), NEG_INF, F32), jnp.zeros((TQ, 1), F32), jnp.zeros((TQ, LANES), F32))
        carry = lax.fori_loop(0, n_full, lambda j, c: step(j, c, False), init)
        _, l, acc = step(n_full, carry, True)
        outs.append(acc / l)
    o_ref[...] = jnp.where(lane < HEAD_DIM, outs[0], outs[1]).astype(o_ref.dtype)


def _fox(q, k, v, cum, cumt, *, TQ, TK, q_off):
    b, sq, _ = q.shape
    sk = k.shape[1]
    return pl.pallas_call(
        functools.partial(_fox_body, TQ=TQ, TK=TK, q_off=q_off),
        grid=(b, N_PAIRS, sq // TQ),
        in_specs=[pl.BlockSpec((None, TQ, PAIR), lambda bi, p, i: (bi, i, p)),
                  pl.BlockSpec((None, sk, PAIR), lambda bi, p, i: (bi, 0, p)),
                  pl.BlockSpec((None, sk, PAIR), lambda bi, p, i: (bi, 0, p)),
                  pl.BlockSpec((None, TQ, LANES), lambda bi, p, i: (bi, i, 0)),
                  pl.BlockSpec((None, sk // TK, N_HEADS, TK), lambda bi, p, i: (bi, 0, 0, 0))],
        out_specs=pl.BlockSpec((None, TQ, PAIR), lambda bi, p, i: (bi, i, p)),
        out_shape=jax.ShapeDtypeStruct((b, sq, WIDTH), BF16),
        compiler_params=_cparams(3),
        name="fox",
    )(q, k, v, cum, cumt)


def _band_body(q_ref, k_ref, v_ref, bias_ref, o_ref, *scratch, TQ, W, n_sub, padded):
    p = pl.program_id(1)
    i = pl.program_id(2)
    if padded:
        kpad_ref, vpad_ref = scratch
        s_len = k_ref.shape[0]

        @pl.when(i == 0)
        def _():
            zeros = jnp.zeros((BAND_REACH, PAIR), BF16)
            kpad_ref[pl.ds(0, BAND_REACH), :] = zeros
            vpad_ref[pl.ds(0, BAND_REACH), :] = zeros
            kpad_ref[pl.ds(BAND_REACH, s_len), :] = k_ref[...]
            vpad_ref[pl.ds(BAND_REACH, s_len), :] = v_ref[...]
    else:
        kpad_ref, vpad_ref = k_ref, v_ref

    lane = lax.broadcasted_iota(jnp.int32, (TQ, PAIR), 1)
    ones_blk = jnp.where(lax.broadcasted_iota(jnp.int32, (W, LANES), 1) == 0, 1.0, 0.0).astype(BF16)
    kcol = lax.broadcasted_iota(jnp.int32, (TQ, W), 1)

    def sub_block(sub, carry):
        r0 = pl.multiple_of(sub * TQ, TQ)
        q0 = i * (n_sub * TQ) + r0
        q = q_ref[pl.ds(r0, TQ), :]
        kw = kpad_ref[pl.ds(pl.multiple_of(q0, TQ), W), :] if padded else kpad_ref[...]
        vw = vpad_ref[pl.ds(pl.multiple_of(q0, TQ), W), :] if padded else vpad_ref[...]
        vcat = jnp.concatenate([vw, ones_blk], axis=1)
        outs = []
        for par in range(2):
            h = 2 * p + par
            qm = jnp.where((lane >= HEAD_DIM) == (par == 1), q, jnp.zeros_like(q))
            s = _dot_nt(qm, kw) + bias_ref[h]
            if padded:
                s = jnp.where(kcol >= BAND_REACH - q0, s, NEG_INF)
            m = jnp.max(s, axis=1, keepdims=True)
            pexp = jnp.exp2(s - m)
            pv = _dot(pexp.astype(BF16), vcat)
            outs.append(pv[:, :LANES] / pv[:, LANES:LANES + 1])
        o_ref[pl.ds(r0, TQ), :] = jnp.where(lane < HEAD_DIM, outs[0], outs[1]).astype(o_ref.dtype)
        return carry

    lax.fori_loop(0, n_sub, sub_block, 0)


def _band(q, k, v, bias, *, TQ, n_sub, padded):
    b, sq, _ = q.shape
    sk = k.shape[1]
    w = BAND_REACH + TQ if padded else sk
    tqb = TQ * n_sub
    scratch = [pltpu.VMEM((sk + BAND_REACH, PAIR), BF16)] * 2 if padded else []
    return pl.pallas_call(
        functools.partial(_band_body, TQ=TQ, W=w, n_sub=n_sub, padded=padded),
        grid=(b, N_PAIRS, sq // tqb),
        in_specs=[pl.BlockSpec((None, tqb, PAIR), lambda bi, p, i: (bi, i, p)),
                  pl.BlockSpec((None, sk, PAIR), lambda bi, p, i: (bi, 0, p)),
                  pl.BlockSpec((None, sk, PAIR), lambda bi, p, i: (bi, 0, p)),
                  pl.BlockSpec(bias.shape, lambda bi, p, i: (0, 0, 0))],
        out_specs=pl.BlockSpec((None, tqb, PAIR), lambda bi, p, i: (bi, i, p)),
        out_shape=jax.ShapeDtypeStruct((b, sq, WIDTH), BF16),
        scratch_shapes=scratch,
        compiler_params=_cparams(3),
        name="band",
    )(q, k, v, bias)


def _band_bias_tile(rel_bias, tq, w):
    r = jnp.arange(tq)[:, None]
    c = jnp.arange(w)[None, :]
    dist = r + BAND_REACH - c
    in_band = (c // CHUNK >= r // CHUNK) & (c // CHUNK <= r // CHUNK + BAND_REACH // CHUNK)
    vals = rel_bias[:, jnp.clip(dist, -REL_CLIP, REL_CLIP) + REL_CLIP] * LOG2E
    return jnp.where(in_band[None], vals, NEG_INF).astype(F32)


def _first_index(is_max, idx, axis, big):
    return jnp.min(jnp.where(is_max, idx, big), axis=axis, keepdims=True)


def _route(scores, choice):
    t = scores.shape[1]
    c3 = choice.reshape(N_GROUPS, GROUP_SIZE, t)
    j_idx = lax.broadcasted_iota(jnp.int32, c3.shape, 1)
    top1 = jnp.max(c3, axis=1, keepdims=True)
    first = _first_index(c3 == top1, j_idx, 1, GROUP_SIZE)
    top2 = jnp.max(jnp.where(j_idx == first, -jnp.inf, c3), axis=1, keepdims=True)
    gscore = (top1 + top2).reshape(N_GROUPS, t)

    g_idx = lax.broadcasted_iota(jnp.int32, gscore.shape, 0)
    gsel = jnp.zeros(gscore.shape, F32)
    work = gscore
    for _ in range(TOPK_GROUPS):
        gm = jnp.max(work, axis=0, keepdims=True)
        pick = g_idx == _first_index(work == gm, g_idx, 0, N_GROUPS)
        gsel = jnp.where(pick, 1.0, gsel)
        work = jnp.where(pick, -jnp.inf, work)

    emask = jnp.broadcast_to(gsel.reshape(N_GROUPS, 1, t), c3.shape) > 0.0
    work = jnp.where(emask, c3, NEG_INF)
    e_idx = lax.broadcasted_iota(jnp.int32, c3.shape, 0) * GROUP_SIZE + j_idx
    esel = jnp.zeros(c3.shape, F32)
    for _ in range(TOP_K):
        em = jnp.max(jnp.max(work, axis=1, keepdims=True), axis=0, keepdims=True)
        cand = jnp.where(work == em, e_idx, N_EXPERTS)
        first = jnp.min(jnp.min(cand, axis=1, keepdims=True), axis=0, keepdims=True)
        pick = e_idx == first
        esel = jnp.where(pick, 1.0, esel)
        work = jnp.where(pick, -jnp.inf, work)

    w = esel * scores.reshape(c3.shape)
    denom = jnp.sum(jnp.sum(w, axis=1, keepdims=True), axis=0, keepdims=True)
    return (w / denom * ROUTED_SCALE).reshape(N_EXPERTS, t)


def _merge_body(of_ref, ob_ref, x_ref, gate_ref, sh_ref, sc_ref, ogf_ref, ogb_ref, wo_ref, g2_ref,
                wrh_ref, wrl_ref, br_ref, x1_ref, h2_ref, comb_ref):
    def group_norm(t_ref, gain_ref):
        t = t_ref[...].astype(F32)
        ms = jnp.mean(t * t, axis=-1, keepdims=True)
        return (t * lax.rsqrt(ms + EPS) * gain_ref[...]).astype(BF16)

    y = _dot(group_norm(of_ref, ogf_ref), wo_ref[0:WIDTH, :]) + _dot(group_norm(ob_ref, ogb_ref), wo_ref[WIDTH:, :])
    x = x_ref[...]
    g, r, d = x.shape
    x1 = x + gate_ref[...] * y.reshape(g, r, d)
    x1_ref[...] = x1
    ms = jnp.mean(x1 * x1, axis=-1, keepdims=True)
    h2 = (x1 * lax.rsqrt(ms + EPS) * g2_ref[...] * (1.0 + sc_ref[...]) + sh_ref[...]).reshape(g * r, d)
    h_hi, h_lo = _split2(h2)
    h2_ref[...] = h_hi
    logits = _dot_nt(wrh_ref[...], h_hi) + _dot_nt(wrh_ref[...], h_lo) + _dot_nt(wrl_ref[...], h_hi)
    scores = jax.nn.sigmoid(logits)
    t = scores.shape[1]
    bias = jnp.concatenate([br_ref[...]] * (t // LANES), axis=1)
    comb = _route(scores, scores + bias)
    comb_pad = jnp.concatenate([comb, jnp.zeros((LANES - N_EXPERTS, t), F32)], axis=0)
    comb_ref[...] = comb_pad.T


def _merge(of, ob, x, gate, shift, scale, ogf, ogb, wo, g2, wr_hi, wr_lo, br, *, G, R):
    nb, s, d = x.shape
    n = nb * s
    tm = G * R
    nbi, nsi = nb // G, s // R
    row = lambda b, i: (b * nsi + i, 0)
    const = lambda b, i: (0, 0)
    mod_spec = pl.BlockSpec((G, 1, d), lambda b, i: (b, 0, 0))
    x_spec = pl.BlockSpec((G, R, d), lambda b, i: (b, i, 0))
    return pl.pallas_call(
        _merge_body,
        grid=(nbi, nsi),
        in_specs=[pl.BlockSpec((tm, WIDTH), row), pl.BlockSpec((tm, WIDTH), row), x_spec,
                  mod_spec, mod_spec, mod_spec,
                  pl.BlockSpec((1, WIDTH), const), pl.BlockSpec((1, WIDTH), const),
                  pl.BlockSpec(wo.shape, const), pl.BlockSpec((1, d), const),
                  pl.BlockSpec(wr_hi.shape, const), pl.BlockSpec(wr_lo.shape, const),
                  pl.BlockSpec(br.shape, const)],
        out_specs=[x_spec, pl.BlockSpec((tm, d), row), pl.BlockSpec((tm, LANES), row)],
        out_shape=[jax.ShapeDtypeStruct((nb, s, d), F32), jax.ShapeDtypeStruct((n, d), BF16),
                   jax.ShapeDtypeStruct((n, LANES), F32)],
        compiler_params=_cparams(2),
        name="merge",
    )(of, ob, x, gate, shift, scale, ogf, ogb, wo, g2, wr_hi, wr_lo, br)


def _silu(g):
    return g * jax.nn.sigmoid(g)


def _moe_body(h_ref, comb_ref, x1_ref, gate_ref, wg_ref, wu_ref, wd_ref, sg_ref, su_ref, sd_ref, y_ref, acc_ref):
    e = pl.program_id(2)
    hb = h_ref[...]

    @pl.when(e == 0)
    def _():
        a = _silu(_dot(hb, sg_ref[...])) * _dot(hb, su_ref[...])
        acc_ref[...] = _dot(a.astype(BF16), sd_ref[...])

    comb = comb_ref[...]
    lane = lax.broadcasted_iota(jnp.int32, comb.shape, 1)
    c_e = jnp.sum(jnp.where(lane == e, comb, 0.0), axis=1, keepdims=True)
    a = _silu(_dot(hb, wg_ref[...])) * _dot(hb, wu_ref[...]) * c_e
    acc_ref[...] += _dot(a.astype(BF16), wd_ref[...])

    @pl.when(e == pl.num_programs(2) - 1)
    def _():
        x1 = x1_ref[...]
        g, r, d = x1.shape
        y_ref[...] = x1 + gate_ref[...] * acc_ref[...].reshape(g, r, d)


def _moe(h2, comb, x1, gate, wg, wu, wd, sg, su, sd, *, G, R):
    nb, s, d = x1.shape
    tm = G * R
    nbi, nsi = nb // G, s // R
    ff = wg.shape[2]
    row = lambda b, i, e: (b * nsi + i, 0)
    const = lambda b, i, e: (0, 0)
    x_spec = pl.BlockSpec((G, R, d), lambda b, i, e: (b, i, 0))
    return pl.pallas_call(
        _moe_body,
        grid=(nbi, nsi, N_EXPERTS),
        in_specs=[pl.BlockSpec((tm, d), row), pl.BlockSpec((tm, LANES), row), x_spec,
                  pl.BlockSpec((G, 1, d), lambda b, i, e: (b, 0, 0)),
                  pl.BlockSpec((None, d, ff), lambda b, i, e: (e, 0, 0)),
                  pl.BlockSpec((None, d, ff), lambda b, i, e: (e, 0, 0)),
                  pl.BlockSpec((None, ff, d), lambda b, i, e: (e, 0, 0)),
                  pl.BlockSpec(sg.shape, const), pl.BlockSpec(su.shape, const), pl.BlockSpec(sd.shape, const)],
        out_specs=x_spec,
        out_shape=jax.ShapeDtypeStruct((nb, s, d), F32),
        scratch_shapes=[pltpu.VMEM((tm, d), F32)],
        compiler_params=_cparams(3),
        name="moe",
    )(h2, comb, x1, gate, wg, wu, wd, sg, su, sd)


def _tile_heads(g, mult=1.0):
    return (jnp.tile(g.astype(F32), N_HEADS) * mult).reshape(1, WIDTH)


def kernel(x_prompt, x_sample, cache_fox_k, cache_fox_v, cache_fox_logf, cache_band_k, cache_band_v, c_prompt, c_sample, w_ada, b_ada, norm1_g, norm2_g, w_in, b_forget, g_q_fox, g_k_fox, g_q_band, g_k_band, rel_bias, out_g_fox, out_g_band, w_out, w_router, b_router, w_gate, w_up, w_down, ws_gate, ws_up, ws_down):
    depth = w_ada.shape[0]
    assert depth == 1
    bsz, seq, d = x_prompt.shape
    dbs, dseq, _ = x_sample.shape
    past = cache_fox_k.shape[2]
    n_cache = cache_band_k.shape[2]
    assert n_cache == BAND_REACH and dseq == CHUNK and seq % BAND_REACH == 0

    wi = w_in[0]
    cols = [wi[:, 0:512], wi[:, 512:1024], wi[:, 1024:1536], wi[:, 1544:2056], wi[:, 2056:2568], wi[:, 2568:3080],
            wi[:, 1536:1544], jnp.zeros((d, LANES - N_HEADS), F32)]
    w_all = jnp.concatenate(cols, axis=1).astype(BF16)
    hd = jnp.arange(WIDTH) // HEAD_DIM
    bd = jnp.where(hd[:, None] == hd[None, :], 1.0 / HEAD_DIM, 0.0).astype(BF16)
    qscale = ATTN_SCALE * LOG2E
    gqf, gkf = _tile_heads(g_q_fox[0], qscale), _tile_heads(g_k_fox[0])
    gqb, gkb = _tile_heads(g_q_band[0], qscale), _tile_heads(g_k_band[0])
    bf_row = jnp.concatenate([b_forget[0], jnp.zeros((LANES - N_HEADS,), F32)]).reshape(1, LANES)
    g1 = norm1_g[0].reshape(1, d)
    g2 = norm2_g[0].reshape(1, d)
    ogf = out_g_fox[0].reshape(1, WIDTH)
    ogb = out_g_band[0].reshape(1, WIDTH)
    wo = w_out[0].astype(BF16)
    wr_t = w_router[0].T
    wr_hi = wr_t.astype(BF16)
    wr_lo = (wr_t - wr_hi.astype(F32)).astype(BF16)
    br = jnp.broadcast_to(b_router[0].reshape(N_EXPERTS, 1), (N_EXPERTS, LANES)).astype(F32)
    wg, wu, wd = w_gate[0].astype(BF16), w_up[0].astype(BF16), w_down[0].astype(BF16)
    sg, su, sd = ws_gate[0].astype(BF16), ws_up[0].astype(BF16), ws_down[0].astype(BF16)

    n_c = bsz + dbs
    rows = -(-n_c // 8) * 8
    c_all = jnp.concatenate([c_prompt, c_sample, jnp.zeros((rows - n_c, d), F32)], axis=0)
    mod = _ada(c_all, w_ada[0], b_ada[0].reshape(1, -1))

    def mods(lo, hi):
        return [mod[lo:hi, j * d:(j + 1) * d].reshape(hi - lo, 1, d) for j in range(6)]

    shift1_p, scale1_p, gate1_p, shift2_p, scale2_p, gate2_p = mods(0, bsz)
    shift1_s, scale1_s, gate1_s, shift2_s, scale2_s, gate2_s = mods(bsz, n_c)

    TM = BAND_REACH
    (qf, kf, vf, kf32, vf32, lf, qb, kb, vb, kb32, vb32) = _proj(
        x_prompt, shift1_p, scale1_p, g1, w_all, bd, gqf, gkf, gqb, gkb, bf_row, G=1, R=TM, band_last_only=True)
    r3 = lambda a: a.reshape(bsz, seq, a.shape[-1])
    TQ, TK = 256, 512
    cum, cumt = _scan(r3(lf), TS=TK, TK=TK)
    of = _fox(r3(qf), r3(kf), r3(vf), cum, cumt, TQ=TQ, TK=TK, q_off=0)
    band_tq = 128
    bias_p = _band_bias_tile(rel_bias[0], band_tq, BAND_REACH + band_tq)
    n_sub = min(16, seq // band_tq)
    ob = _band(r3(qb), r3(kb), r3(vb), bias_p, TQ=band_tq, n_sub=n_sub, padded=True)
    x1_p, h2_p, comb_p = _merge(of.reshape(-1, WIDTH), ob.reshape(-1, WIDTH), x_prompt, gate1_p, shift2_p, scale2_p,
                                ogf, ogb, wo, g2, wr_hi, wr_lo, br, G=1, R=TM)
    moe_r = 1024 if seq % 1024 == 0 else TM
    y_p = _moe(h2_p, comb_p, x1_p, gate2_p, wg, wu, wd, sg, su, sd, G=1, R=moe_r)

    GS = 8
    (qf_s, kf_s, vf_s, kf32_s, vf32_s, lf_s, qb_s, kb_s, vb_s, kb32_s, vb32_s) = _proj(
        x_sample, shift1_s, scale1_s, g1, w_all, bd, gqf, gkf, gqb, gkb, bf_row, G=GS, R=dseq, band_last_only=False)
    s3 = lambda a: a.reshape(dbs, dseq, a.shape[-1])
    sk = past + dseq
    skp = -(-sk // LANES) * LANES
    pad_k = skp - sk
    lf_cache = jnp.pad(cache_fox_logf[0], ((0, 0), (0, 0), (0, LANES - N_HEADS)))
    lf_all = jnp.concatenate([lf_cache, s3(lf_s), jnp.zeros((dbs, pad_k, LANES), F32)], axis=1)
    cum_s, cumt_s = _scan(lf_all, TS=skp, TK=skp)
    zpad = jnp.zeros((dbs, pad_k, WIDTH), BF16)
    k_all = jnp.concatenate([cache_fox_k[0].reshape(dbs, past, WIDTH).astype(BF16), s3(kf_s), zpad], axis=1)
    v_all = jnp.concatenate([cache_fox_v[0].reshape(dbs, past, WIDTH).astype(BF16), s3(vf_s), zpad], axis=1)
    of_s = _fox(s3(qf_s), k_all, v_all, cum_s[:, past:past + dseq], cumt_s, TQ=dseq, TK=skp, q_off=past)
    bias_s = _band_bias_tile(rel_bias[0], dseq, BAND_REACH + LANES)
    zb = jnp.zeros((dbs, LANES - dseq, WIDTH), BF16)
    kb_all = jnp.concatenate([cache_band_k[0].reshape(dbs, n_cache, WIDTH).astype(BF16), s3(kb_s), zb], axis=1)
    vb_all = jnp.concatenate([cache_band_v[0].reshape(dbs, n_cache, WIDTH).astype(BF16), s3(vb_s), zb], axis=1)
    ob_s = _band(s3(qb_s), kb_all, vb_all, bias_s, TQ=dseq, n_sub=1, padded=False)
    x1_s, h2_s, comb_s = _merge(of_s.reshape(-1, WIDTH), ob_s.reshape(-1, WIDTH), x_sample, gate1_s, shift2_s,
                                scale2_s, ogf, ogb, wo, g2, wr_hi, wr_lo, br, G=GS, R=dseq)
    y_s = _moe(h2_s, comb_s, x1_s, gate2_s, wg, wu, wd, sg, su, sd, G=dbs, R=dseq)

    hshape = (N_HEADS, HEAD_DIM)
    new_bk_s = jnp.concatenate([cache_band_k[0], s3(kb32_s).reshape(dbs, dseq, *hshape)], axis=1)[:, -n_cache:]
    new_bv_s = jnp.concatenate([cache_band_v[0], s3(vb32_s).reshape(dbs, dseq, *hshape)], axis=1)[:, -n_cache:]
    return (y_p, y_s,
            kf32.reshape(1, bsz, seq, *hshape), vf32.reshape(1, bsz, seq, *hshape),
            lf[:, :N_HEADS].reshape(1, bsz, seq, N_HEADS),
            kb32.reshape(1, bsz, BAND_REACH, *hshape), vb32.reshape(1, bsz, BAND_REACH, *hshape),
            kf32_s.reshape(1, dbs, dseq, *hshape), vf32_s.reshape(1, dbs, dseq, *hshape),
            lf_s[:, :N_HEADS].reshape(1, dbs, dseq, N_HEADS),
            new_bk_s[None], new_bv_s[None])
```

```python
import functools

import jax
import jax.numpy as jnp
from jax import lax
from jax.experimental import pallas as pl
from jax.experimental.pallas import tpu as pltpu

F32 = jnp.float32
BF16 = jnp.bfloat16

HEAD_DIM = 64
N_HEADS = 8
WIDTH = N_HEADS * HEAD_DIM
PAIR = 2 * HEAD_DIM
N_PAIRS = N_HEADS // 2
LANES = 128
CHUNK = 64
BAND_REACH = 512
REL_CLIP = 256
N_EXPERTS = 64
N_GROUPS = 8
GROUP_SIZE = N_EXPERTS // N_GROUPS
TOPK_GROUPS = 4
TOP_K = 8
ROUTED_SCALE = 2.5
EPS = 1e-6
NEG_INF = -1e30
ATTN_SCALE = HEAD_DIM ** -0.5
LOG2E = 1.4426950408889634
VMEM_LIMIT = 56 * 1024 * 1024


def _cparams(n_axes):
    return pltpu.CompilerParams(dimension_semantics=("arbitrary",) * n_axes,
                                vmem_limit_bytes=VMEM_LIMIT)


def _dot(a, b):
    return jnp.dot(a, b, preferred_element_type=F32)


def _dot_nt(a, b):
    return lax.dot_general(a, b, (((1,), (1,)), ((), ())), preferred_element_type=F32)


def _split2(a):
    hi = a.astype(BF16)
    lo = (a - hi.astype(F32)).astype(BF16)
    return hi, lo


def _split3(a):
    hi = a.astype(BF16)
    r = a - hi.astype(F32)
    mid = r.astype(BF16)
    lo = (r - mid.astype(F32)).astype(BF16)
    return hi, mid, lo


def _ada_body(c_ref, w_ref, b_ref, o_ref):
    c = c_ref[...]
    a = c * jax.nn.sigmoid(c)
    a_hi, a_lo = _split2(a)
    w_hi, w_lo = _split2(w_ref[...])
    o_ref[...] = _dot(a_hi, w_hi) + _dot(a_hi, w_lo) + _dot(a_lo, w_hi) + b_ref[...]


def _ada(c_all, w_ada, b_ada):
    rows, d = c_all.shape
    n = w_ada.shape[1]
    tn = 1024
    return pl.pallas_call(
        _ada_body,
        grid=(n // tn,),
        in_specs=[pl.BlockSpec((rows, d), lambda j: (0, 0)),
                  pl.BlockSpec((d, tn), lambda j: (0, j)),
                  pl.BlockSpec((1, tn), lambda j: (0, j))],
        out_specs=pl.BlockSpec((rows, tn), lambda j: (0, j)),
        out_shape=jax.ShapeDtypeStruct((rows, n), F32),
        compiler_params=_cparams(1),
        name="ada",
    )(c_all, w_ada, b_ada)


def _log_sigmoid(z):
    return jnp.minimum(z, 0.0) - jnp.log(1.0 + jnp.exp(-jnp.abs(z)))


def _proj_body(x_ref, sh_ref, sc_ref, g1_ref, w_ref, bd_ref, gqf_ref, gkf_ref, gqb_ref, gkb_ref, bf_ref, wvt_ref,
               qf_ref, kf_ref, vf_ref, kf32_ref, vf32_ref, lf_ref, qb_ref, kb_ref, vb_ref, kb32_ref, vb32_ref,
               *, band_last_only):
    x = x_ref[...]
    g, r, d = x.shape
    ms = jnp.mean(x * x, axis=-1, keepdims=True)
    h = x * lax.rsqrt(ms + EPS) * g1_ref[...] * (1.0 + sc_ref[...]) + sh_ref[...]
    hb = h.reshape(g * r, d).astype(BF16)

    def seg(i):
        return _dot(hb, w_ref[:, i * WIDTH:(i + 1) * WIDTH])

    def head_norm(t, gain_ref):
        ssq = _dot((t * t).astype(BF16), bd_ref[...])
        return t * lax.rsqrt(ssq + EPS) * gain_ref[...]

    qf_ref[...] = head_norm(seg(0), gqf_ref).astype(BF16)
    kf = head_norm(seg(1), gkf_ref)
    kf32_ref[...] = kf
    kf_ref[...] = kf.astype(BF16)
    vf = seg(2)
    vf32_ref[...] = vf
    if band_last_only:
        vf_ref[...] = _dot_nt(wvt_ref[...], hb).astype(BF16)
    else:
        vf_ref[...] = vf.astype(BF16)
    z = _dot(hb, w_ref[:, 6 * WIDTH:6 * WIDTH + LANES]) + bf_ref[...]
    lf_ref[...] = _log_sigmoid(z)
    qb_ref[...] = head_norm(seg(3), gqb_ref).astype(BF16)
    kb = head_norm(seg(4), gkb_ref)
    kb_ref[...] = kb.astype(BF16)
    vb = seg(5)
    vb_ref[...] = vb.astype(BF16)

    if band_last_only:
        @pl.when(pl.program_id(1) == pl.num_programs(1) - 1)
        def _():
            kb32_ref[...] = kb
            vb32_ref[...] = vb
    else:
        kb32_ref[...] = kb
        vb32_ref[...] = vb


def _proj(x, shift, scale, g1, w_all, bd, gqf, gkf, gqb, gkb, bf_row, wvt, *, G, R, band_last_only):
    nb, s, d = x.shape
    n = nb * s
    tm = G * R
    nbi, nsi = nb // G, s // R
    grid = (nbi, nsi)
    row = lambda b, i: (b * nsi + i, 0)
    const = lambda b, i: (0, 0)
    mod_spec = pl.BlockSpec((G, 1, d), lambda b, i: (b, 0, 0))
    out_bf = jax.ShapeDtypeStruct((n, WIDTH), BF16)
    out_f32 = jax.ShapeDtypeStruct((n, WIDTH), F32)
    tile = pl.BlockSpec((tm, WIDTH), row)
    if band_last_only:
        assert G == 1 and R == BAND_REACH
        band_shape = jax.ShapeDtypeStruct((nb, BAND_REACH, WIDTH), F32)
        band_spec = pl.BlockSpec((None, BAND_REACH, WIDTH), lambda b, i: (b, 0, 0))
        v_shape = jax.ShapeDtypeStruct((nb, nsi, WIDTH, tm), BF16)
        v_spec = pl.BlockSpec((None, None, WIDTH, tm), lambda b, i: (b, i, 0, 0))
    else:
        band_shape, band_spec = out_f32, tile
        v_shape, v_spec = out_bf, tile
    return pl.pallas_call(
        functools.partial(_proj_body, band_last_only=band_last_only),
        grid=grid,
        in_specs=[pl.BlockSpec((G, R, d), lambda b, i: (b, i, 0)), mod_spec, mod_spec,
                  pl.BlockSpec((1, d), const), pl.BlockSpec(w_all.shape, const), pl.BlockSpec(bd.shape, const),
                  pl.BlockSpec((1, WIDTH), const), pl.BlockSpec((1, WIDTH), const),
                  pl.BlockSpec((1, WIDTH), const), pl.BlockSpec((1, WIDTH), const),
                  pl.BlockSpec((1, LANES), const), pl.BlockSpec(wvt.shape, const)],
        out_specs=[tile, tile, v_spec, tile, tile, pl.BlockSpec((tm, LANES), row), tile, tile, tile,
                   band_spec, band_spec],
        out_shape=[out_bf, out_bf, v_shape, out_f32, out_f32, jax.ShapeDtypeStruct((n, LANES), F32),
                   out_bf, out_bf, out_bf, band_shape, band_shape],
        compiler_params=_cparams(2),
        name="proj",
    )(x, shift, scale, g1, w_all, bd, gqf, gkf, gqb, gkb, bf_row, wvt)


def _scan_body(lf_ref, cum_ref, cumt_ref, carry_ref):
    @pl.when(pl.program_id(1) == 0)
    def _():
        carry_ref[...] = jnp.zeros_like(carry_ref)

    lf = lf_ref[...]
    ts = lf.shape[0]
    lane = lax.broadcasted_iota(jnp.int32, lf.shape, 1)
    lf = jnp.where(lane < N_HEADS, lf, 0.0)
    hi, mid, lo = _split3(lf)
    rr = lax.broadcasted_iota(jnp.int32, (ts, ts), 0)
    cc = lax.broadcasted_iota(jnp.int32, (ts, ts), 1)
    tri = jnp.where(cc <= rr, 1.0, 0.0).astype(BF16)
    cum = _dot(tri, hi) + _dot(tri, mid) + _dot(tri, lo) + carry_ref[0:1, :]
    carry_ref[...] = jnp.broadcast_to(cum[ts - 1:ts, :], carry_ref.shape)
    cum2 = cum * LOG2E
    cum_ref[...] = cum2
    cumt_ref[...] = cum2.T[0:N_HEADS, :]


def _scan(lf, *, TS, TK):
    b, s, _ = lf.shape
    assert TS == TK
    t_block = (None, None, N_HEADS, TK)
    t_map = lambda bi, i: (bi, i, 0, 0)
    return pl.pallas_call(
        _scan_body,
        grid=(b, s // TS),
        in_specs=[pl.BlockSpec((None, TS, LANES), lambda bi, i: (bi, i, 0))],
        out_specs=[pl.BlockSpec((None, TS, LANES), lambda bi, i: (bi, i, 0)),
                   pl.BlockSpec(t_block, t_map)],
        out_shape=[jax.ShapeDtypeStruct((b, s, LANES), F32),
                   jax.ShapeDtypeStruct((b, s // TK, N_HEADS, TK), F32)],
        scratch_shapes=[pltpu.VMEM((8, LANES), F32)],
        compiler_params=_cparams(2),
        name="scan",
    )(lf)


def _fox_body(q_ref, k_ref, v_ref, cq_ref, ck_ref, o_ref, *, TQ, TK, q_off):
    p = pl.program_id(1)
    i = pl.program_id(2)
    q = q_ref[...]
    cq_blk = cq_ref[...]
    lane = lax.broadcasted_iota(jnp.int32, (TQ, PAIR), 1)
    ones_blk = jnp.where(lax.broadcasted_iota(jnp.int32, (TK, LANES), 1) == 0, 1.0, 0.0).astype(BF16)
    q0 = q_off + i * TQ
    n_full = q0 // TK
    qpos = q0 + lax.broadcasted_iota(jnp.int32, (TQ, TK), 0)
    kcol = lax.broadcasted_iota(jnp.int32, (TQ, TK), 1)

    outs = []
    for par in range(2):
        h = 2 * p + par
        qm = jnp.where((lane >= HEAD_DIM) == (par == 1), q, jnp.zeros_like(q))
        cq_col = jnp.sum(jnp.where(lane == h, cq_blk, 0.0), axis=1, keepdims=True)
        ref0 = cq_col[0:1, :]
        cqr = cq_col - ref0

        def step(j, carry, masked):
            m, l, acc = carry
            k0 = pl.multiple_of(j * TK, TK)
            kb = k_ref[pl.ds(k0, TK), :]
            vb = v_ref[pl.ds(k0, TK), :]
            s = _dot_nt(qm, kb)
            ck = ck_ref[j, pl.ds(h, 1), :]
            u = s - (ck - ref0)
            if masked:
                u = jnp.where(k0 + kcol <= qpos, u, NEG_INF)
            m_new = jnp.maximum(m, jnp.max(u, axis=1, keepdims=True) + cqr)
            alpha = jnp.exp2(m - m_new)
            pexp = jnp.exp2(u + (cqr - m_new))
            pv = _dot(pexp.astype(BF16), jnp.concatenate([vb, ones_blk], axis=1))
            return m_new, alpha * l + pv[:, LANES:LANES + 1], alpha * acc + pv[:, :LANES]

        init = (jnp.full((TQ, 1), NEG_INF, F32), jnp.zeros((TQ, 1), F32), jnp.zeros((TQ, LANES), F32))
        carry = lax.fori_loop(0, n_full, lambda j, c: step(j, c, False), init)
        _, l, acc = step(n_full, carry, True)
        outs.append(acc / l)
    o_ref[...] = jnp.where(lane < HEAD_DIM, outs[0], outs[1]).astype(o_ref.dtype)


def _fox(q, k, v, cum, cumt, *, TQ, TK, q_off):
    b, sq, _ = q.shape
    sk = k.shape[1]
    return pl.pallas_call(
        functools.partial(_fox_body, TQ=TQ, TK=TK, q_off=q_off),
        grid=(b, N_PAIRS, sq // TQ),
        in_specs=[pl.BlockSpec((None, TQ, PAIR), lambda bi, p, i: (bi, i, p)),
                  pl.BlockSpec((None, sk, PAIR), lambda bi, p, i: (bi, 0, p)),
                  pl.BlockSpec((None, sk, PAIR), lambda bi, p, i: (bi, 0, p)),
                  pl.BlockSpec((None, TQ, LANES), lambda bi, p, i: (bi, i, 0)),
                  pl.BlockSpec((None, sk // TK, N_HEADS, TK), lambda bi, p, i: (bi, 0, 0, 0))],
        out_specs=pl.BlockSpec((None, TQ, PAIR), lambda bi, p, i: (bi, i, p)),
        out_shape=jax.ShapeDtypeStruct((b, sq, WIDTH), BF16),
        compiler_params=_cparams(3),
        name="fox",
    )(q, k, v, cum, cumt)


AUG_PIECES = 3


def _scan_t_body(lf_ref, place_ref, ct_ref, ka_ref, carry_ref):
    @pl.when(pl.program_id(1) == 0)
    def _():
        carry_ref[...] = jnp.zeros_like(carry_ref)

    lf = lf_ref[...]
    ts = lf.shape[0]
    lane = lax.broadcasted_iota(jnp.int32, lf.shape, 1)
    lf = jnp.where(lane < N_HEADS, lf, 0.0)
    hi, mid, lo = _split3(lf)
    rr = lax.broadcasted_iota(jnp.int32, (ts, ts), 0)
    cc = lax.broadcasted_iota(jnp.int32, (ts, ts), 1)
    tri = jnp.where(cc <= rr, 1.0, 0.0).astype(BF16)
    cum = _dot(tri, hi) + _dot(tri, mid) + _dot(tri, lo) + carry_ref[0:1, :]
    carry_ref[...] = jnp.broadcast_to(cum[ts - 1:ts, :], carry_ref.shape)
    cum2 = cum * LOG2E
    ct_ref[...] = cum2.T[0:N_HEADS, :]
    pieces = _split3(cum2 - cum2[0:1, :])
    ka = _dot(pieces[0], place_ref[0]) + _dot(pieces[1], place_ref[1]) + _dot(pieces[2], place_ref[2])
    ka_ref[...] = ka.astype(BF16)


def _aug_placement():
    h = jnp.arange(LANES)[:, None]
    col = jnp.arange(WIDTH)[None, :]
    mats = []
    for x in range(AUG_PIECES):
        tgt = PAIR * (h // 2) + AUG_PIECES * (h % 2) + x
        mats.append(jnp.where((h < N_HEADS) & (col == tgt), 1.0, 0.0))
    return jnp.stack(mats).astype(BF16)


def _scan_t(lf, *, T):
    b, s, _ = lf.shape
    place = _aug_placement()
    return pl.pallas_call(
        _scan_t_body,
        grid=(b, s // T),
        in_specs=[pl.BlockSpec((None, T, LANES), lambda bi, i: (bi, i, 0)),
                  pl.BlockSpec(place.shape, lambda bi, i: (0, 0, 0))],
        out_specs=[pl.BlockSpec((None, None, N_HEADS, T), lambda bi, i: (bi, i, 0, 0)),
                   pl.BlockSpec((None, T, WIDTH), lambda bi, i: (bi, i, 0))],
        out_shape=[jax.ShapeDtypeStruct((b, s // T, N_HEADS, T), F32),
                   jax.ShapeDtypeStruct((b, s, WIDTH), BF16)],
        scratch_shapes=[pltpu.VMEM((8, LANES), F32)],
        compiler_params=_cparams(2),
        name="scan_t",
    )(lf, place)


def _foxt_body(q_ref, k_ref, ka_ref, vt_ref, ct_ref, o_ref, *, T):
    p = pl.program_id(1)
    i = pl.program_id(2)
    q = q_ref[...]
    lane = lax.broadcasted_iota(jnp.int32, (T, PAIR), 1)
    halves = []
    for par in range(2):
        qm = jnp.where((lane >= HEAD_DIM) == (par == 1), q, jnp.zeros_like(q))
        lo_lane = AUG_PIECES * par
        qa = jnp.where((lane >= lo_lane) & (lane < lo_lane + AUG_PIECES), -1.0, 0.0).astype(BF16)
        halves.append(jnp.concatenate([qm, qa], axis=1))
    qcat = jnp.concatenate(halves, axis=0)
    h_even = 2 * p
    cq = jnp.concatenate([ct_ref[i, pl.ds(h_even, 1), :], ct_ref[i, pl.ds(h_even + 1, 1), :]], axis=1)
    ones_rows = jnp.ones((16, T), BF16)
    krow = lax.broadcasted_iota(jnp.int32, (T, 2 * T), 0)
    qcol = lax.broadcasted_iota(jnp.int32, (T, 2 * T), 1) % T

    def step(j, carry, masked):
        m, acc_e, acc_o = carry
        k0 = pl.multiple_of(j * T, T)
        kcat = jnp.concatenate([k_ref[pl.ds(k0, T), :], ka_ref[pl.ds(k0, T), :]], axis=1)
        st = _dot_nt(kcat, qcat)
        c0 = jnp.concatenate([jnp.broadcast_to(ct_ref[j, pl.ds(h_even, 1), :][:, 0:1], (1, T)),
                              jnp.broadcast_to(ct_ref[j, pl.ds(h_even + 1, 1), :][:, 0:1], (1, T))], axis=1)
        rb = cq - c0
        if masked:
            st = jnp.where(krow <= qcol, st, NEG_INF)
        m_new = jnp.maximum(m, jnp.max(st, axis=0, keepdims=True) + rb)
        alpha = jnp.exp2(m - m_new)
        pt = jnp.exp2(st + (rb - m_new)).astype(BF16)
        vt = vt_ref[j]
        pv_e = _dot(jnp.concatenate([vt[0:HEAD_DIM], ones_rows], axis=0), pt[:, 0:T])
        pv_o = _dot(jnp.concatenate([vt[HEAD_DIM:PAIR], ones_rows], axis=0), pt[:, T:2 * T])
        return m_new, alpha[:, 0:T] * acc_e + pv_e, alpha[:, T:2 * T] * acc_o + pv_o

    rows = HEAD_DIM + 16
    init = (jnp.full((1, 2 * T), NEG_INF, F32), jnp.zeros((rows, T), F32), jnp.zeros((rows, T), F32))
    n_pairs = i // 2
    carry = lax.fori_loop(0, n_pairs, lambda t, c: step(2 * t + 1, step(2 * t, c, False), False), init)
    carry = lax.fori_loop(2 * n_pairs, i, lambda j, c: step(j, c, False), carry)
    _, acc_e, acc_o = step(i, carry, True)
    o_t = jnp.concatenate([acc_e[0:HEAD_DIM] / acc_e[HEAD_DIM:HEAD_DIM + 1],
                           acc_o[0:HEAD_DIM] / acc_o[HEAD_DIM:HEAD_DIM + 1]], axis=0)
    o_ref[...] = o_t.T.astype(o_ref.dtype)


def _foxt(q, k, ka, vt, ct, *, T):
    b, s, _ = q.shape
    nt = s // T
    return pl.pallas_call(
        functools.partial(_foxt_body, T=T),
        grid=(b, N_PAIRS, nt),
        in_specs=[pl.BlockSpec((None, T, PAIR), lambda bi, p, i: (bi, i, p)),
                  pl.BlockSpec((None, s, PAIR), lambda bi, p, i: (bi, 0, p)),
                  pl.BlockSpec((None, s, PAIR), lambda bi, p, i: (bi, 0, p)),
                  pl.BlockSpec((None, nt, PAIR, T), lambda bi, p, i: (bi, 0, p, 0)),
                  pl.BlockSpec((None, nt, N_HEADS, T), lambda bi, p, i: (bi, 0, 0, 0))],
        out_specs=pl.BlockSpec((None, T, PAIR), lambda bi, p, i: (bi, i, p)),
        out_shape=jax.ShapeDtypeStruct((b, s, WIDTH), BF16),
        compiler_params=_cparams(3),
        name="foxt",
    )(q, k, ka, vt, ct)


def _band_body(q_ref, k_ref, v_ref, bias_ref, o_ref, *scratch, TQ, W, n_sub, padded):
    p = pl.program_id(1)
    i = pl.program_id(2)
    if padded:
        kpad_ref, vpad_ref = scratch
        s_len = k_ref.shape[0]

        @pl.when(i == 0)
        def _():
            zeros = jnp.zeros((BAND_REACH, PAIR), BF16)
            kpad_ref[pl.ds(0, BAND_REACH), :] = zeros
            vpad_ref[pl.ds(0, BAND_REACH), :] = zeros
            kpad_ref[pl.ds(BAND_REACH, s_len), :] = k_ref[...]
            vpad_ref[pl.ds(BAND_REACH, s_len), :] = v_ref[...]
    else:
        kpad_ref, vpad_ref = k_ref, v_ref

    lane = lax.broadcasted_iota(jnp.int32, (TQ, PAIR), 1)
    ones_blk = jnp.where(lax.broadcasted_iota(jnp.int32, (W, LANES), 1) == 0, 1.0, 0.0).astype(BF16)
    kcol = lax.broadcasted_iota(jnp.int32, (TQ, W), 1)

    def sub_block(sub, carry):
        r0 = pl.multiple_of(sub * TQ, TQ)
        q0 = i * (n_sub * TQ) + r0
        q = q_ref[pl.ds(r0, TQ), :]
        kw = kpad_ref[pl.ds(pl.multiple_of(q0, TQ), W), :] if padded else kpad_ref[...]
        vw = vpad_ref[pl.ds(pl.multiple_of(q0, TQ), W), :] if padded else vpad_ref[...]
        vcat = jnp.concatenate([vw, ones_blk], axis=1)
        outs = []
        for par in range(2):
            h = 2 * p + par
            qm = jnp.where((lane >= HEAD_DIM) == (par == 1), q, jnp.zeros_like(q))
            s = _dot_nt(qm, kw) + bias_ref[h]
            if padded:
                s = jnp.where(kcol >= BAND_REACH - q0, s, NEG_INF)
            m = jnp.max(s, axis=1, keepdims=True)
            pexp = jnp.exp2(s - m)
            pv = _dot(pexp.astype(BF16), vcat)
            outs.append(pv[:, :LANES] / pv[:, LANES:LANES + 1])
        o_ref[pl.ds(r0, TQ), :] = jnp.where(lane < HEAD_DIM, outs[0], outs[1]).astype(o_ref.dtype)
        return carry

    lax.fori_loop(0, n_sub, sub_block, 0)


def _band(q, k, v, bias, *, TQ, n_sub, padded):
    b, sq, _ = q.shape
    sk = k.shape[1]
    w = BAND_REACH + TQ if padded else sk
    tqb = TQ * n_sub
    scratch = [pltpu.VMEM((sk + BAND_REACH, PAIR), BF16)] * 2 if padded else []
    return pl.pallas_call(
        functools.partial(_band_body, TQ=TQ, W=w, n_sub=n_sub, padded=padded),
        grid=(b, N_PAIRS, sq // tqb),
        in_specs=[pl.BlockSpec((None, tqb, PAIR), lambda bi, p, i: (bi, i, p)),
                  pl.BlockSpec((None, sk, PAIR), lambda bi, p, i: (bi, 0, p)),
                  pl.BlockSpec((None, sk, PAIR), lambda bi, p, i: (bi, 0, p)),
                  pl.BlockSpec(bias.shape, lambda bi, p, i: (0, 0, 0))],
        out_specs=pl.BlockSpec((None, tqb, PAIR), lambda bi, p, i: (bi, i, p)),
        out_shape=jax.ShapeDtypeStruct((b, sq, WIDTH), BF16),
        scratch_shapes=scratch,
        compiler_params=_cparams(3),
        name="band",
    )(q, k, v, bias)


def _band_bias_tile(rel_bias, tq, w):
    r = jnp.arange(tq)[:, None]
    c = jnp.arange(w)[None, :]
    dist = r + BAND_REACH - c
    in_band = (c // CHUNK >= r // CHUNK) & (c // CHUNK <= r // CHUNK + BAND_REACH // CHUNK)
    vals = rel_bias[:, jnp.clip(dist, -REL_CLIP, REL_CLIP) + REL_CLIP] * LOG2E
    return jnp.where(in_band[None], vals, NEG_INF).astype(F32)


def _first_index(is_max, idx, axis, big):
    return jnp.min(jnp.where(is_max, idx, big), axis=axis, keepdims=True)


def _route(scores, choice):
    t = scores.shape[1]
    c3 = choice.reshape(N_GROUPS, GROUP_SIZE, t)
    j_idx = lax.broadcasted_iota(jnp.int32, c3.shape, 1)
    top1 = jnp.max(c3, axis=1, keepdims=True)
    first = _first_index(c3 == top1, j_idx, 1, GROUP_SIZE)
    top2 = jnp.max(jnp.where(j_idx == first, -jnp.inf, c3), axis=1, keepdims=True)
    gscore = (top1 + top2).reshape(N_GROUPS, t)

    g_idx = lax.broadcasted_iota(jnp.int32, gscore.shape, 0)
    gsel = jnp.zeros(gscore.shape, F32)
    work = gscore
    for _ in range(TOPK_GROUPS):
        gm = jnp.max(work, axis=0, keepdims=True)
        pick = g_idx == _first_index(work == gm, g_idx, 0, N_GROUPS)
        gsel = jnp.where(pick, 1.0, gsel)
        work = jnp.where(pick, -jnp.inf, work)

    emask = jnp.broadcast_to(gsel.reshape(N_GROUPS, 1, t), c3.shape) > 0.0
    work = jnp.where(emask, c3, NEG_INF)
    e_idx = lax.broadcasted_iota(jnp.int32, c3.shape, 0) * GROUP_SIZE + j_idx
    esel = jnp.zeros(c3.shape, F32)
    for _ in range(TOP_K):
        em = jnp.max(jnp.max(work, axis=1, keepdims=True), axis=0, keepdims=True)
        cand = jnp.where(work == em, e_idx, N_EXPERTS)
        first = jnp.min(jnp.min(cand, axis=1, keepdims=True), axis=0, keepdims=True)
        pick = e_idx == first
        esel = jnp.where(pick, 1.0, esel)
        work = jnp.where(pick, -jnp.inf, work)

    w = esel * scores.reshape(c3.shape)
    denom = jnp.sum(jnp.sum(w, axis=1, keepdims=True), axis=0, keepdims=True)
    return (w / denom * ROUTED_SCALE).reshape(N_EXPERTS, t)


def _merge_body(of_ref, ob_ref, x_ref, gate_ref, sh_ref, sc_ref, ogf_ref, ogb_ref, wo_ref, g2_ref,
                wrh_ref, wrl_ref, br_ref, x1_ref, h2_ref, comb_ref):
    def group_norm(t_ref, gain_ref):
        t = t_ref[...].astype(F32)
        ms = jnp.mean(t * t, axis=-1, keepdims=True)
        return (t * lax.rsqrt(ms + EPS) * gain_ref[...]).astype(BF16)

    y = _dot(group_norm(of_ref, ogf_ref), wo_ref[0:WIDTH, :]) + _dot(group_norm(ob_ref, ogb_ref), wo_ref[WIDTH:, :])
    x = x_ref[...]
    g, r, d = x.shape
    x1 = x + gate_ref[...] * y.reshape(g, r, d)
    x1_ref[...] = x1
    ms = jnp.mean(x1 * x1, axis=-1, keepdims=True)
    h2 = (x1 * lax.rsqrt(ms + EPS) * g2_ref[...] * (1.0 + sc_ref[...]) + sh_ref[...]).reshape(g * r, d)
    h_hi, h_lo = _split2(h2)
    h2_ref[...] = h_hi
    logits = _dot_nt(wrh_ref[...], h_hi) + _dot_nt(wrh_ref[...], h_lo) + _dot_nt(wrl_ref[...], h_hi)
    scores = jax.nn.sigmoid(logits)
    t = scores.shape[1]
    bias = jnp.concatenate([br_ref[...]] * (t // LANES), axis=1)
    comb = _route(scores, scores + bias)
    comb_pad = jnp.concatenate([comb, jnp.zeros((LANES - N_EXPERTS, t), F32)], axis=0)
    comb_ref[...] = comb_pad.T


def _merge(of, ob, x, gate, shift, scale, ogf, ogb, wo, g2, wr_hi, wr_lo, br, *, G, R):
    nb, s, d = x.shape
    n = nb * s
    tm = G * R
    nbi, nsi = nb // G, s // R
    row = lambda b, i: (b * nsi + i, 0)
    const = lambda b, i: (0, 0)
    mod_spec = pl.BlockSpec((G, 1, d), lambda b, i: (b, 0, 0))
    x_spec = pl.BlockSpec((G, R, d), lambda b, i: (b, i, 0))
    return pl.pallas_call(
        _merge_body,
        grid=(nbi, nsi),
        in_specs=[pl.BlockSpec((tm, WIDTH), row), pl.BlockSpec((tm, WIDTH), row), x_spec,
                  mod_spec, mod_spec, mod_spec,
                  pl.BlockSpec((1, WIDTH), const), pl.BlockSpec((1, WIDTH), const),
                  pl.BlockSpec(wo.shape, const), pl.BlockSpec((1, d), const),
                  pl.BlockSpec(wr_hi.shape, const), pl.BlockSpec(wr_lo.shape, const),
                  pl.BlockSpec(br.shape, const)],
        out_specs=[x_spec, pl.BlockSpec((tm, d), row), pl.BlockSpec((tm, LANES), row)],
        out_shape=[jax.ShapeDtypeStruct((nb, s, d), F32), jax.ShapeDtypeStruct((n, d), BF16),
                   jax.ShapeDtypeStruct((n, LANES), F32)],
        compiler_params=_cparams(2),
        name="merge",
    )(of, ob, x, gate, shift, scale, ogf, ogb, wo, g2, wr_hi, wr_lo, br)


def _silu(g):
    return g * jax.nn.sigmoid(g)


def _moe_body(h_ref, comb_ref, x1_ref, gate_ref, wg_ref, wu_ref, wd_ref, sg_ref, su_ref, sd_ref, y_ref, acc_ref):
    e = pl.program_id(2)
    hb = h_ref[...]

    @pl.when(e == 0)
    def _():
        a = _silu(_dot(hb, sg_ref[...])) * _dot(hb, su_ref[...])
        acc_ref[...] = _dot(a.astype(BF16), sd_ref[...])

    comb = comb_ref[...]
    lane = lax.broadcasted_iota(jnp.int32, comb.shape, 1)
    c_e = jnp.sum(jnp.where(lane == e, comb, 0.0), axis=1, keepdims=True)
    a = _silu(_dot(hb, wg_ref[...])) * _dot(hb, wu_ref[...]) * c_e
    acc_ref[...] += _dot(a.astype(BF16), wd_ref[...])

    @pl.when(e == pl.num_programs(2) - 1)
    def _():
        x1 = x1_ref[...]
        g, r, d = x1.shape
        y_ref[...] = x1 + gate_ref[...] * acc_ref[...].reshape(g, r, d)


def _moe(h2, comb, x1, gate, wg, wu, wd, sg, su, sd, *, G, R):
    nb, s, d = x1.shape
    tm = G * R
    nbi, nsi = nb // G, s // R
    ff = wg.shape[2]
    row = lambda b, i, e: (b * nsi + i, 0)
    const = lambda b, i, e: (0, 0)
    x_spec = pl.BlockSpec((G, R, d), lambda b, i, e: (b, i, 0))
    return pl.pallas_call(
        _moe_body,
        grid=(nbi, nsi, N_EXPERTS),
        in_specs=[pl.BlockSpec((tm, d), row), pl.BlockSpec((tm, LANES), row), x_spec,
                  pl.BlockSpec((G, 1, d), lambda b, i, e: (b, 0, 0)),
                  pl.BlockSpec((None, d, ff), lambda b, i, e: (e, 0, 0)),
                  pl.BlockSpec((None, d, ff), lambda b, i, e: (e, 0, 0)),
                  pl.BlockSpec((None, ff, d), lambda b, i, e: (e, 0, 0)),
                  pl.BlockSpec(sg.shape, const), pl.BlockSpec(su.shape, const), pl.BlockSpec(sd.shape, const)],
        out_specs=x_spec,
        out_shape=jax.ShapeDtypeStruct((nb, s, d), F32),
        scratch_shapes=[pltpu.VMEM((tm, d), F32)],
        compiler_params=_cparams(3),
        name="moe",
    )(h2, comb, x1, gate, wg, wu, wd, sg, su, sd)


def _tile_heads(g, mult=1.0):
    return (jnp.tile(g.astype(F32), N_HEADS) * mult).reshape(1, WIDTH)


def kernel(x_prompt, x_sample, cache_fox_k, cache_fox_v, cache_fox_logf, cache_band_k, cache_band_v, c_prompt, c_sample, w_ada, b_ada, norm1_g, norm2_g, w_in, b_forget, g_q_fox, g_k_fox, g_q_band, g_k_band, rel_bias, out_g_fox, out_g_band, w_out, w_router, b_router, w_gate, w_up, w_down, ws_gate, ws_up, ws_down):
    depth = w_ada.shape[0]
    assert depth == 1
    bsz, seq, d = x_prompt.shape
    dbs, dseq, _ = x_sample.shape
    past = cache_fox_k.shape[2]
    n_cache = cache_band_k.shape[2]
    assert n_cache == BAND_REACH and dseq == CHUNK and seq % BAND_REACH == 0

    wi = w_in[0]
    cols = [wi[:, 0:512], wi[:, 512:1024], wi[:, 1024:1536], wi[:, 1544:2056], wi[:, 2056:2568], wi[:, 2568:3080],
            wi[:, 1536:1544], jnp.zeros((d, LANES - N_HEADS), F32)]
    w_all = jnp.concatenate(cols, axis=1).astype(BF16)
    hd = jnp.arange(WIDTH) // HEAD_DIM
    bd = jnp.where(hd[:, None] == hd[None, :], 1.0 / HEAD_DIM, 0.0).astype(BF16)
    qscale = ATTN_SCALE * LOG2E
    gqf, gkf = _tile_heads(g_q_fox[0], qscale), _tile_heads(g_k_fox[0])
    gqb, gkb = _tile_heads(g_q_band[0], qscale), _tile_heads(g_k_band[0])
    bf_row = jnp.concatenate([b_forget[0], jnp.zeros((LANES - N_HEADS,), F32)]).reshape(1, LANES)
    g1 = norm1_g[0].reshape(1, d)
    g2 = norm2_g[0].reshape(1, d)
    ogf = out_g_fox[0].reshape(1, WIDTH)
    ogb = out_g_band[0].reshape(1, WIDTH)
    wo = w_out[0].astype(BF16)
    wr_t = w_router[0].T
    wr_hi = wr_t.astype(BF16)
    wr_lo = (wr_t - wr_hi.astype(F32)).astype(BF16)
    br = jnp.broadcast_to(b_router[0].reshape(N_EXPERTS, 1), (N_EXPERTS, LANES)).astype(F32)
    wg, wu, wd = w_gate[0].astype(BF16), w_up[0].astype(BF16), w_down[0].astype(BF16)
    sg, su, sd = ws_gate[0].astype(BF16), ws_up[0].astype(BF16), ws_down[0].astype(BF16)
    wvt = wi[:, 2 * WIDTH:3 * WIDTH].T.astype(BF16)

    n_c = bsz + dbs
    rows = -(-n_c // 8) * 8
    c_all = jnp.concatenate([c_prompt, c_sample, jnp.zeros((rows - n_c, d), F32)], axis=0)
    mod = _ada(c_all, w_ada[0], b_ada[0].reshape(1, -1))

    def mods(lo, hi):
        return [mod[lo:hi, j * d:(j + 1) * d].reshape(hi - lo, 1, d) for j in range(6)]

    shift1_p, scale1_p, gate1_p, shift2_p, scale2_p, gate2_p = mods(0, bsz)
    shift1_s, scale1_s, gate1_s, shift2_s, scale2_s, gate2_s = mods(bsz, n_c)

    TM = BAND_REACH
    (qf, kf, vft, kf32, vf32, lf, qb, kb, vb, kb32, vb32) = _proj(
        x_prompt, shift1_p, scale1_p, g1, w_all, bd, gqf, gkf, gqb, gkb, bf_row, wvt, G=1, R=TM, band_last_only=True)
    r3 = lambda a: a.reshape(bsz, seq, a.shape[-1])
    ct, ka = _scan_t(r3(lf), T=TM)
    of = _foxt(r3(qf), r3(kf), ka, vft, ct, T=TM)
    band_tq = 128
    bias_p = _band_bias_tile(rel_bias[0], band_tq, BAND_REACH + band_tq)
    n_sub = max(dv for dv in range(1, 17) if (seq // band_tq) % dv == 0)
    ob = _band(r3(qb), r3(kb), r3(vb), bias_p, TQ=band_tq, n_sub=n_sub, padded=True)
    x1_p, h2_p, comb_p = _merge(of.reshape(-1, WIDTH), ob.reshape(-1, WIDTH), x_prompt, gate1_p, shift2_p, scale2_p,
                                ogf, ogb, wo, g2, wr_hi, wr_lo, br, G=1, R=TM)
    moe_r = 1024 if seq % 1024 == 0 else TM
    y_p = _moe(h2_p, comb_p, x1_p, gate2_p, wg, wu, wd, sg, su, sd, G=1, R=moe_r)

    GS = 8
    (qf_s, kf_s, vf_s, kf32_s, vf32_s, lf_s, qb_s, kb_s, vb_s, kb32_s, vb32_s) = _proj(
        x_sample, shift1_s, scale1_s, g1, w_all, bd, gqf, gkf, gqb, gkb, bf_row, wvt, G=GS, R=dseq,
        band_last_only=False)
    s3 = lambda a: a.reshape(dbs, dseq, a.shape[-1])
    sk = past + dseq
    skp = -(-sk // LANES) * LANES
    pad_k = skp - sk
    lf_cache = jnp.pad(cache_fox_logf[0], ((0, 0), (0, 0), (0, LANES - N_HEADS)))
    lf_all = jnp.concatenate([lf_cache, s3(lf_s), jnp.zeros((dbs, pad_k, LANES), F32)], axis=1)
    cum_s, cumt_s = _scan(lf_all, TS=skp, TK=skp)
    zpad = jnp.zeros((dbs, pad_k, WIDTH), BF16)
    k_all = jnp.concatenate([cache_fox_k[0].reshape(dbs, past, WIDTH).astype(BF16), s3(kf_s), zpad], axis=1)
    v_all = jnp.concatenate([cache_fox_v[0].reshape(dbs, past, WIDTH).astype(BF16), s3(vf_s), zpad], axis=1)
    of_s = _fox(s3(qf_s), k_all, v_all, cum_s[:, past:past + dseq], cumt_s, TQ=dseq, TK=skp, q_off=past)
    bias_s = _band_bias_tile(rel_bias[0], dseq, BAND_REACH + LANES)
    zb = jnp.zeros((dbs, LANES - dseq, WIDTH), BF16)
    kb_all = jnp.concatenate([cache_band_k[0].reshape(dbs, n_cache, WIDTH).astype(BF16), s3(kb_s), zb], axis=1)
    vb_all = jnp.concatenate([cache_band_v[0].reshape(dbs, n_cache, WIDTH).astype(BF16), s3(vb_s), zb], axis=1)
    ob_s = _band(s3(qb_s), kb_all, vb_all, bias_s, TQ=dseq, n_sub=1, padded=False)
    x1_s, h2_s, comb_s = _merge(of_s.reshape(-1, WIDTH), ob_s.reshape(-1, WIDTH), x_sample, gate1_s, shift2_s,
                                scale2_s, ogf, ogb, wo, g2, wr_hi, wr_lo, br, G=GS, R=dseq)
    y_s = _moe(h2_s, comb_s, x1_s, gate2_s, wg, wu, wd, sg, su, sd, G=dbs, R=dseq)

    hshape = (N_HEADS, HEAD_DIM)
    new_bk_s = jnp.concatenate([cache_band_k[0], s3(kb32_s).reshape(dbs, dseq, *hshape)], axis=1)[:, -n_cache:]
    new_bv_s = jnp.concatenate([cache_band_v[0], s3(vb32_s).reshape(dbs, dseq, *hshape)], axis=1)[:, -n_cache:]
    return (y_p, y_s,
            kf32.reshape(1, bsz, seq, *hshape), vf32.reshape(1, bsz, seq, *hshape),
            lf[:, :N_HEADS].reshape(1, bsz, seq, N_HEADS),
            kb32.reshape(1, bsz, BAND_REACH, *hshape), vb32.reshape(1, bsz, BAND_REACH, *hshape),
            kf32_s.reshape(1, dbs, dseq, *hshape), vf32_s.reshape(1, dbs, dseq, *hshape),
            lf_s[:, :N_HEADS].reshape(1, dbs, dseq, N_HEADS),
            new_bk_s[None], new_bv_s[None])
```

```python
import functools

import jax
import jax.numpy as jnp
from jax import lax
from jax.experimental import pallas as pl
from jax.experimental.pallas import tpu as pltpu

F32 = jnp.float32
BF16 = jnp.bfloat16

HEAD_DIM = 64
N_HEADS = 8
WIDTH = N_HEADS * HEAD_DIM
PAIR = 2 * HEAD_DIM
N_PAIRS = N_HEADS // 2
LANES = 128
CHUNK = 64
BAND_REACH = 512
REL_CLIP = 256
N_EXPERTS = 64
N_GROUPS = 8
GROUP_SIZE = N_EXPERTS // N_GROUPS
TOPK_GROUPS = 4
TOP_K = 8
ROUTED_SCALE = 2.5
EPS = 1e-6
NEG_INF = -1e30
ATTN_SCALE = HEAD_DIM ** -0.5
LOG2E = 1.4426950408889634
VMEM_LIMIT = 56 * 1024 * 1024


def _cparams(n_axes):
    return pltpu.CompilerParams(dimension_semantics=("arbitrary",) * n_axes,
                                vmem_limit_bytes=VMEM_LIMIT)


def _dot(a, b):
    return jnp.dot(a, b, preferred_element_type=F32)


def _dot_nt(a, b):
    return lax.dot_general(a, b, (((1,), (1,)), ((), ())), preferred_element_type=F32)


def _split2(a):
    hi = a.astype(BF16)
    lo = (a - hi.astype(F32)).astype(BF16)
    return hi, lo


def _split3(a):
    hi = a.astype(BF16)
    r = a - hi.astype(F32)
    mid = r.astype(BF16)
    lo = (r - mid.astype(F32)).astype(BF16)
    return hi, mid, lo


def _ada_body(c_ref, w_ref, b_ref, o_ref):
    c = c_ref[...]
    a = c * jax.nn.sigmoid(c)
    a_hi, a_lo = _split2(a)
    w_hi, w_lo = _split2(w_ref[...])
    o_ref[...] = _dot(a_hi, w_hi) + _dot(a_hi, w_lo) + _dot(a_lo, w_hi) + b_ref[...]


def _ada(c_all, w_ada, b_ada):
    rows, d = c_all.shape
    n = w_ada.shape[1]
    tn = 1024
    return pl.pallas_call(
        _ada_body,
        grid=(n // tn,),
        in_specs=[pl.BlockSpec((rows, d), lambda j: (0, 0)),
                  pl.BlockSpec((d, tn), lambda j: (0, j)),
                  pl.BlockSpec((1, tn), lambda j: (0, j))],
        out_specs=pl.BlockSpec((rows, tn), lambda j: (0, j)),
        out_shape=jax.ShapeDtypeStruct((rows, n), F32),
        compiler_params=_cparams(1),
        name="ada",
    )(c_all, w_ada, b_ada)


def _log_sigmoid(z):
    return jnp.minimum(z, 0.0) - jnp.log(1.0 + jnp.exp(-jnp.abs(z)))


def _proj_body(x_ref, sh_ref, sc_ref, g1_ref, w_ref, bd_ref, gqf_ref, gkf_ref, gqb_ref, gkb_ref, bf_ref, wvt_ref,
               qf_ref, kf_ref, vf_ref, kf32_ref, vf32_ref, lf_ref, qb_ref, kb_ref, vb_ref, kb32_ref, vb32_ref,
               *, band_last_only):
    x = x_ref[...]
    g, r, d = x.shape
    ms = jnp.mean(x * x, axis=-1, keepdims=True)
    h = x * lax.rsqrt(ms + EPS) * g1_ref[...] * (1.0 + sc_ref[...]) + sh_ref[...]
    hb = h.reshape(g * r, d).astype(BF16)

    def seg(i):
        return _dot(hb, w_ref[:, i * WIDTH:(i + 1) * WIDTH])

    def head_norm(t, gain_ref):
        ssq = _dot((t * t).astype(BF16), bd_ref[...])
        return t * lax.rsqrt(ssq + EPS) * gain_ref[...]

    qf_ref[...] = head_norm(seg(0), gqf_ref).astype(BF16)
    kf = head_norm(seg(1), gkf_ref)
    kf32_ref[...] = kf
    kf_ref[...] = kf.astype(BF16)
    vf = seg(2)
    vf32_ref[...] = vf
    if band_last_only:
        vf_ref[...] = _dot_nt(wvt_ref[...], hb).astype(BF16)
    else:
        vf_ref[...] = vf.astype(BF16)
    z = _dot(hb, w_ref[:, 6 * WIDTH:6 * WIDTH + LANES]) + bf_ref[...]
    lf_ref[...] = _log_sigmoid(z)
    qb_ref[...] = head_norm(seg(3), gqb_ref).astype(BF16)
    kb = head_norm(seg(4), gkb_ref)
    kb_ref[...] = kb.astype(BF16)
    vb = seg(5)
    vb_ref[...] = vb.astype(BF16)

    if band_last_only:
        @pl.when(pl.program_id(1) == pl.num_programs(1) - 1)
        def _():
            kb32_ref[...] = kb
            vb32_ref[...] = vb
    else:
        kb32_ref[...] = kb
        vb32_ref[...] = vb


def _proj(x, shift, scale, g1, w_all, bd, gqf, gkf, gqb, gkb, bf_row, wvt, *, G, R, band_last_only):
    nb, s, d = x.shape
    n = nb * s
    tm = G * R
    nbi, nsi = nb // G, s // R
    grid = (nbi, nsi)
    row = lambda b, i: (b * nsi + i, 0)
    const = lambda b, i: (0, 0)
    mod_spec = pl.BlockSpec((G, 1, d), lambda b, i: (b, 0, 0))
    out_bf = jax.ShapeDtypeStruct((n, WIDTH), BF16)
    out_f32 = jax.ShapeDtypeStruct((n, WIDTH), F32)
    tile = pl.BlockSpec((tm, WIDTH), row)
    if band_last_only:
        assert G == 1 and R == BAND_REACH
        band_shape = jax.ShapeDtypeStruct((nb, BAND_REACH, WIDTH), F32)
        band_spec = pl.BlockSpec((None, BAND_REACH, WIDTH), lambda b, i: (b, 0, 0))
        v_shape = jax.ShapeDtypeStruct((nb, nsi, WIDTH, tm), BF16)
        v_spec = pl.BlockSpec((None, None, WIDTH, tm), lambda b, i: (b, i, 0, 0))
    else:
        band_shape, band_spec = out_f32, tile
        v_shape, v_spec = out_bf, tile
    return pl.pallas_call(
        functools.partial(_proj_body, band_last_only=band_last_only),
        grid=grid,
        in_specs=[pl.BlockSpec((G, R, d), lambda b, i: (b, i, 0)), mod_spec, mod_spec,
                  pl.BlockSpec((1, d), const), pl.BlockSpec(w_all.shape, const), pl.BlockSpec(bd.shape, const),
                  pl.BlockSpec((1, WIDTH), const), pl.BlockSpec((1, WIDTH), const),
                  pl.BlockSpec((1, WIDTH), const), pl.BlockSpec((1, WIDTH), const),
                  pl.BlockSpec((1, LANES), const), pl.BlockSpec(wvt.shape, const)],
        out_specs=[tile, tile, v_spec, tile, tile, pl.BlockSpec((tm, LANES), row), tile, tile, tile,
                   band_spec, band_spec],
        out_shape=[out_bf, out_bf, v_shape, out_f32, out_f32, jax.ShapeDtypeStruct((n, LANES), F32),
                   out_bf, out_bf, out_bf, band_shape, band_shape],
        compiler_params=_cparams(2),
        name="proj",
    )(x, shift, scale, g1, w_all, bd, gqf, gkf, gqb, gkb, bf_row, wvt)


def _scan_body(lf_ref, cum_ref, cumt_ref, carry_ref):
    @pl.when(pl.program_id(1) == 0)
    def _():
        carry_ref[...] = jnp.zeros_like(carry_ref)

    lf = lf_ref[...]
    ts = lf.shape[0]
    lane = lax.broadcasted_iota(jnp.int32, lf.shape, 1)
    lf = jnp.where(lane < N_HEADS, lf, 0.0)
    hi, mid, lo = _split3(lf)
    rr = lax.broadcasted_iota(jnp.int32, (ts, ts), 0)
    cc = lax.broadcasted_iota(jnp.int32, (ts, ts), 1)
    tri = jnp.where(cc <= rr, 1.0, 0.0).astype(BF16)
    cum = _dot(tri, hi) + _dot(tri, mid) + _dot(tri, lo) + carry_ref[0:1, :]
    carry_ref[...] = jnp.broadcast_to(cum[ts - 1:ts, :], carry_ref.shape)
    cum2 = cum * LOG2E
    cum_ref[...] = cum2
    cumt_ref[...] = cum2.T[0:N_HEADS, :]


def _scan(lf, *, TS, TK):
    b, s, _ = lf.shape
    assert TS == TK
    t_block = (None, None, N_HEADS, TK)
    t_map = lambda bi, i: (bi, i, 0, 0)
    return pl.pallas_call(
        _scan_body,
        grid=(b, s // TS),
        in_specs=[pl.BlockSpec((None, TS, LANES), lambda bi, i: (bi, i, 0))],
        out_specs=[pl.BlockSpec((None, TS, LANES), lambda bi, i: (bi, i, 0)),
                   pl.BlockSpec(t_block, t_map)],
        out_shape=[jax.ShapeDtypeStruct((b, s, LANES), F32),
                   jax.ShapeDtypeStruct((b, s // TK, N_HEADS, TK), F32)],
        scratch_shapes=[pltpu.VMEM((8, LANES), F32)],
        compiler_params=_cparams(2),
        name="scan",
    )(lf)


def _fox_body(q_ref, k_ref, v_ref, cq_ref, ck_ref, o_ref, *, TQ, TK, q_off):
    p = pl.program_id(1)
    i = pl.program_id(2)
    q = q_ref[...]
    cq_blk = cq_ref[...]
    lane = lax.broadcasted_iota(jnp.int32, (TQ, PAIR), 1)
    ones_blk = jnp.where(lax.broadcasted_iota(jnp.int32, (TK, LANES), 1) == 0, 1.0, 0.0).astype(BF16)
    q0 = q_off + i * TQ
    n_full = q0 // TK
    qpos = q0 + lax.broadcasted_iota(jnp.int32, (TQ, TK), 0)
    kcol = lax.broadcasted_iota(jnp.int32, (TQ, TK), 1)

    outs = []
    for par in range(2):
        h = 2 * p + par
        qm = jnp.where((lane >= HEAD_DIM) == (par == 1), q, jnp.zeros_like(q))
        cq_col = jnp.sum(jnp.where(lane == h, cq_blk, 0.0), axis=1, keepdims=True)
        ref0 = cq_col[0:1, :]
        cqr = cq_col - ref0

        def step(j, carry, masked):
            m, l, acc = carry
            k0 = pl.multiple_of(j * TK, TK)
            kb = k_ref[pl.ds(k0, TK), :]
            vb = v_ref[pl.ds(k0, TK), :]
            s = _dot_nt(qm, kb)
            ck = ck_ref[j, pl.ds(h, 1), :]
            u = s - (ck - ref0)
            if masked:
                u = jnp.where(k0 + kcol <= qpos, u, NEG_INF)
            m_new = jnp.maximum(m, jnp.max(u, axis=1, keepdims=True) + cqr)
            alpha = jnp.exp2(m - m_new)
            pexp = jnp.exp2(u + (cqr - m_new))
            pv = _dot(pexp.astype(BF16), jnp.concatenate([vb, ones_blk], axis=1))
            return m_new, alpha * l + pv[:, LANES:LANES + 1], alpha * acc + pv[:, :LANES]

        init = (jnp.full((TQ, 1), NEG_INF, F32), jnp.zeros((TQ, 1), F32), jnp.zeros((TQ, LANES), F32))
        carry = lax.fori_loop(0, n_full, lambda j, c: step(j, c, False), init)
        _, l, acc = step(n_full, carry, True)
        outs.append(acc / l)
    o_ref[...] = jnp.where(lane < HEAD_DIM, outs[0], outs[1]).astype(o_ref.dtype)


def _fox(q, k, v, cum, cumt, *, TQ, TK, q_off):
    b, sq, _ = q.shape
    sk = k.shape[1]
    return pl.pallas_call(
        functools.partial(_fox_body, TQ=TQ, TK=TK, q_off=q_off),
        grid=(b, N_PAIRS, sq // TQ),
        in_specs=[pl.BlockSpec((None, TQ, PAIR), lambda bi, p, i: (bi, i, p)),
                  pl.BlockSpec((None, sk, PAIR), lambda bi, p, i: (bi, 0, p)),
                  pl.BlockSpec((None, sk, PAIR), lambda bi, p, i: (bi, 0, p)),
                  pl.BlockSpec((None, TQ, LANES), lambda bi, p, i: (bi, i, 0)),
                  pl.BlockSpec((None, sk // TK, N_HEADS, TK), lambda bi, p, i: (bi, 0, 0, 0))],
        out_specs=pl.BlockSpec((None, TQ, PAIR), lambda bi, p, i: (bi, i, p)),
        out_shape=jax.ShapeDtypeStruct((b, sq, WIDTH), BF16),
        compiler_params=_cparams(3),
        name="fox",
    )(q, k, v, cum, cumt)


AUG_PIECES = 3


def _scan_t_body(lf_ref, place_ref, ct_ref, ka_ref, carry_ref):
    @pl.when(pl.program_id(1) == 0)
    def _():
        carry_ref[...] = jnp.zeros_like(carry_ref)

    lf = lf_ref[...]
    ts = lf.shape[0]
    lane = lax.broadcasted_iota(jnp.int32, lf.shape, 1)
    lf = jnp.where(lane < N_HEADS, lf, 0.0)
    hi, mid, lo = _split3(lf)
    rr = lax.broadcasted_iota(jnp.int32, (ts, ts), 0)
    cc = lax.broadcasted_iota(jnp.int32, (ts, ts), 1)
    tri = jnp.where(cc <= rr, 1.0, 0.0).astype(BF16)
    cum = _dot(tri, hi) + _dot(tri, mid) + _dot(tri, lo) + carry_ref[0:1, :]
    carry_ref[...] = jnp.broadcast_to(cum[ts - 1:ts, :], carry_ref.shape)
    cum2 = cum * LOG2E
    ct_ref[...] = cum2.T[0:N_HEADS, :]
    pieces = _split3(cum2 - cum2[0:1, :])
    ka = _dot(pieces[0], place_ref[0]) + _dot(pieces[1], place_ref[1]) + _dot(pieces[2], place_ref[2])
    ka_ref[...] = ka.astype(BF16)


def _aug_placement():
    h = jnp.arange(LANES)[:, None]
    col = jnp.arange(WIDTH)[None, :]
    mats = []
    for x in range(AUG_PIECES):
        tgt = PAIR * (h // 2) + AUG_PIECES * (h % 2) + x
        mats.append(jnp.where((h < N_HEADS) & (col == tgt), 1.0, 0.0))
    return jnp.stack(mats).astype(BF16)


def _scan_t(lf, *, T):
    b, s, _ = lf.shape
    place = _aug_placement()
    return pl.pallas_call(
        _scan_t_body,
        grid=(b, s // T),
        in_specs=[pl.BlockSpec((None, T, LANES), lambda bi, i: (bi, i, 0)),
                  pl.BlockSpec(place.shape, lambda bi, i: (0, 0, 0))],
        out_specs=[pl.BlockSpec((None, None, N_HEADS, T), lambda bi, i: (bi, i, 0, 0)),
                   pl.BlockSpec((None, T, WIDTH), lambda bi, i: (bi, i, 0))],
        out_shape=[jax.ShapeDtypeStruct((b, s // T, N_HEADS, T), F32),
                   jax.ShapeDtypeStruct((b, s, WIDTH), BF16)],
        scratch_shapes=[pltpu.VMEM((8, LANES), F32)],
        compiler_params=_cparams(2),
        name="scan_t",
    )(lf, place)


def _foxt_body(q_ref, k_ref, ka_ref, vt_ref, ct_ref, o_ref, *, T):
    p = pl.program_id(1)
    i = pl.program_id(2)
    q = q_ref[...]
    lane = lax.broadcasted_iota(jnp.int32, (T, PAIR), 1)
    halves = []
    for par in range(2):
        qm = jnp.where((lane >= HEAD_DIM) == (par == 1), q, jnp.zeros_like(q))
        lo_lane = AUG_PIECES * par
        qa = jnp.where((lane >= lo_lane) & (lane < lo_lane + AUG_PIECES), -1.0, 0.0).astype(BF16)
        halves.append(jnp.concatenate([qm, qa], axis=1))
    qcat = jnp.concatenate(halves, axis=0)
    h_even = 2 * p
    cq = jnp.concatenate([ct_ref[i, pl.ds(h_even, 1), :], ct_ref[i, pl.ds(h_even + 1, 1), :]], axis=1)
    ones_rows = jnp.ones((16, T), BF16)
    krow = lax.broadcasted_iota(jnp.int32, (T, 2 * T), 0)
    qcol = lax.broadcasted_iota(jnp.int32, (T, 2 * T), 1) % T

    def step(j, carry, masked):
        m, acc_e, acc_o = carry
        k0 = pl.multiple_of(j * T, T)
        kcat = jnp.concatenate([k_ref[pl.ds(k0, T), :], ka_ref[pl.ds(k0, T), :]], axis=1)
        st = _dot_nt(kcat, qcat)
        c0 = jnp.concatenate([jnp.broadcast_to(ct_ref[j, pl.ds(h_even, 1), :][:, 0:1], (1, T)),
                              jnp.broadcast_to(ct_ref[j, pl.ds(h_even + 1, 1), :][:, 0:1], (1, T))], axis=1)
        rb = cq - c0
        if masked:
            st = jnp.where(krow <= qcol, st, NEG_INF)
        m_new = jnp.maximum(m, jnp.max(st, axis=0, keepdims=True) + rb)
        alpha = jnp.exp2(m - m_new)
        pt = jnp.exp2(st + (rb - m_new)).astype(BF16)
        vt = vt_ref[j]
        pv_e = _dot(jnp.concatenate([vt[0:HEAD_DIM], ones_rows], axis=0), pt[:, 0:T])
        pv_o = _dot(jnp.concatenate([vt[HEAD_DIM:PAIR], ones_rows], axis=0), pt[:, T:2 * T])
        return m_new, alpha[:, 0:T] * acc_e + pv_e, alpha[:, T:2 * T] * acc_o + pv_o

    rows = HEAD_DIM + 16
    init = (jnp.full((1, 2 * T), NEG_INF, F32), jnp.zeros((rows, T), F32), jnp.zeros((rows, T), F32))
    n_pairs = i // 2
    carry = lax.fori_loop(0, n_pairs, lambda t, c: step(2 * t + 1, step(2 * t, c, False), False), init)
    carry = lax.fori_loop(2 * n_pairs, i, lambda j, c: step(j, c, False), carry)
    _, acc_e, acc_o = step(i, carry, True)
    o_t = jnp.concatenate([acc_e[0:HEAD_DIM] / acc_e[HEAD_DIM:HEAD_DIM + 1],
                           acc_o[0:HEAD_DIM] / acc_o[HEAD_DIM:HEAD_DIM + 1]], axis=0)
    o_ref[...] = o_t.T.astype(o_ref.dtype)


def _foxt(q, k, ka, vt, ct, *, T):
    b, s, _ = q.shape
    nt = s // T
    return pl.pallas_call(
        functools.partial(_foxt_body, T=T),
        grid=(b, N_PAIRS, nt),
        in_specs=[pl.BlockSpec((None, T, PAIR), lambda bi, p, i: (bi, i, p)),
                  pl.BlockSpec((None, s, PAIR), lambda bi, p, i: (bi, 0, p)),
                  pl.BlockSpec((None, s, PAIR), lambda bi, p, i: (bi, 0, p)),
                  pl.BlockSpec((None, nt, PAIR, T), lambda bi, p, i: (bi, 0, p, 0)),
                  pl.BlockSpec((None, nt, N_HEADS, T), lambda bi, p, i: (bi, 0, 0, 0))],
        out_specs=pl.BlockSpec((None, T, PAIR), lambda bi, p, i: (bi, i, p)),
        out_shape=jax.ShapeDtypeStruct((b, s, WIDTH), BF16),
        compiler_params=_cparams(3),
        name="foxt",
    )(q, k, ka, vt, ct)


def _band_body(q_ref, k_ref, v_ref, bias_ref, o_ref, *scratch, TQ, W, n_sub, padded):
    p = pl.program_id(1)
    i = pl.program_id(2)
    if padded:
        kpad_ref, vpad_ref = scratch
        s_len = k_ref.shape[0]

        @pl.when(i == 0)
        def _():
            zeros = jnp.zeros((BAND_REACH, PAIR), BF16)
            kpad_ref[pl.ds(0, BAND_REACH), :] = zeros
            vpad_ref[pl.ds(0, BAND_REACH), :] = zeros
            kpad_ref[pl.ds(BAND_REACH, s_len), :] = k_ref[...]
            vpad_ref[pl.ds(BAND_REACH, s_len), :] = v_ref[...]
    else:
        kpad_ref, vpad_ref = k_ref, v_ref

    lane = lax.broadcasted_iota(jnp.int32, (TQ, PAIR), 1)
    ones_blk = jnp.where(lax.broadcasted_iota(jnp.int32, (W, LANES), 1) == 0, 1.0, 0.0).astype(BF16)
    kcol = lax.broadcasted_iota(jnp.int32, (TQ, W), 1)

    def sub_block(sub, carry):
        r0 = pl.multiple_of(sub * TQ, TQ)
        q0 = i * (n_sub * TQ) + r0
        q = q_ref[pl.ds(r0, TQ), :]
        kw = kpad_ref[pl.ds(pl.multiple_of(q0, TQ), W), :] if padded else kpad_ref[...]
        vw = vpad_ref[pl.ds(pl.multiple_of(q0, TQ), W), :] if padded else vpad_ref[...]
        vcat = jnp.concatenate([vw, ones_blk], axis=1)
        outs = []
        for par in range(2):
            h = 2 * p + par
            qm = jnp.where((lane >= HEAD_DIM) == (par == 1), q, jnp.zeros_like(q))
            s = _dot_nt(qm, kw) + bias_ref[h]
            if padded:
                s = jnp.where(kcol >= BAND_REACH - q0, s, NEG_INF)
            m = jnp.max(s, axis=1, keepdims=True)
            pexp = jnp.exp2(s - m)
            pv = _dot(pexp.astype(BF16), vcat)
            outs.append(pv[:, :LANES] / pv[:, LANES:LANES + 1])
        o_ref[pl.ds(r0, TQ), :] = jnp.where(lane < HEAD_DIM, outs[0], outs[1]).astype(o_ref.dtype)
        return carry

    lax.fori_loop(0, n_sub, sub_block, 0)


def _band(q, k, v, bias, *, TQ, n_sub, padded):
    b, sq, _ = q.shape
    sk = k.shape[1]
    w = BAND_REACH + TQ if padded else sk
    tqb = TQ * n_sub
    scratch = [pltpu.VMEM((sk + BAND_REACH, PAIR), BF16)] * 2 if padded else []
    return pl.pallas_call(
        functools.partial(_band_body, TQ=TQ, W=w, n_sub=n_sub, padded=padded),
        grid=(b, N_PAIRS, sq // tqb),
        in_specs=[pl.BlockSpec((None, tqb, PAIR), lambda bi, p, i: (bi, i, p)),
                  pl.BlockSpec((None, sk, PAIR), lambda bi, p, i: (bi, 0, p)),
                  pl.BlockSpec((None, sk, PAIR), lambda bi, p, i: (bi, 0, p)),
                  pl.BlockSpec(bias.shape, lambda bi, p, i: (0, 0, 0))],
        out_specs=pl.BlockSpec((None, tqb, PAIR), lambda bi, p, i: (bi, i, p)),
        out_shape=jax.ShapeDtypeStruct((b, sq, WIDTH), BF16),
        scratch_shapes=scratch,
        compiler_params=_cparams(3),
        name="band",
    )(q, k, v, bias)


def _band_bias_tile(rel_bias, tq, w):
    r = jnp.arange(tq)[:, None]
    c = jnp.arange(w)[None, :]
    dist = r + BAND_REACH - c
    in_band = (c // CHUNK >= r // CHUNK) & (c // CHUNK <= r // CHUNK + BAND_REACH // CHUNK)
    vals = rel_bias[:, jnp.clip(dist, -REL_CLIP, REL_CLIP) + REL_CLIP] * LOG2E
    return jnp.where(in_band[None], vals, NEG_INF).astype(F32)


def _first_index(is_max, idx, axis, big):
    return jnp.min(jnp.where(is_max, idx, big), axis=axis, keepdims=True)


def _route(scores, choice):
    t = scores.shape[1]
    c3 = choice.reshape(N_GROUPS, GROUP_SIZE, t)
    j_idx = lax.broadcasted_iota(jnp.int32, c3.shape, 1)
    top1 = jnp.max(c3, axis=1, keepdims=True)
    first = _first_index(c3 == top1, j_idx, 1, GROUP_SIZE)
    top2 = jnp.max(jnp.where(j_idx == first, -jnp.inf, c3), axis=1, keepdims=True)
    gscore = (top1 + top2).reshape(N_GROUPS, t)

    g_idx = lax.broadcasted_iota(jnp.int32, gscore.shape, 0)
    gsel = jnp.zeros(gscore.shape, F32)
    work = gscore
    for _ in range(TOPK_GROUPS):
        gm = jnp.max(work, axis=0, keepdims=True)
        pick = g_idx == _first_index(work == gm, g_idx, 0, N_GROUPS)
        gsel = jnp.where(pick, 1.0, gsel)
        work = jnp.where(pick, -jnp.inf, work)

    emask = jnp.broadcast_to(gsel.reshape(N_GROUPS, 1, t), c3.shape) > 0.0
    work = jnp.where(emask, c3, NEG_INF)
    e_idx = lax.broadcasted_iota(jnp.int32, c3.shape, 0) * GROUP_SIZE + j_idx
    esel = jnp.zeros(c3.shape, F32)
    for _ in range(TOP_K):
        em = jnp.max(jnp.max(work, axis=1, keepdims=True), axis=0, keepdims=True)
        cand = jnp.where(work == em, e_idx, N_EXPERTS)
        first = jnp.min(jnp.min(cand, axis=1, keepdims=True), axis=0, keepdims=True)
        pick = e_idx == first
        esel = jnp.where(pick, 1.0, esel)
        work = jnp.where(pick, -jnp.inf, work)

    w = esel * scores.reshape(c3.shape)
    denom = jnp.sum(jnp.sum(w, axis=1, keepdims=True), axis=0, keepdims=True)
    return (w / denom * ROUTED_SCALE).reshape(N_EXPERTS, t)


def _merge_body(of_ref, ob_ref, x_ref, gate_ref, sh_ref, sc_ref, ogf_ref, ogb_ref, wo_ref, g2_ref,
                wrh_ref, wrl_ref, br_ref, x1_ref, h2_ref, comb_ref, combt_ref, cnt_ref):
    def group_norm(t_ref, gain_ref):
        t = t_ref[...].astype(F32)
        ms = jnp.mean(t * t, axis=-1, keepdims=True)
        return (t * lax.rsqrt(ms + EPS) * gain_ref[...]).astype(BF16)

    y = _dot(group_norm(of_ref, ogf_ref), wo_ref[0:WIDTH, :]) + _dot(group_norm(ob_ref, ogb_ref), wo_ref[WIDTH:, :])
    x = x_ref[...]
    g, r, d = x.shape
    x1 = x + gate_ref[...] * y.reshape(g, r, d)
    x1_ref[...] = x1
    ms = jnp.mean(x1 * x1, axis=-1, keepdims=True)
    h2 = (x1 * lax.rsqrt(ms + EPS) * g2_ref[...] * (1.0 + sc_ref[...]) + sh_ref[...]).reshape(g * r, d)
    h_hi, h_lo = _split2(h2)
    h2_ref[...] = h_hi
    logits = _dot_nt(wrh_ref[...], h_hi) + _dot_nt(wrh_ref[...], h_lo) + _dot_nt(wrl_ref[...], h_hi)
    scores = jax.nn.sigmoid(logits)
    t = scores.shape[1]
    bias = jnp.concatenate([br_ref[...]] * (t // LANES), axis=1)
    comb = _route(scores, scores + bias)
    comb_pad = jnp.concatenate([comb, jnp.zeros((LANES - N_EXPERTS, t), F32)], axis=0)
    comb_ref[...] = comb_pad.T
    combt_ref[...] = comb
    cnt = jnp.sum(jnp.where(comb > 0.0, 1.0, 0.0), axis=1, keepdims=True)
    cnt_ref[...] = jnp.broadcast_to(cnt, (N_EXPERTS, LANES))


def _merge(of, ob, x, gate, shift, scale, ogf, ogb, wo, g2, wr_hi, wr_lo, br, *, G, R):
    nb, s, d = x.shape
    n = nb * s
    tm = G * R
    nbi, nsi = nb // G, s // R
    row = lambda b, i: (b * nsi + i, 0)
    const = lambda b, i: (0, 0)
    mod_spec = pl.BlockSpec((G, 1, d), lambda b, i: (b, 0, 0))
    x_spec = pl.BlockSpec((G, R, d), lambda b, i: (b, i, 0))
    return pl.pallas_call(
        _merge_body,
        grid=(nbi, nsi),
        in_specs=[pl.BlockSpec((tm, WIDTH), row), pl.BlockSpec((tm, WIDTH), row), x_spec,
                  mod_spec, mod_spec, mod_spec,
                  pl.BlockSpec((1, WIDTH), const), pl.BlockSpec((1, WIDTH), const),
                  pl.BlockSpec(wo.shape, const), pl.BlockSpec((1, d), const),
                  pl.BlockSpec(wr_hi.shape, const), pl.BlockSpec(wr_lo.shape, const),
                  pl.BlockSpec(br.shape, const)],
        out_specs=[x_spec, pl.BlockSpec((tm, d), row), pl.BlockSpec((tm, LANES), row),
                   pl.BlockSpec((N_EXPERTS, tm), lambda b, i: (0, b * nsi + i)),
                   pl.BlockSpec((None, N_EXPERTS, LANES), lambda b, i: (b * nsi + i, 0, 0))],
        out_shape=[jax.ShapeDtypeStruct((nb, s, d), F32), jax.ShapeDtypeStruct((n, d), BF16),
                   jax.ShapeDtypeStruct((n, LANES), F32), jax.ShapeDtypeStruct((N_EXPERTS, n), F32),
                   jax.ShapeDtypeStruct((n // tm, N_EXPERTS, LANES), F32)],
        compiler_params=_cparams(2),
        name="merge",
    )(of, ob, x, gate, shift, scale, ogf, ogb, wo, g2, wr_hi, wr_lo, br)


def _silu(g):
    return g * jax.nn.sigmoid(g)


def _moe_body(h_ref, comb_ref, x1_ref, gate_ref, wg_ref, wu_ref, wd_ref, sg_ref, su_ref, sd_ref, y_ref, acc_ref):
    e = pl.program_id(2)
    hb = h_ref[...]

    @pl.when(e == 0)
    def _():
        a = _silu(_dot(hb, sg_ref[...])) * _dot(hb, su_ref[...])
        acc_ref[...] = _dot(a.astype(BF16), sd_ref[...])

    comb = comb_ref[...]
    lane = lax.broadcasted_iota(jnp.int32, comb.shape, 1)
    c_e = jnp.sum(jnp.where(lane == e, comb, 0.0), axis=1, keepdims=True)
    a = _silu(_dot(hb, wg_ref[...])) * _dot(hb, wu_ref[...]) * c_e
    acc_ref[...] += _dot(a.astype(BF16), wd_ref[...])

    @pl.when(e == pl.num_programs(2) - 1)
    def _():
        x1 = x1_ref[...]
        g, r, d = x1.shape
        y_ref[...] = x1 + gate_ref[...] * acc_ref[...].reshape(g, r, d)


def _moe(h2, comb, x1, gate, wg, wu, wd, sg, su, sd, *, G, R):
    nb, s, d = x1.shape
    tm = G * R
    nbi, nsi = nb // G, s // R
    ff = wg.shape[2]
    row = lambda b, i, e: (b * nsi + i, 0)
    const = lambda b, i, e: (0, 0)
    x_spec = pl.BlockSpec((G, R, d), lambda b, i, e: (b, i, 0))
    return pl.pallas_call(
        _moe_body,
        grid=(nbi, nsi, N_EXPERTS),
        in_specs=[pl.BlockSpec((tm, d), row), pl.BlockSpec((tm, LANES), row), x_spec,
                  pl.BlockSpec((G, 1, d), lambda b, i, e: (b, 0, 0)),
                  pl.BlockSpec((None, d, ff), lambda b, i, e: (e, 0, 0)),
                  pl.BlockSpec((None, d, ff), lambda b, i, e: (e, 0, 0)),
                  pl.BlockSpec((None, ff, d), lambda b, i, e: (e, 0, 0)),
                  pl.BlockSpec(sg.shape, const), pl.BlockSpec(su.shape, const), pl.BlockSpec(sd.shape, const)],
        out_specs=x_spec,
        out_shape=jax.ShapeDtypeStruct((nb, s, d), F32),
        scratch_shapes=[pltpu.VMEM((tm, d), F32)],
        compiler_params=_cparams(3),
        name="moe",
    )(h2, comb, x1, gate, wg, wu, wd, sg, su, sd)


MOE_TM = 512
MOE_CH = 16
MOE_SLOTS = TOP_K * MOE_TM + N_EXPERTS * MOE_CH
MOE_NCHUNK = MOE_SLOTS // MOE_CH
MOE_PIECE = 512
MOE_RB = 512
TAU_RADIX = 64.0


def _moe_plan(cnt, n_tiles):
    pc = (cnt + MOE_CH - 1) // MOE_CH * MOE_CH
    off = jnp.cumsum(pc, axis=1) - pc
    end = off + pc
    n_used = (jnp.sum(pc, axis=1) // MOE_CH).astype(jnp.int32)
    tot = jnp.sum(pc, axis=0)
    reg = (tot + MOE_RB - 1) // MOE_RB * MOE_RB
    reg_end = jnp.cumsum(reg)
    reg_start = reg_end - reg
    dest_base = reg_start[None, :] + jnp.cumsum(pc, axis=0) - pc
    chunk_row = jnp.arange(MOE_NCHUNK, dtype=jnp.int32)[None, :] * MOE_CH
    e_of_chunk = jnp.sum((chunk_row[:, :, None] >= end[:, None, :]).astype(jnp.int32), axis=2)
    e_of_chunk = jnp.minimum(e_of_chunk, N_EXPERTS - 1)
    cdest = (jnp.take_along_axis(dest_base, e_of_chunk, axis=1) + chunk_row
             - jnp.take_along_axis(off, e_of_chunk, axis=1)).astype(jnp.int32)
    r_max = (n_tiles * MOE_SLOTS + N_EXPERTS * MOE_RB) // MOE_RB
    n_active = (reg_end[-1] // MOE_RB).astype(jnp.int32).reshape(1)
    tile_row = jnp.arange(r_max, dtype=jnp.int32) * MOE_RB
    tile_expert = jnp.minimum(jnp.sum((tile_row[:, None] >= reg_end[None, :]).astype(jnp.int32), axis=1),
                              N_EXPERTS - 1).astype(jnp.int32)
    valid = jnp.clip((reg_start + tot)[tile_expert] - tile_row, 0, MOE_RB).astype(jnp.int32)
    f = lambda a: a.astype(F32)
    zeros64 = jnp.zeros((n_tiles, N_EXPERTS), F32)
    row2 = lambda a: jnp.broadcast_to(jnp.concatenate([f(a), f(a)], axis=1)[:, None, :], (n_tiles, 8, LANES))
    col = lambda a: jnp.broadcast_to(f(a)[:, :, None], (n_tiles, N_EXPERTS, LANES))
    col128 = lambda a: jnp.broadcast_to(jnp.concatenate([f(a), zeros64], axis=1)[:, :, None], (n_tiles, LANES, LANES))
    row1 = lambda a: jnp.broadcast_to(jnp.concatenate([f(a), zeros64], axis=1)[:, None, :], (n_tiles, 8, LANES))
    return dict(n_used=n_used, cdest=cdest.reshape(n_tiles, 1, MOE_NCHUNK), r_max=r_max, n_active=n_active,
                tile_expert=tile_expert, valid=valid,
                off_row2=row2(off), end_row2=row2(end), off_col=col(off),
                off_row1=row1(off), off_col128=col128(off), end_col128=col128(end))


def _tau_pieces(sel, tau):
    tau = jnp.where(sel, tau, -1.0)
    hi = jnp.floor(tau * (1.0 / TAU_RADIX)) * TAU_RADIX
    return hi.astype(BF16), (tau - hi).astype(BF16)


def _dispatch_body(nused_ref, cdest_ref, h_ref, combt_ref, offcol_ref, offrow_ref, endrow_ref, sorted_ref,
                   buf_ref, sem):
    t = pl.program_id(0)
    n_used = nused_ref[t]
    tm = h_ref.shape[0]
    sel = combt_ref[...] > 0.0
    rr = lax.broadcasted_iota(jnp.int32, (tm, tm), 0)
    cc = lax.broadcasted_iota(jnp.int32, (tm, tm), 1)
    upper = jnp.where(rr < cc, 1.0, 0.0).astype(BF16)
    rank = _dot(jnp.where(sel, 1.0, 0.0).astype(BF16), upper)
    cols = jnp.concatenate([offcol_ref[...]] * (tm // LANES), axis=1)
    tau_hi, tau_lo = _tau_pieces(sel, cols + rank)
    taucat = jnp.concatenate([tau_hi, tau_lo], axis=0)
    off_row = offrow_ref[0:1, :]
    end_row = endrow_ref[0:1, :]
    hb = h_ref[...]

    for piece in range(MOE_SLOTS // MOE_PIECE):
        @pl.when(piece * (MOE_PIECE // MOE_CH) < n_used)
        def _():
            base = piece * MOE_PIECE
            s_col = (base + lax.broadcasted_iota(jnp.int32, (MOE_PIECE, LANES), 0)).astype(F32)
            onehot = jnp.where((s_col >= off_row) & (s_col < end_row), 1.0, 0.0).astype(BF16)
            q = _dot(onehot, taucat)
            s_mat = (base + lax.broadcasted_iota(jnp.int32, (MOE_PIECE, tm), 0)).astype(F32)
            g = jnp.where(q == s_mat, 1.0, 0.0).astype(BF16)
            buf_ref[pl.ds(base, MOE_PIECE), :] = _dot(g, hb).astype(BF16)

    def chunk_copy(c):
        src = buf_ref.at[pl.ds(pl.multiple_of(c * MOE_CH, MOE_CH), MOE_CH), :]
        dst = sorted_ref.at[pl.ds(pl.multiple_of(cdest_ref[0, c], MOE_CH), MOE_CH), :]
        return pltpu.make_async_copy(src, dst, sem)

    def issue(c, carry):
        chunk_copy(c).start()
        return carry

    def drain(c, carry):
        chunk_copy(c).wait()
        return carry

    lax.fori_loop(0, n_used, issue, 0)
    lax.fori_loop(0, n_used, drain, 0)


def _dispatch(h2, combt, plan, n_tiles):
    n, d = h2.shape
    r_total = plan["r_max"] * MOE_RB
    grid_spec = pltpu.PrefetchScalarGridSpec(
        num_scalar_prefetch=1,
        grid=(n_tiles,),
        in_specs=[pl.BlockSpec((None, 1, MOE_NCHUNK), lambda t, nu: (t, 0, 0), memory_space=pltpu.SMEM),
                  pl.BlockSpec((MOE_TM, d), lambda t, nu: (t, 0)),
                  pl.BlockSpec((N_EXPERTS, MOE_TM), lambda t, nu: (0, t)),
                  pl.BlockSpec((None, N_EXPERTS, LANES), lambda t, nu: (t, 0, 0)),
                  pl.BlockSpec((None, 8, LANES), lambda t, nu: (t, 0, 0)),
                  pl.BlockSpec((None, 8, LANES), lambda t, nu: (t, 0, 0))],
        out_specs=pl.BlockSpec(memory_space=pl.ANY),
        scratch_shapes=[pltpu.VMEM((MOE_SLOTS, d), BF16), pltpu.SemaphoreType.DMA(())],
    )
    return pl.pallas_call(
        _dispatch_body,
        grid_spec=grid_spec,
        out_shape=jax.ShapeDtypeStruct((r_total, d), BF16),
        compiler_params=_cparams(1),
        name="moe_dispatch",
    )(plan["n_used"], plan["cdest"], h2, combt, plan["off_col"], plan["off_row2"], plan["end_row2"])


def _ffn_body(texp_ref, nact_ref, valid_ref, x_ref, wg_ref, wu_ref, wd_ref, o_ref):
    r = pl.program_id(0)

    @pl.when(r < nact_ref[0])
    def _():
        x = x_ref[...]
        rows = lax.broadcasted_iota(jnp.int32, x.shape, 0)
        x = jnp.where(rows < valid_ref[r], x, jnp.zeros_like(x))
        a = _silu(_dot(x, wg_ref[...])) * _dot(x, wu_ref[...])
        o_ref[...] = _dot(a.astype(BF16), wd_ref[...]).astype(o_ref.dtype)

    @pl.when(r >= nact_ref[0])
    def _():
        o_ref[...] = jnp.zeros_like(o_ref)


def _ffn(xs, wg, wu, wd, plan):
    r_total, d = xs.shape
    ff = wg.shape[2]
    last = lambda r, te, na, va: (jnp.minimum(r, na[0] - 1), 0)
    wmap = lambda r, te, na, va: (te[r], 0, 0)
    grid_spec = pltpu.PrefetchScalarGridSpec(
        num_scalar_prefetch=3,
        grid=(plan["r_max"],),
        in_specs=[pl.BlockSpec((MOE_RB, d), last),
                  pl.BlockSpec((None, d, ff), wmap), pl.BlockSpec((None, d, ff), wmap),
                  pl.BlockSpec((None, ff, d), wmap)],
        out_specs=pl.BlockSpec((MOE_RB, d), lambda r, te, na, va: (r, 0)),
    )
    return pl.pallas_call(
        _ffn_body,
        grid_spec=grid_spec,
        out_shape=jax.ShapeDtypeStruct((r_total, d), BF16),
        compiler_params=_cparams(1),
        name="moe_ffn",
    )(plan["tile_expert"], plan["n_active"], plan["valid"], xs, wg, wu, wd)


def _combine_body(nused_ref, cdest_ref, h_ref, comb_ref, x1_ref, gate_ref, offrow_ref, offcol_ref, endcol_ref,
                  sg_ref, su_ref, sd_ref, ys_ref, y_ref, buf_ref, sem):
    t = pl.program_id(0)
    n_used = nused_ref[t]
    tm = h_ref.shape[0]

    @pl.when(t == 0)
    def _():
        buf_ref[...] = jnp.zeros_like(buf_ref)

    def chunk_copy(c):
        src = ys_ref.at[pl.ds(pl.multiple_of(cdest_ref[0, c], MOE_CH), MOE_CH), :]
        dst = buf_ref.at[pl.ds(pl.multiple_of(c * MOE_CH, MOE_CH), MOE_CH), :]
        return pltpu.make_async_copy(src, dst, sem)

    def issue(c, carry):
        chunk_copy(c).start()
        return carry

    def drain(c, carry):
        chunk_copy(c).wait()
        return carry

    lax.fori_loop(0, n_used, issue, 0)

    hb = h_ref[...]
    acc = _dot((_silu(_dot(hb, sg_ref[...])) * _dot(hb, su_ref[...])).astype(BF16), sd_ref[...])
    comb = comb_ref[...]
    sel = comb > 0.0
    rr = lax.broadcasted_iota(jnp.int32, (tm, tm), 0)
    cc = lax.broadcasted_iota(jnp.int32, (tm, tm), 1)
    lower = jnp.where(cc < rr, 1.0, 0.0).astype(BF16)
    rank = _dot(lower, jnp.where(sel, 1.0, 0.0).astype(BF16))
    tau_hi, tau_lo = _tau_pieces(sel, offrow_ref[0:1, :] + rank)
    taucat = jnp.concatenate([tau_hi, tau_lo], axis=1)
    c_hi, c_lo = _split2(comb)
    off_col = offcol_ref[...]
    end_col = endcol_ref[...]

    lax.fori_loop(0, n_used, drain, 0)

    for piece in range(MOE_SLOTS // MOE_PIECE):
        base = piece * MOE_PIECE
        s_row = (base + lax.broadcasted_iota(jnp.int32, (LANES, MOE_PIECE), 1)).astype(F32)
        off_b = jnp.concatenate([off_col] * (MOE_PIECE // LANES), axis=1)
        end_b = jnp.concatenate([end_col] * (MOE_PIECE // LANES), axis=1)
        onehot = jnp.where((s_row >= off_b) & (s_row < end_b), 1.0, 0.0).astype(BF16)
        q = _dot(taucat, jnp.concatenate([onehot, onehot], axis=0))
        w = _dot(c_hi, onehot) + _dot(c_lo, onehot)
        s_mat = (base + lax.broadcasted_iota(jnp.int32, (tm, MOE_PIECE), 1)).astype(F32)
        gw = jnp.where(q == s_mat, w, 0.0).astype(BF16)
        acc = acc + _dot(gw, buf_ref[pl.ds(base, MOE_PIECE), :])
    y_ref[...] = x1_ref[...] + gate_ref[...] * acc


def _combine(h2, comb, x1, gate, ys, sg, su, sd, plan, n_tiles):
    nb, s, d = x1.shape
    n = nb * s
    per_b = s // MOE_TM
    x1f = x1.reshape(n, d)
    const2 = lambda t, nu: (0, 0)
    grid_spec = pltpu.PrefetchScalarGridSpec(
        num_scalar_prefetch=1,
        grid=(n_tiles,),
        in_specs=[pl.BlockSpec((None, 1, MOE_NCHUNK), lambda t, nu: (t, 0, 0), memory_space=pltpu.SMEM),
                  pl.BlockSpec((MOE_TM, d), lambda t, nu: (t, 0)),
                  pl.BlockSpec((MOE_TM, LANES), lambda t, nu: (t, 0)),
                  pl.BlockSpec((MOE_TM, d), lambda t, nu: (t, 0)),
                  pl.BlockSpec((None, 1, d), lambda t, nu: (t // per_b, 0, 0)),
                  pl.BlockSpec((None, 8, LANES), lambda t, nu: (t, 0, 0)),
                  pl.BlockSpec((None, LANES, LANES), lambda t, nu: (t, 0, 0)),
                  pl.BlockSpec((None, LANES, LANES), lambda t, nu: (t, 0, 0)),
                  pl.BlockSpec(sg.shape, const2), pl.BlockSpec(su.shape, const2), pl.BlockSpec(sd.shape, const2),
                  pl.BlockSpec(memory_space=pl.ANY)],
        out_specs=pl.BlockSpec((MOE_TM, d), lambda t, nu: (t, 0)),
        scratch_shapes=[pltpu.VMEM((MOE_SLOTS, d), BF16), pltpu.SemaphoreType.DMA(())],
    )
    y = pl.pallas_call(
        _combine_body,
        grid_spec=grid_spec,
        out_shape=jax.ShapeDtypeStruct((n, d), F32),
        compiler_params=_cparams(1),
        name="moe_combine",
    )(plan["n_used"], plan["cdest"], h2, comb, x1f, gate, plan["off_row1"], plan["off_col128"], plan["end_col128"],
      sg, su, sd, ys)
    return y.reshape(nb, s, d)


def _moe_sparse(h2, comb, combt, cnt, x1, gate, wg, wu, wd, sg, su, sd):
    n = h2.shape[0]
    n_tiles = n // MOE_TM
    plan = _moe_plan(cnt[:, :, 0].astype(jnp.int32), n_tiles)
    xs = _dispatch(h2, combt, plan, n_tiles)
    ys = _ffn(xs, wg, wu, wd, plan)
    return _combine(h2, comb, x1, gate, ys, sg, su, sd, plan, n_tiles)


def _tile_heads(g, mult=1.0):
    return (jnp.tile(g.astype(F32), N_HEADS) * mult).reshape(1, WIDTH)


def kernel(x_prompt, x_sample, cache_fox_k, cache_fox_v, cache_fox_logf, cache_band_k, cache_band_v, c_prompt, c_sample, w_ada, b_ada, norm1_g, norm2_g, w_in, b_forget, g_q_fox, g_k_fox, g_q_band, g_k_band, rel_bias, out_g_fox, out_g_band, w_out, w_router, b_router, w_gate, w_up, w_down, ws_gate, ws_up, ws_down):
    depth = w_ada.shape[0]
    assert depth == 1
    bsz, seq, d = x_prompt.shape
    dbs, dseq, _ = x_sample.shape
    past = cache_fox_k.shape[2]
    n_cache = cache_band_k.shape[2]
    assert n_cache == BAND_REACH and dseq == CHUNK and seq % BAND_REACH == 0

    wi = w_in[0]
    cols = [wi[:, 0:512], wi[:, 512:1024], wi[:, 1024:1536], wi[:, 1544:2056], wi[:, 2056:2568], wi[:, 2568:3080],
            wi[:, 1536:1544], jnp.zeros((d, LANES - N_HEADS), F32)]
    w_all = jnp.concatenate(cols, axis=1).astype(BF16)
    hd = jnp.arange(WIDTH) // HEAD_DIM
    bd = jnp.where(hd[:, None] == hd[None, :], 1.0 / HEAD_DIM, 0.0).astype(BF16)
    qscale = ATTN_SCALE * LOG2E
    gqf, gkf = _tile_heads(g_q_fox[0], qscale), _tile_heads(g_k_fox[0])
    gqb, gkb = _tile_heads(g_q_band[0], qscale), _tile_heads(g_k_band[0])
    bf_row = jnp.concatenate([b_forget[0], jnp.zeros((LANES - N_HEADS,), F32)]).reshape(1, LANES)
    g1 = norm1_g[0].reshape(1, d)
    g2 = norm2_g[0].reshape(1, d)
    ogf = out_g_fox[0].reshape(1, WIDTH)
    ogb = out_g_band[0].reshape(1, WIDTH)
    wo = w_out[0].astype(BF16)
    wr_t = w_router[0].T
    wr_hi = wr_t.astype(BF16)
    wr_lo = (wr_t - wr_hi.astype(F32)).astype(BF16)
    br = jnp.broadcast_to(b_router[0].reshape(N_EXPERTS, 1), (N_EXPERTS, LANES)).astype(F32)
    wg, wu, wd = w_gate[0].astype(BF16), w_up[0].astype(BF16), w_down[0].astype(BF16)
    sg, su, sd = ws_gate[0].astype(BF16), ws_up[0].astype(BF16), ws_down[0].astype(BF16)
    wvt = wi[:, 2 * WIDTH:3 * WIDTH].T.astype(BF16)

    n_c = bsz + dbs
    rows = -(-n_c // 8) * 8
    c_all = jnp.concatenate([c_prompt, c_sample, jnp.zeros((rows - n_c, d), F32)], axis=0)
    mod = _ada(c_all, w_ada[0], b_ada[0].reshape(1, -1))

    def mods(lo, hi):
        return [mod[lo:hi, j * d:(j + 1) * d].reshape(hi - lo, 1, d) for j in range(6)]

    shift1_p, scale1_p, gate1_p, shift2_p, scale2_p, gate2_p = mods(0, bsz)
    shift1_s, scale1_s, gate1_s, shift2_s, scale2_s, gate2_s = mods(bsz, n_c)

    TM = BAND_REACH
    (qf, kf, vft, kf32, vf32, lf, qb, kb, vb, kb32, vb32) = _proj(
        x_prompt, shift1_p, scale1_p, g1, w_all, bd, gqf, gkf, gqb, gkb, bf_row, wvt, G=1, R=TM, band_last_only=True)
    r3 = lambda a: a.reshape(bsz, seq, a.shape[-1])
    ct, ka = _scan_t(r3(lf), T=TM)
    of = _foxt(r3(qf), r3(kf), ka, vft, ct, T=TM)
    band_tq = 128
    bias_p = _band_bias_tile(rel_bias[0], band_tq, BAND_REACH + band_tq)
    n_sub = max(dv for dv in range(1, 17) if (seq // band_tq) % dv == 0)
    ob = _band(r3(qb), r3(kb), r3(vb), bias_p, TQ=band_tq, n_sub=n_sub, padded=True)
    assert TM == MOE_TM
    x1_p, h2_p, comb_p, combt_p, cnt_p = _merge(of.reshape(-1, WIDTH), ob.reshape(-1, WIDTH), x_prompt, gate1_p,
                                                shift2_p, scale2_p, ogf, ogb, wo, g2, wr_hi, wr_lo, br, G=1, R=TM)
    y_p = _moe_sparse(h2_p, comb_p, combt_p, cnt_p, x1_p, gate2_p, wg, wu, wd, sg, su, sd)

    GS = 8
    (qf_s, kf_s, vf_s, kf32_s, vf32_s, lf_s, qb_s, kb_s, vb_s, kb32_s, vb32_s) = _proj(
        x_sample, shift1_s, scale1_s, g1, w_all, bd, gqf, gkf, gqb, gkb, bf_row, wvt, G=GS, R=dseq,
        band_last_only=False)
    s3 = lambda a: a.reshape(dbs, dseq, a.shape[-1])
    sk = past + dseq
    skp = -(-sk // LANES) * LANES
    pad_k = skp - sk
    lf_cache = jnp.pad(cache_fox_logf[0], ((0, 0), (0, 0), (0, LANES - N_HEADS)))
    lf_all = jnp.concatenate([lf_cache, s3(lf_s), jnp.zeros((dbs, pad_k, LANES), F32)], axis=1)
    cum_s, cumt_s = _scan(lf_all, TS=skp, TK=skp)
    zpad = jnp.zeros((dbs, pad_k, WIDTH), BF16)
    k_all = jnp.concatenate([cache_fox_k[0].reshape(dbs, past, WIDTH).astype(BF16), s3(kf_s), zpad], axis=1)
    v_all = jnp.concatenate([cache_fox_v[0].reshape(dbs, past, WIDTH).astype(BF16), s3(vf_s), zpad], axis=1)
    of_s = _fox(s3(qf_s), k_all, v_all, cum_s[:, past:past + dseq], cumt_s, TQ=dseq, TK=skp, q_off=past)
    bias_s = _band_bias_tile(rel_bias[0], dseq, BAND_REACH + LANES)
    zb = jnp.zeros((dbs, LANES - dseq, WIDTH), BF16)
    kb_all = jnp.concatenate([cache_band_k[0].reshape(dbs, n_cache, WIDTH).astype(BF16), s3(kb_s), zb], axis=1)
    vb_all = jnp.concatenate([cache_band_v[0].reshape(dbs, n_cache, WIDTH).astype(BF16), s3(vb_s), zb], axis=1)
    ob_s = _band(s3(qb_s), kb_all, vb_all, bias_s, TQ=dseq, n_sub=1, padded=False)
    x1_s, h2_s, comb_s, _, _ = _merge(of_s.reshape(-1, WIDTH), ob_s.reshape(-1, WIDTH), x_sample, gate1_s, shift2_s,
                                      scale2_s, ogf, ogb, wo, g2, wr_hi, wr_lo, br, G=GS, R=dseq)
    y_s = _moe(h2_s, comb_s, x1_s, gate2_s, wg, wu, wd, sg, su, sd, G=dbs, R=dseq)

    hshape = (N_HEADS, HEAD_DIM)
    new_bk_s = jnp.concatenate([cache_band_k[0], s3(kb32_s).reshape(dbs, dseq, *hshape)], axis=1)[:, -n_cache:]
    new_bv_s = jnp.concatenate([cache_band_v[0], s3(vb32_s).reshape(dbs, dseq, *hshape)], axis=1)[:, -n_cache:]
    return (y_p, y_s,
            kf32.reshape(1, bsz, seq, *hshape), vf32.reshape(1, bsz, seq, *hshape),
            lf[:, :N_HEADS].reshape(1, bsz, seq, N_HEADS),
            kb32.reshape(1, bsz, BAND_REACH, *hshape), vb32.reshape(1, bsz, BAND_REACH, *hshape),
            kf32_s.reshape(1, dbs, dseq, *hshape), vf32_s.reshape(1, dbs, dseq, *hshape),
            lf_s[:, :N_HEADS].reshape(1, dbs, dseq, N_HEADS),
            new_bk_s[None], new_bv_s[None])
```

```python
import functools

import jax
import jax.numpy as jnp
import numpy as np
from jax import lax
from jax.experimental import pallas as pl
from jax.experimental.pallas import tpu as pltpu

F32 = jnp.float32
BF16 = jnp.bfloat16

HEAD_DIM = 64
N_HEADS = 8
WIDTH = N_HEADS * HEAD_DIM
PAIR = 2 * HEAD_DIM
N_PAIRS = N_HEADS // 2
LANES = 128
CHUNK = 64
BAND_REACH = 512
REL_CLIP = 256
N_EXPERTS = 64
N_GROUPS = 8
GROUP_SIZE = N_EXPERTS // N_GROUPS
TOPK_GROUPS = 4
TOP_K = 8
ROUTED_SCALE = 2.5
EPS = 1e-6
NEG_INF = -1e30
ATTN_SCALE = HEAD_DIM ** -0.5
LOG2E = 1.4426950408889634
VMEM_LIMIT = 56 * 1024 * 1024


def _cparams(n_axes):
    return pltpu.CompilerParams(dimension_semantics=("arbitrary",) * n_axes,
                                vmem_limit_bytes=VMEM_LIMIT)


def _dot(a, b):
    return jnp.dot(a, b, preferred_element_type=F32)


def _dot_nt(a, b):
    return lax.dot_general(a, b, (((1,), (1,)), ((), ())), preferred_element_type=F32)


def _split2(a):
    hi = a.astype(BF16)
    lo = (a - hi.astype(F32)).astype(BF16)
    return hi, lo


def _split3(a):
    hi = a.astype(BF16)
    r = a - hi.astype(F32)
    mid = r.astype(BF16)
    lo = (r - mid.astype(F32)).astype(BF16)
    return hi, mid, lo


def _ada_body(c_ref, w_ref, b_ref, o_ref):
    c = c_ref[...]
    a = c * jax.nn.sigmoid(c)
    a_hi, a_lo = _split2(a)
    w_hi, w_lo = _split2(w_ref[...])
    o_ref[...] = _dot(a_hi, w_hi) + _dot(a_hi, w_lo) + _dot(a_lo, w_hi) + b_ref[...]


def _ada(c_all, w_ada, b_ada):
    rows, d = c_all.shape
    n = w_ada.shape[1]
    tn = 1024
    return pl.pallas_call(
        _ada_body,
        grid=(n // tn,),
        in_specs=[pl.BlockSpec((rows, d), lambda j: (0, 0)),
                  pl.BlockSpec((d, tn), lambda j: (0, j)),
                  pl.BlockSpec((1, tn), lambda j: (0, j))],
        out_specs=pl.BlockSpec((rows, tn), lambda j: (0, j)),
        out_shape=jax.ShapeDtypeStruct((rows, n), F32),
        compiler_params=_cparams(1),
        name="ada",
    )(c_all, w_ada, b_ada)


def _log_sigmoid(z):
    return jnp.minimum(z, 0.0) - jnp.log(1.0 + jnp.exp(-jnp.abs(z)))


def _proj_body(x_ref, sh_ref, sc_ref, g1_ref, w_ref, bd_ref, gqf_ref, gkf_ref, gqb_ref, gkb_ref, bf_ref, wvt_ref,
               qf_ref, kf_ref, vf_ref, kf32_ref, vf32_ref, lf_ref, qb_ref, kb_ref, vb_ref, kb32_ref, vb32_ref,
               *, band_last_only):
    x = x_ref[...]
    g, r, d = x.shape
    ms = jnp.mean(x * x, axis=-1, keepdims=True)
    h = x * lax.rsqrt(ms + EPS) * g1_ref[...] * (1.0 + sc_ref[...]) + sh_ref[...]
    hb = h.reshape(g * r, d).astype(BF16)

    def seg(i):
        return _dot(hb, w_ref[:, i * WIDTH:(i + 1) * WIDTH])

    def head_norm(t, gain_ref):
        ssq = _dot((t * t).astype(BF16), bd_ref[...])
        return t * lax.rsqrt(ssq + EPS) * gain_ref[...]

    qf_ref[...] = head_norm(seg(0), gqf_ref).astype(BF16)
    kf = head_norm(seg(1), gkf_ref)
    kf32_ref[...] = kf
    kf_ref[...] = kf.astype(BF16)
    vf = seg(2)
    vf32_ref[...] = vf
    if band_last_only:
        vf_ref[...] = _dot_nt(wvt_ref[0:WIDTH, :], hb).astype(BF16)
    else:
        vf_ref[...] = vf.astype(BF16)
    z = _dot(hb, w_ref[:, 6 * WIDTH:6 * WIDTH + LANES]) + bf_ref[...]
    lf_ref[...] = _log_sigmoid(z)
    qb_ref[...] = head_norm(seg(3), gqb_ref).astype(BF16)
    kb = head_norm(seg(4), gkb_ref)
    kb_ref[...] = kb.astype(BF16)
    vb = seg(5)
    if band_last_only:
        vb_ref[...] = _dot_nt(wvt_ref[WIDTH:2 * WIDTH, :], hb).astype(BF16)
    else:
        vb_ref[...] = vb.astype(BF16)

    if band_last_only:
        @pl.when(pl.program_id(1) == pl.num_programs(1) - 1)
        def _():
            kb32_ref[...] = kb
            vb32_ref[...] = vb
    else:
        kb32_ref[...] = kb
        vb32_ref[...] = vb


def _proj(x, shift, scale, g1, w_all, bd, gqf, gkf, gqb, gkb, bf_row, wvt, *, G, R, band_last_only):
    nb, s, d = x.shape
    n = nb * s
    tm = G * R
    nbi, nsi = nb // G, s // R
    grid = (nbi, nsi)
    row = lambda b, i: (b * nsi + i, 0)
    const = lambda b, i: (0, 0)
    mod_spec = pl.BlockSpec((G, 1, d), lambda b, i: (b, 0, 0))
    out_bf = jax.ShapeDtypeStruct((n, WIDTH), BF16)
    out_f32 = jax.ShapeDtypeStruct((n, WIDTH), F32)
    tile = pl.BlockSpec((tm, WIDTH), row)
    if band_last_only:
        assert G == 1 and R == BAND_REACH
        band_shape = jax.ShapeDtypeStruct((nb, BAND_REACH, WIDTH), F32)
        band_spec = pl.BlockSpec((None, BAND_REACH, WIDTH), lambda b, i: (b, 0, 0))
        v_shape = jax.ShapeDtypeStruct((nb, nsi, WIDTH, tm), BF16)
        v_spec = pl.BlockSpec((None, None, WIDTH, tm), lambda b, i: (b, i, 0, 0))
    else:
        band_shape, band_spec = out_f32, tile
        v_shape, v_spec = out_bf, tile
    return pl.pallas_call(
        functools.partial(_proj_body, band_last_only=band_last_only),
        grid=grid,
        in_specs=[pl.BlockSpec((G, R, d), lambda b, i: (b, i, 0)), mod_spec, mod_spec,
                  pl.BlockSpec((1, d), const), pl.BlockSpec(w_all.shape, const), pl.BlockSpec(bd.shape, const),
                  pl.BlockSpec((1, WIDTH), const), pl.BlockSpec((1, WIDTH), const),
                  pl.BlockSpec((1, WIDTH), const), pl.BlockSpec((1, WIDTH), const),
                  pl.BlockSpec((1, LANES), const), pl.BlockSpec(wvt.shape, const)],
        out_specs=[tile, tile, v_spec, tile, tile, pl.BlockSpec((tm, LANES), row), tile, tile, v_spec,
                   band_spec, band_spec],
        out_shape=[out_bf, out_bf, v_shape, out_f32, out_f32, jax.ShapeDtypeStruct((n, LANES), F32),
                   out_bf, out_bf, v_shape, band_shape, band_shape],
        compiler_params=_cparams(2),
        name="proj",
    )(x, shift, scale, g1, w_all, bd, gqf, gkf, gqb, gkb, bf_row, wvt)


def _scan_body(lf_ref, cum_ref, cumt_ref, carry_ref):
    @pl.when(pl.program_id(1) == 0)
    def _():
        carry_ref[...] = jnp.zeros_like(carry_ref)

    lf = lf_ref[...]
    ts = lf.shape[0]
    lane = lax.broadcasted_iota(jnp.int32, lf.shape, 1)
    lf = jnp.where(lane < N_HEADS, lf, 0.0)
    hi, mid, lo = _split3(lf)
    rr = lax.broadcasted_iota(jnp.int32, (ts, ts), 0)
    cc = lax.broadcasted_iota(jnp.int32, (ts, ts), 1)
    tri = jnp.where(cc <= rr, 1.0, 0.0).astype(BF16)
    cum = _dot(tri, hi) + _dot(tri, mid) + _dot(tri, lo) + carry_ref[0:1, :]
    carry_ref[...] = jnp.broadcast_to(cum[ts - 1:ts, :], carry_ref.shape)
    cum2 = cum * LOG2E
    cum_ref[...] = cum2
    cumt_ref[...] = cum2.T[0:N_HEADS, :]


def _scan(lf, *, TS, TK):
    b, s, _ = lf.shape
    assert TS == TK
    t_block = (None, None, N_HEADS, TK)
    t_map = lambda bi, i: (bi, i, 0, 0)
    return pl.pallas_call(
        _scan_body,
        grid=(b, s // TS),
        in_specs=[pl.BlockSpec((None, TS, LANES), lambda bi, i: (bi, i, 0))],
        out_specs=[pl.BlockSpec((None, TS, LANES), lambda bi, i: (bi, i, 0)),
                   pl.BlockSpec(t_block, t_map)],
        out_shape=[jax.ShapeDtypeStruct((b, s, LANES), F32),
                   jax.ShapeDtypeStruct((b, s // TK, N_HEADS, TK), F32)],
        scratch_shapes=[pltpu.VMEM((8, LANES), F32)],
        compiler_params=_cparams(2),
        name="scan",
    )(lf)


def _fox_body(q_ref, k_ref, v_ref, cq_ref, ck_ref, o_ref, *, TQ, TK, q_off):
    p = pl.program_id(1)
    i = pl.program_id(2)
    q = q_ref[...]
    cq_blk = cq_ref[...]
    lane = lax.broadcasted_iota(jnp.int32, (TQ, PAIR), 1)
    ones_blk = jnp.where(lax.broadcasted_iota(jnp.int32, (TK, LANES), 1) == 0, 1.0, 0.0).astype(BF16)
    q0 = q_off + i * TQ
    n_full = q0 // TK
    qpos = q0 + lax.broadcasted_iota(jnp.int32, (TQ, TK), 0)
    kcol = lax.broadcasted_iota(jnp.int32, (TQ, TK), 1)

    outs = []
    for par in range(2):
        h = 2 * p + par
        qm = jnp.where((lane >= HEAD_DIM) == (par == 1), q, jnp.zeros_like(q))
        cq_col = jnp.sum(jnp.where(lane == h, cq_blk, 0.0), axis=1, keepdims=True)
        ref0 = cq_col[0:1, :]
        cqr = cq_col - ref0

        def step(j, carry, masked):
            m, l, acc = carry
            k0 = pl.multiple_of(j * TK, TK)
            kb = k_ref[pl.ds(k0, TK), :]
            vb = v_ref[pl.ds(k0, TK), :]
            s = _dot_nt(qm, kb)
            ck = ck_ref[j, pl.ds(h, 1), :]
            u = s - (ck - ref0)
            if masked:
                u = jnp.where(k0 + kcol <= qpos, u, NEG_INF)
            m_new = jnp.maximum(m, jnp.max(u, axis=1, keepdims=True) + cqr)
            alpha = jnp.exp2(m - m_new)
            pexp = jnp.exp2(u + (cqr - m_new))
            pv = _dot(pexp.astype(BF16), jnp.concatenate([vb, ones_blk], axis=1))
            return m_new, alpha * l + pv[:, LANES:LANES + 1], alpha * acc + pv[:, :LANES]

        init = (jnp.full((TQ, 1), NEG_INF, F32), jnp.zeros((TQ, 1), F32), jnp.zeros((TQ, LANES), F32))
        carry = lax.fori_loop(0, n_full, lambda j, c: step(j, c, False), init)
        _, l, acc = step(n_full, carry, True)
        outs.append(acc / l)
    o_ref[...] = jnp.where(lane < HEAD_DIM, outs[0], outs[1]).astype(o_ref.dtype)


def _fox(q, k, v, cum, cumt, *, TQ, TK, q_off):
    b, sq, _ = q.shape
    sk = k.shape[1]
    return pl.pallas_call(
        functools.partial(_fox_body, TQ=TQ, TK=TK, q_off=q_off),
        grid=(b, N_PAIRS, sq // TQ),
        in_specs=[pl.BlockSpec((None, TQ, PAIR), lambda bi, p, i: (bi, i, p)),
                  pl.BlockSpec((None, sk, PAIR), lambda bi, p, i: (bi, 0, p)),
                  pl.BlockSpec((None, sk, PAIR), lambda bi, p, i: (bi, 0, p)),
                  pl.BlockSpec((None, TQ, LANES), lambda bi, p, i: (bi, i, 0)),
                  pl.BlockSpec((None, sk // TK, N_HEADS, TK), lambda bi, p, i: (bi, 0, 0, 0))],
        out_specs=pl.BlockSpec((None, TQ, PAIR), lambda bi, p, i: (bi, i, p)),
        out_shape=jax.ShapeDtypeStruct((b, sq, WIDTH), BF16),
        compiler_params=_cparams(3),
        name="fox",
    )(q, k, v, cum, cumt)


AUG_PIECES = 3


def _scan_t_body(lf_ref, place_ref, ct_ref, ka_ref, carry_ref):
    @pl.when(pl.program_id(1) == 0)
    def _():
        carry_ref[...] = jnp.zeros_like(carry_ref)

    lf = lf_ref[...]
    ts = lf.shape[0]
    lane = lax.broadcasted_iota(jnp.int32, lf.shape, 1)
    lf = jnp.where(lane < N_HEADS, lf, 0.0)
    hi, mid, lo = _split3(lf)
    rr = lax.broadcasted_iota(jnp.int32, (ts, ts), 0)
    cc = lax.broadcasted_iota(jnp.int32, (ts, ts), 1)
    tri = jnp.where(cc <= rr, 1.0, 0.0).astype(BF16)
    cum = _dot(tri, hi) + _dot(tri, mid) + _dot(tri, lo) + carry_ref[0:1, :]
    carry_ref[...] = jnp.broadcast_to(cum[ts - 1:ts, :], carry_ref.shape)
    cum2 = cum * LOG2E
    ct_ref[...] = cum2.T[0:N_HEADS, :]
    pieces = _split3(cum2 - cum2[0:1, :])
    ka = _dot(pieces[0], place_ref[0]) + _dot(pieces[1], place_ref[1]) + _dot(pieces[2], place_ref[2])
    ka_ref[...] = ka.astype(BF16)


def _aug_placement():
    h = jnp.arange(LANES)[:, None]
    col = jnp.arange(WIDTH)[None, :]
    mats = []
    for x in range(AUG_PIECES):
        tgt = PAIR * (h // 2) + AUG_PIECES * (h % 2) + x
        mats.append(jnp.where((h < N_HEADS) & (col == tgt), 1.0, 0.0))
    return jnp.stack(mats).astype(BF16)


def _scan_t(lf, *, T):
    b, s, _ = lf.shape
    place = _aug_placement()
    return pl.pallas_call(
        _scan_t_body,
        grid=(b, s // T),
        in_specs=[pl.BlockSpec((None, T, LANES), lambda bi, i: (bi, i, 0)),
                  pl.BlockSpec(place.shape, lambda bi, i: (0, 0, 0))],
        out_specs=[pl.BlockSpec((None, None, N_HEADS, T), lambda bi, i: (bi, i, 0, 0)),
                   pl.BlockSpec((None, T, WIDTH), lambda bi, i: (bi, i, 0))],
        out_shape=[jax.ShapeDtypeStruct((b, s // T, N_HEADS, T), F32),
                   jax.ShapeDtypeStruct((b, s, WIDTH), BF16)],
        scratch_shapes=[pltpu.VMEM((8, LANES), F32)],
        compiler_params=_cparams(2),
        name="scan_t",
    )(lf, place)


def _foxt_body(q_ref, k_ref, ka_ref, vt_ref, ct_ref, o_ref, *, T):
    p = pl.program_id(1)
    i = pl.program_id(2)
    q = q_ref[...]
    lane = lax.broadcasted_iota(jnp.int32, (T, PAIR), 1)
    halves = []
    for par in range(2):
        qm = jnp.where((lane >= HEAD_DIM) == (par == 1), q, jnp.zeros_like(q))
        lo_lane = AUG_PIECES * par
        qa = jnp.where((lane >= lo_lane) & (lane < lo_lane + AUG_PIECES), -1.0, 0.0).astype(BF16)
        halves.append(jnp.concatenate([qm, qa], axis=1))
    qcat = jnp.concatenate(halves, axis=0)
    h_even = 2 * p
    cq = jnp.concatenate([ct_ref[i, pl.ds(h_even, 1), :], ct_ref[i, pl.ds(h_even + 1, 1), :]], axis=1)
    ones_rows = jnp.ones((16, T), BF16)
    krow = lax.broadcasted_iota(jnp.int32, (T, 2 * T), 0)
    qcol = lax.broadcasted_iota(jnp.int32, (T, 2 * T), 1) % T

    def step(j, carry, masked):
        m, acc_e, acc_o = carry
        k0 = pl.multiple_of(j * T, T)
        kcat = jnp.concatenate([k_ref[pl.ds(k0, T), :], ka_ref[pl.ds(k0, T), :]], axis=1)
        st = _dot_nt(kcat, qcat)
        c0 = jnp.concatenate([jnp.broadcast_to(ct_ref[j, pl.ds(h_even, 1), :][:, 0:1], (1, T)),
                              jnp.broadcast_to(ct_ref[j, pl.ds(h_even + 1, 1), :][:, 0:1], (1, T))], axis=1)
        rb = cq - c0
        if masked:
            st = jnp.where(krow <= qcol, st, NEG_INF)
        m_new = jnp.maximum(m, jnp.max(st, axis=0, keepdims=True) + rb)
        alpha = jnp.exp2(m - m_new)
        pt = jnp.exp2(st + (rb - m_new)).astype(BF16)
        vt = vt_ref[j]
        pv_e = _dot(jnp.concatenate([vt[0:HEAD_DIM], ones_rows], axis=0), pt[:, 0:T])
        pv_o = _dot(jnp.concatenate([vt[HEAD_DIM:PAIR], ones_rows], axis=0), pt[:, T:2 * T])
        return m_new, alpha[:, 0:T] * acc_e + pv_e, alpha[:, T:2 * T] * acc_o + pv_o

    rows = HEAD_DIM + 16
    init = (jnp.full((1, 2 * T), NEG_INF, F32), jnp.zeros((rows, T), F32), jnp.zeros((rows, T), F32))
    n_pairs = i // 2
    carry = lax.fori_loop(0, n_pairs, lambda t, c: step(2 * t + 1, step(2 * t, c, False), False), init)
    carry = lax.fori_loop(2 * n_pairs, i, lambda j, c: step(j, c, False), carry)
    _, acc_e, acc_o = step(i, carry, True)
    o_t = jnp.concatenate([acc_e[0:HEAD_DIM] / acc_e[HEAD_DIM:HEAD_DIM + 1],
                           acc_o[0:HEAD_DIM] / acc_o[HEAD_DIM:HEAD_DIM + 1]], axis=0)
    o_ref[...] = o_t.T.astype(o_ref.dtype)


def _foxt(q, k, ka, vt, ct, *, T):
    b, s, _ = q.shape
    nt = s // T
    return pl.pallas_call(
        functools.partial(_foxt_body, T=T),
        grid=(b, N_PAIRS, nt),
        in_specs=[pl.BlockSpec((None, T, PAIR), lambda bi, p, i: (bi, i, p)),
                  pl.BlockSpec((None, s, PAIR), lambda bi, p, i: (bi, 0, p)),
                  pl.BlockSpec((None, s, PAIR), lambda bi, p, i: (bi, 0, p)),
                  pl.BlockSpec((None, nt, PAIR, T), lambda bi, p, i: (bi, 0, p, 0)),
                  pl.BlockSpec((None, nt, N_HEADS, T), lambda bi, p, i: (bi, 0, 0, 0))],
        out_specs=pl.BlockSpec((None, T, PAIR), lambda bi, p, i: (bi, i, p)),
        out_shape=jax.ShapeDtypeStruct((b, s, WIDTH), BF16),
        compiler_params=_cparams(3),
        name="foxt",
    )(q, k, ka, vt, ct)


def _band_body(q_ref, k_ref, v_ref, bias_ref, o_ref, *scratch, TQ, W, n_sub, padded):
    p = pl.program_id(1)
    i = pl.program_id(2)
    if padded:
        kpad_ref, vpad_ref = scratch
        s_len = k_ref.shape[0]

        @pl.when(i == 0)
        def _():
            zeros = jnp.zeros((BAND_REACH, PAIR), BF16)
            kpad_ref[pl.ds(0, BAND_REACH), :] = zeros
            vpad_ref[pl.ds(0, BAND_REACH), :] = zeros
            kpad_ref[pl.ds(BAND_REACH, s_len), :] = k_ref[...]
            vpad_ref[pl.ds(BAND_REACH, s_len), :] = v_ref[...]
    else:
        kpad_ref, vpad_ref = k_ref, v_ref

    lane = lax.broadcasted_iota(jnp.int32, (TQ, PAIR), 1)
    ones_blk = jnp.where(lax.broadcasted_iota(jnp.int32, (W, LANES), 1) == 0, 1.0, 0.0).astype(BF16)
    kcol = lax.broadcasted_iota(jnp.int32, (TQ, W), 1)

    def sub_block(sub, carry):
        r0 = pl.multiple_of(sub * TQ, TQ)
        q0 = i * (n_sub * TQ) + r0
        q = q_ref[pl.ds(r0, TQ), :]
        kw = kpad_ref[pl.ds(pl.multiple_of(q0, TQ), W), :] if padded else kpad_ref[...]
        vw = vpad_ref[pl.ds(pl.multiple_of(q0, TQ), W), :] if padded else vpad_ref[...]
        vcat = jnp.concatenate([vw, ones_blk], axis=1)
        outs = []
        for par in range(2):
            h = 2 * p + par
            qm = jnp.where((lane >= HEAD_DIM) == (par == 1), q, jnp.zeros_like(q))
            s = _dot_nt(qm, kw) + bias_ref[h]
            if padded:
                s = jnp.where(kcol >= BAND_REACH - q0, s, NEG_INF)
            m = jnp.max(s, axis=1, keepdims=True)
            pexp = jnp.exp2(s - m)
            pv = _dot(pexp.astype(BF16), vcat)
            outs.append(pv[:, :LANES] / pv[:, LANES:LANES + 1])
        o_ref[pl.ds(r0, TQ), :] = jnp.where(lane < HEAD_DIM, outs[0], outs[1]).astype(o_ref.dtype)
        return carry

    lax.fori_loop(0, n_sub, sub_block, 0)


def _band(q, k, v, bias, *, TQ, n_sub, padded):
    b, sq, _ = q.shape
    sk = k.shape[1]
    w = BAND_REACH + TQ if padded else sk
    tqb = TQ * n_sub
    scratch = [pltpu.VMEM((sk + BAND_REACH, PAIR), BF16)] * 2 if padded else []
    return pl.pallas_call(
        functools.partial(_band_body, TQ=TQ, W=w, n_sub=n_sub, padded=padded),
        grid=(b, N_PAIRS, sq // tqb),
        in_specs=[pl.BlockSpec((None, tqb, PAIR), lambda bi, p, i: (bi, i, p)),
                  pl.BlockSpec((None, sk, PAIR), lambda bi, p, i: (bi, 0, p)),
                  pl.BlockSpec((None, sk, PAIR), lambda bi, p, i: (bi, 0, p)),
                  pl.BlockSpec(bias.shape, lambda bi, p, i: (0, 0, 0))],
        out_specs=pl.BlockSpec((None, tqb, PAIR), lambda bi, p, i: (bi, i, p)),
        out_shape=jax.ShapeDtypeStruct((b, sq, WIDTH), BF16),
        scratch_shapes=scratch,
        compiler_params=_cparams(3),
        name="band",
    )(q, k, v, bias)


BAND_TQ = 128
BAND_STEP = 512


def _bandt_body(q_ref, kp_ref, kc_ref, vp_ref, vc_ref, bias_ref, o_ref):
    i = pl.program_id(2)
    w = BAND_REACH + BAND_TQ
    k2 = jnp.concatenate([kp_ref[...], kc_ref[...]], axis=0)
    vt2 = jnp.concatenate([vp_ref[...], vc_ref[...]], axis=1)
    lane = lax.broadcasted_iota(jnp.int32, (BAND_TQ, PAIR), 1)
    ones_rows = jnp.ones((16, w), BF16)
    krow = lax.broadcasted_iota(jnp.int32, (w, 2 * BAND_TQ), 0)
    bias = bias_ref[...]
    for sub in range(BAND_STEP // BAND_TQ):
        r0 = sub * BAND_TQ
        q = q_ref[r0:r0 + BAND_TQ, :]
        qcat = jnp.concatenate([jnp.where(lane < HEAD_DIM, q, jnp.zeros_like(q)),
                                jnp.where(lane >= HEAD_DIM, q, jnp.zeros_like(q))], axis=0)
        st = _dot_nt(k2[r0:r0 + w], qcat) + bias
        st = jnp.where(krow >= (1 - i) * BAND_STEP - r0, st, NEG_INF)
        m = jnp.max(st, axis=0, keepdims=True)
        pt = jnp.exp2(st - m).astype(BF16)
        vwin = vt2[:, r0:r0 + w]
        outs = []
        for par in range(2):
            vcat = jnp.concatenate([vwin[par * HEAD_DIM:(par + 1) * HEAD_DIM], ones_rows], axis=0)
            pv = _dot(vcat, pt[:, par * BAND_TQ:(par + 1) * BAND_TQ])
            outs.append(pv[0:HEAD_DIM] / pv[HEAD_DIM:HEAD_DIM + 1])
        o_ref[r0:r0 + BAND_TQ, :] = jnp.concatenate(outs, axis=0).T.astype(o_ref.dtype)


def _bandt(q, k, vt, bias_t):
    b, s, _ = q.shape
    prev = lambda i: jnp.maximum(i - 1, 0)
    return pl.pallas_call(
        _bandt_body,
        grid=(b, N_PAIRS, s // BAND_STEP),
        in_specs=[pl.BlockSpec((None, BAND_STEP, PAIR), lambda bi, p, i: (bi, i, p)),
                  pl.BlockSpec((None, BAND_STEP, PAIR), lambda bi, p, i: (bi, prev(i), p)),
                  pl.BlockSpec((None, BAND_STEP, PAIR), lambda bi, p, i: (bi, i, p)),
                  pl.BlockSpec((None, None, PAIR, BAND_STEP), lambda bi, p, i: (bi, prev(i), p, 0)),
                  pl.BlockSpec((None, None, PAIR, BAND_STEP), lambda bi, p, i: (bi, i, p, 0)),
                  pl.BlockSpec((None,) + bias_t.shape[1:], lambda bi, p, i: (p, 0, 0))],
        out_specs=pl.BlockSpec((None, BAND_STEP, PAIR), lambda bi, p, i: (bi, i, p)),
        out_shape=jax.ShapeDtypeStruct((b, s, WIDTH), BF16),
        compiler_params=_cparams(3),
        name="bandt",
    )(q, k, k, vt, vt, bias_t)


def _band_bias_tile_t(rel_bias):
    tile = _band_bias_tile(rel_bias, BAND_TQ, BAND_REACH + BAND_TQ)
    t = jnp.swapaxes(tile, 1, 2)
    return jnp.concatenate([t[0::2], t[1::2]], axis=2)


def _band_bias_tile(rel_bias, tq, w):
    span = w + tq - 1
    period = span + 1
    v = np.arange(period)
    d = np.where(v < w, v, v - period)
    table_idx = np.clip(BAND_REACH - d, -REL_CLIP, REL_CLIP) + REL_CLIP
    table = rel_bias[:, table_idx] * LOG2E
    n_h = rel_bias.shape[0]
    vals = jnp.tile(table, (1, tq))[:, :tq * span].reshape(n_h, tq, span)[:, :, :w]
    r = np.arange(tq)[:, None]
    c = np.arange(w)[None, :]
    in_band = (c // CHUNK >= r // CHUNK) & (c // CHUNK <= r // CHUNK + BAND_REACH // CHUNK)
    return jnp.where(jnp.asarray(in_band)[None], vals, NEG_INF).astype(F32)


def _first_index(is_max, idx, axis, big):
    return jnp.min(jnp.where(is_max, idx, big), axis=axis, keepdims=True)


def _route(scores, choice):
    t = scores.shape[1]
    c3 = choice.reshape(N_GROUPS, GROUP_SIZE, t)
    j_idx = lax.broadcasted_iota(jnp.int32, c3.shape, 1)
    top1 = jnp.max(c3, axis=1, keepdims=True)
    first = _first_index(c3 == top1, j_idx, 1, GROUP_SIZE)
    top2 = jnp.max(jnp.where(j_idx == first, -jnp.inf, c3), axis=1, keepdims=True)
    gscore = (top1 + top2).reshape(N_GROUPS, t)

    g_idx = lax.broadcasted_iota(jnp.int32, gscore.shape, 0)
    gsel = jnp.zeros(gscore.shape, F32)
    work = gscore
    for _ in range(TOPK_GROUPS):
        gm = jnp.max(work, axis=0, keepdims=True)
        pick = g_idx == _first_index(work == gm, g_idx, 0, N_GROUPS)
        gsel = jnp.where(pick, 1.0, gsel)
        work = jnp.where(pick, -jnp.inf, work)

    emask = jnp.broadcast_to(gsel.reshape(N_GROUPS, 1, t), c3.shape) > 0.0
    work = jnp.where(emask, c3, NEG_INF)
    e_idx = lax.broadcasted_iota(jnp.int32, c3.shape, 0) * GROUP_SIZE + j_idx
    esel = jnp.zeros(c3.shape, F32)
    for _ in range(TOP_K):
        em = jnp.max(jnp.max(work, axis=1, keepdims=True), axis=0, keepdims=True)
        cand = jnp.where(work == em, e_idx, N_EXPERTS)
        first = jnp.min(jnp.min(cand, axis=1, keepdims=True), axis=0, keepdims=True)
        pick = e_idx == first
        esel = jnp.where(pick, 1.0, esel)
        work = jnp.where(pick, -jnp.inf, work)

    w = esel * scores.reshape(c3.shape)
    denom = jnp.sum(jnp.sum(w, axis=1, keepdims=True), axis=0, keepdims=True)
    return (w / denom * ROUTED_SCALE).reshape(N_EXPERTS, t)


def _merge_body(of_ref, ob_ref, x_ref, gate_ref, sh_ref, sc_ref, ogf_ref, ogb_ref, wo_ref, g2_ref,
                wrh_ref, wrl_ref, br_ref, x1_ref, h2_ref, comb_ref, combt_ref, cnt_ref):
    def group_norm(t_ref, gain_ref):
        t = t_ref[...].astype(F32)
        ms = jnp.mean(t * t, axis=-1, keepdims=True)
        return (t * lax.rsqrt(ms + EPS) * gain_ref[...]).astype(BF16)

    y = _dot(group_norm(of_ref, ogf_ref), wo_ref[0:WIDTH, :]) + _dot(group_norm(ob_ref, ogb_ref), wo_ref[WIDTH:, :])
    x = x_ref[...]
    g, r, d = x.shape
    x1 = x + gate_ref[...] * y.reshape(g, r, d)
    x1_ref[...] = x1
    ms = jnp.mean(x1 * x1, axis=-1, keepdims=True)
    h2 = (x1 * lax.rsqrt(ms + EPS) * g2_ref[...] * (1.0 + sc_ref[...]) + sh_ref[...]).reshape(g * r, d)
    h_hi, h_lo = _split2(h2)
    h2_ref[...] = h_hi
    logits = _dot_nt(wrh_ref[...], h_hi) + _dot_nt(wrh_ref[...], h_lo) + _dot_nt(wrl_ref[...], h_hi)
    scores = jax.nn.sigmoid(logits)
    t = scores.shape[1]
    bias = jnp.concatenate([br_ref[...]] * (t // LANES), axis=1)
    comb = _route(scores, scores + bias)
    comb_pad = jnp.concatenate([comb, jnp.zeros((LANES - N_EXPERTS, t), F32)], axis=0)
    comb_ref[...] = comb_pad.T
    combt_ref[...] = comb
    cnt = jnp.sum(jnp.where(comb > 0.0, 1.0, 0.0), axis=1, keepdims=True)
    cnt_ref[...] = jnp.broadcast_to(cnt, (N_EXPERTS, LANES))


def _merge(of, ob, x, gate, shift, scale, ogf, ogb, wo, g2, wr_hi, wr_lo, br, *, G, R):
    nb, s, d = x.shape
    n = nb * s
    tm = G * R
    nbi, nsi = nb // G, s // R
    row = lambda b, i: (b * nsi + i, 0)
    const = lambda b, i: (0, 0)
    mod_spec = pl.BlockSpec((G, 1, d), lambda b, i: (b, 0, 0))
    x_spec = pl.BlockSpec((G, R, d), lambda b, i: (b, i, 0))
    return pl.pallas_call(
        _merge_body,
        grid=(nbi, nsi),
        in_specs=[pl.BlockSpec((tm, WIDTH), row), pl.BlockSpec((tm, WIDTH), row), x_spec,
                  mod_spec, mod_spec, mod_spec,
                  pl.BlockSpec((1, WIDTH), const), pl.BlockSpec((1, WIDTH), const),
                  pl.BlockSpec(wo.shape, const), pl.BlockSpec((1, d), const),
                  pl.BlockSpec(wr_hi.shape, const), pl.BlockSpec(wr_lo.shape, const),
                  pl.BlockSpec(br.shape, const)],
        out_specs=[x_spec, pl.BlockSpec((tm, d), row), pl.BlockSpec((tm, LANES), row),
                   pl.BlockSpec((N_EXPERTS, tm), lambda b, i: (0, b * nsi + i)),
                   pl.BlockSpec((None, N_EXPERTS, LANES), lambda b, i: (b * nsi + i, 0, 0))],
        out_shape=[jax.ShapeDtypeStruct((nb, s, d), F32), jax.ShapeDtypeStruct((n, d), BF16),
                   jax.ShapeDtypeStruct((n, LANES), F32), jax.ShapeDtypeStruct((N_EXPERTS, n), F32),
                   jax.ShapeDtypeStruct((n // tm, N_EXPERTS, LANES), F32)],
        compiler_params=_cparams(2),
        name="merge",
    )(of, ob, x, gate, shift, scale, ogf, ogb, wo, g2, wr_hi, wr_lo, br)


def _silu(g):
    return g * jax.nn.sigmoid(g)


def _moe_body(h_ref, comb_ref, x1_ref, gate_ref, wg_ref, wu_ref, wd_ref, sg_ref, su_ref, sd_ref, y_ref, acc_ref):
    e = pl.program_id(2)
    hb = h_ref[...]

    @pl.when(e == 0)
    def _():
        a = _silu(_dot(hb, sg_ref[...])) * _dot(hb, su_ref[...])
        acc_ref[...] = _dot(a.astype(BF16), sd_ref[...])

    comb = comb_ref[...]
    lane = lax.broadcasted_iota(jnp.int32, comb.shape, 1)
    c_e = jnp.sum(jnp.where(lane == e, comb, 0.0), axis=1, keepdims=True)
    a = _silu(_dot(hb, wg_ref[...])) * _dot(hb, wu_ref[...]) * c_e
    acc_ref[...] += _dot(a.astype(BF16), wd_ref[...])

    @pl.when(e == pl.num_programs(2) - 1)
    def _():
        x1 = x1_ref[...]
        g, r, d = x1.shape
        y_ref[...] = x1 + gate_ref[...] * acc_ref[...].reshape(g, r, d)


def _moe(h2, comb, x1, gate, wg, wu, wd, sg, su, sd, *, G, R):
    nb, s, d = x1.shape
    tm = G * R
    nbi, nsi = nb // G, s // R
    ff = wg.shape[2]
    row = lambda b, i, e: (b * nsi + i, 0)
    const = lambda b, i, e: (0, 0)
    x_spec = pl.BlockSpec((G, R, d), lambda b, i, e: (b, i, 0))
    return pl.pallas_call(
        _moe_body,
        grid=(nbi, nsi, N_EXPERTS),
        in_specs=[pl.BlockSpec((tm, d), row), pl.BlockSpec((tm, LANES), row), x_spec,
                  pl.BlockSpec((G, 1, d), lambda b, i, e: (b, 0, 0)),
                  pl.BlockSpec((None, d, ff), lambda b, i, e: (e, 0, 0)),
                  pl.BlockSpec((None, d, ff), lambda b, i, e: (e, 0, 0)),
                  pl.BlockSpec((None, ff, d), lambda b, i, e: (e, 0, 0)),
                  pl.BlockSpec(sg.shape, const), pl.BlockSpec(su.shape, const), pl.BlockSpec(sd.shape, const)],
        out_specs=x_spec,
        out_shape=jax.ShapeDtypeStruct((nb, s, d), F32),
        scratch_shapes=[pltpu.VMEM((tm, d), F32)],
        compiler_params=_cparams(3),
        name="moe",
    )(h2, comb, x1, gate, wg, wu, wd, sg, su, sd)


MOE_TM = 512
MOE_CH = 16
MOE_SLOTS = TOP_K * MOE_TM + N_EXPERTS * MOE_CH
MOE_NCHUNK = MOE_SLOTS // MOE_CH
MOE_PIECE = 512
MOE_RB = 512
TAU_RADIX = 64.0


def _moe_plan(cnt, n_tiles):
    pc = (cnt + MOE_CH - 1) // MOE_CH * MOE_CH
    off = jnp.cumsum(pc, axis=1) - pc
    end = off + pc
    n_used = (jnp.sum(pc, axis=1) // MOE_CH).astype(jnp.int32)
    tot = jnp.sum(pc, axis=0)
    reg = (tot + MOE_RB - 1) // MOE_RB * MOE_RB
    reg_end = jnp.cumsum(reg)
    reg_start = reg_end - reg
    dest_base = reg_start[None, :] + jnp.cumsum(pc, axis=0) - pc
    chunk_row = jnp.arange(MOE_NCHUNK, dtype=jnp.int32)[None, :] * MOE_CH
    in_group = (chunk_row[:, :, None] >= off[:, None, :]) & (chunk_row[:, :, None] < end[:, None, :])
    cdest = (chunk_row + jnp.sum(jnp.where(in_group, (dest_base - off)[:, None, :], 0), axis=2)).astype(jnp.int32)
    r_max = (n_tiles * MOE_SLOTS + N_EXPERTS * MOE_RB) // MOE_RB
    n_active = (reg_end[-1] // MOE_RB).astype(jnp.int32).reshape(1)
    tile_row = jnp.arange(r_max, dtype=jnp.int32) * MOE_RB
    tile_expert = jnp.minimum(jnp.sum((tile_row[:, None] >= reg_end[None, :]).astype(jnp.int32), axis=1),
                              N_EXPERTS - 1).astype(jnp.int32)
    in_region = (tile_row[:, None] >= reg_start[None, :]) & (tile_row[:, None] < reg_end[None, :])
    rows_end = jnp.sum(jnp.where(in_region, (reg_start + tot)[None, :], 0), axis=1)
    valid = jnp.clip(rows_end - tile_row, 0, MOE_RB).astype(jnp.int32)
    f = lambda a: a.astype(F32)
    zeros64 = jnp.zeros((n_tiles, N_EXPERTS), F32)
    row2 = lambda a: jnp.broadcast_to(jnp.concatenate([f(a), f(a)], axis=1)[:, None, :], (n_tiles, 8, LANES))
    col = lambda a: jnp.broadcast_to(f(a)[:, :, None], (n_tiles, N_EXPERTS, LANES))
    col128 = lambda a: jnp.broadcast_to(jnp.concatenate([f(a), zeros64], axis=1)[:, :, None], (n_tiles, LANES, LANES))
    row1 = lambda a: jnp.broadcast_to(jnp.concatenate([f(a), zeros64], axis=1)[:, None, :], (n_tiles, 8, LANES))
    return dict(n_used=n_used, cdest=cdest.reshape(n_tiles, 1, MOE_NCHUNK), r_max=r_max, n_active=n_active,
                tile_expert=tile_expert, valid=valid,
                off_row2=row2(off), end_row2=row2(end), off_col=col(off),
                off_row1=row1(off), off_col128=col128(off), end_col128=col128(end))


def _tau_pieces(sel, tau):
    tau = jnp.where(sel, tau, -1.0)
    hi = jnp.floor(tau * (1.0 / TAU_RADIX)) * TAU_RADIX
    return hi.astype(BF16), (tau - hi).astype(BF16)


def _dispatch_body(nused_ref, cdest_ref, h_ref, combt_ref, offcol_ref, offrow_ref, endrow_ref, sorted_ref,
                   buf_ref, sem):
    t = pl.program_id(0)
    n_used = nused_ref[t]
    tm = h_ref.shape[0]
    sel = combt_ref[...] > 0.0
    rr = lax.broadcasted_iota(jnp.int32, (tm, tm), 0)
    cc = lax.broadcasted_iota(jnp.int32, (tm, tm), 1)
    upper = jnp.where(rr < cc, 1.0, 0.0).astype(BF16)
    rank = _dot(jnp.where(sel, 1.0, 0.0).astype(BF16), upper)
    cols = jnp.concatenate([offcol_ref[...]] * (tm // LANES), axis=1)
    tau_hi, tau_lo = _tau_pieces(sel, cols + rank)
    taucat = jnp.concatenate([tau_hi, tau_lo], axis=0)
    off_row = offrow_ref[0:1, :]
    end_row = endrow_ref[0:1, :]
    hb = h_ref[...]

    for piece in range(MOE_SLOTS // MOE_PIECE):
        @pl.when(piece * (MOE_PIECE // MOE_CH) < n_used)
        def _():
            base = piece * MOE_PIECE
            s_col = (base + lax.broadcasted_iota(jnp.int32, (MOE_PIECE, LANES), 0)).astype(F32)
            onehot = jnp.where((s_col >= off_row) & (s_col < end_row), 1.0, 0.0).astype(BF16)
            q = _dot(onehot, taucat)
            s_mat = (base + lax.broadcasted_iota(jnp.int32, (MOE_PIECE, tm), 0)).astype(F32)
            g = jnp.where(q == s_mat, 1.0, 0.0).astype(BF16)
            buf_ref[pl.ds(base, MOE_PIECE), :] = _dot(g, hb).astype(BF16)

    def chunk_copy(c):
        src = buf_ref.at[pl.ds(pl.multiple_of(c * MOE_CH, MOE_CH), MOE_CH), :]
        dst = sorted_ref.at[pl.ds(pl.multiple_of(cdest_ref[0, c], MOE_CH), MOE_CH), :]
        return pltpu.make_async_copy(src, dst, sem)

    def issue(c, carry):
        chunk_copy(c).start()
        return carry

    def drain(c, carry):
        chunk_copy(c).wait()
        return carry

    lax.fori_loop(0, n_used, issue, 0)
    lax.fori_loop(0, n_used, drain, 0)


def _dispatch(h2, combt, plan, n_tiles):
    n, d = h2.shape
    r_total = plan["r_max"] * MOE_RB
    grid_spec = pltpu.PrefetchScalarGridSpec(
        num_scalar_prefetch=1,
        grid=(n_tiles,),
        in_specs=[pl.BlockSpec((None, 1, MOE_NCHUNK), lambda t, nu: (t, 0, 0), memory_space=pltpu.SMEM),
                  pl.BlockSpec((MOE_TM, d), lambda t, nu: (t, 0)),
                  pl.BlockSpec((N_EXPERTS, MOE_TM), lambda t, nu: (0, t)),
                  pl.BlockSpec((None, N_EXPERTS, LANES), lambda t, nu: (t, 0, 0)),
                  pl.BlockSpec((None, 8, LANES), lambda t, nu: (t, 0, 0)),
                  pl.BlockSpec((None, 8, LANES), lambda t, nu: (t, 0, 0))],
        out_specs=pl.BlockSpec(memory_space=pl.ANY),
        scratch_shapes=[pltpu.VMEM((MOE_SLOTS, d), BF16), pltpu.SemaphoreType.DMA(())],
    )
    return pl.pallas_call(
        _dispatch_body,
        grid_spec=grid_spec,
        out_shape=jax.ShapeDtypeStruct((r_total, d), BF16),
        compiler_params=_cparams(1),
        name="moe_dispatch",
    )(plan["n_used"], plan["cdest"], h2, combt, plan["off_col"], plan["off_row2"], plan["end_row2"])


def _ffn_body(texp_ref, nact_ref, valid_ref, x_ref, wg_ref, wu_ref, wd_ref, o_ref):
    r = pl.program_id(0)

    @pl.when(r < nact_ref[0])
    def _():
        x = x_ref[...]
        rows = lax.broadcasted_iota(jnp.int32, x.shape, 0)
        x = jnp.where(rows < valid_ref[r], x, jnp.zeros_like(x))
        a = _silu(_dot(x, wg_ref[...])) * _dot(x, wu_ref[...])
        o_ref[...] = _dot(a.astype(BF16), wd_ref[...]).astype(o_ref.dtype)

    @pl.when(r >= nact_ref[0])
    def _():
        o_ref[...] = jnp.zeros_like(o_ref)


def _ffn(xs, wg, wu, wd, plan):
    r_total, d = xs.shape
    ff = wg.shape[2]
    last = lambda r, te, na, va: (jnp.minimum(r, na[0] - 1), 0)
    wmap = lambda r, te, na, va: (te[r], 0, 0)
    grid_spec = pltpu.PrefetchScalarGridSpec(
        num_scalar_prefetch=3,
        grid=(plan["r_max"],),
        in_specs=[pl.BlockSpec((MOE_RB, d), last),
                  pl.BlockSpec((None, d, ff), wmap), pl.BlockSpec((None, d, ff), wmap),
                  pl.BlockSpec((None, ff, d), wmap)],
        out_specs=pl.BlockSpec((MOE_RB, d), lambda r, te, na, va: (r, 0)),
    )
    return pl.pallas_call(
        _ffn_body,
        grid_spec=grid_spec,
        out_shape=jax.ShapeDtypeStruct((r_total, d), BF16),
        compiler_params=_cparams(1),
        name="moe_ffn",
    )(plan["tile_expert"], plan["n_active"], plan["valid"], xs, wg, wu, wd)


def _combine_body(nused_ref, cdest_ref, h_ref, comb_ref, x1_ref, gate_ref, offrow_ref, offcol_ref, endcol_ref,
                  sg_ref, su_ref, sd_ref, ys_ref, y_ref, buf_ref, sem):
    t = pl.program_id(0)
    n_used = nused_ref[t]
    tm = h_ref.shape[0]

    @pl.when(t == 0)
    def _():
        buf_ref[...] = jnp.zeros_like(buf_ref)

    def chunk_copy(c):
        src = ys_ref.at[pl.ds(pl.multiple_of(cdest_ref[0, c], MOE_CH), MOE_CH), :]
        dst = buf_ref.at[pl.ds(pl.multiple_of(c * MOE_CH, MOE_CH), MOE_CH), :]
        return pltpu.make_async_copy(src, dst, sem)

    def issue(c, carry):
        chunk_copy(c).start()
        return carry

    def drain(c, carry):
        chunk_copy(c).wait()
        return carry

    lax.fori_loop(0, n_used, issue, 0)

    hb = h_ref[...]
    acc = _dot((_silu(_dot(hb, sg_ref[...])) * _dot(hb, su_ref[...])).astype(BF16), sd_ref[...])
    comb = comb_ref[...]
    sel = comb > 0.0
    rr = lax.broadcasted_iota(jnp.int32, (tm, tm), 0)
    cc = lax.broadcasted_iota(jnp.int32, (tm, tm), 1)
    lower = jnp.where(cc < rr, 1.0, 0.0).astype(BF16)
    rank = _dot(lower, jnp.where(sel, 1.0, 0.0).astype(BF16))
    tau_hi, tau_lo = _tau_pieces(sel, offrow_ref[0:1, :] + rank)
    taucat = jnp.concatenate([tau_hi, tau_lo], axis=1)
    c_hi, c_lo = _split2(comb)
    off_col = offcol_ref[...]
    end_col = endcol_ref[...]

    lax.fori_loop(0, n_used, drain, 0)

    for piece in range(MOE_SLOTS // MOE_PIECE):
        base = piece * MOE_PIECE
        s_row = (base + lax.broadcasted_iota(jnp.int32, (LANES, MOE_PIECE), 1)).astype(F32)
        off_b = jnp.concatenate([off_col] * (MOE_PIECE // LANES), axis=1)
        end_b = jnp.concatenate([end_col] * (MOE_PIECE // LANES), axis=1)
        onehot = jnp.where((s_row >= off_b) & (s_row < end_b), 1.0, 0.0).astype(BF16)
        q = _dot(taucat, jnp.concatenate([onehot, onehot], axis=0))
        w = _dot(c_hi, onehot) + _dot(c_lo, onehot)
        s_mat = (base + lax.broadcasted_iota(jnp.int32, (tm, MOE_PIECE), 1)).astype(F32)
        gw = jnp.where(q == s_mat, w, 0.0).astype(BF16)
        acc = acc + _dot(gw, buf_ref[pl.ds(base, MOE_PIECE), :])
    y_ref[...] = x1_ref[...] + gate_ref[...] * acc


def _combine(h2, comb, x1, gate, ys, sg, su, sd, plan, n_tiles):
    nb, s, d = x1.shape
    n = nb * s
    per_b = s // MOE_TM
    x1f = x1.reshape(n, d)
    const2 = lambda t, nu: (0, 0)
    grid_spec = pltpu.PrefetchScalarGridSpec(
        num_scalar_prefetch=1,
        grid=(n_tiles,),
        in_specs=[pl.BlockSpec((None, 1, MOE_NCHUNK), lambda t, nu: (t, 0, 0), memory_space=pltpu.SMEM),
                  pl.BlockSpec((MOE_TM, d), lambda t, nu: (t, 0)),
                  pl.BlockSpec((MOE_TM, LANES), lambda t, nu: (t, 0)),
                  pl.BlockSpec((MOE_TM, d), lambda t, nu: (t, 0)),
                  pl.BlockSpec((None, 1, d), lambda t, nu: (t // per_b, 0, 0)),
                  pl.BlockSpec((None, 8, LANES), lambda t, nu: (t, 0, 0)),
                  pl.BlockSpec((None, LANES, LANES), lambda t, nu: (t, 0, 0)),
                  pl.BlockSpec((None, LANES, LANES), lambda t, nu: (t, 0, 0)),
                  pl.BlockSpec(sg.shape, const2), pl.BlockSpec(su.shape, const2), pl.BlockSpec(sd.shape, const2),
                  pl.BlockSpec(memory_space=pl.ANY)],
        out_specs=pl.BlockSpec((MOE_TM, d), lambda t, nu: (t, 0)),
        scratch_shapes=[pltpu.VMEM((MOE_SLOTS, d), BF16), pltpu.SemaphoreType.DMA(())],
    )
    y = pl.pallas_call(
        _combine_body,
        grid_spec=grid_spec,
        out_shape=jax.ShapeDtypeStruct((n, d), F32),
        compiler_params=_cparams(1),
        name="moe_combine",
    )(plan["n_used"], plan["cdest"], h2, comb, x1f, gate, plan["off_row1"], plan["off_col128"], plan["end_col128"],
      sg, su, sd, ys)
    return y.reshape(nb, s, d)


def _moe_sparse(h2, comb, combt, cnt, x1, gate, wg, wu, wd, sg, su, sd):
    n = h2.shape[0]
    n_tiles = n // MOE_TM
    plan = _moe_plan(cnt[:, :, 0].astype(jnp.int32), n_tiles)
    xs = _dispatch(h2, combt, plan, n_tiles)
    ys = _ffn(xs, wg, wu, wd, plan)
    return _combine(h2, comb, x1, gate, ys, sg, su, sd, plan, n_tiles)


def _tile_heads(g, mult=1.0):
    return (jnp.tile(g.astype(F32), N_HEADS) * mult).reshape(1, WIDTH)


def kernel(x_prompt, x_sample, cache_fox_k, cache_fox_v, cache_fox_logf, cache_band_k, cache_band_v, c_prompt, c_sample, w_ada, b_ada, norm1_g, norm2_g, w_in, b_forget, g_q_fox, g_k_fox, g_q_band, g_k_band, rel_bias, out_g_fox, out_g_band, w_out, w_router, b_router, w_gate, w_up, w_down, ws_gate, ws_up, ws_down):
    depth = w_ada.shape[0]
    assert depth == 1
    bsz, seq, d = x_prompt.shape
    dbs, dseq, _ = x_sample.shape
    past = cache_fox_k.shape[2]
    n_cache = cache_band_k.shape[2]
    assert n_cache == BAND_REACH and dseq == CHUNK and seq % BAND_REACH == 0

    wi = w_in[0]
    cols = [wi[:, 0:512], wi[:, 512:1024], wi[:, 1024:1536], wi[:, 1544:2056], wi[:, 2056:2568], wi[:, 2568:3080],
            wi[:, 1536:1544], jnp.zeros((d, LANES - N_HEADS), F32)]
    w_all = jnp.concatenate(cols, axis=1).astype(BF16)
    hd = jnp.arange(WIDTH) // HEAD_DIM
    bd = jnp.where(hd[:, None] == hd[None, :], 1.0 / HEAD_DIM, 0.0).astype(BF16)
    qscale = ATTN_SCALE * LOG2E
    gqf, gkf = _tile_heads(g_q_fox[0], qscale), _tile_heads(g_k_fox[0])
    gqb, gkb = _tile_heads(g_q_band[0], qscale), _tile_heads(g_k_band[0])
    bf_row = jnp.concatenate([b_forget[0], jnp.zeros((LANES - N_HEADS,), F32)]).reshape(1, LANES)
    g1 = norm1_g[0].reshape(1, d)
    g2 = norm2_g[0].reshape(1, d)
    ogf = out_g_fox[0].reshape(1, WIDTH)
    ogb = out_g_band[0].reshape(1, WIDTH)
    wo = w_out[0].astype(BF16)
    wr_t = w_router[0].T
    wr_hi = wr_t.astype(BF16)
    wr_lo = (wr_t - wr_hi.astype(F32)).astype(BF16)
    br = jnp.broadcast_to(b_router[0].reshape(N_EXPERTS, 1), (N_EXPERTS, LANES)).astype(F32)
    wg, wu, wd = w_gate[0].astype(BF16), w_up[0].astype(BF16), w_down[0].astype(BF16)
    sg, su, sd = ws_gate[0].astype(BF16), ws_up[0].astype(BF16), ws_down[0].astype(BF16)
    wvt = jnp.concatenate([wi[:, 2 * WIDTH:3 * WIDTH], wi[:, 2568:3080]], axis=1).T.astype(BF16)

    n_c = bsz + dbs
    rows = -(-n_c // 8) * 8
    c_all = jnp.concatenate([c_prompt, c_sample, jnp.zeros((rows - n_c, d), F32)], axis=0)
    mod = _ada(c_all, w_ada[0], b_ada[0].reshape(1, -1))

    def mods(lo, hi):
        return [mod[lo:hi, j * d:(j + 1) * d].reshape(hi - lo, 1, d) for j in range(6)]

    shift1_p, scale1_p, gate1_p, shift2_p, scale2_p, gate2_p = mods(0, bsz)
    shift1_s, scale1_s, gate1_s, shift2_s, scale2_s, gate2_s = mods(bsz, n_c)

    TM = BAND_REACH
    (qf, kf, vft, kf32, vf32, lf, qb, kb, vbt, kb32, vb32) = _proj(
        x_prompt, shift1_p, scale1_p, g1, w_all, bd, gqf, gkf, gqb, gkb, bf_row, wvt, G=1, R=TM, band_last_only=True)
    r3 = lambda a: a.reshape(bsz, seq, a.shape[-1])
    ct, ka = _scan_t(r3(lf), T=TM)
    of = _foxt(r3(qf), r3(kf), ka, vft, ct, T=TM)
    assert TM == BAND_STEP
    ob = _bandt(r3(qb), r3(kb), vbt, _band_bias_tile_t(rel_bias[0]))
    assert TM == MOE_TM
    x1_p, h2_p, comb_p, combt_p, cnt_p = _merge(of.reshape(-1, WIDTH), ob.reshape(-1, WIDTH), x_prompt, gate1_p,
                                                shift2_p, scale2_p, ogf, ogb, wo, g2, wr_hi, wr_lo, br, G=1, R=TM)
    y_p = _moe_sparse(h2_p, comb_p, combt_p, cnt_p, x1_p, gate2_p, wg, wu, wd, sg, su, sd)

    GS = 8
    (qf_s, kf_s, vf_s, kf32_s, vf32_s, lf_s, qb_s, kb_s, vb_s, kb32_s, vb32_s) = _proj(
        x_sample, shift1_s, scale1_s, g1, w_all, bd, gqf, gkf, gqb, gkb, bf_row, wvt, G=GS, R=dseq,
        band_last_only=False)
    s3 = lambda a: a.reshape(dbs, dseq, a.shape[-1])
    sk = past + dseq
    skp = -(-sk // LANES) * LANES
    pad_k = skp - sk
    lf_cache = jnp.pad(cache_fox_logf[0], ((0, 0), (0, 0), (0, LANES - N_HEADS)))
    lf_all = jnp.concatenate([lf_cache, s3(lf_s), jnp.zeros((dbs, pad_k, LANES), F32)], axis=1)
    cum_s, cumt_s = _scan(lf_all, TS=skp, TK=skp)
    zpad = jnp.zeros((dbs, pad_k, WIDTH), BF16)
    k_all = jnp.concatenate([cache_fox_k[0].reshape(dbs, past, WIDTH).astype(BF16), s3(kf_s), zpad], axis=1)
    v_all = jnp.concatenate([cache_fox_v[0].reshape(dbs, past, WIDTH).astype(BF16), s3(vf_s), zpad], axis=1)
    of_s = _fox(s3(qf_s), k_all, v_all, cum_s[:, past:past + dseq], cumt_s, TQ=dseq, TK=skp, q_off=past)
    bias_s = _band_bias_tile(rel_bias[0], dseq, BAND_REACH + LANES)
    zb = jnp.zeros((dbs, LANES - dseq, WIDTH), BF16)
    kb_all = jnp.concatenate([cache_band_k[0].reshape(dbs, n_cache, WIDTH).astype(BF16), s3(kb_s), zb], axis=1)
    vb_all = jnp.concatenate([cache_band_v[0].reshape(dbs, n_cache, WIDTH).astype(BF16), s3(vb_s), zb], axis=1)
    ob_s = _band(s3(qb_s), kb_all, vb_all, bias_s, TQ=dseq, n_sub=1, padded=False)
    x1_s, h2_s, comb_s, _, _ = _merge(of_s.reshape(-1, WIDTH), ob_s.reshape(-1, WIDTH), x_sample, gate1_s, shift2_s,
                                      scale2_s, ogf, ogb, wo, g2, wr_hi, wr_lo, br, G=GS, R=dseq)
    y_s = _moe(h2_s, comb_s, x1_s, gate2_s, wg, wu, wd, sg, su, sd, G=dbs, R=dseq)

    hshape = (N_HEADS, HEAD_DIM)
    new_bk_s = jnp.concatenate([cache_band_k[0], s3(kb32_s).reshape(dbs, dseq, *hshape)], axis=1)[:, -n_cache:]
    new_bv_s = jnp.concatenate([cache_band_v[0], s3(vb32_s).reshape(dbs, dseq, *hshape)], axis=1)[:, -n_cache:]
    return (y_p, y_s,
            kf32.reshape(1, bsz, seq, *hshape), vf32.reshape(1, bsz, seq, *hshape),
            lf[:, :N_HEADS].reshape(1, bsz, seq, N_HEADS),
            kb32.reshape(1, bsz, BAND_REACH, *hshape), vb32.reshape(1, bsz, BAND_REACH, *hshape),
            kf32_s.reshape(1, dbs, dseq, *hshape), vf32_s.reshape(1, dbs, dseq, *hshape),
            lf_s[:, :N_HEADS].reshape(1, dbs, dseq, N_HEADS),
            new_bk_s[None], new_bv_s[None])
```

```python
import functools

import jax
import jax.numpy as jnp
import numpy as np
from jax import lax
from jax.experimental import pallas as pl
from jax.experimental.pallas import tpu as pltpu

F32 = jnp.float32
BF16 = jnp.bfloat16

HEAD_DIM = 64
N_HEADS = 8
WIDTH = N_HEADS * HEAD_DIM
PAIR = 2 * HEAD_DIM
N_PAIRS = N_HEADS // 2
LANES = 128
CHUNK = 64
BAND_REACH = 512
REL_CLIP = 256
N_EXPERTS = 64
N_GROUPS = 8
GROUP_SIZE = N_EXPERTS // N_GROUPS
TOPK_GROUPS = 4
TOP_K = 8
ROUTED_SCALE = 2.5
EPS = 1e-6
NEG_INF = -1e30
ATTN_SCALE = HEAD_DIM ** -0.5
LOG2E = 1.4426950408889634
VMEM_LIMIT = 56 * 1024 * 1024


def _cparams(n_axes):
    return pltpu.CompilerParams(dimension_semantics=("arbitrary",) * n_axes,
                                vmem_limit_bytes=VMEM_LIMIT)


def _dot(a, b):
    return jnp.dot(a, b, preferred_element_type=F32)


def _dot_nt(a, b):
    return lax.dot_general(a, b, (((1,), (1,)), ((), ())), preferred_element_type=F32)


def _split2(a):
    hi = a.astype(BF16)
    lo = (a - hi.astype(F32)).astype(BF16)
    return hi, lo


def _split3(a):
    hi = a.astype(BF16)
    r = a - hi.astype(F32)
    mid = r.astype(BF16)
    lo = (r - mid.astype(F32)).astype(BF16)
    return hi, mid, lo


def _ada_body(c_ref, w_ref, b_ref, o_ref):
    c = c_ref[...]
    a = c * jax.nn.sigmoid(c)
    a_hi, a_lo = _split2(a)
    w_hi, w_lo = _split2(w_ref[...])
    o_ref[...] = _dot(a_hi, w_hi) + _dot(a_hi, w_lo) + _dot(a_lo, w_hi) + b_ref[...]


def _ada(c_all, w_ada, b_ada):
    rows, d = c_all.shape
    n = w_ada.shape[1]
    tn = 1024
    return pl.pallas_call(
        _ada_body,
        grid=(n // tn,),
        in_specs=[pl.BlockSpec((rows, d), lambda j: (0, 0)),
                  pl.BlockSpec((d, tn), lambda j: (0, j)),
                  pl.BlockSpec((1, tn), lambda j: (0, j))],
        out_specs=pl.BlockSpec((rows, tn), lambda j: (0, j)),
        out_shape=jax.ShapeDtypeStruct((rows, n), F32),
        compiler_params=_cparams(1),
        name="ada",
    )(c_all, w_ada, b_ada)


def _log_sigmoid(z):
    return jnp.minimum(z, 0.0) - jnp.log(1.0 + jnp.exp(-jnp.abs(z)))


def _proj_body(x_ref, sh_ref, sc_ref, g1_ref, w_ref, bd_ref, gqf_ref, gkf_ref, gqb_ref, gkb_ref, bf_ref, wvt_ref,
               qf_ref, kf_ref, vf_ref, kf32_ref, vf32_ref, lf_ref, qb_ref, kb_ref, vb_ref, kb32_ref, vb32_ref,
               *, band_last_only):
    x = x_ref[...]
    g, r, d = x.shape
    ms = jnp.mean(x * x, axis=-1, keepdims=True)
    h = x * lax.rsqrt(ms + EPS) * g1_ref[...] * (1.0 + sc_ref[...]) + sh_ref[...]
    hb = h.reshape(g * r, d).astype(BF16)

    def seg(i):
        return _dot(hb, w_ref[:, i * WIDTH:(i + 1) * WIDTH])

    def head_norm(t, gain_ref):
        ssq = _dot((t * t).astype(BF16), bd_ref[...])
        return t * lax.rsqrt(ssq + EPS) * gain_ref[...]

    qf_ref[...] = head_norm(seg(0), gqf_ref).astype(BF16)
    kf = head_norm(seg(1), gkf_ref)
    kf32_ref[...] = kf
    kf_ref[...] = kf.astype(BF16)
    vf = seg(2)
    vf32_ref[...] = vf
    if band_last_only:
        vf_ref[...] = _dot_nt(wvt_ref[0:WIDTH, :], hb).astype(BF16)
    else:
        vf_ref[...] = vf.astype(BF16)
    z = _dot(hb, w_ref[:, 6 * WIDTH:6 * WIDTH + LANES]) + bf_ref[...]
    lf_ref[...] = _log_sigmoid(z)
    qb_ref[...] = head_norm(seg(3), gqb_ref).astype(BF16)
    kb = head_norm(seg(4), gkb_ref)
    kb_ref[...] = kb.astype(BF16)
    vb = seg(5)
    if band_last_only:
        vb_ref[...] = _dot_nt(wvt_ref[WIDTH:2 * WIDTH, :], hb).astype(BF16)
    else:
        vb_ref[...] = vb.astype(BF16)

    if band_last_only:
        @pl.when(pl.program_id(1) == pl.num_programs(1) - 1)
        def _():
            kb32_ref[...] = kb
            vb32_ref[...] = vb
    else:
        kb32_ref[...] = kb
        vb32_ref[...] = vb


def _proj(x, shift, scale, g1, w_all, bd, gqf, gkf, gqb, gkb, bf_row, wvt, *, G, R, band_last_only):
    nb, s, d = x.shape
    n = nb * s
    tm = G * R
    nbi, nsi = nb // G, s // R
    grid = (nbi, nsi)
    row = lambda b, i: (b * nsi + i, 0)
    const = lambda b, i: (0, 0)
    mod_spec = pl.BlockSpec((G, 1, d), lambda b, i: (b, 0, 0))
    out_bf = jax.ShapeDtypeStruct((n, WIDTH), BF16)
    out_f32 = jax.ShapeDtypeStruct((n, WIDTH), F32)
    tile = pl.BlockSpec((tm, WIDTH), row)
    if band_last_only:
        assert G == 1 and R == BAND_REACH
        band_shape = jax.ShapeDtypeStruct((nb, BAND_REACH, WIDTH), F32)
        band_spec = pl.BlockSpec((None, BAND_REACH, WIDTH), lambda b, i: (b, 0, 0))
        v_shape = jax.ShapeDtypeStruct((nb, nsi, WIDTH, tm), BF16)
        v_spec = pl.BlockSpec((None, None, WIDTH, tm), lambda b, i: (b, i, 0, 0))
    else:
        band_shape, band_spec = out_f32, tile
        v_shape, v_spec = out_bf, tile
    return pl.pallas_call(
        functools.partial(_proj_body, band_last_only=band_last_only),
        grid=grid,
        in_specs=[pl.BlockSpec((G, R, d), lambda b, i: (b, i, 0)), mod_spec, mod_spec,
                  pl.BlockSpec((1, d), const), pl.BlockSpec(w_all.shape, const), pl.BlockSpec(bd.shape, const),
                  pl.BlockSpec((1, WIDTH), const), pl.BlockSpec((1, WIDTH), const),
                  pl.BlockSpec((1, WIDTH), const), pl.BlockSpec((1, WIDTH), const),
                  pl.BlockSpec((1, LANES), const), pl.BlockSpec(wvt.shape, const)],
        out_specs=[tile, tile, v_spec, tile, tile, pl.BlockSpec((tm, LANES), row), tile, tile, v_spec,
                   band_spec, band_spec],
        out_shape=[out_bf, out_bf, v_shape, out_f32, out_f32, jax.ShapeDtypeStruct((n, LANES), F32),
                   out_bf, out_bf, v_shape, band_shape, band_shape],
        compiler_params=_cparams(2),
        name="proj",
    )(x, shift, scale, g1, w_all, bd, gqf, gkf, gqb, gkb, bf_row, wvt)


def _scan_body(lf_ref, cum_ref, cumt_ref, carry_ref):
    @pl.when(pl.program_id(1) == 0)
    def _():
        carry_ref[...] = jnp.zeros_like(carry_ref)

    lf = lf_ref[...]
    ts = lf.shape[0]
    lane = lax.broadcasted_iota(jnp.int32, lf.shape, 1)
    lf = jnp.where(lane < N_HEADS, lf, 0.0)
    hi, mid, lo = _split3(lf)
    rr = lax.broadcasted_iota(jnp.int32, (ts, ts), 0)
    cc = lax.broadcasted_iota(jnp.int32, (ts, ts), 1)
    tri = jnp.where(cc <= rr, 1.0, 0.0).astype(BF16)
    cum = _dot(tri, hi) + _dot(tri, mid) + _dot(tri, lo) + carry_ref[0:1, :]
    carry_ref[...] = jnp.broadcast_to(cum[ts - 1:ts, :], carry_ref.shape)
    cum2 = cum * LOG2E
    cum_ref[...] = cum2
    cumt_ref[...] = cum2.T[0:N_HEADS, :]


def _scan(lf, *, TS, TK):
    b, s, _ = lf.shape
    assert TS == TK
    t_block = (None, None, N_HEADS, TK)
    t_map = lambda bi, i: (bi, i, 0, 0)
    return pl.pallas_call(
        _scan_body,
        grid=(b, s // TS),
        in_specs=[pl.BlockSpec((None, TS, LANES), lambda bi, i: (bi, i, 0))],
        out_specs=[pl.BlockSpec((None, TS, LANES), lambda bi, i: (bi, i, 0)),
                   pl.BlockSpec(t_block, t_map)],
        out_shape=[jax.ShapeDtypeStruct((b, s, LANES), F32),
                   jax.ShapeDtypeStruct((b, s // TK, N_HEADS, TK), F32)],
        scratch_shapes=[pltpu.VMEM((8, LANES), F32)],
        compiler_params=_cparams(2),
        name="scan",
    )(lf)


def _fox_body(q_ref, k_ref, v_ref, cq_ref, ck_ref, o_ref, *, TQ, TK, q_off):
    p = pl.program_id(1)
    i = pl.program_id(2)
    q = q_ref[...]
    cq_blk = cq_ref[...]
    lane = lax.broadcasted_iota(jnp.int32, (TQ, PAIR), 1)
    ones_blk = jnp.where(lax.broadcasted_iota(jnp.int32, (TK, LANES), 1) == 0, 1.0, 0.0).astype(BF16)
    q0 = q_off + i * TQ
    n_full = q0 // TK
    qpos = q0 + lax.broadcasted_iota(jnp.int32, (TQ, TK), 0)
    kcol = lax.broadcasted_iota(jnp.int32, (TQ, TK), 1)

    outs = []
    for par in range(2):
        h = 2 * p + par
        qm = jnp.where((lane >= HEAD_DIM) == (par == 1), q, jnp.zeros_like(q))
        cq_col = jnp.sum(jnp.where(lane == h, cq_blk, 0.0), axis=1, keepdims=True)
        ref0 = cq_col[0:1, :]
        cqr = cq_col - ref0

        def step(j, carry, masked):
            m, l, acc = carry
            k0 = pl.multiple_of(j * TK, TK)
            kb = k_ref[pl.ds(k0, TK), :]
            vb = v_ref[pl.ds(k0, TK), :]
            s = _dot_nt(qm, kb)
            ck = ck_ref[j, pl.ds(h, 1), :]
            u = s - (ck - ref0)
            if masked:
                u = jnp.where(k0 + kcol <= qpos, u, NEG_INF)
            m_new = jnp.maximum(m, jnp.max(u, axis=1, keepdims=True) + cqr)
            alpha = jnp.exp2(m - m_new)
            pexp = jnp.exp2(u + (cqr - m_new))
            pv = _dot(pexp.astype(BF16), jnp.concatenate([vb, ones_blk], axis=1))
            return m_new, alpha * l + pv[:, LANES:LANES + 1], alpha * acc + pv[:, :LANES]

        init = (jnp.full((TQ, 1), NEG_INF, F32), jnp.zeros((TQ, 1), F32), jnp.zeros((TQ, LANES), F32))
        carry = lax.fori_loop(0, n_full, lambda j, c: step(j, c, False), init)
        _, l, acc = step(n_full, carry, True)
        outs.append(acc / l)
    o_ref[...] = jnp.where(lane < HEAD_DIM, outs[0], outs[1]).astype(o_ref.dtype)


def _fox(q, k, v, cum, cumt, *, TQ, TK, q_off):
    b, sq, _ = q.shape
    sk = k.shape[1]
    return pl.pallas_call(
        functools.partial(_fox_body, TQ=TQ, TK=TK, q_off=q_off),
        grid=(b, N_PAIRS, sq // TQ),
        in_specs=[pl.BlockSpec((None, TQ, PAIR), lambda bi, p, i: (bi, i, p)),
                  pl.BlockSpec((None, sk, PAIR), lambda bi, p, i: (bi, 0, p)),
                  pl.BlockSpec((None, sk, PAIR), lambda bi, p, i: (bi, 0, p)),
                  pl.BlockSpec((None, TQ, LANES), lambda bi, p, i: (bi, i, 0)),
                  pl.BlockSpec((None, sk // TK, N_HEADS, TK), lambda bi, p, i: (bi, 0, 0, 0))],
        out_specs=pl.BlockSpec((None, TQ, PAIR), lambda bi, p, i: (bi, i, p)),
        out_shape=jax.ShapeDtypeStruct((b, sq, WIDTH), BF16),
        compiler_params=_cparams(3),
        name="fox",
    )(q, k, v, cum, cumt)


AUG_PIECES = 3


def _scan_t_body(lf_ref, place_ref, ct_ref, ka_ref, carry_ref):
    @pl.when(pl.program_id(1) == 0)
    def _():
        carry_ref[...] = jnp.zeros_like(carry_ref)

    lf = lf_ref[...]
    ts = lf.shape[0]
    lane = lax.broadcasted_iota(jnp.int32, lf.shape, 1)
    lf = jnp.where(lane < N_HEADS, lf, 0.0)
    hi, mid, lo = _split3(lf)
    rr = lax.broadcasted_iota(jnp.int32, (ts, ts), 0)
    cc = lax.broadcasted_iota(jnp.int32, (ts, ts), 1)
    tri = jnp.where(cc <= rr, 1.0, 0.0).astype(BF16)
    cum = _dot(tri, hi) + _dot(tri, mid) + _dot(tri, lo) + carry_ref[0:1, :]
    carry_ref[...] = jnp.broadcast_to(cum[ts - 1:ts, :], carry_ref.shape)
    cum2 = cum * LOG2E
    ct_ref[...] = cum2.T[0:N_HEADS, :]
    pieces = _split3(cum2 - cum2[0:1, :])
    ka = _dot(pieces[0], place_ref[0]) + _dot(pieces[1], place_ref[1]) + _dot(pieces[2], place_ref[2])
    ka_ref[...] = ka.astype(BF16)


def _aug_placement():
    h = jnp.arange(LANES)[:, None]
    col = jnp.arange(WIDTH)[None, :]
    mats = []
    for x in range(AUG_PIECES):
        tgt = PAIR * (h // 2) + AUG_PIECES * (h % 2) + x
        mats.append(jnp.where((h < N_HEADS) & (col == tgt), 1.0, 0.0))
    return jnp.stack(mats).astype(BF16)


def _scan_t(lf, *, T):
    b, s, _ = lf.shape
    place = _aug_placement()
    return pl.pallas_call(
        _scan_t_body,
        grid=(b, s // T),
        in_specs=[pl.BlockSpec((None, T, LANES), lambda bi, i: (bi, i, 0)),
                  pl.BlockSpec(place.shape, lambda bi, i: (0, 0, 0))],
        out_specs=[pl.BlockSpec((None, None, N_HEADS, T), lambda bi, i: (bi, i, 0, 0)),
                   pl.BlockSpec((None, T, WIDTH), lambda bi, i: (bi, i, 0))],
        out_shape=[jax.ShapeDtypeStruct((b, s // T, N_HEADS, T), F32),
                   jax.ShapeDtypeStruct((b, s, WIDTH), BF16)],
        scratch_shapes=[pltpu.VMEM((8, LANES), F32)],
        compiler_params=_cparams(2),
        name="scan_t",
    )(lf, place)


def _foxt_body(q_ref, k_ref, ka_ref, vt_ref, ct_ref, o_ref, *, T):
    p = pl.program_id(1)
    i = pl.program_id(2)
    q = q_ref[...]
    lane = lax.broadcasted_iota(jnp.int32, (T, PAIR), 1)
    halves = []
    for par in range(2):
        qm = jnp.where((lane >= HEAD_DIM) == (par == 1), q, jnp.zeros_like(q))
        lo_lane = AUG_PIECES * par
        qa = jnp.where((lane >= lo_lane) & (lane < lo_lane + AUG_PIECES), -1.0, 0.0).astype(BF16)
        halves.append(jnp.concatenate([qm, qa], axis=1))
    qcat = jnp.concatenate(halves, axis=0)
    h_even = 2 * p
    cq = jnp.concatenate([ct_ref[i, pl.ds(h_even, 1), :], ct_ref[i, pl.ds(h_even + 1, 1), :]], axis=1)
    ones_rows = jnp.ones((16, T), BF16)
    krow = lax.broadcasted_iota(jnp.int32, (T, 2 * T), 0)
    qcol = lax.broadcasted_iota(jnp.int32, (T, 2 * T), 1) % T

    def step(j, carry, masked):
        m, acc_e, acc_o = carry
        k0 = pl.multiple_of(j * T, T)
        kcat = jnp.concatenate([k_ref[pl.ds(k0, T), :], ka_ref[pl.ds(k0, T), :]], axis=1)
        st = _dot_nt(kcat, qcat)
        c0 = jnp.concatenate([jnp.broadcast_to(ct_ref[j, pl.ds(h_even, 1), :][:, 0:1], (1, T)),
                              jnp.broadcast_to(ct_ref[j, pl.ds(h_even + 1, 1), :][:, 0:1], (1, T))], axis=1)
        rb = cq - c0
        if masked:
            st = jnp.where(krow <= qcol, st, NEG_INF)
        m_new = jnp.maximum(m, jnp.max(st, axis=0, keepdims=True) + rb)
        alpha = jnp.exp2(m - m_new)
        pt = jnp.exp2(st + (rb - m_new)).astype(BF16)
        vt = vt_ref[j]
        pv_e = _dot(jnp.concatenate([vt[0:HEAD_DIM], ones_rows], axis=0), pt[:, 0:T])
        pv_o = _dot(jnp.concatenate([vt[HEAD_DIM:PAIR], ones_rows], axis=0), pt[:, T:2 * T])
        return m_new, alpha[:, 0:T] * acc_e + pv_e, alpha[:, T:2 * T] * acc_o + pv_o

    rows = HEAD_DIM + 16
    init = (jnp.full((1, 2 * T), NEG_INF, F32), jnp.zeros((rows, T), F32), jnp.zeros((rows, T), F32))
    n_pairs = i // 2
    carry = lax.fori_loop(0, n_pairs, lambda t, c: step(2 * t + 1, step(2 * t, c, False), False), init)
    carry = lax.fori_loop(2 * n_pairs, i, lambda j, c: step(j, c, False), carry)
    _, acc_e, acc_o = step(i, carry, True)
    o_t = jnp.concatenate([acc_e[0:HEAD_DIM] / acc_e[HEAD_DIM:HEAD_DIM + 1],
                           acc_o[0:HEAD_DIM] / acc_o[HEAD_DIM:HEAD_DIM + 1]], axis=0)
    o_ref[...] = o_t.T.astype(o_ref.dtype)


def _foxt(q, k, ka, vt, ct, *, T):
    b, s, _ = q.shape
    nt = s // T
    return pl.pallas_call(
        functools.partial(_foxt_body, T=T),
        grid=(b, N_PAIRS, nt),
        in_specs=[pl.BlockSpec((None, T, PAIR), lambda bi, p, i: (bi, i, p)),
                  pl.BlockSpec((None, s, PAIR), lambda bi, p, i: (bi, 0, p)),
                  pl.BlockSpec((None, s, PAIR), lambda bi, p, i: (bi, 0, p)),
                  pl.BlockSpec((None, nt, PAIR, T), lambda bi, p, i: (bi, 0, p, 0)),
                  pl.BlockSpec((None, nt, N_HEADS, T), lambda bi, p, i: (bi, 0, 0, 0))],
        out_specs=pl.BlockSpec((None, T, PAIR), lambda bi, p, i: (bi, i, p)),
        out_shape=jax.ShapeDtypeStruct((b, s, WIDTH), BF16),
        compiler_params=_cparams(3),
        name="foxt",
    )(q, k, ka, vt, ct)


def _band_body(q_ref, k_ref, v_ref, bias_ref, o_ref, *scratch, TQ, W, n_sub, padded):
    p = pl.program_id(1)
    i = pl.program_id(2)
    if padded:
        kpad_ref, vpad_ref = scratch
        s_len = k_ref.shape[0]

        @pl.when(i == 0)
        def _():
            zeros = jnp.zeros((BAND_REACH, PAIR), BF16)
            kpad_ref[pl.ds(0, BAND_REACH), :] = zeros
            vpad_ref[pl.ds(0, BAND_REACH), :] = zeros
            kpad_ref[pl.ds(BAND_REACH, s_len), :] = k_ref[...]
            vpad_ref[pl.ds(BAND_REACH, s_len), :] = v_ref[...]
    else:
        kpad_ref, vpad_ref = k_ref, v_ref

    lane = lax.broadcasted_iota(jnp.int32, (TQ, PAIR), 1)
    ones_blk = jnp.where(lax.broadcasted_iota(jnp.int32, (W, LANES), 1) == 0, 1.0, 0.0).astype(BF16)
    kcol = lax.broadcasted_iota(jnp.int32, (TQ, W), 1)

    def sub_block(sub, carry):
        r0 = pl.multiple_of(sub * TQ, TQ)
        q0 = i * (n_sub * TQ) + r0
        q = q_ref[pl.ds(r0, TQ), :]
        kw = kpad_ref[pl.ds(pl.multiple_of(q0, TQ), W), :] if padded else kpad_ref[...]
        vw = vpad_ref[pl.ds(pl.multiple_of(q0, TQ), W), :] if padded else vpad_ref[...]
        vcat = jnp.concatenate([vw, ones_blk], axis=1)
        outs = []
        for par in range(2):
            h = 2 * p + par
            qm = jnp.where((lane >= HEAD_DIM) == (par == 1), q, jnp.zeros_like(q))
            s = _dot_nt(qm, kw) + bias_ref[h]
            if padded:
                s = jnp.where(kcol >= BAND_REACH - q0, s, NEG_INF)
            m = jnp.max(s, axis=1, keepdims=True)
            pexp = jnp.exp2(s - m)
            pv = _dot(pexp.astype(BF16), vcat)
            outs.append(pv[:, :LANES] / pv[:, LANES:LANES + 1])
        o_ref[pl.ds(r0, TQ), :] = jnp.where(lane < HEAD_DIM, outs[0], outs[1]).astype(o_ref.dtype)
        return carry

    lax.fori_loop(0, n_sub, sub_block, 0)


def _band(q, k, v, bias, *, TQ, n_sub, padded):
    b, sq, _ = q.shape
    sk = k.shape[1]
    w = BAND_REACH + TQ if padded else sk
    tqb = TQ * n_sub
    scratch = [pltpu.VMEM((sk + BAND_REACH, PAIR), BF16)] * 2 if padded else []
    return pl.pallas_call(
        functools.partial(_band_body, TQ=TQ, W=w, n_sub=n_sub, padded=padded),
        grid=(b, N_PAIRS, sq // tqb),
        in_specs=[pl.BlockSpec((None, tqb, PAIR), lambda bi, p, i: (bi, i, p)),
                  pl.BlockSpec((None, sk, PAIR), lambda bi, p, i: (bi, 0, p)),
                  pl.BlockSpec((None, sk, PAIR), lambda bi, p, i: (bi, 0, p)),
                  pl.BlockSpec(bias.shape, lambda bi, p, i: (0, 0, 0))],
        out_specs=pl.BlockSpec((None, tqb, PAIR), lambda bi, p, i: (bi, i, p)),
        out_shape=jax.ShapeDtypeStruct((b, sq, WIDTH), BF16),
        scratch_shapes=scratch,
        compiler_params=_cparams(3),
        name="band",
    )(q, k, v, bias)


BAND_TQ = 128
BAND_STEP = 512


def _bandt_body(q_ref, kp_ref, kc_ref, vp_ref, vc_ref, bias_ref, o_ref):
    i = pl.program_id(2)
    w = BAND_REACH + BAND_TQ
    k2 = jnp.concatenate([kp_ref[...], kc_ref[...]], axis=0)
    vt2 = jnp.concatenate([vp_ref[...], vc_ref[...]], axis=1)
    lane = lax.broadcasted_iota(jnp.int32, (BAND_TQ, PAIR), 1)
    ones_rows = jnp.ones((16, w), BF16)
    krow = lax.broadcasted_iota(jnp.int32, (w, 2 * BAND_TQ), 0)
    bias = bias_ref[...]
    for sub in range(BAND_STEP // BAND_TQ):
        r0 = sub * BAND_TQ
        q = q_ref[r0:r0 + BAND_TQ, :]
        qcat = jnp.concatenate([jnp.where(lane < HEAD_DIM, q, jnp.zeros_like(q)),
                                jnp.where(lane >= HEAD_DIM, q, jnp.zeros_like(q))], axis=0)
        st = _dot_nt(k2[r0:r0 + w], qcat) + bias
        st = jnp.where(krow >= (1 - i) * BAND_STEP - r0, st, NEG_INF)
        m = jnp.max(st, axis=0, keepdims=True)
        pt = jnp.exp2(st - m).astype(BF16)
        vwin = vt2[:, r0:r0 + w]
        outs = []
        for par in range(2):
            vcat = jnp.concatenate([vwin[par * HEAD_DIM:(par + 1) * HEAD_DIM], ones_rows], axis=0)
            pv = _dot(vcat, pt[:, par * BAND_TQ:(par + 1) * BAND_TQ])
            outs.append(pv[0:HEAD_DIM] / pv[HEAD_DIM:HEAD_DIM + 1])
        o_ref[r0:r0 + BAND_TQ, :] = jnp.concatenate(outs, axis=0).T.astype(o_ref.dtype)


def _bandt(q, k, vt, bias_t):
    b, s, _ = q.shape
    prev = lambda i: jnp.maximum(i - 1, 0)
    return pl.pallas_call(
        _bandt_body,
        grid=(b, N_PAIRS, s // BAND_STEP),
        in_specs=[pl.BlockSpec((None, BAND_STEP, PAIR), lambda bi, p, i: (bi, i, p)),
                  pl.BlockSpec((None, BAND_STEP, PAIR), lambda bi, p, i: (bi, prev(i), p)),
                  pl.BlockSpec((None, BAND_STEP, PAIR), lambda bi, p, i: (bi, i, p)),
                  pl.BlockSpec((None, None, PAIR, BAND_STEP), lambda bi, p, i: (bi, prev(i), p, 0)),
                  pl.BlockSpec((None, None, PAIR, BAND_STEP), lambda bi, p, i: (bi, i, p, 0)),
                  pl.BlockSpec((None,) + bias_t.shape[1:], lambda bi, p, i: (p, 0, 0))],
        out_specs=pl.BlockSpec((None, BAND_STEP, PAIR), lambda bi, p, i: (bi, i, p)),
        out_shape=jax.ShapeDtypeStruct((b, s, WIDTH), BF16),
        compiler_params=_cparams(3),
        name="bandt",
    )(q, k, k, vt, vt, bias_t)


def _band_bias_tile_t(rel_bias):
    tile = _band_bias_tile(rel_bias, BAND_TQ, BAND_REACH + BAND_TQ)
    t = jnp.swapaxes(tile, 1, 2)
    return jnp.concatenate([t[0::2], t[1::2]], axis=2)


def _band_bias_tile(rel_bias, tq, w):
    span = w + tq - 1
    period = span + 1
    v = np.arange(period)
    d = np.where(v < w, v, v - period)
    table_idx = np.clip(BAND_REACH - d, -REL_CLIP, REL_CLIP) + REL_CLIP
    table = rel_bias[:, table_idx] * LOG2E
    n_h = rel_bias.shape[0]
    vals = jnp.tile(table, (1, tq))[:, :tq * span].reshape(n_h, tq, span)[:, :, :w]
    r = np.arange(tq)[:, None]
    c = np.arange(w)[None, :]
    in_band = (c // CHUNK >= r // CHUNK) & (c // CHUNK <= r // CHUNK + BAND_REACH // CHUNK)
    return jnp.where(jnp.asarray(in_band)[None], vals, NEG_INF).astype(F32)


def _first_index(is_max, idx, axis, big):
    return jnp.min(jnp.where(is_max, idx, big), axis=axis, keepdims=True)


def _route(scores, choice):
    t = scores.shape[1]
    c3 = choice.reshape(N_GROUPS, GROUP_SIZE, t)
    j_idx = lax.broadcasted_iota(jnp.int32, c3.shape, 1)
    top1 = jnp.max(c3, axis=1, keepdims=True)
    first = _first_index(c3 == top1, j_idx, 1, GROUP_SIZE)
    top2 = jnp.max(jnp.where(j_idx == first, -jnp.inf, c3), axis=1, keepdims=True)
    gscore = (top1 + top2).reshape(N_GROUPS, t)

    g_idx = lax.broadcasted_iota(jnp.int32, gscore.shape, 0)
    gsel = jnp.zeros(gscore.shape, F32)
    work = gscore
    for _ in range(TOPK_GROUPS):
        gm = jnp.max(work, axis=0, keepdims=True)
        pick = g_idx == _first_index(work == gm, g_idx, 0, N_GROUPS)
        gsel = jnp.where(pick, 1.0, gsel)
        work = jnp.where(pick, -jnp.inf, work)

    emask = jnp.broadcast_to(gsel.reshape(N_GROUPS, 1, t), c3.shape) > 0.0
    work = jnp.where(emask, c3, NEG_INF)
    e_idx = lax.broadcasted_iota(jnp.int32, c3.shape, 0) * GROUP_SIZE + j_idx
    esel = jnp.zeros(c3.shape, F32)
    for _ in range(TOP_K):
        em = jnp.max(jnp.max(work, axis=1, keepdims=True), axis=0, keepdims=True)
        cand = jnp.where(work == em, e_idx, N_EXPERTS)
        first = jnp.min(jnp.min(cand, axis=1, keepdims=True), axis=0, keepdims=True)
        pick = e_idx == first
        esel = jnp.where(pick, 1.0, esel)
        work = jnp.where(pick, -jnp.inf, work)

    w = esel * scores.reshape(c3.shape)
    denom = jnp.sum(jnp.sum(w, axis=1, keepdims=True), axis=0, keepdims=True)
    return (w / denom * ROUTED_SCALE).reshape(N_EXPERTS, t)


def _merge_body(of_ref, ob_ref, x_ref, gate_ref, sh_ref, sc_ref, ogf_ref, ogb_ref, wo_ref, g2_ref,
                wrh_ref, wrl_ref, br_ref, x1_ref, h2_ref, comb_ref, combt_ref, cnt_ref):
    def group_norm(t_ref, gain_ref):
        t = t_ref[...].astype(F32)
        ms = jnp.mean(t * t, axis=-1, keepdims=True)
        return (t * lax.rsqrt(ms + EPS) * gain_ref[...]).astype(BF16)

    y = _dot(group_norm(of_ref, ogf_ref), wo_ref[0:WIDTH, :]) + _dot(group_norm(ob_ref, ogb_ref), wo_ref[WIDTH:, :])
    x = x_ref[...]
    g, r, d = x.shape
    x1 = x + gate_ref[...] * y.reshape(g, r, d)
    x1_ref[...] = x1
    ms = jnp.mean(x1 * x1, axis=-1, keepdims=True)
    h2 = (x1 * lax.rsqrt(ms + EPS) * g2_ref[...] * (1.0 + sc_ref[...]) + sh_ref[...]).reshape(g * r, d)
    h_hi, h_lo = _split2(h2)
    h2_ref[...] = h_hi
    logits = _dot_nt(wrh_ref[...], h_hi) + _dot_nt(wrh_ref[...], h_lo) + _dot_nt(wrl_ref[...], h_hi)
    scores = jax.nn.sigmoid(logits)
    t = scores.shape[1]
    bias = jnp.concatenate([br_ref[...]] * (t // LANES), axis=1)
    comb = _route(scores, scores + bias)
    comb_pad = jnp.concatenate([comb, jnp.zeros((LANES - N_EXPERTS, t), F32)], axis=0)
    comb_ref[...] = comb_pad.T
    combt_ref[...] = comb
    picked = jnp.where(comb > 0.0, 1.0, 0.0)
    for sub in range(cnt_ref.shape[0]):
        cnt = jnp.sum(picked[:, sub * MOE_TM:(sub + 1) * MOE_TM], axis=1, keepdims=True)
        cnt_ref[sub] = jnp.broadcast_to(cnt, (N_EXPERTS, LANES))


def _merge(of, ob, x, gate, shift, scale, ogf, ogb, wo, g2, wr_hi, wr_lo, br, *, G, R):
    nb, s, d = x.shape
    n = nb * s
    tm = G * R
    nbi, nsi = nb // G, s // R
    row = lambda b, i: (b * nsi + i, 0)
    const = lambda b, i: (0, 0)
    mod_spec = pl.BlockSpec((G, 1, d), lambda b, i: (b, 0, 0))
    x_spec = pl.BlockSpec((G, R, d), lambda b, i: (b, i, 0))
    return pl.pallas_call(
        _merge_body,
        grid=(nbi, nsi),
        in_specs=[pl.BlockSpec((tm, WIDTH), row), pl.BlockSpec((tm, WIDTH), row), x_spec,
                  mod_spec, mod_spec, mod_spec,
                  pl.BlockSpec((1, WIDTH), const), pl.BlockSpec((1, WIDTH), const),
                  pl.BlockSpec(wo.shape, const), pl.BlockSpec((1, d), const),
                  pl.BlockSpec(wr_hi.shape, const), pl.BlockSpec(wr_lo.shape, const),
                  pl.BlockSpec(br.shape, const)],
        out_specs=[x_spec, pl.BlockSpec((tm, d), row), pl.BlockSpec((tm, LANES), row),
                   pl.BlockSpec((N_EXPERTS, tm), lambda b, i: (0, b * nsi + i)),
                   pl.BlockSpec((tm // MOE_TM, N_EXPERTS, LANES), lambda b, i: (b * nsi + i, 0, 0))],
        out_shape=[jax.ShapeDtypeStruct((nb, s, d), F32), jax.ShapeDtypeStruct((n, d), BF16),
                   jax.ShapeDtypeStruct((n, LANES), F32), jax.ShapeDtypeStruct((N_EXPERTS, n), F32),
                   jax.ShapeDtypeStruct((n // MOE_TM, N_EXPERTS, LANES), F32)],
        compiler_params=_cparams(2),
        name="merge",
    )(of, ob, x, gate, shift, scale, ogf, ogb, wo, g2, wr_hi, wr_lo, br)


def _silu(g):
    return g * jax.nn.sigmoid(g)


def _moe_body(h_ref, comb_ref, x1_ref, gate_ref, wg_ref, wu_ref, wd_ref, sg_ref, su_ref, sd_ref, y_ref, acc_ref):
    e = pl.program_id(2)
    hb = h_ref[...]

    @pl.when(e == 0)
    def _():
        a = _silu(_dot(hb, sg_ref[...])) * _dot(hb, su_ref[...])
        acc_ref[...] = _dot(a.astype(BF16), sd_ref[...])

    comb = comb_ref[...]
    lane = lax.broadcasted_iota(jnp.int32, comb.shape, 1)
    c_e = jnp.sum(jnp.where(lane == e, comb, 0.0), axis=1, keepdims=True)
    a = _silu(_dot(hb, wg_ref[...])) * _dot(hb, wu_ref[...]) * c_e
    acc_ref[...] += _dot(a.astype(BF16), wd_ref[...])

    @pl.when(e == pl.num_programs(2) - 1)
    def _():
        x1 = x1_ref[...]
        g, r, d = x1.shape
        y_ref[...] = x1 + gate_ref[...] * acc_ref[...].reshape(g, r, d)


def _moe(h2, comb, x1, gate, wg, wu, wd, sg, su, sd, *, G, R):
    nb, s, d = x1.shape
    tm = G * R
    nbi, nsi = nb // G, s // R
    ff = wg.shape[2]
    row = lambda b, i, e: (b * nsi + i, 0)
    const = lambda b, i, e: (0, 0)
    x_spec = pl.BlockSpec((G, R, d), lambda b, i, e: (b, i, 0))
    return pl.pallas_call(
        _moe_body,
        grid=(nbi, nsi, N_EXPERTS),
        in_specs=[pl.BlockSpec((tm, d), row), pl.BlockSpec((tm, LANES), row), x_spec,
                  pl.BlockSpec((G, 1, d), lambda b, i, e: (b, 0, 0)),
                  pl.BlockSpec((None, d, ff), lambda b, i, e: (e, 0, 0)),
                  pl.BlockSpec((None, d, ff), lambda b, i, e: (e, 0, 0)),
                  pl.BlockSpec((None, ff, d), lambda b, i, e: (e, 0, 0)),
                  pl.BlockSpec(sg.shape, const), pl.BlockSpec(su.shape, const), pl.BlockSpec(sd.shape, const)],
        out_specs=x_spec,
        out_shape=jax.ShapeDtypeStruct((nb, s, d), F32),
        scratch_shapes=[pltpu.VMEM((tm, d), F32)],
        compiler_params=_cparams(3),
        name="moe",
    )(h2, comb, x1, gate, wg, wu, wd, sg, su, sd)


MOE_TM = 256
MOE_CH = 16
MOE_SLOTS = TOP_K * MOE_TM + N_EXPERTS * MOE_CH
MOE_NCHUNK = MOE_SLOTS // MOE_CH
MOE_PIECE = 512
MOE_NPIECE = MOE_SLOTS // MOE_PIECE
MOE_CPP = MOE_PIECE // MOE_CH
MOE_RB = 512
TAU_RADIX = 64.0
assert MOE_SLOTS % MOE_PIECE == 0


def _moe_plan(cnt, n_tiles):
    pc = (cnt + MOE_CH - 1) // MOE_CH * MOE_CH
    off = jnp.cumsum(pc, axis=1) - pc
    end = off + pc
    n_used = (jnp.sum(pc, axis=1) // MOE_CH).astype(jnp.int32)
    tot = jnp.sum(pc, axis=0)
    reg = (tot + MOE_RB - 1) // MOE_RB * MOE_RB
    reg_end = jnp.cumsum(reg)
    reg_start = reg_end - reg
    dest_base = reg_start[None, :] + jnp.cumsum(pc, axis=0) - pc
    chunk_row = jnp.arange(MOE_NCHUNK, dtype=jnp.int32)[None, :] * MOE_CH
    in_group = (chunk_row[:, :, None] >= off[:, None, :]) & (chunk_row[:, :, None] < end[:, None, :])
    used = jnp.any(in_group, axis=2)
    cdest = chunk_row + jnp.sum(jnp.where(in_group, (dest_base - off)[:, None, :], 0), axis=2)
    worst_rows = TOP_K * MOE_TM * n_tiles + n_tiles * N_EXPERTS * (MOE_CH - 1) + N_EXPERTS * (MOE_RB - MOE_CH)
    r_max = -(-worst_rows // MOE_RB)
    cdest_d = jnp.where(used, cdest, r_max * MOE_RB + chunk_row).astype(jnp.int32)
    cdest_c = jnp.where(used, cdest, chunk_row).astype(jnp.int32)
    n_active = (reg_end[-1] // MOE_RB).astype(jnp.int32).reshape(1)
    tile_row = jnp.arange(r_max, dtype=jnp.int32) * MOE_RB
    tile_expert = jnp.minimum(jnp.sum((tile_row[:, None] >= reg_end[None, :]).astype(jnp.int32), axis=1),
                              N_EXPERTS - 1).astype(jnp.int32)
    in_region = (tile_row[:, None] >= reg_start[None, :]) & (tile_row[:, None] < reg_end[None, :])
    rows_end = jnp.sum(jnp.where(in_region, (reg_start + tot)[None, :], 0), axis=1)
    valid = jnp.clip(rows_end - tile_row, 0, MOE_RB).astype(jnp.int32)
    f = lambda a: a.astype(F32)
    zeros64 = jnp.zeros((n_tiles, N_EXPERTS), F32)
    row2 = lambda a: jnp.broadcast_to(jnp.concatenate([f(a), f(a)], axis=1)[:, None, :], (n_tiles, 8, LANES))
    col = lambda a: jnp.broadcast_to(f(a)[:, :, None], (n_tiles, N_EXPERTS, LANES))
    col128 = lambda a: jnp.broadcast_to(jnp.concatenate([f(a), zeros64], axis=1)[:, :, None], (n_tiles, LANES, LANES))
    row1 = lambda a: jnp.broadcast_to(jnp.concatenate([f(a), zeros64], axis=1)[:, None, :], (n_tiles, 8, LANES))
    return dict(n_used=n_used, cdest_d=cdest_d.reshape(n_tiles, 1, MOE_NCHUNK),
                cdest_c=cdest_c.reshape(n_tiles, 1, MOE_NCHUNK), r_max=r_max, n_active=n_active,
                tile_expert=tile_expert, valid=valid,
                off_row2=row2(off), end_row2=row2(end), off_col=col(off),
                off_row1=row1(off), off_col128=col128(off), end_col128=col128(end))


def _tau_pieces(sel, tau):
    tau = jnp.where(sel, tau, -1.0)
    hi = jnp.floor(tau * (1.0 / TAU_RADIX)) * TAU_RADIX
    return hi.astype(BF16), (tau - hi).astype(BF16)


def _dispatch_body(nused_ref, cdest_ref, h_ref, combt_ref, offcol_ref, offrow_ref, endrow_ref, sorted_ref,
                   buf_ref, sem):
    t = pl.program_id(0)
    n_used = nused_ref[t]
    tm = h_ref.shape[0]
    sel = combt_ref[...] > 0.0
    rr = lax.broadcasted_iota(jnp.int32, (tm, tm), 0)
    cc = lax.broadcasted_iota(jnp.int32, (tm, tm), 1)
    upper = jnp.where(rr < cc, 1.0, 0.0).astype(BF16)
    rank = _dot(jnp.where(sel, 1.0, 0.0).astype(BF16), upper)
    cols = jnp.concatenate([offcol_ref[...]] * (tm // LANES), axis=1)
    tau_hi, tau_lo = _tau_pieces(sel, cols + rank)
    taucat = jnp.concatenate([tau_hi, tau_lo], axis=0)
    off_row = offrow_ref[0:1, :]
    end_row = endrow_ref[0:1, :]
    hb = h_ref[...]

    def start_piece(piece):
        chunks = range(piece * MOE_CPP, (piece + 1) * MOE_CPP)
        dests = [cdest_ref[0, c] for c in chunks]
        for c, row in zip(chunks, dests):
            dst = sorted_ref.at[pl.ds(pl.multiple_of(row, MOE_CH), MOE_CH), :]
            pltpu.make_async_copy(buf_ref.at[pl.ds(c * MOE_CH, MOE_CH), :], dst, sem).start()

    def wait_piece(piece):
        rows = pl.ds(piece * MOE_PIECE, MOE_PIECE)
        pltpu.make_async_copy(buf_ref.at[rows, :], sorted_ref.at[rows, :], sem).wait()

    for piece in range(MOE_NPIECE):
        @pl.when(piece * MOE_CPP < n_used)
        def _():
            if piece > 0:
                start_piece(piece - 1)
            base = piece * MOE_PIECE
            s_col = (base + lax.broadcasted_iota(jnp.int32, (MOE_PIECE, LANES), 0)).astype(F32)
            onehot = jnp.where((s_col >= off_row) & (s_col < end_row), 1.0, 0.0).astype(BF16)
            q = _dot(onehot, taucat)
            s_mat = (base + lax.broadcasted_iota(jnp.int32, (MOE_PIECE, tm), 0)).astype(F32)
            g = jnp.where(q == s_mat, 1.0, 0.0).astype(BF16)
            buf_ref[pl.ds(base, MOE_PIECE), :] = _dot(g, hb).astype(BF16)

    last = (n_used - 1) // MOE_CPP
    for piece in range(MOE_NPIECE):
        @pl.when(piece == last)
        def _():
            start_piece(piece)

    for piece in range(MOE_NPIECE):
        @pl.when(piece * MOE_CPP < n_used)
        def _():
            wait_piece(piece)


def _dispatch(h2, combt, plan, n_tiles):
    n, d = h2.shape
    r_total = plan["r_max"] * MOE_RB + MOE_SLOTS
    grid_spec = pltpu.PrefetchScalarGridSpec(
        num_scalar_prefetch=1,
        grid=(n_tiles,),
        in_specs=[pl.BlockSpec((None, 1, MOE_NCHUNK), lambda t, nu: (t, 0, 0), memory_space=pltpu.SMEM),
                  pl.BlockSpec((MOE_TM, d), lambda t, nu: (t, 0)),
                  pl.BlockSpec((N_EXPERTS, MOE_TM), lambda t, nu: (0, t)),
                  pl.BlockSpec((None, N_EXPERTS, LANES), lambda t, nu: (t, 0, 0)),
                  pl.BlockSpec((None, 8, LANES), lambda t, nu: (t, 0, 0)),
                  pl.BlockSpec((None, 8, LANES), lambda t, nu: (t, 0, 0))],
        out_specs=pl.BlockSpec(memory_space=pl.ANY),
        scratch_shapes=[pltpu.VMEM((MOE_SLOTS, d), BF16), pltpu.SemaphoreType.DMA(())],
    )
    return pl.pallas_call(
        _dispatch_body,
        grid_spec=grid_spec,
        out_shape=jax.ShapeDtypeStruct((r_total, d), BF16),
        compiler_params=_cparams(1),
        name="moe_dispatch",
    )(plan["n_used"], plan["cdest_d"], h2, combt, plan["off_col"], plan["off_row2"], plan["end_row2"])


def _ffn_body(texp_ref, nact_ref, valid_ref, x_ref, wg_ref, wu_ref, wd_ref, o_ref):
    r = pl.program_id(0)

    @pl.when(r < nact_ref[0])
    def _():
        x = x_ref[...]
        rows = lax.broadcasted_iota(jnp.int32, x.shape, 0)
        x = jnp.where(rows < valid_ref[r], x, jnp.zeros_like(x))
        a = _silu(_dot(x, wg_ref[...])) * _dot(x, wu_ref[...])
        o_ref[...] = _dot(a.astype(BF16), wd_ref[...]).astype(o_ref.dtype)


def _ffn(xs, wg, wu, wd, plan):
    r_total, d = xs.shape
    ff = wg.shape[2]
    last = lambda r, te, na, va: (jnp.minimum(r, na[0] - 1), 0)
    wmap = lambda r, te, na, va: (te[r], 0, 0)
    grid_spec = pltpu.PrefetchScalarGridSpec(
        num_scalar_prefetch=3,
        grid=(plan["r_max"],),
        in_specs=[pl.BlockSpec((MOE_RB, d), last),
                  pl.BlockSpec((None, d, ff), wmap), pl.BlockSpec((None, d, ff), wmap),
                  pl.BlockSpec((None, ff, d), wmap)],
        out_specs=pl.BlockSpec((MOE_RB, d), last),
    )
    return pl.pallas_call(
        _ffn_body,
        grid_spec=grid_spec,
        out_shape=jax.ShapeDtypeStruct((r_total, d), BF16),
        compiler_params=_cparams(1),
        name="moe_ffn",
    )(plan["tile_expert"], plan["n_active"], plan["valid"], xs, wg, wu, wd)


def _combine_body(nused_ref, cdest_ref, h_ref, comb_ref, x1_ref, gate_ref, offrow_ref, offcol_ref, endcol_ref,
                  sg_ref, su_ref, sd_ref, ys_ref, y_ref, buf_ref, acc_ref, sem):
    t = pl.program_id(0)
    n_used = nused_ref[t]
    tm = h_ref.shape[0]

    @pl.when(t == 0)
    def _():
        buf_ref[...] = jnp.zeros_like(buf_ref)

    def wait_piece(piece):
        rows = pl.ds(piece * MOE_PIECE, MOE_PIECE)
        pltpu.make_async_copy(ys_ref.at[rows, :], buf_ref.at[rows, :], sem.at[piece]).wait()

    for piece in range(MOE_NPIECE):
        @pl.when(piece * MOE_CPP < n_used)
        def _():
            chunks = range(piece * MOE_CPP, (piece + 1) * MOE_CPP)
            srcs = [cdest_ref[0, c] for c in chunks]
            for c, row in zip(chunks, srcs):
                src = ys_ref.at[pl.ds(pl.multiple_of(row, MOE_CH), MOE_CH), :]
                pltpu.make_async_copy(src, buf_ref.at[pl.ds(c * MOE_CH, MOE_CH), :], sem.at[piece]).start()

    hb = h_ref[...]
    acc = _dot((_silu(_dot(hb, sg_ref[...])) * _dot(hb, su_ref[...])).astype(BF16), sd_ref[...])
    comb = comb_ref[...]
    sel = comb > 0.0
    rr = lax.broadcasted_iota(jnp.int32, (tm, tm), 0)
    cc = lax.broadcasted_iota(jnp.int32, (tm, tm), 1)
    lower = jnp.where(cc < rr, 1.0, 0.0).astype(BF16)
    rank = _dot(lower, jnp.where(sel, 1.0, 0.0).astype(BF16))
    tau_hi, tau_lo = _tau_pieces(sel, offrow_ref[0:1, :] + rank)
    taucat = jnp.concatenate([tau_hi, tau_lo], axis=1)
    lhs = jnp.concatenate([taucat, jnp.concatenate([comb.astype(BF16), jnp.zeros((tm, LANES), BF16)], axis=1)],
                          axis=0)
    off_col = offcol_ref[...]
    end_col = endcol_ref[...]
    acc_ref[...] = acc

    for piece in range(MOE_NPIECE):
        @pl.when(piece * MOE_CPP < n_used)
        def _():
            base = piece * MOE_PIECE
            s_row = (base + lax.broadcasted_iota(jnp.int32, (LANES, MOE_PIECE), 1)).astype(F32)
            off_b = jnp.concatenate([off_col] * (MOE_PIECE // LANES), axis=1)
            end_b = jnp.concatenate([end_col] * (MOE_PIECE // LANES), axis=1)
            onehot = jnp.where((s_row >= off_b) & (s_row < end_b), 1.0, 0.0).astype(BF16)
            qw = _dot(lhs, jnp.concatenate([onehot, onehot], axis=0))
            s_mat = (base + lax.broadcasted_iota(jnp.int32, (tm, MOE_PIECE), 1)).astype(F32)
            gw = jnp.where(qw[0:tm] == s_mat, qw[tm:2 * tm], 0.0).astype(BF16)
            wait_piece(piece)
            acc_ref[...] += _dot(gw, buf_ref[pl.ds(base, MOE_PIECE), :])
    y_ref[...] = x1_ref[...] + gate_ref[...] * acc_ref[...]


def _combine(h2, comb, x1, gate, ys, sg, su, sd, plan, n_tiles):
    nb, s, d = x1.shape
    n = nb * s
    per_b = s // MOE_TM
    x1f = x1.reshape(n, d)
    const2 = lambda t, nu: (0, 0)
    grid_spec = pltpu.PrefetchScalarGridSpec(
        num_scalar_prefetch=1,
        grid=(n_tiles,),
        in_specs=[pl.BlockSpec((None, 1, MOE_NCHUNK), lambda t, nu: (t, 0, 0), memory_space=pltpu.SMEM),
                  pl.BlockSpec((MOE_TM, d), lambda t, nu: (t, 0)),
                  pl.BlockSpec((MOE_TM, LANES), lambda t, nu: (t, 0)),
                  pl.BlockSpec((MOE_TM, d), lambda t, nu: (t, 0)),
                  pl.BlockSpec((None, 1, d), lambda t, nu: (t // per_b, 0, 0)),
                  pl.BlockSpec((None, 8, LANES), lambda t, nu: (t, 0, 0)),
                  pl.BlockSpec((None, LANES, LANES), lambda t, nu: (t, 0, 0)),
                  pl.BlockSpec((None, LANES, LANES), lambda t, nu: (t, 0, 0)),
                  pl.BlockSpec(sg.shape, const2), pl.BlockSpec(su.shape, const2), pl.BlockSpec(sd.shape, const2),
                  pl.BlockSpec(memory_space=pl.ANY)],
        out_specs=pl.BlockSpec((MOE_TM, d), lambda t, nu: (t, 0)),
        scratch_shapes=[pltpu.VMEM((MOE_SLOTS, d), BF16), pltpu.VMEM((MOE_TM, d), F32),
                        pltpu.SemaphoreType.DMA((MOE_NPIECE,))],
    )
    y = pl.pallas_call(
        _combine_body,
        grid_spec=grid_spec,
        out_shape=jax.ShapeDtypeStruct((n, d), F32),
        compiler_params=_cparams(1),
        name="moe_combine",
    )(plan["n_used"], plan["cdest_c"], h2, comb, x1f, gate, plan["off_row1"], plan["off_col128"], plan["end_col128"],
      sg, su, sd, ys)
    return y.reshape(nb, s, d)


def _moe_sparse(h2, comb, combt, cnt, x1, gate, wg, wu, wd, sg, su, sd):
    n = h2.shape[0]
    n_tiles = n // MOE_TM
    plan = _moe_plan(cnt[:, :, 0].astype(jnp.int32), n_tiles)
    xs = _dispatch(h2, combt, plan, n_tiles)
    ys = _ffn(xs, wg, wu, wd, plan)
    return _combine(h2, comb, x1, gate, ys, sg, su, sd, plan, n_tiles)


def _tile_heads(g, mult=1.0):
    return (jnp.tile(g.astype(F32), N_HEADS) * mult).reshape(1, WIDTH)


def kernel(x_prompt, x_sample, cache_fox_k, cache_fox_v, cache_fox_logf, cache_band_k, cache_band_v, c_prompt, c_sample, w_ada, b_ada, norm1_g, norm2_g, w_in, b_forget, g_q_fox, g_k_fox, g_q_band, g_k_band, rel_bias, out_g_fox, out_g_band, w_out, w_router, b_router, w_gate, w_up, w_down, ws_gate, ws_up, ws_down):
    depth = w_ada.shape[0]
    assert depth == 1
    bsz, seq, d = x_prompt.shape
    dbs, dseq, _ = x_sample.shape
    past = cache_fox_k.shape[2]
    n_cache = cache_band_k.shape[2]
    assert n_cache == BAND_REACH and dseq == CHUNK and seq % BAND_REACH == 0

    wi = w_in[0]
    cols = [wi[:, 0:512], wi[:, 512:1024], wi[:, 1024:1536], wi[:, 1544:2056], wi[:, 2056:2568], wi[:, 2568:3080],
            wi[:, 1536:1544], jnp.zeros((d, LANES - N_HEADS), F32)]
    w_all = jnp.concatenate(cols, axis=1).astype(BF16)
    hd = jnp.arange(WIDTH) // HEAD_DIM
    bd = jnp.where(hd[:, None] == hd[None, :], 1.0 / HEAD_DIM, 0.0).astype(BF16)
    qscale = ATTN_SCALE * LOG2E
    gqf, gkf = _tile_heads(g_q_fox[0], qscale), _tile_heads(g_k_fox[0])
    gqb, gkb = _tile_heads(g_q_band[0], qscale), _tile_heads(g_k_band[0])
    bf_row = jnp.concatenate([b_forget[0], jnp.zeros((LANES - N_HEADS,), F32)]).reshape(1, LANES)
    g1 = norm1_g[0].reshape(1, d)
    g2 = norm2_g[0].reshape(1, d)
    ogf = out_g_fox[0].reshape(1, WIDTH)
    ogb = out_g_band[0].reshape(1, WIDTH)
    wo = w_out[0].astype(BF16)
    wr_t = w_router[0].T
    wr_hi = wr_t.astype(BF16)
    wr_lo = (wr_t - wr_hi.astype(F32)).astype(BF16)
    br = jnp.broadcast_to(b_router[0].reshape(N_EXPERTS, 1), (N_EXPERTS, LANES)).astype(F32)
    wg, wu, wd = w_gate[0].astype(BF16), w_up[0].astype(BF16), w_down[0].astype(BF16)
    sg, su, sd = ws_gate[0].astype(BF16), ws_up[0].astype(BF16), ws_down[0].astype(BF16)
    wvt = jnp.concatenate([wi[:, 2 * WIDTH:3 * WIDTH], wi[:, 2568:3080]], axis=1).T.astype(BF16)

    n_c = bsz + dbs
    rows = -(-n_c // 8) * 8
    c_all = jnp.concatenate([c_prompt, c_sample, jnp.zeros((rows - n_c, d), F32)], axis=0)
    mod = _ada(c_all, w_ada[0], b_ada[0].reshape(1, -1))

    def mods(lo, hi):
        return [mod[lo:hi, j * d:(j + 1) * d].reshape(hi - lo, 1, d) for j in range(6)]

    shift1_p, scale1_p, gate1_p, shift2_p, scale2_p, gate2_p = mods(0, bsz)
    shift1_s, scale1_s, gate1_s, shift2_s, scale2_s, gate2_s = mods(bsz, n_c)

    TM = BAND_REACH
    (qf, kf, vft, kf32, vf32, lf, qb, kb, vbt, kb32, vb32) = _proj(
        x_prompt, shift1_p, scale1_p, g1, w_all, bd, gqf, gkf, gqb, gkb, bf_row, wvt, G=1, R=TM, band_last_only=True)
    r3 = lambda a: a.reshape(bsz, seq, a.shape[-1])
    ct, ka = _scan_t(r3(lf), T=TM)
    of = _foxt(r3(qf), r3(kf), ka, vft, ct, T=TM)
    assert TM == BAND_STEP
    ob = _bandt(r3(qb), r3(kb), vbt, _band_bias_tile_t(rel_bias[0]))
    assert TM % MOE_TM == 0
    x1_p, h2_p, comb_p, combt_p, cnt_p = _merge(of.reshape(-1, WIDTH), ob.reshape(-1, WIDTH), x_prompt, gate1_p,
                                                shift2_p, scale2_p, ogf, ogb, wo, g2, wr_hi, wr_lo, br, G=1, R=TM)
    y_p = _moe_sparse(h2_p, comb_p, combt_p, cnt_p, x1_p, gate2_p, wg, wu, wd, sg, su, sd)

    GS = 8
    (qf_s, kf_s, vf_s, kf32_s, vf32_s, lf_s, qb_s, kb_s, vb_s, kb32_s, vb32_s) = _proj(
        x_sample, shift1_s, scale1_s, g1, w_all, bd, gqf, gkf, gqb, gkb, bf_row, wvt, G=GS, R=dseq,
        band_last_only=False)
    s3 = lambda a: a.reshape(dbs, dseq, a.shape[-1])
    sk = past + dseq
    skp = -(-sk // LANES) * LANES
    pad_k = skp - sk
    lf_cache = jnp.pad(cache_fox_logf[0], ((0, 0), (0, 0), (0, LANES - N_HEADS)))
    lf_all = jnp.concatenate([lf_cache, s3(lf_s), jnp.zeros((dbs, pad_k, LANES), F32)], axis=1)
    cum_s, cumt_s = _scan(lf_all, TS=skp, TK=skp)
    zpad = jnp.zeros((dbs, pad_k, WIDTH), BF16)
    k_all = jnp.concatenate([cache_fox_k[0].reshape(dbs, past, WIDTH).astype(BF16), s3(kf_s), zpad], axis=1)
    v_all = jnp.concatenate([cache_fox_v[0].reshape(dbs, past, WIDTH).astype(BF16), s3(vf_s), zpad], axis=1)
    of_s = _fox(s3(qf_s), k_all, v_all, cum_s[:, past:past + dseq], cumt_s, TQ=dseq, TK=skp, q_off=past)
    bias_s = _band_bias_tile(rel_bias[0], dseq, BAND_REACH + LANES)
    zb = jnp.zeros((dbs, LANES - dseq, WIDTH), BF16)
    kb_all = jnp.concatenate([cache_band_k[0].reshape(dbs, n_cache, WIDTH).astype(BF16), s3(kb_s), zb], axis=1)
    vb_all = jnp.concatenate([cache_band_v[0].reshape(dbs, n_cache, WIDTH).astype(BF16), s3(vb_s), zb], axis=1)
    ob_s = _band(s3(qb_s), kb_all, vb_all, bias_s, TQ=dseq, n_sub=1, padded=False)
    x1_s, h2_s, comb_s, _, _ = _merge(of_s.reshape(-1, WIDTH), ob_s.reshape(-1, WIDTH), x_sample, gate1_s, shift2_s,
                                      scale2_s, ogf, ogb, wo, g2, wr_hi, wr_lo, br, G=GS, R=dseq)
    y_s = _moe(h2_s, comb_s, x1_s, gate2_s, wg, wu, wd, sg, su, sd, G=dbs, R=dseq)

    hshape = (N_HEADS, HEAD_DIM)
    new_bk_s = jnp.concatenate([cache_band_k[0], s3(kb32_s).reshape(dbs, dseq, *hshape)], axis=1)[:, -n_cache:]
    new_bv_s = jnp.concatenate([cache_band_v[0], s3(vb32_s).reshape(dbs, dseq, *hshape)], axis=1)[:, -n_cache:]
    return (y_p, y_s,
            kf32.reshape(1, bsz, seq, *hshape), vf32.reshape(1, bsz, seq, *hshape),
            lf[:, :N_HEADS].reshape(1, bsz, seq, N_HEADS),
            kb32.reshape(1, bsz, BAND_REACH, *hshape), vb32.reshape(1, bsz, BAND_REACH, *hshape),
            kf32_s.reshape(1, dbs, dseq, *hshape), vf32_s.reshape(1, dbs, dseq, *hshape),
            lf_s[:, :N_HEADS].reshape(1, dbs, dseq, N_HEADS),
            new_bk_s[None], new_bv_s[None])
```

```python
import functools

import jax
import jax.numpy as jnp
import numpy as np
from jax import lax
from jax.experimental import pallas as pl
from jax.experimental.pallas import tpu as pltpu

F32 = jnp.float32
BF16 = jnp.bfloat16

HEAD_DIM = 64
N_HEADS = 8
WIDTH = N_HEADS * HEAD_DIM
PAIR = 2 * HEAD_DIM
N_PAIRS = N_HEADS // 2
LANES = 128
CHUNK = 64
BAND_REACH = 512
REL_CLIP = 256
N_EXPERTS = 64
N_GROUPS = 8
GROUP_SIZE = N_EXPERTS // N_GROUPS
TOPK_GROUPS = 4
TOP_K = 8
ROUTED_SCALE = 2.5
EPS = 1e-6
NEG_INF = -1e30
ATTN_SCALE = HEAD_DIM ** -0.5
LOG2E = 1.4426950408889634
VMEM_LIMIT = 56 * 1024 * 1024


def _cparams(n_axes):
    return pltpu.CompilerParams(dimension_semantics=("arbitrary",) * n_axes,
                                vmem_limit_bytes=VMEM_LIMIT)


def _dot(a, b):
    return jnp.dot(a, b, preferred_element_type=F32)


def _dot_nt(a, b):
    return lax.dot_general(a, b, (((1,), (1,)), ((), ())), preferred_element_type=F32)


def _split2(a):
    hi = a.astype(BF16)
    lo = (a - hi.astype(F32)).astype(BF16)
    return hi, lo


def _split3(a):
    hi = a.astype(BF16)
    r = a - hi.astype(F32)
    mid = r.astype(BF16)
    lo = (r - mid.astype(F32)).astype(BF16)
    return hi, mid, lo


def _ada_body(c_ref, w_ref, b_ref, o_ref):
    c = c_ref[...]
    a = c * jax.nn.sigmoid(c)
    a_hi, a_lo = _split2(a)
    w_hi, w_lo = _split2(w_ref[...])
    o_ref[...] = _dot(a_hi, w_hi) + _dot(a_hi, w_lo) + _dot(a_lo, w_hi) + b_ref[...]


def _ada(c_all, w_ada, b_ada):
    rows, d = c_all.shape
    n = w_ada.shape[1]
    tn = 1024
    return pl.pallas_call(
        _ada_body,
        grid=(n // tn,),
        in_specs=[pl.BlockSpec((rows, d), lambda j: (0, 0)),
                  pl.BlockSpec((d, tn), lambda j: (0, j)),
                  pl.BlockSpec((1, tn), lambda j: (0, j))],
        out_specs=pl.BlockSpec((rows, tn), lambda j: (0, j)),
        out_shape=jax.ShapeDtypeStruct((rows, n), F32),
        compiler_params=_cparams(1),
        name="ada",
    )(c_all, w_ada, b_ada)


def _log_sigmoid(z):
    return jnp.minimum(z, 0.0) - jnp.log(1.0 + jnp.exp(-jnp.abs(z)))


def _proj_body(x_ref, sh_ref, sc_ref, g1_ref, w_ref, bd_ref, gqf_ref, gkf_ref, gqb_ref, gkb_ref, bf_ref, wvt_ref,
               qf_ref, kf_ref, vf_ref, kf32_ref, vf32_ref, lf_ref, qb_ref, kb_ref, vb_ref, kb32_ref, vb32_ref,
               *, band_last_only):
    x = x_ref[...]
    g, r, d = x.shape
    ms = jnp.mean(x * x, axis=-1, keepdims=True)
    h = x * lax.rsqrt(ms + EPS) * g1_ref[...] * (1.0 + sc_ref[...]) + sh_ref[...]
    hb = h.reshape(g * r, d).astype(BF16)

    def seg(i):
        return _dot(hb, w_ref[:, i * WIDTH:(i + 1) * WIDTH])

    def head_norm(t, gain_ref):
        ssq = _dot((t * t).astype(BF16), bd_ref[...])
        return t * lax.rsqrt(ssq + EPS) * gain_ref[...]

    qf_ref[...] = head_norm(seg(0), gqf_ref).astype(BF16)
    kf = head_norm(seg(1), gkf_ref)
    kf32_ref[...] = kf
    kf_ref[...] = kf.astype(BF16)
    vf = seg(2)
    vf32_ref[...] = vf
    if band_last_only:
        vf_ref[...] = _dot_nt(wvt_ref[0:WIDTH, :], hb).astype(BF16)
    else:
        vf_ref[...] = vf.astype(BF16)
    z = _dot(hb, w_ref[:, 6 * WIDTH:6 * WIDTH + LANES]) + bf_ref[...]
    lf_ref[...] = _log_sigmoid(z)
    qb_ref[...] = head_norm(seg(3), gqb_ref).astype(BF16)
    kb = head_norm(seg(4), gkb_ref)
    kb_ref[...] = kb.astype(BF16)
    vb = seg(5)
    if band_last_only:
        vb_ref[...] = _dot_nt(wvt_ref[WIDTH:2 * WIDTH, :], hb).astype(BF16)
    else:
        vb_ref[...] = vb.astype(BF16)

    if band_last_only:
        @pl.when(pl.program_id(1) == pl.num_programs(1) - 1)
        def _():
            kb32_ref[...] = kb
            vb32_ref[...] = vb
    else:
        kb32_ref[...] = kb
        vb32_ref[...] = vb


def _proj(x, shift, scale, g1, w_all, bd, gqf, gkf, gqb, gkb, bf_row, wvt, *, G, R, band_last_only):
    nb, s, d = x.shape
    n = nb * s
    tm = G * R
    nbi, nsi = nb // G, s // R
    grid = (nbi, nsi)
    row = lambda b, i: (b * nsi + i, 0)
    const = lambda b, i: (0, 0)
    mod_spec = pl.BlockSpec((G, 1, d), lambda b, i: (b, 0, 0))
    out_bf = jax.ShapeDtypeStruct((n, WIDTH), BF16)
    out_f32 = jax.ShapeDtypeStruct((n, WIDTH), F32)
    tile = pl.BlockSpec((tm, WIDTH), row)
    if band_last_only:
        assert G == 1 and R == BAND_REACH
        band_shape = jax.ShapeDtypeStruct((nb, BAND_REACH, WIDTH), F32)
        band_spec = pl.BlockSpec((None, BAND_REACH, WIDTH), lambda b, i: (b, 0, 0))
        v_shape = jax.ShapeDtypeStruct((nb, nsi, WIDTH, tm), BF16)
        v_spec = pl.BlockSpec((None, None, WIDTH, tm), lambda b, i: (b, i, 0, 0))
    else:
        band_shape, band_spec = out_f32, tile
        v_shape, v_spec = out_bf, tile
    return pl.pallas_call(
        functools.partial(_proj_body, band_last_only=band_last_only),
        grid=grid,
        in_specs=[pl.BlockSpec((G, R, d), lambda b, i: (b, i, 0)), mod_spec, mod_spec,
                  pl.BlockSpec((1, d), const), pl.BlockSpec(w_all.shape, const), pl.BlockSpec(bd.shape, const),
                  pl.BlockSpec((1, WIDTH), const), pl.BlockSpec((1, WIDTH), const),
                  pl.BlockSpec((1, WIDTH), const), pl.BlockSpec((1, WIDTH), const),
                  pl.BlockSpec((1, LANES), const), pl.BlockSpec(wvt.shape, const)],
        out_specs=[tile, tile, v_spec, tile, tile, pl.BlockSpec((tm, LANES), row), tile, tile, v_spec,
                   band_spec, band_spec],
        out_shape=[out_bf, out_bf, v_shape, out_f32, out_f32, jax.ShapeDtypeStruct((n, LANES), F32),
                   out_bf, out_bf, v_shape, band_shape, band_shape],
        compiler_params=_cparams(2),
        name="proj",
    )(x, shift, scale, g1, w_all, bd, gqf, gkf, gqb, gkb, bf_row, wvt)


def _scan_body(lf_ref, cum_ref, cumt_ref, carry_ref):
    @pl.when(pl.program_id(1) == 0)
    def _():
        carry_ref[...] = jnp.zeros_like(carry_ref)

    lf = lf_ref[...]
    ts = lf.shape[0]
    lane = lax.broadcasted_iota(jnp.int32, lf.shape, 1)
    lf = jnp.where(lane < N_HEADS, lf, 0.0)
    hi, mid, lo = _split3(lf)
    rr = lax.broadcasted_iota(jnp.int32, (ts, ts), 0)
    cc = lax.broadcasted_iota(jnp.int32, (ts, ts), 1)
    tri = jnp.where(cc <= rr, 1.0, 0.0).astype(BF16)
    cum = _dot(tri, hi) + _dot(tri, mid) + _dot(tri, lo) + carry_ref[0:1, :]
    carry_ref[...] = jnp.broadcast_to(cum[ts - 1:ts, :], carry_ref.shape)
    cum2 = cum * LOG2E
    cum_ref[...] = cum2
    cumt_ref[...] = cum2.T[0:N_HEADS, :]


def _scan(lf, *, TS, TK):
    b, s, _ = lf.shape
    assert TS == TK
    t_block = (None, None, N_HEADS, TK)
    t_map = lambda bi, i: (bi, i, 0, 0)
    return pl.pallas_call(
        _scan_body,
        grid=(b, s // TS),
        in_specs=[pl.BlockSpec((None, TS, LANES), lambda bi, i: (bi, i, 0))],
        out_specs=[pl.BlockSpec((None, TS, LANES), lambda bi, i: (bi, i, 0)),
                   pl.BlockSpec(t_block, t_map)],
        out_shape=[jax.ShapeDtypeStruct((b, s, LANES), F32),
                   jax.ShapeDtypeStruct((b, s // TK, N_HEADS, TK), F32)],
        scratch_shapes=[pltpu.VMEM((8, LANES), F32)],
        compiler_params=_cparams(2),
        name="scan",
    )(lf)


def _fox_body(q_ref, k_ref, v_ref, cq_ref, ck_ref, o_ref, *, TQ, TK, q_off):
    p = pl.program_id(1)
    i = pl.program_id(2)
    q = q_ref[...]
    cq_blk = cq_ref[...]
    lane = lax.broadcasted_iota(jnp.int32, (TQ, PAIR), 1)
    ones_blk = jnp.where(lax.broadcasted_iota(jnp.int32, (TK, LANES), 1) == 0, 1.0, 0.0).astype(BF16)
    q0 = q_off + i * TQ
    n_full = q0 // TK
    qpos = q0 + lax.broadcasted_iota(jnp.int32, (TQ, TK), 0)
    kcol = lax.broadcasted_iota(jnp.int32, (TQ, TK), 1)

    outs = []
    for par in range(2):
        h = 2 * p + par
        qm = jnp.where((lane >= HEAD_DIM) == (par == 1), q, jnp.zeros_like(q))
        cq_col = jnp.sum(jnp.where(lane == h, cq_blk, 0.0), axis=1, keepdims=True)
        ref0 = cq_col[0:1, :]
        cqr = cq_col - ref0

        def step(j, carry, masked):
            m, l, acc = carry
            k0 = pl.multiple_of(j * TK, TK)
            kb = k_ref[pl.ds(k0, TK), :]
            vb = v_ref[pl.ds(k0, TK), :]
            s = _dot_nt(qm, kb)
            ck = ck_ref[j, pl.ds(h, 1), :]
            u = s - (ck - ref0)
            if masked:
                u = jnp.where(k0 + kcol <= qpos, u, NEG_INF)
            m_new = jnp.maximum(m, jnp.max(u, axis=1, keepdims=True) + cqr)
            alpha = jnp.exp2(m - m_new)
            pexp = jnp.exp2(u + (cqr - m_new))
            pv = _dot(pexp.astype(BF16), jnp.concatenate([vb, ones_blk], axis=1))
            return m_new, alpha * l + pv[:, LANES:LANES + 1], alpha * acc + pv[:, :LANES]

        init = (jnp.full((TQ, 1), NEG_INF, F32), jnp.zeros((TQ, 1), F32), jnp.zeros((TQ, LANES), F32))
        carry = lax.fori_loop(0, n_full, lambda j, c: step(j, c, False), init)
        _, l, acc = step(n_full, carry, True)
        outs.append(acc / l)
    o_ref[...] = jnp.where(lane < HEAD_DIM, outs[0], outs[1]).astype(o_ref.dtype)


def _fox(q, k, v, cum, cumt, *, TQ, TK, q_off):
    b, sq, _ = q.shape
    sk = k.shape[1]
    return pl.pallas_call(
        functools.partial(_fox_body, TQ=TQ, TK=TK, q_off=q_off),
        grid=(b, N_PAIRS, sq // TQ),
        in_specs=[pl.BlockSpec((None, TQ, PAIR), lambda bi, p, i: (bi, i, p)),
                  pl.BlockSpec((None, sk, PAIR), lambda bi, p, i: (bi, 0, p)),
                  pl.BlockSpec((None, sk, PAIR), lambda bi, p, i: (bi, 0, p)),
                  pl.BlockSpec((None, TQ, LANES), lambda bi, p, i: (bi, i, 0)),
                  pl.BlockSpec((None, sk // TK, N_HEADS, TK), lambda bi, p, i: (bi, 0, 0, 0))],
        out_specs=pl.BlockSpec((None, TQ, PAIR), lambda bi, p, i: (bi, i, p)),
        out_shape=jax.ShapeDtypeStruct((b, sq, WIDTH), BF16),
        compiler_params=_cparams(3),
        name="fox",
    )(q, k, v, cum, cumt)


AUG_PIECES = 3
FOX_UNDERFLOW = 160.0
FOX_NORM_SLACK = 1.02
FOX_BOUND_SLACK = 2.0


def _scan_t_body(lf_ref, place_ref, ct_ref, ka_ref, carry_ref):
    @pl.when(pl.program_id(1) == 0)
    def _():
        carry_ref[...] = jnp.zeros_like(carry_ref)

    lf = lf_ref[...]
    ts = lf.shape[0]
    lane = lax.broadcasted_iota(jnp.int32, lf.shape, 1)
    lf = jnp.where(lane < N_HEADS, lf, 0.0)
    hi, mid, lo = _split3(lf)
    rr = lax.broadcasted_iota(jnp.int32, (ts, ts), 0)
    cc = lax.broadcasted_iota(jnp.int32, (ts, ts), 1)
    tri = jnp.where(cc <= rr, 1.0, 0.0).astype(BF16)
    cum = _dot(tri, hi) + _dot(tri, mid) + _dot(tri, lo) + carry_ref[0:1, :]
    carry_ref[...] = jnp.broadcast_to(cum[ts - 1:ts, :], carry_ref.shape)
    cum2 = cum * LOG2E
    ct_ref[...] = cum2.T[0:N_HEADS, :]
    pieces = _split3(cum2 - cum2[0:1, :])
    ka = _dot(pieces[0], place_ref[0]) + _dot(pieces[1], place_ref[1]) + _dot(pieces[2], place_ref[2])
    ka_ref[...] = ka.astype(BF16)


def _aug_placement():
    h = jnp.arange(LANES)[:, None]
    col = jnp.arange(WIDTH)[None, :]
    mats = []
    for x in range(AUG_PIECES):
        tgt = PAIR * (h // 2) + AUG_PIECES * (h % 2) + x
        mats.append(jnp.where((h < N_HEADS) & (col == tgt), 1.0, 0.0))
    return jnp.stack(mats).astype(BF16)


def _scan_t(lf, *, T):
    b, s, _ = lf.shape
    place = _aug_placement()
    return pl.pallas_call(
        _scan_t_body,
        grid=(b, s // T),
        in_specs=[pl.BlockSpec((None, T, LANES), lambda bi, i: (bi, i, 0)),
                  pl.BlockSpec(place.shape, lambda bi, i: (0, 0, 0))],
        out_specs=[pl.BlockSpec((None, None, N_HEADS, T), lambda bi, i: (bi, i, 0, 0)),
                   pl.BlockSpec((None, T, WIDTH), lambda bi, i: (bi, i, 0))],
        out_shape=[jax.ShapeDtypeStruct((b, s // T, N_HEADS, T), F32),
                   jax.ShapeDtypeStruct((b, s, WIDTH), BF16)],
        scratch_shapes=[pltpu.VMEM((8, LANES), F32)],
        compiler_params=_cparams(2),
        name="scan_t",
    )(lf, place)


def _foxt_body(q_ref, k_ref, ka_ref, vt_ref, ct_ref, o_ref, kn_ref, *, T):
    p = pl.program_id(1)
    i = pl.program_id(2)
    q = q_ref[...]
    lane = lax.broadcasted_iota(jnp.int32, (T, PAIR), 1)
    halves = []
    for par in range(2):
        qm = jnp.where((lane >= HEAD_DIM) == (par == 1), q, jnp.zeros_like(q))
        lo_lane = AUG_PIECES * par
        qa = jnp.where((lane >= lo_lane) & (lane < lo_lane + AUG_PIECES), -1.0, 0.0).astype(BF16)
        halves.append(jnp.concatenate([qm, qa], axis=1))
    qcat = jnp.concatenate(halves, axis=0)
    h_even = 2 * p
    cq = jnp.concatenate([ct_ref[i, pl.ds(h_even, 1), :], ct_ref[i, pl.ds(h_even + 1, 1), :]], axis=1)
    ones_rows = jnp.ones((16, T), BF16)
    krow = lax.broadcasted_iota(jnp.int32, (T, 2 * T), 0)
    qcol = lax.broadcasted_iota(jnp.int32, (T, 2 * T), 1) % T

    def step(j, carry, masked):
        m, acc_e, acc_o = carry
        k0 = pl.multiple_of(j * T, T)
        kcat = jnp.concatenate([k_ref[pl.ds(k0, T), :], ka_ref[pl.ds(k0, T), :]], axis=1)
        st = _dot_nt(kcat, qcat)
        c0 = jnp.concatenate([jnp.broadcast_to(ct_ref[j, pl.ds(h_even, 1), :][:, 0:1], (1, T)),
                              jnp.broadcast_to(ct_ref[j, pl.ds(h_even + 1, 1), :][:, 0:1], (1, T))], axis=1)
        rb = cq - c0
        if masked:
            st = jnp.where(krow <= qcol, st, NEG_INF)
        m_new = jnp.maximum(m, jnp.max(st, axis=0, keepdims=True) + rb)
        alpha = jnp.exp2(m - m_new)
        pt = jnp.exp2(st + (rb - m_new)).astype(BF16)
        vt = vt_ref[j]
        pv_e = _dot(jnp.concatenate([vt[0:HEAD_DIM], ones_rows], axis=0), pt[:, 0:T])
        pv_o = _dot(jnp.concatenate([vt[HEAD_DIM:PAIR], ones_rows], axis=0), pt[:, T:2 * T])
        return m_new, alpha[:, 0:T] * acc_e + pv_e, alpha[:, T:2 * T] * acc_o + pv_o

    @pl.when(i == 0)
    def _():
        ones = jnp.ones((PAIR, LANES), BF16)
        kmax = jnp.zeros((1, LANES), F32)
        for c in range(k_ref.shape[0] // T):
            kc = k_ref[c * T:(c + 1) * T, :].astype(F32)
            kmax = jnp.maximum(kmax, jnp.max(_dot((kc * kc).astype(BF16), ones), axis=0, keepdims=True))
        kn_ref[...] = jnp.broadcast_to(kmax, kn_ref.shape)

    rows = HEAD_DIM + 16
    init = (jnp.full((1, 2 * T), NEG_INF, F32), jnp.zeros((rows, T), F32), jnp.zeros((rows, T), F32))
    carry = step(i, init, True)

    qf = q.astype(F32)
    qsq = qf * qf
    kn2 = kn_ref[0:1, 0:1] * FOX_NORM_SLACK
    need = jnp.zeros((1, 1), jnp.int32)
    blk = lax.broadcasted_iota(jnp.int32, (ct_ref.shape[0], 1, 1), 0)
    for par in range(2):
        head_lanes = (lane >= HEAD_DIM) == (par == 1)
        qn2 = jnp.max(jnp.sum(jnp.where(head_lanes, qsq, 0.0), axis=1, keepdims=True), axis=0, keepdims=True)
        reach = jnp.sqrt(qn2 * kn2) + FOX_BOUND_SLACK
        m_min = jnp.min(carry[0][:, par * T:(par + 1) * T], axis=1, keepdims=True)
        cq_first = cq[:, par * T:par * T + 1]
        ck_end = ct_ref[:, pl.ds(h_even + par, 1), :][:, :, T - 1:T]
        live = (reach + cq_first - m_min)[None, :, :] - ck_end > -FOX_UNDERFLOW
        count = jnp.sum(jnp.where(live & (blk < i), 1, 0), axis=0)
        need = jnp.maximum(need, count)
    n_keep = need[0, 0]

    n_pairs = n_keep // 2
    carry = lax.fori_loop(0, n_pairs, lambda u, c: step(i - 2 - 2 * u, step(i - 1 - 2 * u, c, False), False), carry)
    carry = lax.fori_loop(2 * n_pairs, n_keep, lambda u, c: step(i - 1 - u, c, False), carry)
    _, acc_e, acc_o = carry
    o_t = jnp.concatenate([acc_e[0:HEAD_DIM] / acc_e[HEAD_DIM:HEAD_DIM + 1],
                           acc_o[0:HEAD_DIM] / acc_o[HEAD_DIM:HEAD_DIM + 1]], axis=0)
    o_ref[...] = o_t.T.astype(o_ref.dtype)


def _foxt(q, k, ka, vt, ct, *, T):
    b, s, _ = q.shape
    nt = s // T
    return pl.pallas_call(
        functools.partial(_foxt_body, T=T),
        grid=(b, N_PAIRS, nt),
        in_specs=[pl.BlockSpec((None, T, PAIR), lambda bi, p, i: (bi, i, p)),
                  pl.BlockSpec((None, s, PAIR), lambda bi, p, i: (bi, 0, p)),
                  pl.BlockSpec((None, s, PAIR), lambda bi, p, i: (bi, 0, p)),
                  pl.BlockSpec((None, nt, PAIR, T), lambda bi, p, i: (bi, 0, p, 0)),
                  pl.BlockSpec((None, nt, N_HEADS, T), lambda bi, p, i: (bi, 0, 0, 0))],
        out_specs=pl.BlockSpec((None, T, PAIR), lambda bi, p, i: (bi, i, p)),
        out_shape=jax.ShapeDtypeStruct((b, s, WIDTH), BF16),
        scratch_shapes=[pltpu.VMEM((8, LANES), F32)],
        compiler_params=_cparams(3),
        name="foxt",
    )(q, k, ka, vt, ct)


def _band_body(q_ref, k_ref, v_ref, bias_ref, o_ref, *scratch, TQ, W, n_sub, padded):
    p = pl.program_id(1)
    i = pl.program_id(2)
    if padded:
        kpad_ref, vpad_ref = scratch
        s_len = k_ref.shape[0]

        @pl.when(i == 0)
        def _():
            zeros = jnp.zeros((BAND_REACH, PAIR), BF16)
            kpad_ref[pl.ds(0, BAND_REACH), :] = zeros
            vpad_ref[pl.ds(0, BAND_REACH), :] = zeros
            kpad_ref[pl.ds(BAND_REACH, s_len), :] = k_ref[...]
            vpad_ref[pl.ds(BAND_REACH, s_len), :] = v_ref[...]
    else:
        kpad_ref, vpad_ref = k_ref, v_ref

    lane = lax.broadcasted_iota(jnp.int32, (TQ, PAIR), 1)
    ones_blk = jnp.where(lax.broadcasted_iota(jnp.int32, (W, LANES), 1) == 0, 1.0, 0.0).astype(BF16)
    kcol = lax.broadcasted_iota(jnp.int32, (TQ, W), 1)

    def sub_block(sub, carry):
        r0 = pl.multiple_of(sub * TQ, TQ)
        q0 = i * (n_sub * TQ) + r0
        q = q_ref[pl.ds(r0, TQ), :]
        kw = kpad_ref[pl.ds(pl.multiple_of(q0, TQ), W), :] if padded else kpad_ref[...]
        vw = vpad_ref[pl.ds(pl.multiple_of(q0, TQ), W), :] if padded else vpad_ref[...]
        vcat = jnp.concatenate([vw, ones_blk], axis=1)
        outs = []
        for par in range(2):
            h = 2 * p + par
            qm = jnp.where((lane >= HEAD_DIM) == (par == 1), q, jnp.zeros_like(q))
            s = _dot_nt(qm, kw) + bias_ref[h]
            if padded:
                s = jnp.where(kcol >= BAND_REACH - q0, s, NEG_INF)
            m = jnp.max(s, axis=1, keepdims=True)
            pexp = jnp.exp2(s - m)
            pv = _dot(pexp.astype(BF16), vcat)
            outs.append(pv[:, :LANES] / pv[:, LANES:LANES + 1])
        o_ref[pl.ds(r0, TQ), :] = jnp.where(lane < HEAD_DIM, outs[0], outs[1]).astype(o_ref.dtype)
        return carry

    lax.fori_loop(0, n_sub, sub_block, 0)


def _band(q, k, v, bias, *, TQ, n_sub, padded):
    b, sq, _ = q.shape
    sk = k.shape[1]
    w = BAND_REACH + TQ if padded else sk
    tqb = TQ * n_sub
    scratch = [pltpu.VMEM((sk + BAND_REACH, PAIR), BF16)] * 2 if padded else []
    return pl.pallas_call(
        functools.partial(_band_body, TQ=TQ, W=w, n_sub=n_sub, padded=padded),
        grid=(b, N_PAIRS, sq // tqb),
        in_specs=[pl.BlockSpec((None, tqb, PAIR), lambda bi, p, i: (bi, i, p)),
                  pl.BlockSpec((None, sk, PAIR), lambda bi, p, i: (bi, 0, p)),
                  pl.BlockSpec((None, sk, PAIR), lambda bi, p, i: (bi, 0, p)),
                  pl.BlockSpec(bias.shape, lambda bi, p, i: (0, 0, 0))],
        out_specs=pl.BlockSpec((None, tqb, PAIR), lambda bi, p, i: (bi, i, p)),
        out_shape=jax.ShapeDtypeStruct((b, sq, WIDTH), BF16),
        scratch_shapes=scratch,
        compiler_params=_cparams(3),
        name="band",
    )(q, k, v, bias)


BAND_TQ = 128
BAND_STEP = 512


def _bandt_body(q_ref, kp_ref, kc_ref, vp_ref, vc_ref, bias_ref, o_ref):
    i = pl.program_id(2)
    w = BAND_REACH + BAND_TQ
    k2 = jnp.concatenate([kp_ref[...], kc_ref[...]], axis=0)
    vt2 = jnp.concatenate([vp_ref[...], vc_ref[...]], axis=1)
    lane = lax.broadcasted_iota(jnp.int32, (BAND_TQ, PAIR), 1)
    ones_rows = jnp.ones((16, w), BF16)
    krow = lax.broadcasted_iota(jnp.int32, (w, 2 * BAND_TQ), 0)
    bias = bias_ref[...]
    for sub in range(BAND_STEP // BAND_TQ):
        r0 = sub * BAND_TQ
        q = q_ref[r0:r0 + BAND_TQ, :]
        qcat = jnp.concatenate([jnp.where(lane < HEAD_DIM, q, jnp.zeros_like(q)),
                                jnp.where(lane >= HEAD_DIM, q, jnp.zeros_like(q))], axis=0)
        st = _dot_nt(k2[r0:r0 + w], qcat) + bias
        st = jnp.where(krow >= (1 - i) * BAND_STEP - r0, st, NEG_INF)
        m = jnp.max(st, axis=0, keepdims=True)
        pt = jnp.exp2(st - m).astype(BF16)
        vwin = vt2[:, r0:r0 + w]
        outs = []
        for par in range(2):
            vcat = jnp.concatenate([vwin[par * HEAD_DIM:(par + 1) * HEAD_DIM], ones_rows], axis=0)
            pv = _dot(vcat, pt[:, par * BAND_TQ:(par + 1) * BAND_TQ])
            outs.append(pv[0:HEAD_DIM] / pv[HEAD_DIM:HEAD_DIM + 1])
        o_ref[r0:r0 + BAND_TQ, :] = jnp.concatenate(outs, axis=0).T.astype(o_ref.dtype)


def _bandt(q, k, vt, bias_t):
    b, s, _ = q.shape
    prev = lambda i: jnp.maximum(i - 1, 0)
    return pl.pallas_call(
        _bandt_body,
        grid=(b, N_PAIRS, s // BAND_STEP),
        in_specs=[pl.BlockSpec((None, BAND_STEP, PAIR), lambda bi, p, i: (bi, i, p)),
                  pl.BlockSpec((None, BAND_STEP, PAIR), lambda bi, p, i: (bi, prev(i), p)),
                  pl.BlockSpec((None, BAND_STEP, PAIR), lambda bi, p, i: (bi, i, p)),
                  pl.BlockSpec((None, None, PAIR, BAND_STEP), lambda bi, p, i: (bi, prev(i), p, 0)),
                  pl.BlockSpec((None, None, PAIR, BAND_STEP), lambda bi, p, i: (bi, i, p, 0)),
                  pl.BlockSpec((None,) + bias_t.shape[1:], lambda bi, p, i: (p, 0, 0))],
        out_specs=pl.BlockSpec((None, BAND_STEP, PAIR), lambda bi, p, i: (bi, i, p)),
        out_shape=jax.ShapeDtypeStruct((b, s, WIDTH), BF16),
        compiler_params=_cparams(3),
        name="bandt",
    )(q, k, k, vt, vt, bias_t)


def _band_bias_tile_t(rel_bias):
    tile = _band_bias_tile(rel_bias, BAND_TQ, BAND_REACH + BAND_TQ)
    t = jnp.swapaxes(tile, 1, 2)
    return jnp.concatenate([t[0::2], t[1::2]], axis=2)


def _band_bias_tile(rel_bias, tq, w):
    span = w + tq - 1
    period = span + 1
    v = np.arange(period)
    d = np.where(v < w, v, v - period)
    table_idx = np.clip(BAND_REACH - d, -REL_CLIP, REL_CLIP) + REL_CLIP
    table = rel_bias[:, table_idx] * LOG2E
    n_h = rel_bias.shape[0]
    vals = jnp.tile(table, (1, tq))[:, :tq * span].reshape(n_h, tq, span)[:, :, :w]
    r = np.arange(tq)[:, None]
    c = np.arange(w)[None, :]
    in_band = (c // CHUNK >= r // CHUNK) & (c // CHUNK <= r // CHUNK + BAND_REACH // CHUNK)
    return jnp.where(jnp.asarray(in_band)[None], vals, NEG_INF).astype(F32)


def _first_index(is_max, idx, axis, big):
    return jnp.min(jnp.where(is_max, idx, big), axis=axis, keepdims=True)


def _route(scores, choice):
    t = scores.shape[1]
    c3 = choice.reshape(N_GROUPS, GROUP_SIZE, t)
    j_idx = lax.broadcasted_iota(jnp.int32, c3.shape, 1)
    top1 = jnp.max(c3, axis=1, keepdims=True)
    first = _first_index(c3 == top1, j_idx, 1, GROUP_SIZE)
    top2 = jnp.max(jnp.where(j_idx == first, -jnp.inf, c3), axis=1, keepdims=True)
    gscore = (top1 + top2).reshape(N_GROUPS, t)

    g_idx = lax.broadcasted_iota(jnp.int32, gscore.shape, 0)
    gsel = jnp.zeros(gscore.shape, F32)
    work = gscore
    for _ in range(TOPK_GROUPS):
        gm = jnp.max(work, axis=0, keepdims=True)
        pick = g_idx == _first_index(work == gm, g_idx, 0, N_GROUPS)
        gsel = jnp.where(pick, 1.0, gsel)
        work = jnp.where(pick, -jnp.inf, work)

    emask = jnp.broadcast_to(gsel.reshape(N_GROUPS, 1, t), c3.shape) > 0.0
    work = jnp.where(emask, c3, NEG_INF)
    e_idx = lax.broadcasted_iota(jnp.int32, c3.shape, 0) * GROUP_SIZE + j_idx
    esel = jnp.zeros(c3.shape, F32)
    for _ in range(TOP_K):
        em = jnp.max(jnp.max(work, axis=1, keepdims=True), axis=0, keepdims=True)
        cand = jnp.where(work == em, e_idx, N_EXPERTS)
        first = jnp.min(jnp.min(cand, axis=1, keepdims=True), axis=0, keepdims=True)
        pick = e_idx == first
        esel = jnp.where(pick, 1.0, esel)
        work = jnp.where(pick, -jnp.inf, work)

    w = esel * scores.reshape(c3.shape)
    denom = jnp.sum(jnp.sum(w, axis=1, keepdims=True), axis=0, keepdims=True)
    return (w / denom * ROUTED_SCALE).reshape(N_EXPERTS, t)


def _merge_body(of_ref, ob_ref, x_ref, gate_ref, sh_ref, sc_ref, ogf_ref, ogb_ref, wo_ref, g2_ref,
                wrh_ref, wrl_ref, br_ref, x1_ref, h2_ref, comb_ref, combt_ref, cnt_ref):
    def group_norm(t_ref, gain_ref):
        t = t_ref[...].astype(F32)
        ms = jnp.mean(t * t, axis=-1, keepdims=True)
        return (t * lax.rsqrt(ms + EPS) * gain_ref[...]).astype(BF16)

    y = _dot(group_norm(of_ref, ogf_ref), wo_ref[0:WIDTH, :]) + _dot(group_norm(ob_ref, ogb_ref), wo_ref[WIDTH:, :])
    x = x_ref[...]
    g, r, d = x.shape
    x1 = x + gate_ref[...] * y.reshape(g, r, d)
    x1_ref[...] = x1
    ms = jnp.mean(x1 * x1, axis=-1, keepdims=True)
    h2 = (x1 * lax.rsqrt(ms + EPS) * g2_ref[...] * (1.0 + sc_ref[...]) + sh_ref[...]).reshape(g * r, d)
    h_hi, h_lo = _split2(h2)
    h2_ref[...] = h_hi
    logits = _dot_nt(wrh_ref[...], h_hi) + _dot_nt(wrh_ref[...], h_lo) + _dot_nt(wrl_ref[...], h_hi)
    scores = jax.nn.sigmoid(logits)
    t = scores.shape[1]
    bias = jnp.concatenate([br_ref[...]] * (t // LANES), axis=1)
    comb = _route(scores, scores + bias)
    comb_pad = jnp.concatenate([comb, jnp.zeros((LANES - N_EXPERTS, t), F32)], axis=0)
    comb_ref[...] = comb_pad.T
    combt_ref[...] = comb
    picked = jnp.where(comb > 0.0, 1.0, 0.0)
    for sub in range(cnt_ref.shape[0]):
        cnt = jnp.sum(picked[:, sub * MOE_TM:(sub + 1) * MOE_TM], axis=1, keepdims=True)
        cnt_ref[sub] = jnp.broadcast_to(cnt, (N_EXPERTS, LANES))


def _merge(of, ob, x, gate, shift, scale, ogf, ogb, wo, g2, wr_hi, wr_lo, br, *, G, R):
    nb, s, d = x.shape
    n = nb * s
    tm = G * R
    nbi, nsi = nb // G, s // R
    row = lambda b, i: (b * nsi + i, 0)
    const = lambda b, i: (0, 0)
    mod_spec = pl.BlockSpec((G, 1, d), lambda b, i: (b, 0, 0))
    x_spec = pl.BlockSpec((G, R, d), lambda b, i: (b, i, 0))
    return pl.pallas_call(
        _merge_body,
        grid=(nbi, nsi),
        in_specs=[pl.BlockSpec((tm, WIDTH), row), pl.BlockSpec((tm, WIDTH), row), x_spec,
                  mod_spec, mod_spec, mod_spec,
                  pl.BlockSpec((1, WIDTH), const), pl.BlockSpec((1, WIDTH), const),
                  pl.BlockSpec(wo.shape, const), pl.BlockSpec((1, d), const),
                  pl.BlockSpec(wr_hi.shape, const), pl.BlockSpec(wr_lo.shape, const),
                  pl.BlockSpec(br.shape, const)],
        out_specs=[x_spec, pl.BlockSpec((tm, d), row), pl.BlockSpec((tm, LANES), row),
                   pl.BlockSpec((N_EXPERTS, tm), lambda b, i: (0, b * nsi + i)),
                   pl.BlockSpec((tm // MOE_TM, N_EXPERTS, LANES), lambda b, i: (b * nsi + i, 0, 0))],
        out_shape=[jax.ShapeDtypeStruct((nb, s, d), F32), jax.ShapeDtypeStruct((n, d), BF16),
                   jax.ShapeDtypeStruct((n, LANES), F32), jax.ShapeDtypeStruct((N_EXPERTS, n), F32),
                   jax.ShapeDtypeStruct((n // MOE_TM, N_EXPERTS, LANES), F32)],
        compiler_params=_cparams(2),
        name="merge",
    )(of, ob, x, gate, shift, scale, ogf, ogb, wo, g2, wr_hi, wr_lo, br)


def _silu(g):
    return g * jax.nn.sigmoid(g)


def _moe_body(h_ref, comb_ref, x1_ref, gate_ref, wg_ref, wu_ref, wd_ref, sg_ref, su_ref, sd_ref, y_ref, acc_ref):
    e = pl.program_id(2)
    hb = h_ref[...]

    @pl.when(e == 0)
    def _():
        a = _silu(_dot(hb, sg_ref[...])) * _dot(hb, su_ref[...])
        acc_ref[...] = _dot(a.astype(BF16), sd_ref[...])

    comb = comb_ref[...]
    lane = lax.broadcasted_iota(jnp.int32, comb.shape, 1)
    c_e = jnp.sum(jnp.where(lane == e, comb, 0.0), axis=1, keepdims=True)
    a = _silu(_dot(hb, wg_ref[...])) * _dot(hb, wu_ref[...]) * c_e
    acc_ref[...] += _dot(a.astype(BF16), wd_ref[...])

    @pl.when(e == pl.num_programs(2) - 1)
    def _():
        x1 = x1_ref[...]
        g, r, d = x1.shape
        y_ref[...] = x1 + gate_ref[...] * acc_ref[...].reshape(g, r, d)


def _moe(h2, comb, x1, gate, wg, wu, wd, sg, su, sd, *, G, R):
    nb, s, d = x1.shape
    tm = G * R
    nbi, nsi = nb // G, s // R
    ff = wg.shape[2]
    row = lambda b, i, e: (b * nsi + i, 0)
    const = lambda b, i, e: (0, 0)
    x_spec = pl.BlockSpec((G, R, d), lambda b, i, e: (b, i, 0))
    return pl.pallas_call(
        _moe_body,
        grid=(nbi, nsi, N_EXPERTS),
        in_specs=[pl.BlockSpec((tm, d), row), pl.BlockSpec((tm, LANES), row), x_spec,
                  pl.BlockSpec((G, 1, d), lambda b, i, e: (b, 0, 0)),
                  pl.BlockSpec((None, d, ff), lambda b, i, e: (e, 0, 0)),
                  pl.BlockSpec((None, d, ff), lambda b, i, e: (e, 0, 0)),
                  pl.BlockSpec((None, ff, d), lambda b, i, e: (e, 0, 0)),
                  pl.BlockSpec(sg.shape, const), pl.BlockSpec(su.shape, const), pl.BlockSpec(sd.shape, const)],
        out_specs=x_spec,
        out_shape=jax.ShapeDtypeStruct((nb, s, d), F32),
        scratch_shapes=[pltpu.VMEM((tm, d), F32)],
        compiler_params=_cparams(3),
        name="moe",
    )(h2, comb, x1, gate, wg, wu, wd, sg, su, sd)


MOE_TM = 256
MOE_CH = 16
MOE_SLOTS = TOP_K * MOE_TM + N_EXPERTS * MOE_CH
MOE_NCHUNK = MOE_SLOTS // MOE_CH
MOE_PIECE = 512
MOE_NPIECE = MOE_SLOTS // MOE_PIECE
MOE_CPP = MOE_PIECE // MOE_CH
MOE_RB = 512
TAU_RADIX = 64.0
assert MOE_SLOTS % MOE_PIECE == 0


def _moe_plan(cnt, n_tiles):
    pc = (cnt + MOE_CH - 1) // MOE_CH * MOE_CH
    off = jnp.cumsum(pc, axis=1) - pc
    end = off + pc
    n_used = (jnp.sum(pc, axis=1) // MOE_CH).astype(jnp.int32)
    tot = jnp.sum(pc, axis=0)
    reg = (tot + MOE_RB - 1) // MOE_RB * MOE_RB
    reg_end = jnp.cumsum(reg)
    reg_start = reg_end - reg
    dest_base = reg_start[None, :] + jnp.cumsum(pc, axis=0) - pc
    chunk_row = jnp.arange(MOE_NCHUNK, dtype=jnp.int32)[None, :] * MOE_CH
    in_group = (chunk_row[:, :, None] >= off[:, None, :]) & (chunk_row[:, :, None] < end[:, None, :])
    used = jnp.any(in_group, axis=2)
    cdest = chunk_row + jnp.sum(jnp.where(in_group, (dest_base - off)[:, None, :], 0), axis=2)
    worst_rows = TOP_K * MOE_TM * n_tiles + n_tiles * N_EXPERTS * (MOE_CH - 1) + N_EXPERTS * (MOE_RB - MOE_CH)
    r_max = -(-worst_rows // MOE_RB)
    cdest_d = jnp.where(used, cdest, r_max * MOE_RB + chunk_row).astype(jnp.int32)
    cdest_c = jnp.where(used, cdest, chunk_row).astype(jnp.int32)
    n_active = (reg_end[-1] // MOE_RB).astype(jnp.int32).reshape(1)
    tile_row = jnp.arange(r_max, dtype=jnp.int32) * MOE_RB
    tile_expert = jnp.minimum(jnp.sum((tile_row[:, None] >= reg_end[None, :]).astype(jnp.int32), axis=1),
                              N_EXPERTS - 1).astype(jnp.int32)
    in_region = (tile_row[:, None] >= reg_start[None, :]) & (tile_row[:, None] < reg_end[None, :])
    rows_end = jnp.sum(jnp.where(in_region, (reg_start + tot)[None, :], 0), axis=1)
    valid = jnp.clip(rows_end - tile_row, 0, MOE_RB).astype(jnp.int32)
    f = lambda a: a.astype(F32)
    zeros64 = jnp.zeros((n_tiles, N_EXPERTS), F32)
    row2 = lambda a: jnp.broadcast_to(jnp.concatenate([f(a), f(a)], axis=1)[:, None, :], (n_tiles, 8, LANES))
    col = lambda a: jnp.broadcast_to(f(a)[:, :, None], (n_tiles, N_EXPERTS, LANES))
    col128 = lambda a: jnp.broadcast_to(jnp.concatenate([f(a), zeros64], axis=1)[:, :, None], (n_tiles, LANES, LANES))
    row1 = lambda a: jnp.broadcast_to(jnp.concatenate([f(a), zeros64], axis=1)[:, None, :], (n_tiles, 8, LANES))
    return dict(n_used=n_used, cdest_d=cdest_d.reshape(n_tiles, 1, MOE_NCHUNK),
                cdest_c=cdest_c.reshape(n_tiles, 1, MOE_NCHUNK), r_max=r_max, n_active=n_active,
                tile_expert=tile_expert, valid=valid,
                off_row2=row2(off), end_row2=row2(end), off_col=col(off),
                off_row1=row1(off), off_col128=col128(off), end_col128=col128(end))


def _tau_pieces(sel, tau):
    tau = jnp.where(sel, tau, -1.0)
    hi = jnp.floor(tau * (1.0 / TAU_RADIX)) * TAU_RADIX
    return hi.astype(BF16), (tau - hi).astype(BF16)


def _dispatch_body(nused_ref, cdest_ref, h_ref, combt_ref, offcol_ref, offrow_ref, endrow_ref, sorted_ref,
                   buf_ref, sem):
    t = pl.program_id(0)
    n_used = nused_ref[t]
    tm = h_ref.shape[0]
    sel = combt_ref[...] > 0.0
    rr = lax.broadcasted_iota(jnp.int32, (tm, tm), 0)
    cc = lax.broadcasted_iota(jnp.int32, (tm, tm), 1)
    upper = jnp.where(rr < cc, 1.0, 0.0).astype(BF16)
    rank = _dot(jnp.where(sel, 1.0, 0.0).astype(BF16), upper)
    cols = jnp.concatenate([offcol_ref[...]] * (tm // LANES), axis=1)
    tau_hi, tau_lo = _tau_pieces(sel, cols + rank)
    taucat = jnp.concatenate([tau_hi, tau_lo], axis=0)
    off_row = offrow_ref[0:1, :]
    end_row = endrow_ref[0:1, :]
    hb = h_ref[...]

    def start_piece(piece):
        chunks = range(piece * MOE_CPP, (piece + 1) * MOE_CPP)
        dests = [cdest_ref[0, c] for c in chunks]
        for c, row in zip(chunks, dests):
            dst = sorted_ref.at[pl.ds(pl.multiple_of(row, MOE_CH), MOE_CH), :]
            pltpu.make_async_copy(buf_ref.at[pl.ds(c * MOE_CH, MOE_CH), :], dst, sem).start()

    def wait_piece(piece):
        rows = pl.ds(piece * MOE_PIECE, MOE_PIECE)
        pltpu.make_async_copy(buf_ref.at[rows, :], sorted_ref.at[rows, :], sem).wait()

    for piece in range(MOE_NPIECE):
        @pl.when(piece * MOE_CPP < n_used)
        def _():
            if piece > 0:
                start_piece(piece - 1)
            base = piece * MOE_PIECE
            s_col = (base + lax.broadcasted_iota(jnp.int32, (MOE_PIECE, LANES), 0)).astype(F32)
            onehot = jnp.where((s_col >= off_row) & (s_col < end_row), 1.0, 0.0).astype(BF16)
            q = _dot(onehot, taucat)
            s_mat = (base + lax.broadcasted_iota(jnp.int32, (MOE_PIECE, tm), 0)).astype(F32)
            g = jnp.where(q == s_mat, 1.0, 0.0).astype(BF16)
            buf_ref[pl.ds(base, MOE_PIECE), :] = _dot(g, hb).astype(BF16)

    last = (n_used - 1) // MOE_CPP
    for piece in range(MOE_NPIECE):
        @pl.when(piece == last)
        def _():
            start_piece(piece)

    for piece in range(MOE_NPIECE):
        @pl.when(piece * MOE_CPP < n_used)
        def _():
            wait_piece(piece)


def _dispatch(h2, combt, plan, n_tiles):
    n, d = h2.shape
    r_total = plan["r_max"] * MOE_RB + MOE_SLOTS
    grid_spec = pltpu.PrefetchScalarGridSpec(
        num_scalar_prefetch=1,
        grid=(n_tiles,),
        in_specs=[pl.BlockSpec((None, 1, MOE_NCHUNK), lambda t, nu: (t, 0, 0), memory_space=pltpu.SMEM),
                  pl.BlockSpec((MOE_TM, d), lambda t, nu: (t, 0)),
                  pl.BlockSpec((N_EXPERTS, MOE_TM), lambda t, nu: (0, t)),
                  pl.BlockSpec((None, N_EXPERTS, LANES), lambda t, nu: (t, 0, 0)),
                  pl.BlockSpec((None, 8, LANES), lambda t, nu: (t, 0, 0)),
                  pl.BlockSpec((None, 8, LANES), lambda t, nu: (t, 0, 0))],
        out_specs=pl.BlockSpec(memory_space=pl.ANY),
        scratch_shapes=[pltpu.VMEM((MOE_SLOTS, d), BF16), pltpu.SemaphoreType.DMA(())],
    )
    return pl.pallas_call(
        _dispatch_body,
        grid_spec=grid_spec,
        out_shape=jax.ShapeDtypeStruct((r_total, d), BF16),
        compiler_params=_cparams(1),
        name="moe_dispatch",
    )(plan["n_used"], plan["cdest_d"], h2, combt, plan["off_col"], plan["off_row2"], plan["end_row2"])


def _ffn_body(texp_ref, nact_ref, valid_ref, x_ref, wg_ref, wu_ref, wd_ref, o_ref):
    r = pl.program_id(0)

    @pl.when(r < nact_ref[0])
    def _():
        x = x_ref[...]
        rows = lax.broadcasted_iota(jnp.int32, x.shape, 0)
        x = jnp.where(rows < valid_ref[r], x, jnp.zeros_like(x))
        a = _silu(_dot(x, wg_ref[...])) * _dot(x, wu_ref[...])
        o_ref[...] = _dot(a.astype(BF16), wd_ref[...]).astype(o_ref.dtype)


def _ffn(xs, wg, wu, wd, plan):
    r_total, d = xs.shape
    ff = wg.shape[2]
    last = lambda r, te, na, va: (jnp.minimum(r, na[0] - 1), 0)
    wmap = lambda r, te, na, va: (te[r], 0, 0)
    grid_spec = pltpu.PrefetchScalarGridSpec(
        num_scalar_prefetch=3,
        grid=(plan["r_max"],),
        in_specs=[pl.BlockSpec((MOE_RB, d), last),
                  pl.BlockSpec((None, d, ff), wmap), pl.BlockSpec((None, d, ff), wmap),
                  pl.BlockSpec((None, ff, d), wmap)],
        out_specs=pl.BlockSpec((MOE_RB, d), last),
    )
    return pl.pallas_call(
        _ffn_body,
        grid_spec=grid_spec,
        out_shape=jax.ShapeDtypeStruct((r_total, d), BF16),
        compiler_params=_cparams(1),
        name="moe_ffn",
    )(plan["tile_expert"], plan["n_active"], plan["valid"], xs, wg, wu, wd)


def _combine_body(nused_ref, cdest_ref, h_ref, comb_ref, x1_ref, gate_ref, offrow_ref, offcol_ref, endcol_ref,
                  sg_ref, su_ref, sd_ref, ys_ref, y_ref, buf_ref, acc_ref, sem):
    t = pl.program_id(0)
    n_used = nused_ref[t]
    tm = h_ref.shape[0]

    @pl.when(t == 0)
    def _():
        buf_ref[...] = jnp.zeros_like(buf_ref)

    def wait_piece(piece):
        rows = pl.ds(piece * MOE_PIECE, MOE_PIECE)
        pltpu.make_async_copy(ys_ref.at[rows, :], buf_ref.at[rows, :], sem.at[piece]).wait()

    for piece in range(MOE_NPIECE):
        @pl.when(piece * MOE_CPP < n_used)
        def _():
            chunks = range(piece * MOE_CPP, (piece + 1) * MOE_CPP)
            srcs = [cdest_ref[0, c] for c in chunks]
            for c, row in zip(chunks, srcs):
                src = ys_ref.at[pl.ds(pl.multiple_of(row, MOE_CH), MOE_CH), :]
                pltpu.make_async_copy(src, buf_ref.at[pl.ds(c * MOE_CH, MOE_CH), :], sem.at[piece]).start()

    hb = h_ref[...]
    acc = _dot((_silu(_dot(hb, sg_ref[...])) * _dot(hb, su_ref[...])).astype(BF16), sd_ref[...])
    comb = comb_ref[...]
    sel = comb > 0.0
    rr = lax.broadcasted_iota(jnp.int32, (tm, tm), 0)
    cc = lax.broadcasted_iota(jnp.int32, (tm, tm), 1)
    lower = jnp.where(cc < rr, 1.0, 0.0).astype(BF16)
    rank = _dot(lower, jnp.where(sel, 1.0, 0.0).astype(BF16))
    tau_hi, tau_lo = _tau_pieces(sel, offrow_ref[0:1, :] + rank)
    taucat = jnp.concatenate([tau_hi, tau_lo], axis=1)
    lhs = jnp.concatenate([taucat, jnp.concatenate([comb.astype(BF16), jnp.zeros((tm, LANES), BF16)], axis=1)],
                          axis=0)
    off_col = offcol_ref[...]
    end_col = endcol_ref[...]
    acc_ref[...] = acc

    for piece in range(MOE_NPIECE):
        @pl.when(piece * MOE_CPP < n_used)
        def _():
            base = piece * MOE_PIECE
            s_row = (base + lax.broadcasted_iota(jnp.int32, (LANES, MOE_PIECE), 1)).astype(F32)
            off_b = jnp.concatenate([off_col] * (MOE_PIECE // LANES), axis=1)
            end_b = jnp.concatenate([end_col] * (MOE_PIECE // LANES), axis=1)
            onehot = jnp.where((s_row >= off_b) & (s_row < end_b), 1.0, 0.0).astype(BF16)
            qw = _dot(lhs, jnp.concatenate([onehot, onehot], axis=0))
            s_mat = (base + lax.broadcasted_iota(jnp.int32, (tm, MOE_PIECE), 1)).astype(F32)
            gw = jnp.where(qw[0:tm] == s_mat, qw[tm:2 * tm], 0.0).astype(BF16)
            wait_piece(piece)
            acc_ref[...] += _dot(gw, buf_ref[pl.ds(base, MOE_PIECE), :])
    y_ref[...] = x1_ref[...] + gate_ref[...] * acc_ref[...]


def _combine(h2, comb, x1, gate, ys, sg, su, sd, plan, n_tiles):
    nb, s, d = x1.shape
    n = nb * s
    per_b = s // MOE_TM
    x1f = x1.reshape(n, d)
    const2 = lambda t, nu: (0, 0)
    grid_spec = pltpu.PrefetchScalarGridSpec(
        num_scalar_prefetch=1,
        grid=(n_tiles,),
        in_specs=[pl.BlockSpec((None, 1, MOE_NCHUNK), lambda t, nu: (t, 0, 0), memory_space=pltpu.SMEM),
                  pl.BlockSpec((MOE_TM, d), lambda t, nu: (t, 0)),
                  pl.BlockSpec((MOE_TM, LANES), lambda t, nu: (t, 0)),
                  pl.BlockSpec((MOE_TM, d), lambda t, nu: (t, 0)),
                  pl.BlockSpec((None, 1, d), lambda t, nu: (t // per_b, 0, 0)),
                  pl.BlockSpec((None, 8, LANES), lambda t, nu: (t, 0, 0)),
                  pl.BlockSpec((None, LANES, LANES), lambda t, nu: (t, 0, 0)),
                  pl.BlockSpec((None, LANES, LANES), lambda t, nu: (t, 0, 0)),
                  pl.BlockSpec(sg.shape, const2), pl.BlockSpec(su.shape, const2), pl.BlockSpec(sd.shape, const2),
                  pl.BlockSpec(memory_space=pl.ANY)],
        out_specs=pl.BlockSpec((MOE_TM, d), lambda t, nu: (t, 0)),
        scratch_shapes=[pltpu.VMEM((MOE_SLOTS, d), BF16), pltpu.VMEM((MOE_TM, d), F32),
                        pltpu.SemaphoreType.DMA((MOE_NPIECE,))],
    )
    y = pl.pallas_call(
        _combine_body,
        grid_spec=grid_spec,
        out_shape=jax.ShapeDtypeStruct((n, d), F32),
        compiler_params=_cparams(1),
        name="moe_combine",
    )(plan["n_used"], plan["cdest_c"], h2, comb, x1f, gate, plan["off_row1"], plan["off_col128"], plan["end_col128"],
      sg, su, sd, ys)
    return y.reshape(nb, s, d)


def _moe_sparse(h2, comb, combt, cnt, x1, gate, wg, wu, wd, sg, su, sd):
    n = h2.shape[0]
    n_tiles = n // MOE_TM
    plan = _moe_plan(cnt[:, :, 0].astype(jnp.int32), n_tiles)
    xs = _dispatch(h2, combt, plan, n_tiles)
    ys = _ffn(xs, wg, wu, wd, plan)
    return _combine(h2, comb, x1, gate, ys, sg, su, sd, plan, n_tiles)


def _tile_heads(g, mult=1.0):
    return (jnp.tile(g.astype(F32), N_HEADS) * mult).reshape(1, WIDTH)


def kernel(x_prompt, x_sample, cache_fox_k, cache_fox_v, cache_fox_logf, cache_band_k, cache_band_v, c_prompt, c_sample, w_ada, b_ada, norm1_g, norm2_g, w_in, b_forget, g_q_fox, g_k_fox, g_q_band, g_k_band, rel_bias, out_g_fox, out_g_band, w_out, w_router, b_router, w_gate, w_up, w_down, ws_gate, ws_up, ws_down):
    depth = w_ada.shape[0]
    assert depth == 1
    bsz, seq, d = x_prompt.shape
    dbs, dseq, _ = x_sample.shape
    past = cache_fox_k.shape[2]
    n_cache = cache_band_k.shape[2]
    assert n_cache == BAND_REACH and dseq == CHUNK and seq % BAND_REACH == 0

    wi = w_in[0]
    cols = [wi[:, 0:512], wi[:, 512:1024], wi[:, 1024:1536], wi[:, 1544:2056], wi[:, 2056:2568], wi[:, 2568:3080],
            wi[:, 1536:1544], jnp.zeros((d, LANES - N_HEADS), F32)]
    w_all = jnp.concatenate(cols, axis=1).astype(BF16)
    hd = jnp.arange(WIDTH) // HEAD_DIM
    bd = jnp.where(hd[:, None] == hd[None, :], 1.0 / HEAD_DIM, 0.0).astype(BF16)
    qscale = ATTN_SCALE * LOG2E
    gqf, gkf = _tile_heads(g_q_fox[0], qscale), _tile_heads(g_k_fox[0])
    gqb, gkb = _tile_heads(g_q_band[0], qscale), _tile_heads(g_k_band[0])
    bf_row = jnp.concatenate([b_forget[0], jnp.zeros((LANES - N_HEADS,), F32)]).reshape(1, LANES)
    g1 = norm1_g[0].reshape(1, d)
    g2 = norm2_g[0].reshape(1, d)
    ogf = out_g_fox[0].reshape(1, WIDTH)
    ogb = out_g_band[0].reshape(1, WIDTH)
    wo = w_out[0].astype(BF16)
    wr_t = w_router[0].T
    wr_hi = wr_t.astype(BF16)
    wr_lo = (wr_t - wr_hi.astype(F32)).astype(BF16)
    br = jnp.broadcast_to(b_router[0].reshape(N_EXPERTS, 1), (N_EXPERTS, LANES)).astype(F32)
    wg, wu, wd = w_gate[0].astype(BF16), w_up[0].astype(BF16), w_down[0].astype(BF16)
    sg, su, sd = ws_gate[0].astype(BF16), ws_up[0].astype(BF16), ws_down[0].astype(BF16)
    wvt = jnp.concatenate([wi[:, 2 * WIDTH:3 * WIDTH], wi[:, 2568:3080]], axis=1).T.astype(BF16)

    n_c = bsz + dbs
    rows = -(-n_c // 8) * 8
    c_all = jnp.concatenate([c_prompt, c_sample, jnp.zeros((rows - n_c, d), F32)], axis=0)
    mod = _ada(c_all, w_ada[0], b_ada[0].reshape(1, -1))

    def mods(lo, hi):
        return [mod[lo:hi, j * d:(j + 1) * d].reshape(hi - lo, 1, d) for j in range(6)]

    shift1_p, scale1_p, gate1_p, shift2_p, scale2_p, gate2_p = mods(0, bsz)
    shift1_s, scale1_s, gate1_s, shift2_s, scale2_s, gate2_s = mods(bsz, n_c)

    TM = BAND_REACH
    (qf, kf, vft, kf32, vf32, lf, qb, kb, vbt, kb32, vb32) = _proj(
        x_prompt, shift1_p, scale1_p, g1, w_all, bd, gqf, gkf, gqb, gkb, bf_row, wvt, G=1, R=TM, band_last_only=True)
    r3 = lambda a: a.reshape(bsz, seq, a.shape[-1])
    ct, ka = _scan_t(r3(lf), T=TM)
    of = _foxt(r3(qf), r3(kf), ka, vft, ct, T=TM)
    assert TM == BAND_STEP
    ob = _bandt(r3(qb), r3(kb), vbt, _band_bias_tile_t(rel_bias[0]))
    assert TM % MOE_TM == 0
    x1_p, h2_p, comb_p, combt_p, cnt_p = _merge(of.reshape(-1, WIDTH), ob.reshape(-1, WIDTH), x_prompt, gate1_p,
                                                shift2_p, scale2_p, ogf, ogb, wo, g2, wr_hi, wr_lo, br, G=1, R=TM)
    y_p = _moe_sparse(h2_p, comb_p, combt_p, cnt_p, x1_p, gate2_p, wg, wu, wd, sg, su, sd)

    GS = 8
    (qf_s, kf_s, vf_s, kf32_s, vf32_s, lf_s, qb_s, kb_s, vb_s, kb32_s, vb32_s) = _proj(
        x_sample, shift1_s, scale1_s, g1, w_all, bd, gqf, gkf, gqb, gkb, bf_row, wvt, G=GS, R=dseq,
        band_last_only=False)
    s3 = lambda a: a.reshape(dbs, dseq, a.shape[-1])
    sk = past + dseq
    skp = -(-sk // LANES) * LANES
    pad_k = skp - sk
    lf_cache = jnp.pad(cache_fox_logf[0], ((0, 0), (0, 0), (0, LANES - N_HEADS)))
    lf_all = jnp.concatenate([lf_cache, s3(lf_s), jnp.zeros((dbs, pad_k, LANES), F32)], axis=1)
    cum_s, cumt_s = _scan(lf_all, TS=skp, TK=skp)
    zpad = jnp.zeros((dbs, pad_k, WIDTH), BF16)
    k_all = jnp.concatenate([cache_fox_k[0].reshape(dbs, past, WIDTH).astype(BF16), s3(kf_s), zpad], axis=1)
    v_all = jnp.concatenate([cache_fox_v[0].reshape(dbs, past, WIDTH).astype(BF16), s3(vf_s), zpad], axis=1)
    of_s = _fox(s3(qf_s), k_all, v_all, cum_s[:, past:past + dseq], cumt_s, TQ=dseq, TK=skp, q_off=past)
    bias_s = _band_bias_tile(rel_bias[0], dseq, BAND_REACH + LANES)
    zb = jnp.zeros((dbs, LANES - dseq, WIDTH), BF16)
    kb_all = jnp.concatenate([cache_band_k[0].reshape(dbs, n_cache, WIDTH).astype(BF16), s3(kb_s), zb], axis=1)
    vb_all = jnp.concatenate([cache_band_v[0].reshape(dbs, n_cache, WIDTH).astype(BF16), s3(vb_s), zb], axis=1)
    ob_s = _band(s3(qb_s), kb_all, vb_all, bias_s, TQ=dseq, n_sub=1, padded=False)
    x1_s, h2_s, comb_s, _, _ = _merge(of_s.reshape(-1, WIDTH), ob_s.reshape(-1, WIDTH), x_sample, gate1_s, shift2_s,
                                      scale2_s, ogf, ogb, wo, g2, wr_hi, wr_lo, br, G=GS, R=dseq)
    y_s = _moe(h2_s, comb_s, x1_s, gate2_s, wg, wu, wd, sg, su, sd, G=dbs, R=dseq)

    hshape = (N_HEADS, HEAD_DIM)
    new_bk_s = jnp.concatenate([cache_band_k[0], s3(kb32_s).reshape(dbs, dseq, *hshape)], axis=1)[:, -n_cache:]
    new_bv_s = jnp.concatenate([cache_band_v[0], s3(vb32_s).reshape(dbs, dseq, *hshape)], axis=1)[:, -n_cache:]
    return (y_p, y_s,
            kf32.reshape(1, bsz, seq, *hshape), vf32.reshape(1, bsz, seq, *hshape),
            lf[:, :N_HEADS].reshape(1, bsz, seq, N_HEADS),
            kb32.reshape(1, bsz, BAND_REACH, *hshape), vb32.reshape(1, bsz, BAND_REACH, *hshape),
            kf32_s.reshape(1, dbs, dseq, *hshape), vf32_s.reshape(1, dbs, dseq, *hshape),
            lf_s[:, :N_HEADS].reshape(1, dbs, dseq, N_HEADS),
            new_bk_s[None], new_bv_s[None])
```

```python
import functools

import jax
import jax.numpy as jnp
import numpy as np
from jax import lax
from jax.experimental import pallas as pl
from jax.experimental.pallas import tpu as pltpu

F32 = jnp.float32
BF16 = jnp.bfloat16

HEAD_DIM = 64
N_HEADS = 8
WIDTH = N_HEADS * HEAD_DIM
PAIR = 2 * HEAD_DIM
N_PAIRS = N_HEADS // 2
LANES = 128
CHUNK = 64
BAND_REACH = 512
REL_CLIP = 256
N_EXPERTS = 64
N_GROUPS = 8
GROUP_SIZE = N_EXPERTS // N_GROUPS
TOPK_GROUPS = 4
TOP_K = 8
ROUTED_SCALE = 2.5
EPS = 1e-6
NEG_INF = -1e30
ATTN_SCALE = HEAD_DIM ** -0.5
LOG2E = 1.4426950408889634
VMEM_LIMIT = 56 * 1024 * 1024


def _cparams(n_axes):
    return pltpu.CompilerParams(dimension_semantics=("arbitrary",) * n_axes,
                                vmem_limit_bytes=VMEM_LIMIT)


def _dot(a, b):
    return jnp.dot(a, b, preferred_element_type=F32)


def _dot_nt(a, b):
    return lax.dot_general(a, b, (((1,), (1,)), ((), ())), preferred_element_type=F32)


def _split2(a):
    hi = a.astype(BF16)
    lo = (a - hi.astype(F32)).astype(BF16)
    return hi, lo


def _split3(a):
    hi = a.astype(BF16)
    r = a - hi.astype(F32)
    mid = r.astype(BF16)
    lo = (r - mid.astype(F32)).astype(BF16)
    return hi, mid, lo


def _ada_body(c_ref, w_ref, b_ref, o_ref):
    c = c_ref[...]
    a = c * jax.nn.sigmoid(c)
    a_hi, a_lo = _split2(a)
    w_hi, w_lo = _split2(w_ref[...])
    o_ref[...] = _dot(a_hi, w_hi) + _dot(a_hi, w_lo) + _dot(a_lo, w_hi) + b_ref[...]


def _ada(c_all, w_ada, b_ada):
    rows, d = c_all.shape
    n = w_ada.shape[1]
    tn = 1024
    return pl.pallas_call(
        _ada_body,
        grid=(n // tn,),
        in_specs=[pl.BlockSpec((rows, d), lambda j: (0, 0)),
                  pl.BlockSpec((d, tn), lambda j: (0, j)),
                  pl.BlockSpec((1, tn), lambda j: (0, j))],
        out_specs=pl.BlockSpec((rows, tn), lambda j: (0, j)),
        out_shape=jax.ShapeDtypeStruct((rows, n), F32),
        compiler_params=_cparams(1),
        name="ada",
    )(c_all, w_ada, b_ada)


PROJ_KPERM = 6 * WIDTH + LANES
FOX_LANE0 = N_HEADS


def _log_sigmoid(z):
    return jnp.minimum(z, 0.0) - jnp.log(1.0 + jnp.exp(-jnp.abs(z)))


def _proj_body(x_ref, sh_ref, sc_ref, g1_ref, w_ref, bd_ref, gqf_ref, gkf_ref, gqb_ref, gkb_ref, bf_ref, wvt_ref,
               qf_ref, kf_ref, vf_ref, kf32_ref, vf32_ref, lf_ref, qb_ref, kb_ref, vb_ref, kb32_ref, vb32_ref,
               *, band_last_only):
    x = x_ref[...]
    g, r, d = x.shape
    ms = jnp.mean(x * x, axis=-1, keepdims=True)
    h = x * lax.rsqrt(ms + EPS) * g1_ref[...] * (1.0 + sc_ref[...]) + sh_ref[...]
    hb = h.reshape(g * r, d).astype(BF16)

    def seg(i):
        return _dot(hb, w_ref[:, i * WIDTH:(i + 1) * WIDTH])

    def head_norm(t, gain_ref):
        ssq = _dot((t * t).astype(BF16), bd_ref[...])
        return t * lax.rsqrt(ssq + EPS) * gain_ref[...]

    qf_ref[...] = head_norm(seg(0), gqf_ref).astype(BF16)
    kf = head_norm(seg(1), gkf_ref)
    kf32_ref[...] = kf
    if band_last_only:
        kf_ref[...] = head_norm(_dot(hb, w_ref[:, PROJ_KPERM:PROJ_KPERM + WIDTH]), gkf_ref).astype(BF16)
    else:
        kf_ref[...] = kf.astype(BF16)
    vf = seg(2)
    vf32_ref[...] = vf
    if band_last_only:
        vf_ref[...] = _dot_nt(wvt_ref[0:WIDTH, :], hb).astype(BF16)
    else:
        vf_ref[...] = vf.astype(BF16)
    z = _dot(hb, w_ref[:, 6 * WIDTH:6 * WIDTH + LANES]) + bf_ref[...]
    lf_ref[...] = _log_sigmoid(z)
    qb_ref[...] = head_norm(seg(3), gqb_ref).astype(BF16)
    kb = head_norm(seg(4), gkb_ref)
    kb_ref[...] = kb.astype(BF16)
    vb = seg(5)
    if band_last_only:
        vb_ref[...] = _dot_nt(wvt_ref[WIDTH:2 * WIDTH, :], hb).astype(BF16)
    else:
        vb_ref[...] = vb.astype(BF16)

    if band_last_only:
        @pl.when(pl.program_id(1) == pl.num_programs(1) - 1)
        def _():
            kb32_ref[...] = kb
            vb32_ref[...] = vb
    else:
        kb32_ref[...] = kb
        vb32_ref[...] = vb


def _proj(x, shift, scale, g1, w_all, bd, gqf, gkf, gqb, gkb, bf_row, wvt, *, G, R, band_last_only):
    nb, s, d = x.shape
    n = nb * s
    tm = G * R
    nbi, nsi = nb // G, s // R
    grid = (nbi, nsi)
    row = lambda b, i: (b * nsi + i, 0)
    const = lambda b, i: (0, 0)
    mod_spec = pl.BlockSpec((G, 1, d), lambda b, i: (b, 0, 0))
    out_bf = jax.ShapeDtypeStruct((n, WIDTH), BF16)
    out_f32 = jax.ShapeDtypeStruct((n, WIDTH), F32)
    tile = pl.BlockSpec((tm, WIDTH), row)
    if band_last_only:
        assert G == 1 and R == BAND_REACH
        band_shape = jax.ShapeDtypeStruct((nb, BAND_REACH, WIDTH), F32)
        band_spec = pl.BlockSpec((None, BAND_REACH, WIDTH), lambda b, i: (b, 0, 0))
        v_shape = jax.ShapeDtypeStruct((nb, nsi, WIDTH, tm), BF16)
        v_spec = pl.BlockSpec((None, None, WIDTH, tm), lambda b, i: (b, i, 0, 0))
    else:
        band_shape, band_spec = out_f32, tile
        v_shape, v_spec = out_bf, tile
    return pl.pallas_call(
        functools.partial(_proj_body, band_last_only=band_last_only),
        grid=grid,
        in_specs=[pl.BlockSpec((G, R, d), lambda b, i: (b, i, 0)), mod_spec, mod_spec,
                  pl.BlockSpec((1, d), const), pl.BlockSpec(w_all.shape, const), pl.BlockSpec(bd.shape, const),
                  pl.BlockSpec((1, WIDTH), const), pl.BlockSpec((1, WIDTH), const),
                  pl.BlockSpec((1, WIDTH), const), pl.BlockSpec((1, WIDTH), const),
                  pl.BlockSpec((1, LANES), const), pl.BlockSpec(wvt.shape, const)],
        out_specs=[tile, tile, v_spec, tile, tile, pl.BlockSpec((tm, LANES), row), tile, tile, v_spec,
                   band_spec, band_spec],
        out_shape=[out_bf, out_bf, v_shape, out_f32, out_f32, jax.ShapeDtypeStruct((n, LANES), F32),
                   out_bf, out_bf, v_shape, band_shape, band_shape],
        compiler_params=_cparams(2),
        name="proj",
    )(x, shift, scale, g1, w_all, bd, gqf, gkf, gqb, gkb, bf_row, wvt)


def _scan_body(lf_ref, cum_ref, cumt_ref):
    lf = lf_ref[...]
    s = lf.shape[0]
    hi, mid, lo = _split3(lf)
    rr = lax.broadcasted_iota(jnp.int32, (s, s), 0)
    cc = lax.broadcasted_iota(jnp.int32, (s, s), 1)
    tri = jnp.where(cc <= rr, 1.0, 0.0).astype(BF16)
    cum2 = (_dot(tri, hi) + _dot(tri, mid) + _dot(tri, lo)) * LOG2E
    cum_ref[...] = cum2
    cumt_ref[...] = cum2.T


def _scan(lf):
    s, _ = lf.shape
    return pl.pallas_call(
        _scan_body,
        out_shape=[jax.ShapeDtypeStruct((s, LANES), F32), jax.ShapeDtypeStruct((LANES, s), F32)],
        compiler_params=pltpu.CompilerParams(vmem_limit_bytes=VMEM_LIMIT),
        name="scan",
    )(lf)


def _fox_body(q_ref, k_ref, v_ref, cq_ref, ck_ref, o_ref, *, TQ, TK, q_off):
    p = pl.program_id(1)
    i = pl.program_id(2)
    q = q_ref[...]
    cq_blk = cq_ref[...]
    lane = lax.broadcasted_iota(jnp.int32, (TQ, PAIR), 1)
    ones_blk = jnp.where(lax.broadcasted_iota(jnp.int32, (TK, LANES), 1) == 0, 1.0, 0.0).astype(BF16)
    q0 = q_off + i * TQ
    n_full = q0 // TK
    qpos = q0 + lax.broadcasted_iota(jnp.int32, (TQ, TK), 0)
    kcol = lax.broadcasted_iota(jnp.int32, (TQ, TK), 1)

    outs = []
    for par in range(2):
        h = 2 * p + par
        qm = jnp.where((lane >= HEAD_DIM) == (par == 1), q, jnp.zeros_like(q))
        cq_col = jnp.sum(jnp.where(lane == h, cq_blk, 0.0), axis=1, keepdims=True)
        ref0 = cq_col[0:1, :]
        cqr = cq_col - ref0

        def step(j, carry, masked):
            m, l, acc = carry
            k0 = pl.multiple_of(j * TK, TK)
            kb = k_ref[pl.ds(k0, TK), :]
            vb = v_ref[pl.ds(k0, TK), :]
            s = _dot_nt(qm, kb)
            ck = ck_ref[j, pl.ds(h, 1), :]
            u = s - (ck - ref0)
            if masked:
                u = jnp.where(k0 + kcol <= qpos, u, NEG_INF)
            m_new = jnp.maximum(m, jnp.max(u, axis=1, keepdims=True) + cqr)
            alpha = jnp.exp2(m - m_new)
            pexp = jnp.exp2(u + (cqr - m_new))
            pv = _dot(pexp.astype(BF16), jnp.concatenate([vb, ones_blk], axis=1))
            return m_new, alpha * l + pv[:, LANES:LANES + 1], alpha * acc + pv[:, :LANES]

        init = (jnp.full((TQ, 1), NEG_INF, F32), jnp.zeros((TQ, 1), F32), jnp.zeros((TQ, LANES), F32))
        carry = lax.fori_loop(0, n_full, lambda j, c: step(j, c, False), init)
        _, l, acc = step(n_full, carry, True)
        outs.append(acc / l)
    o_ref[...] = jnp.where(lane < HEAD_DIM, outs[0], outs[1]).astype(o_ref.dtype)


def _fox(q, k, v, cum, cumt, *, TQ, TK, q_off):
    b, sq, _ = q.shape
    sk = k.shape[1]
    return pl.pallas_call(
        functools.partial(_fox_body, TQ=TQ, TK=TK, q_off=q_off),
        grid=(b, N_PAIRS, sq // TQ),
        in_specs=[pl.BlockSpec((None, TQ, PAIR), lambda bi, p, i: (bi, i, p)),
                  pl.BlockSpec((None, sk, PAIR), lambda bi, p, i: (bi, 0, p)),
                  pl.BlockSpec((None, sk, PAIR), lambda bi, p, i: (bi, 0, p)),
                  pl.BlockSpec((None, TQ, LANES), lambda bi, p, i: (bi, i, 0)),
                  pl.BlockSpec((None, sk // TK, N_HEADS, TK), lambda bi, p, i: (bi, 0, 0, 0))],
        out_specs=pl.BlockSpec((None, TQ, PAIR), lambda bi, p, i: (bi, i, p)),
        out_shape=jax.ShapeDtypeStruct((b, sq, WIDTH), BF16),
        compiler_params=_cparams(3),
        name="fox",
    )(q, k, v, cum, cumt)


AUG_PIECES = 3
FOX_UNDERFLOW = 160.0
FOX_NORM_SLACK = 1.02
FOX_BOUND_SLACK = 2.0


def _scan_t_body(lf_ref, place_ref, ct_ref, ka_ref, carry_ref):
    @pl.when(pl.program_id(1) == 0)
    def _():
        carry_ref[...] = jnp.zeros_like(carry_ref)

    lf = lf_ref[...]
    ts = lf.shape[0]
    lane = lax.broadcasted_iota(jnp.int32, lf.shape, 1)
    lf = jnp.where((lane >= FOX_LANE0) & (lane < FOX_LANE0 + N_HEADS), lf, 0.0)
    hi, mid, lo = _split3(lf)
    rr = lax.broadcasted_iota(jnp.int32, (ts, ts), 0)
    cc = lax.broadcasted_iota(jnp.int32, (ts, ts), 1)
    tri = jnp.where(cc <= rr, 1.0, 0.0).astype(BF16)
    cum = _dot(tri, hi) + _dot(tri, mid) + _dot(tri, lo) + carry_ref[0:1, :]
    carry_ref[...] = jnp.broadcast_to(cum[ts - 1:ts, :], carry_ref.shape)
    cum2 = cum * LOG2E
    ct_ref[...] = cum2.T[FOX_LANE0:FOX_LANE0 + N_HEADS, :]
    pieces = _split3(cum2 - cum2[0:1, :])
    ka = _dot(pieces[0], place_ref[0]) + _dot(pieces[1], place_ref[1]) + _dot(pieces[2], place_ref[2])
    ka_ref[...] = ka.astype(BF16)


def _aug_placement():
    h = jnp.arange(LANES)[:, None] - FOX_LANE0
    col = jnp.arange(WIDTH)[None, :]
    mats = []
    for x in range(AUG_PIECES):
        tgt = PAIR * (h // 2) + AUG_PIECES * (h % 2) + x
        mats.append(jnp.where((h >= 0) & (h < N_HEADS) & (col == tgt), 1.0, 0.0))
    return jnp.stack(mats).astype(BF16)


def _scan_t(lf, *, T):
    b, s, _ = lf.shape
    place = _aug_placement()
    return pl.pallas_call(
        _scan_t_body,
        grid=(b, s // T),
        in_specs=[pl.BlockSpec((None, T, LANES), lambda bi, i: (bi, i, 0)),
                  pl.BlockSpec(place.shape, lambda bi, i: (0, 0, 0))],
        out_specs=[pl.BlockSpec((None, None, N_HEADS, T), lambda bi, i: (bi, i, 0, 0)),
                   pl.BlockSpec((None, T, WIDTH), lambda bi, i: (bi, i, 0))],
        out_shape=[jax.ShapeDtypeStruct((b, s // T, N_HEADS, T), F32),
                   jax.ShapeDtypeStruct((b, s, WIDTH), BF16)],
        scratch_shapes=[pltpu.VMEM((8, LANES), F32)],
        compiler_params=_cparams(2),
        name="scan_t",
    )(lf, place)


def _foxt_body(q_ref, k_ref, ka_ref, vt_ref, ct_ref, o_ref, kn_ref, *, T):
    p = pl.program_id(1)
    i = pl.program_id(2)
    q = q_ref[...]
    lane = lax.broadcasted_iota(jnp.int32, (T, PAIR), 1)
    halves = []
    for par in range(2):
        qm = jnp.where((lane >= HEAD_DIM) == (par == 1), q, jnp.zeros_like(q))
        lo_lane = AUG_PIECES * par
        qa = jnp.where((lane >= lo_lane) & (lane < lo_lane + AUG_PIECES), -1.0, 0.0).astype(BF16)
        halves.append(jnp.concatenate([qm, qa], axis=1))
    qcat = jnp.concatenate(halves, axis=0)
    h_even = 2 * p
    cq = jnp.concatenate([ct_ref[i, pl.ds(h_even, 1), :], ct_ref[i, pl.ds(h_even + 1, 1), :]], axis=1)
    ones_rows = jnp.ones((16, T), BF16)
    krow = lax.broadcasted_iota(jnp.int32, (T, 2 * T), 0)
    qcol = lax.broadcasted_iota(jnp.int32, (T, 2 * T), 1) % T

    def step(j, carry, masked):
        m, acc_e, acc_o = carry
        k0 = pl.multiple_of(j * T, T)
        kcat = jnp.concatenate([k_ref[pl.ds(k0, T), :], ka_ref[pl.ds(k0, T), :]], axis=1)
        st = _dot_nt(kcat, qcat)
        c0 = jnp.concatenate([jnp.broadcast_to(ct_ref[j, pl.ds(h_even, 1), :][:, 0:1], (1, T)),
                              jnp.broadcast_to(ct_ref[j, pl.ds(h_even + 1, 1), :][:, 0:1], (1, T))], axis=1)
        rb = cq - c0
        if masked:
            st = jnp.where(krow <= qcol, st, NEG_INF)
        m_new = jnp.maximum(m, jnp.max(st, axis=0, keepdims=True) + rb)
        alpha = jnp.exp2(m - m_new)
        pt = jnp.exp2(st + (rb - m_new)).astype(BF16)
        vt = vt_ref[j]
        pv_e = _dot(jnp.concatenate([vt[0:HEAD_DIM], ones_rows], axis=0), pt[:, 0:T])
        pv_o = _dot(jnp.concatenate([vt[HEAD_DIM:PAIR], ones_rows], axis=0), pt[:, T:2 * T])
        return m_new, alpha[:, 0:T] * acc_e + pv_e, alpha[:, T:2 * T] * acc_o + pv_o

    @pl.when(i == 0)
    def _():
        ones = jnp.ones((PAIR, LANES), BF16)
        kmax = jnp.zeros((1, LANES), F32)
        for c in range(k_ref.shape[0] // T):
            kc = k_ref[c * T:(c + 1) * T, :].astype(F32)
            kmax = jnp.maximum(kmax, jnp.max(_dot((kc * kc).astype(BF16), ones), axis=0, keepdims=True))
        kn_ref[...] = jnp.broadcast_to(kmax, kn_ref.shape)

    rows = HEAD_DIM + 16
    init = (jnp.full((1, 2 * T), NEG_INF, F32), jnp.zeros((rows, T), F32), jnp.zeros((rows, T), F32))
    carry = step(i, init, True)

    qf = q.astype(F32)
    qsq = qf * qf
    kn2 = kn_ref[0:1, 0:1] * FOX_NORM_SLACK
    need = jnp.zeros((1, 1), jnp.int32)
    blk = lax.broadcasted_iota(jnp.int32, (ct_ref.shape[0], 1, 1), 0)
    for par in range(2):
        head_lanes = (lane >= HEAD_DIM) == (par == 1)
        qn2 = jnp.max(jnp.sum(jnp.where(head_lanes, qsq, 0.0), axis=1, keepdims=True), axis=0, keepdims=True)
        reach = jnp.sqrt(qn2 * kn2) + FOX_BOUND_SLACK
        m_min = jnp.min(carry[0][:, par * T:(par + 1) * T], axis=1, keepdims=True)
        cq_first = cq[:, par * T:par * T + 1]
        ck_end = ct_ref[:, pl.ds(h_even + par, 1), :][:, :, T - 1:T]
        live = (reach + cq_first - m_min)[None, :, :] - ck_end > -FOX_UNDERFLOW
        count = jnp.sum(jnp.where(live & (blk < i), 1, 0), axis=0)
        need = jnp.maximum(need, count)
    n_keep = need[0, 0]

    n_pairs = n_keep // 2
    carry = lax.fori_loop(0, n_pairs, lambda u, c: step(i - 2 - 2 * u, step(i - 1 - 2 * u, c, False), False), carry)
    carry = lax.fori_loop(2 * n_pairs, n_keep, lambda u, c: step(i - 1 - u, c, False), carry)
    _, acc_e, acc_o = carry
    o_t = jnp.concatenate([acc_e[0:HEAD_DIM] / acc_e[HEAD_DIM:HEAD_DIM + 1],
                           acc_o[0:HEAD_DIM] / acc_o[HEAD_DIM:HEAD_DIM + 1]], axis=0)
    o_ref[...] = o_t.T.astype(o_ref.dtype)


def _foxt(q, k, ka, vt, ct, *, T):
    b, s, _ = q.shape
    nt = s // T
    return pl.pallas_call(
        functools.partial(_foxt_body, T=T),
        grid=(b, N_PAIRS, nt),
        in_specs=[pl.BlockSpec((None, T, PAIR), lambda bi, p, i: (bi, i, p)),
                  pl.BlockSpec((None, s, PAIR), lambda bi, p, i: (bi, 0, p)),
                  pl.BlockSpec((None, s, PAIR), lambda bi, p, i: (bi, 0, p)),
                  pl.BlockSpec((None, nt, PAIR, T), lambda bi, p, i: (bi, 0, p, 0)),
                  pl.BlockSpec((None, nt, N_HEADS, T), lambda bi, p, i: (bi, 0, 0, 0))],
        out_specs=pl.BlockSpec((None, T, PAIR), lambda bi, p, i: (bi, i, p)),
        out_shape=jax.ShapeDtypeStruct((b, s, WIDTH), BF16),
        scratch_shapes=[pltpu.VMEM((8, LANES), F32)],
        compiler_params=_cparams(3),
        name="foxt",
    )(q, k, ka, vt, ct)


def _band_body(q_ref, k_ref, v_ref, bias_ref, o_ref, *scratch, TQ, W, n_sub, padded):
    p = pl.program_id(1)
    i = pl.program_id(2)
    if padded:
        kpad_ref, vpad_ref = scratch
        s_len = k_ref.shape[0]

        @pl.when(i == 0)
        def _():
            zeros = jnp.zeros((BAND_REACH, PAIR), BF16)
            kpad_ref[pl.ds(0, BAND_REACH), :] = zeros
            vpad_ref[pl.ds(0, BAND_REACH), :] = zeros
            kpad_ref[pl.ds(BAND_REACH, s_len), :] = k_ref[...]
            vpad_ref[pl.ds(BAND_REACH, s_len), :] = v_ref[...]
    else:
        kpad_ref, vpad_ref = k_ref, v_ref

    lane = lax.broadcasted_iota(jnp.int32, (TQ, PAIR), 1)
    ones_blk = jnp.where(lax.broadcasted_iota(jnp.int32, (W, LANES), 1) == 0, 1.0, 0.0).astype(BF16)
    kcol = lax.broadcasted_iota(jnp.int32, (TQ, W), 1)

    def sub_block(sub, carry):
        r0 = pl.multiple_of(sub * TQ, TQ)
        q0 = i * (n_sub * TQ) + r0
        q = q_ref[pl.ds(r0, TQ), :]
        kw = kpad_ref[pl.ds(pl.multiple_of(q0, TQ), W), :] if padded else kpad_ref[...]
        vw = vpad_ref[pl.ds(pl.multiple_of(q0, TQ), W), :] if padded else vpad_ref[...]
        vcat = jnp.concatenate([vw, ones_blk], axis=1)
        outs = []
        for par in range(2):
            h = 2 * p + par
            qm = jnp.where((lane >= HEAD_DIM) == (par == 1), q, jnp.zeros_like(q))
            s = _dot_nt(qm, kw) + bias_ref[h]
            if padded:
                s = jnp.where(kcol >= BAND_REACH - q0, s, NEG_INF)
            m = jnp.max(s, axis=1, keepdims=True)
            pexp = jnp.exp2(s - m)
            pv = _dot(pexp.astype(BF16), vcat)
            outs.append(pv[:, :LANES] / pv[:, LANES:LANES + 1])
        o_ref[pl.ds(r0, TQ), :] = jnp.where(lane < HEAD_DIM, outs[0], outs[1]).astype(o_ref.dtype)
        return carry

    lax.fori_loop(0, n_sub, sub_block, 0)


def _band(q, k, v, bias, *, TQ, n_sub, padded):
    b, sq, _ = q.shape
    sk = k.shape[1]
    w = BAND_REACH + TQ if padded else sk
    tqb = TQ * n_sub
    scratch = [pltpu.VMEM((sk + BAND_REACH, PAIR), BF16)] * 2 if padded else []
    return pl.pallas_call(
        functools.partial(_band_body, TQ=TQ, W=w, n_sub=n_sub, padded=padded),
        grid=(b, N_PAIRS, sq // tqb),
        in_specs=[pl.BlockSpec((None, tqb, PAIR), lambda bi, p, i: (bi, i, p)),
                  pl.BlockSpec((None, sk, PAIR), lambda bi, p, i: (bi, 0, p)),
                  pl.BlockSpec((None, sk, PAIR), lambda bi, p, i: (bi, 0, p)),
                  pl.BlockSpec(bias.shape, lambda bi, p, i: (0, 0, 0))],
        out_specs=pl.BlockSpec((None, tqb, PAIR), lambda bi, p, i: (bi, i, p)),
        out_shape=jax.ShapeDtypeStruct((b, sq, WIDTH), BF16),
        scratch_shapes=scratch,
        compiler_params=_cparams(3),
        name="band",
    )(q, k, v, bias)


BAND_TQ = 128
BAND_STEP = 512


def _bandt_body(q_ref, kp_ref, kc_ref, vp_ref, vc_ref, bias_ref, o_ref):
    i = pl.program_id(2)
    w = BAND_REACH + BAND_TQ
    k2 = jnp.concatenate([kp_ref[...], kc_ref[...]], axis=0)
    vt2 = jnp.concatenate([vp_ref[...], vc_ref[...]], axis=1)
    lane = lax.broadcasted_iota(jnp.int32, (BAND_TQ, PAIR), 1)
    ones_rows = jnp.ones((16, w), BF16)
    krow = lax.broadcasted_iota(jnp.int32, (w, 2 * BAND_TQ), 0)
    bias = bias_ref[...]
    for sub in range(BAND_STEP // BAND_TQ):
        r0 = sub * BAND_TQ
        q = q_ref[r0:r0 + BAND_TQ, :]
        qcat = jnp.concatenate([jnp.where(lane < HEAD_DIM, q, jnp.zeros_like(q)),
                                jnp.where(lane >= HEAD_DIM, q, jnp.zeros_like(q))], axis=0)
        st = _dot_nt(k2[r0:r0 + w], qcat) + bias
        st = jnp.where(krow >= (1 - i) * BAND_STEP - r0, st, NEG_INF)
        m = jnp.max(st, axis=0, keepdims=True)
        pt = jnp.exp2(st - m).astype(BF16)
        vwin = vt2[:, r0:r0 + w]
        outs = []
        for par in range(2):
            vcat = jnp.concatenate([vwin[par * HEAD_DIM:(par + 1) * HEAD_DIM], ones_rows], axis=0)
            pv = _dot(vcat, pt[:, par * BAND_TQ:(par + 1) * BAND_TQ])
            outs.append(pv[0:HEAD_DIM] / pv[HEAD_DIM:HEAD_DIM + 1])
        o_ref[r0:r0 + BAND_TQ, :] = jnp.concatenate(outs, axis=0).T.astype(o_ref.dtype)


def _bandt(q, k, vt, bias_t):
    b, s, _ = q.shape
    prev = lambda i: jnp.maximum(i - 1, 0)
    return pl.pallas_call(
        _bandt_body,
        grid=(b, N_PAIRS, s // BAND_STEP),
        in_specs=[pl.BlockSpec((None, BAND_STEP, PAIR), lambda bi, p, i: (bi, i, p)),
                  pl.BlockSpec((None, BAND_STEP, PAIR), lambda bi, p, i: (bi, prev(i), p)),
                  pl.BlockSpec((None, BAND_STEP, PAIR), lambda bi, p, i: (bi, i, p)),
                  pl.BlockSpec((None, None, PAIR, BAND_STEP), lambda bi, p, i: (bi, prev(i), p, 0)),
                  pl.BlockSpec((None, None, PAIR, BAND_STEP), lambda bi, p, i: (bi, i, p, 0)),
                  pl.BlockSpec((None,) + bias_t.shape[1:], lambda bi, p, i: (p, 0, 0))],
        out_specs=pl.BlockSpec((None, BAND_STEP, PAIR), lambda bi, p, i: (bi, i, p)),
        out_shape=jax.ShapeDtypeStruct((b, s, WIDTH), BF16),
        compiler_params=_cparams(3),
        name="bandt",
    )(q, k, k, vt, vt, bias_t)


def _band_bias_tile_t(rel_bias):
    tile = _band_bias_tile(rel_bias, BAND_TQ, BAND_REACH + BAND_TQ)
    t = jnp.swapaxes(tile, 1, 2)
    return jnp.concatenate([t[0::2], t[1::2]], axis=2)


def _band_bias_tile(rel_bias, tq, w):
    span = w + tq - 1
    period = span + 1
    v = np.arange(period)
    d = np.where(v < w, v, v - period)
    table_idx = np.clip(BAND_REACH - d, -REL_CLIP, REL_CLIP) + REL_CLIP
    table = rel_bias[:, table_idx] * LOG2E
    n_h = rel_bias.shape[0]
    vals = jnp.tile(table, (1, tq))[:, :tq * span].reshape(n_h, tq, span)[:, :, :w]
    r = np.arange(tq)[:, None]
    c = np.arange(w)[None, :]
    in_band = (c // CHUNK >= r // CHUNK) & (c // CHUNK <= r // CHUNK + BAND_REACH // CHUNK)
    return jnp.where(jnp.asarray(in_band)[None], vals, NEG_INF).astype(F32)


def _first_index(is_max, idx, axis, big):
    return jnp.min(jnp.where(is_max, idx, big), axis=axis, keepdims=True)


def _route(scores, choice):
    t = scores.shape[1]
    c3 = choice.reshape(N_GROUPS, GROUP_SIZE, t)
    j_idx = lax.broadcasted_iota(jnp.int32, c3.shape, 1)
    top1 = jnp.max(c3, axis=1, keepdims=True)
    first = _first_index(c3 == top1, j_idx, 1, GROUP_SIZE)
    top2 = jnp.max(jnp.where(j_idx == first, -jnp.inf, c3), axis=1, keepdims=True)
    gscore = (top1 + top2).reshape(N_GROUPS, t)

    g_idx = lax.broadcasted_iota(jnp.int32, gscore.shape, 0)
    gsel = jnp.zeros(gscore.shape, F32)
    work = gscore
    for _ in range(TOPK_GROUPS):
        gm = jnp.max(work, axis=0, keepdims=True)
        pick = g_idx == _first_index(work == gm, g_idx, 0, N_GROUPS)
        gsel = jnp.where(pick, 1.0, gsel)
        work = jnp.where(pick, -jnp.inf, work)

    emask = jnp.broadcast_to(gsel.reshape(N_GROUPS, 1, t), c3.shape) > 0.0
    work = jnp.where(emask, c3, NEG_INF)
    e_idx = lax.broadcasted_iota(jnp.int32, c3.shape, 0) * GROUP_SIZE + j_idx
    esel = jnp.zeros(c3.shape, F32)
    for _ in range(TOP_K):
        em = jnp.max(jnp.max(work, axis=1, keepdims=True), axis=0, keepdims=True)
        cand = jnp.where(work == em, e_idx, N_EXPERTS)
        first = jnp.min(jnp.min(cand, axis=1, keepdims=True), axis=0, keepdims=True)
        pick = e_idx == first
        esel = jnp.where(pick, 1.0, esel)
        work = jnp.where(pick, -jnp.inf, work)

    w = esel * scores.reshape(c3.shape)
    denom = jnp.sum(jnp.sum(w, axis=1, keepdims=True), axis=0, keepdims=True)
    return (w / denom * ROUTED_SCALE).reshape(N_EXPERTS, t)


def _merge_body(of_ref, ob_ref, x_ref, gate_ref, sh_ref, sc_ref, ogf_ref, ogb_ref, wo_ref, g2_ref,
                wrh_ref, wrl_ref, br_ref, x1_ref, h2_ref, comb_ref, combt_ref, cnt_ref):
    def group_norm(t_ref, gain_ref):
        t = t_ref[...].astype(F32)
        ms = jnp.mean(t * t, axis=-1, keepdims=True)
        return (t * lax.rsqrt(ms + EPS) * gain_ref[...]).astype(BF16)

    y = _dot(group_norm(of_ref, ogf_ref), wo_ref[0:WIDTH, :]) + _dot(group_norm(ob_ref, ogb_ref), wo_ref[WIDTH:, :])
    x = x_ref[...]
    g, r, d = x.shape
    x1 = x + gate_ref[...] * y.reshape(g, r, d)
    x1_ref[...] = x1
    ms = jnp.mean(x1 * x1, axis=-1, keepdims=True)
    h2 = (x1 * lax.rsqrt(ms + EPS) * g2_ref[...] * (1.0 + sc_ref[...]) + sh_ref[...]).reshape(g * r, d)
    h_hi, h_lo = _split2(h2)
    h2_ref[...] = h_hi
    logits = _dot_nt(wrh_ref[...], h_hi) + _dot_nt(wrh_ref[...], h_lo) + _dot_nt(wrl_ref[...], h_hi)
    scores = jax.nn.sigmoid(logits)
    t = scores.shape[1]
    bias = jnp.concatenate([br_ref[...]] * (t // LANES), axis=1)
    comb = _route(scores, scores + bias)
    comb_pad = jnp.concatenate([comb, jnp.zeros((LANES - N_EXPERTS, t), F32)], axis=0)
    comb_ref[...] = comb_pad.T
    combt_ref[...] = comb
    picked = jnp.where(comb > 0.0, 1.0, 0.0)
    for sub in range(cnt_ref.shape[0]):
        cnt = jnp.sum(picked[:, sub * MOE_TM:(sub + 1) * MOE_TM], axis=1, keepdims=True)
        cnt_ref[sub] = jnp.broadcast_to(cnt, (N_EXPERTS, LANES))


def _merge(of, ob, x, gate, shift, scale, ogf, ogb, wo, g2, wr_hi, wr_lo, br, *, G, R):
    nb, s, d = x.shape
    n = nb * s
    tm = G * R
    nbi, nsi = nb // G, s // R
    row = lambda b, i: (b * nsi + i, 0)
    const = lambda b, i: (0, 0)
    mod_spec = pl.BlockSpec((G, 1, d), lambda b, i: (b, 0, 0))
    x_spec = pl.BlockSpec((G, R, d), lambda b, i: (b, i, 0))
    return pl.pallas_call(
        _merge_body,
        grid=(nbi, nsi),
        in_specs=[pl.BlockSpec((tm, WIDTH), row), pl.BlockSpec((tm, WIDTH), row), x_spec,
                  mod_spec, mod_spec, mod_spec,
                  pl.BlockSpec((1, WIDTH), const), pl.BlockSpec((1, WIDTH), const),
                  pl.BlockSpec(wo.shape, const), pl.BlockSpec((1, d), const),
                  pl.BlockSpec(wr_hi.shape, const), pl.BlockSpec(wr_lo.shape, const),
                  pl.BlockSpec(br.shape, const)],
        out_specs=[x_spec, pl.BlockSpec((tm, d), row), pl.BlockSpec((tm, LANES), row),
                   pl.BlockSpec((N_EXPERTS, tm), lambda b, i: (0, b * nsi + i)),
                   pl.BlockSpec((tm // MOE_TM, N_EXPERTS, LANES), lambda b, i: (b * nsi + i, 0, 0))],
        out_shape=[jax.ShapeDtypeStruct((nb, s, d), F32), jax.ShapeDtypeStruct((n, d), BF16),
                   jax.ShapeDtypeStruct((n, LANES), F32), jax.ShapeDtypeStruct((N_EXPERTS, n), F32),
                   jax.ShapeDtypeStruct((n // MOE_TM, N_EXPERTS, LANES), F32)],
        compiler_params=_cparams(2),
        name="merge",
    )(of, ob, x, gate, shift, scale, ogf, ogb, wo, g2, wr_hi, wr_lo, br)


def _silu(g):
    return g * jax.nn.sigmoid(g)


def _moe_body(h_ref, comb_ref, x1_ref, gate_ref, wg_ref, wu_ref, wd_ref, sg_ref, su_ref, sd_ref, y_ref, acc_ref):
    e = pl.program_id(2)
    hb = h_ref[...]

    @pl.when(e == 0)
    def _():
        a = _silu(_dot(hb, sg_ref[...])) * _dot(hb, su_ref[...])
        acc_ref[...] = _dot(a.astype(BF16), sd_ref[...])

    comb = comb_ref[...]
    lane = lax.broadcasted_iota(jnp.int32, comb.shape, 1)
    c_e = jnp.sum(jnp.where(lane == e, comb, 0.0), axis=1, keepdims=True)
    a = _silu(_dot(hb, wg_ref[...])) * _dot(hb, wu_ref[...]) * c_e
    acc_ref[...] += _dot(a.astype(BF16), wd_ref[...])

    @pl.when(e == pl.num_programs(2) - 1)
    def _():
        x1 = x1_ref[...]
        g, r, d = x1.shape
        y_ref[...] = x1 + gate_ref[...] * acc_ref[...].reshape(g, r, d)


def _moe(h2, comb, x1, gate, wg, wu, wd, sg, su, sd, *, G, R):
    nb, s, d = x1.shape
    tm = G * R
    nbi, nsi = nb // G, s // R
    ff = wg.shape[2]
    row = lambda b, i, e: (b * nsi + i, 0)
    const = lambda b, i, e: (0, 0)
    x_spec = pl.BlockSpec((G, R, d), lambda b, i, e: (b, i, 0))
    return pl.pallas_call(
        _moe_body,
        grid=(nbi, nsi, N_EXPERTS),
        in_specs=[pl.BlockSpec((tm, d), row), pl.BlockSpec((tm, LANES), row), x_spec,
                  pl.BlockSpec((G, 1, d), lambda b, i, e: (b, 0, 0)),
                  pl.BlockSpec((None, d, ff), lambda b, i, e: (e, 0, 0)),
                  pl.BlockSpec((None, d, ff), lambda b, i, e: (e, 0, 0)),
                  pl.BlockSpec((None, ff, d), lambda b, i, e: (e, 0, 0)),
                  pl.BlockSpec(sg.shape, const), pl.BlockSpec(su.shape, const), pl.BlockSpec(sd.shape, const)],
        out_specs=x_spec,
        out_shape=jax.ShapeDtypeStruct((nb, s, d), F32),
        scratch_shapes=[pltpu.VMEM((tm, d), F32)],
        compiler_params=_cparams(3),
        name="moe",
    )(h2, comb, x1, gate, wg, wu, wd, sg, su, sd)


MOE_TM = 256
MOE_CH = 16
MOE_SLOTS = TOP_K * MOE_TM + N_EXPERTS * MOE_CH
MOE_NCHUNK = MOE_SLOTS // MOE_CH
MOE_PIECE = 512
MOE_NPIECE = MOE_SLOTS // MOE_PIECE
MOE_CPP = MOE_PIECE // MOE_CH
MOE_RB = 512
TAU_RADIX = 64.0
assert MOE_SLOTS % MOE_PIECE == 0


def _moe_plan(cnt, n_tiles):
    pc = (cnt + MOE_CH - 1) // MOE_CH * MOE_CH
    off = jnp.cumsum(pc, axis=1) - pc
    end = off + pc
    n_used = (jnp.sum(pc, axis=1) // MOE_CH).astype(jnp.int32)
    tot = jnp.sum(pc, axis=0)
    reg = (tot + MOE_RB - 1) // MOE_RB * MOE_RB
    reg_end = jnp.cumsum(reg)
    reg_start = reg_end - reg
    dest_base = reg_start[None, :] + jnp.cumsum(pc, axis=0) - pc
    chunk_row = jnp.arange(MOE_NCHUNK, dtype=jnp.int32)[None, :] * MOE_CH
    in_group = (chunk_row[:, :, None] >= off[:, None, :]) & (chunk_row[:, :, None] < end[:, None, :])
    used = jnp.any(in_group, axis=2)
    cdest = chunk_row + jnp.sum(jnp.where(in_group, (dest_base - off)[:, None, :], 0), axis=2)
    worst_rows = TOP_K * MOE_TM * n_tiles + n_tiles * N_EXPERTS * (MOE_CH - 1) + N_EXPERTS * (MOE_RB - MOE_CH)
    r_max = -(-worst_rows // MOE_RB)
    cdest_d = jnp.where(used, cdest, r_max * MOE_RB + chunk_row).astype(jnp.int32)
    cdest_c = jnp.where(used, cdest, chunk_row).astype(jnp.int32)
    n_active = (reg_end[-1] // MOE_RB).astype(jnp.int32).reshape(1)
    tile_row = jnp.arange(r_max, dtype=jnp.int32) * MOE_RB
    tile_expert = jnp.minimum(jnp.sum((tile_row[:, None] >= reg_end[None, :]).astype(jnp.int32), axis=1),
                              N_EXPERTS - 1).astype(jnp.int32)
    in_region = (tile_row[:, None] >= reg_start[None, :]) & (tile_row[:, None] < reg_end[None, :])
    rows_end = jnp.sum(jnp.where(in_region, (reg_start + tot)[None, :], 0), axis=1)
    valid = jnp.clip(rows_end - tile_row, 0, MOE_RB).astype(jnp.int32)
    f = lambda a: a.astype(F32)
    zeros64 = jnp.zeros((n_tiles, N_EXPERTS), F32)
    row2 = lambda a: jnp.broadcast_to(jnp.concatenate([f(a), f(a)], axis=1)[:, None, :], (n_tiles, 8, LANES))
    col = lambda a: jnp.broadcast_to(f(a)[:, :, None], (n_tiles, N_EXPERTS, LANES))
    col128 = lambda a: jnp.broadcast_to(jnp.concatenate([f(a), zeros64], axis=1)[:, :, None], (n_tiles, LANES, LANES))
    row1 = lambda a: jnp.broadcast_to(jnp.concatenate([f(a), zeros64], axis=1)[:, None, :], (n_tiles, 8, LANES))
    return dict(n_used=n_used, cdest_d=cdest_d.reshape(n_tiles, 1, MOE_NCHUNK),
                cdest_c=cdest_c.reshape(n_tiles, 1, MOE_NCHUNK), r_max=r_max, n_active=n_active,
                tile_expert=tile_expert, valid=valid,
                off_row2=row2(off), end_row2=row2(end), off_col=col(off),
                off_row1=row1(off), off_col128=col128(off), end_col128=col128(end))


def _tau_pieces(sel, tau):
    tau = jnp.where(sel, tau, -1.0)
    hi = jnp.floor(tau * (1.0 / TAU_RADIX)) * TAU_RADIX
    return hi.astype(BF16), (tau - hi).astype(BF16)


def _dispatch_body(nused_ref, cdest_ref, h_ref, combt_ref, offcol_ref, offrow_ref, endrow_ref, sorted_ref,
                   buf_ref, sem):
    t = pl.program_id(0)
    n_used = nused_ref[t]
    tm = h_ref.shape[0]
    sel = combt_ref[...] > 0.0
    rr = lax.broadcasted_iota(jnp.int32, (tm, tm), 0)
    cc = lax.broadcasted_iota(jnp.int32, (tm, tm), 1)
    upper = jnp.where(rr < cc, 1.0, 0.0).astype(BF16)
    rank = _dot(jnp.where(sel, 1.0, 0.0).astype(BF16), upper)
    cols = jnp.concatenate([offcol_ref[...]] * (tm // LANES), axis=1)
    tau_hi, tau_lo = _tau_pieces(sel, cols + rank)
    taucat = jnp.concatenate([tau_hi, tau_lo], axis=0)
    off_row = offrow_ref[0:1, :]
    end_row = endrow_ref[0:1, :]
    hb = h_ref[...]

    def start_piece(piece):
        chunks = range(piece * MOE_CPP, (piece + 1) * MOE_CPP)
        dests = [cdest_ref[0, c] for c in chunks]
        for c, row in zip(chunks, dests):
            dst = sorted_ref.at[pl.ds(pl.multiple_of(row, MOE_CH), MOE_CH), :]
            pltpu.make_async_copy(buf_ref.at[pl.ds(c * MOE_CH, MOE_CH), :], dst, sem).start()

    def wait_piece(piece):
        rows = pl.ds(piece * MOE_PIECE, MOE_PIECE)
        pltpu.make_async_copy(buf_ref.at[rows, :], sorted_ref.at[rows, :], sem).wait()

    for piece in range(MOE_NPIECE):
        @pl.when(piece * MOE_CPP < n_used)
        def _():
            if piece > 0:
                start_piece(piece - 1)
            base = piece * MOE_PIECE
            s_col = (base + lax.broadcasted_iota(jnp.int32, (MOE_PIECE, LANES), 0)).astype(F32)
            onehot = jnp.where((s_col >= off_row) & (s_col < end_row), 1.0, 0.0).astype(BF16)
            q = _dot(onehot, taucat)
            s_mat = (base + lax.broadcasted_iota(jnp.int32, (MOE_PIECE, tm), 0)).astype(F32)
            g = jnp.where(q == s_mat, 1.0, 0.0).astype(BF16)
            buf_ref[pl.ds(base, MOE_PIECE), :] = _dot(g, hb).astype(BF16)

    last = (n_used - 1) // MOE_CPP
    for piece in range(MOE_NPIECE):
        @pl.when(piece == last)
        def _():
            start_piece(piece)

    for piece in range(MOE_NPIECE):
        @pl.when(piece * MOE_CPP < n_used)
        def _():
            wait_piece(piece)


def _dispatch(h2, combt, plan, n_tiles):
    n, d = h2.shape
    r_total = plan["r_max"] * MOE_RB + MOE_SLOTS
    grid_spec = pltpu.PrefetchScalarGridSpec(
        num_scalar_prefetch=1,
        grid=(n_tiles,),
        in_specs=[pl.BlockSpec((None, 1, MOE_NCHUNK), lambda t, nu: (t, 0, 0), memory_space=pltpu.SMEM),
                  pl.BlockSpec((MOE_TM, d), lambda t, nu: (t, 0)),
                  pl.BlockSpec((N_EXPERTS, MOE_TM), lambda t, nu: (0, t)),
                  pl.BlockSpec((None, N_EXPERTS, LANES), lambda t, nu: (t, 0, 0)),
                  pl.BlockSpec((None, 8, LANES), lambda t, nu: (t, 0, 0)),
                  pl.BlockSpec((None, 8, LANES), lambda t, nu: (t, 0, 0))],
        out_specs=pl.BlockSpec(memory_space=pl.ANY),
        scratch_shapes=[pltpu.VMEM((MOE_SLOTS, d), BF16), pltpu.SemaphoreType.DMA(())],
    )
    return pl.pallas_call(
        _dispatch_body,
        grid_spec=grid_spec,
        out_shape=jax.ShapeDtypeStruct((r_total, d), BF16),
        compiler_params=_cparams(1),
        name="moe_dispatch",
    )(plan["n_used"], plan["cdest_d"], h2, combt, plan["off_col"], plan["off_row2"], plan["end_row2"])


def _ffn_body(texp_ref, nact_ref, valid_ref, x_ref, wg_ref, wu_ref, wd_ref, o_ref):
    r = pl.program_id(0)

    @pl.when(r < nact_ref[0])
    def _():
        x = x_ref[...]
        rows = lax.broadcasted_iota(jnp.int32, x.shape, 0)
        x = jnp.where(rows < valid_ref[r], x, jnp.zeros_like(x))
        a = _silu(_dot(x, wg_ref[...])) * _dot(x, wu_ref[...])
        o_ref[...] = _dot(a.astype(BF16), wd_ref[...]).astype(o_ref.dtype)


def _ffn(xs, wg, wu, wd, plan):
    r_total, d = xs.shape
    ff = wg.shape[2]
    last = lambda r, te, na, va: (jnp.minimum(r, na[0] - 1), 0)
    wmap = lambda r, te, na, va: (te[r], 0, 0)
    grid_spec = pltpu.PrefetchScalarGridSpec(
        num_scalar_prefetch=3,
        grid=(plan["r_max"],),
        in_specs=[pl.BlockSpec((MOE_RB, d), last),
                  pl.BlockSpec((None, d, ff), wmap), pl.BlockSpec((None, d, ff), wmap),
                  pl.BlockSpec((None, ff, d), wmap)],
        out_specs=pl.BlockSpec((MOE_RB, d), last),
    )
    return pl.pallas_call(
        _ffn_body,
        grid_spec=grid_spec,
        out_shape=jax.ShapeDtypeStruct((r_total, d), BF16),
        compiler_params=_cparams(1),
        name="moe_ffn",
    )(plan["tile_expert"], plan["n_active"], plan["valid"], xs, wg, wu, wd)


def _combine_body(nused_ref, cdest_ref, h_ref, comb_ref, x1_ref, gate_ref, offrow_ref, offcol_ref, endcol_ref,
                  sg_ref, su_ref, sd_ref, ys_ref, y_ref, buf_ref, acc_ref, sem):
    t = pl.program_id(0)
    n_used = nused_ref[t]
    tm = h_ref.shape[0]

    @pl.when(t == 0)
    def _():
        buf_ref[...] = jnp.zeros_like(buf_ref)

    def wait_piece(piece):
        rows = pl.ds(piece * MOE_PIECE, MOE_PIECE)
        pltpu.make_async_copy(ys_ref.at[rows, :], buf_ref.at[rows, :], sem.at[piece]).wait()

    for piece in range(MOE_NPIECE):
        @pl.when(piece * MOE_CPP < n_used)
        def _():
            chunks = range(piece * MOE_CPP, (piece + 1) * MOE_CPP)
            srcs = [cdest_ref[0, c] for c in chunks]
            for c, row in zip(chunks, srcs):
                src = ys_ref.at[pl.ds(pl.multiple_of(row, MOE_CH), MOE_CH), :]
                pltpu.make_async_copy(src, buf_ref.at[pl.ds(c * MOE_CH, MOE_CH), :], sem.at[piece]).start()

    hb = h_ref[...]
    acc = _dot((_silu(_dot(hb, sg_ref[...])) * _dot(hb, su_ref[...])).astype(BF16), sd_ref[...])
    comb = comb_ref[...]
    sel = comb > 0.0
    rr = lax.broadcasted_iota(jnp.int32, (tm, tm), 0)
    cc = lax.broadcasted_iota(jnp.int32, (tm, tm), 1)
    lower = jnp.where(cc < rr, 1.0, 0.0).astype(BF16)
    rank = _dot(lower, jnp.where(sel, 1.0, 0.0).astype(BF16))
    tau_hi, tau_lo = _tau_pieces(sel, offrow_ref[0:1, :] + rank)
    taucat = jnp.concatenate([tau_hi, tau_lo], axis=1)
    lhs = jnp.concatenate([taucat, jnp.concatenate([comb.astype(BF16), jnp.zeros((tm, LANES), BF16)], axis=1)],
                          axis=0)
    off_col = offcol_ref[...]
    end_col = endcol_ref[...]
    acc_ref[...] = acc

    for piece in range(MOE_NPIECE):
        @pl.when(piece * MOE_CPP < n_used)
        def _():
            base = piece * MOE_PIECE
            s_row = (base + lax.broadcasted_iota(jnp.int32, (LANES, MOE_PIECE), 1)).astype(F32)
            off_b = jnp.concatenate([off_col] * (MOE_PIECE // LANES), axis=1)
            end_b = jnp.concatenate([end_col] * (MOE_PIECE // LANES), axis=1)
            onehot = jnp.where((s_row >= off_b) & (s_row < end_b), 1.0, 0.0).astype(BF16)
            qw = _dot(lhs, jnp.concatenate([onehot, onehot], axis=0))
            s_mat = (base + lax.broadcasted_iota(jnp.int32, (tm, MOE_PIECE), 1)).astype(F32)
            gw = jnp.where(qw[0:tm] == s_mat, qw[tm:2 * tm], 0.0).astype(BF16)
            wait_piece(piece)
            acc_ref[...] += _dot(gw, buf_ref[pl.ds(base, MOE_PIECE), :])
    y_ref[...] = x1_ref[...] + gate_ref[...] * acc_ref[...]


def _combine(h2, comb, x1, gate, ys, sg, su, sd, plan, n_tiles):
    nb, s, d = x1.shape
    n = nb * s
    per_b = s // MOE_TM
    x1f = x1.reshape(n, d)
    const2 = lambda t, nu: (0, 0)
    grid_spec = pltpu.PrefetchScalarGridSpec(
        num_scalar_prefetch=1,
        grid=(n_tiles,),
        in_specs=[pl.BlockSpec((None, 1, MOE_NCHUNK), lambda t, nu: (t, 0, 0), memory_space=pltpu.SMEM),
                  pl.BlockSpec((MOE_TM, d), lambda t, nu: (t, 0)),
                  pl.BlockSpec((MOE_TM, LANES), lambda t, nu: (t, 0)),
                  pl.BlockSpec((MOE_TM, d), lambda t, nu: (t, 0)),
                  pl.BlockSpec((None, 1, d), lambda t, nu: (t // per_b, 0, 0)),
                  pl.BlockSpec((None, 8, LANES), lambda t, nu: (t, 0, 0)),
                  pl.BlockSpec((None, LANES, LANES), lambda t, nu: (t, 0, 0)),
                  pl.BlockSpec((None, LANES, LANES), lambda t, nu: (t, 0, 0)),
                  pl.BlockSpec(sg.shape, const2), pl.BlockSpec(su.shape, const2), pl.BlockSpec(sd.shape, const2),
                  pl.BlockSpec(memory_space=pl.ANY)],
        out_specs=pl.BlockSpec((MOE_TM, d), lambda t, nu: (t, 0)),
        scratch_shapes=[pltpu.VMEM((MOE_SLOTS, d), BF16), pltpu.VMEM((MOE_TM, d), F32),
                        pltpu.SemaphoreType.DMA((MOE_NPIECE,))],
    )
    y = pl.pallas_call(
        _combine_body,
        grid_spec=grid_spec,
        out_shape=jax.ShapeDtypeStruct((n, d), F32),
        compiler_params=_cparams(1),
        name="moe_combine",
    )(plan["n_used"], plan["cdest_c"], h2, comb, x1f, gate, plan["off_row1"], plan["off_col128"], plan["end_col128"],
      sg, su, sd, ys)
    return y.reshape(nb, s, d)


def _moe_sparse(h2, comb, combt, cnt, x1, gate, wg, wu, wd, sg, su, sd):
    n = h2.shape[0]
    n_tiles = n // MOE_TM
    plan = _moe_plan(cnt[:, :, 0].astype(jnp.int32), n_tiles)
    xs = _dispatch(h2, combt, plan, n_tiles)
    ys = _ffn(xs, wg, wu, wd, plan)
    return _combine(h2, comb, x1, gate, ys, sg, su, sd, plan, n_tiles)


def _tile_heads(g, mult=1.0):
    return (jnp.tile(g.astype(F32), N_HEADS) * mult).reshape(1, WIDTH)


def kernel(x_prompt, x_sample, cache_fox_k, cache_fox_v, cache_fox_logf, cache_band_k, cache_band_v, c_prompt, c_sample, w_ada, b_ada, norm1_g, norm2_g, w_in, b_forget, g_q_fox, g_k_fox, g_q_band, g_k_band, rel_bias, out_g_fox, out_g_band, w_out, w_router, b_router, w_gate, w_up, w_down, ws_gate, ws_up, ws_down):
    depth = w_ada.shape[0]
    assert depth == 1
    bsz, seq, d = x_prompt.shape
    dbs, dseq, _ = x_sample.shape
    past = cache_fox_k.shape[2]
    n_cache = cache_band_k.shape[2]
    assert n_cache == BAND_REACH and dseq == CHUNK and seq % BAND_REACH == 0

    wi = w_in[0]
    cols = [wi[:, 0:512], wi[:, 512:1024], wi[:, 1024:1536], wi[:, 1544:2056], wi[:, 2056:2568], wi[:, 2568:3080],
            wi[:, 1536:1544], jnp.zeros((d, LANES - N_HEADS), F32)]
    w_all = jnp.concatenate(cols, axis=1).astype(BF16)
    hd = jnp.arange(WIDTH) // HEAD_DIM
    bd = jnp.where(hd[:, None] == hd[None, :], 1.0 / HEAD_DIM, 0.0).astype(BF16)
    qscale = ATTN_SCALE * LOG2E
    gqf, gkf = _tile_heads(g_q_fox[0], qscale), _tile_heads(g_k_fox[0])
    gqb, gkb = _tile_heads(g_q_band[0], qscale), _tile_heads(g_k_band[0])
    bf_row = jnp.concatenate([b_forget[0], jnp.zeros((LANES - N_HEADS,), F32)]).reshape(1, LANES)
    g1 = norm1_g[0].reshape(1, d)
    g2 = norm2_g[0].reshape(1, d)
    ogf = out_g_fox[0].reshape(1, WIDTH)
    ogb = out_g_band[0].reshape(1, WIDTH)
    wo = w_out[0].astype(BF16)
    wr_t = w_router[0].T
    wr_hi = wr_t.astype(BF16)
    wr_lo = (wr_t - wr_hi.astype(F32)).astype(BF16)
    br = jnp.broadcast_to(b_router[0].reshape(N_EXPERTS, 1), (N_EXPERTS, LANES)).astype(F32)
    wg, wu, wd = w_gate[0].astype(BF16), w_up[0].astype(BF16), w_down[0].astype(BF16)
    sg, su, sd = ws_gate[0].astype(BF16), ws_up[0].astype(BF16), ws_down[0].astype(BF16)
    perm = jnp.argsort(b_forget[0])

    def by_head(cols_, axis):
        shp = cols_.shape
        split = shp[:axis] + (N_HEADS, HEAD_DIM) + shp[axis + 1:]
        return jnp.take(cols_.reshape(split), perm, axis=axis).reshape(shp)

    f_block = jnp.concatenate([wi[:, 1536:1544], jnp.take(wi[:, 1536:1544], perm, axis=1),
                               jnp.zeros((d, LANES - 2 * N_HEADS), F32)], axis=1)
    w_all_p = jnp.concatenate([by_head(wi[:, 0:512], 1)] + cols[1:6] + [f_block, by_head(wi[:, 512:1024], 1)],
                              axis=1).astype(BF16)
    bf_row_p = jnp.concatenate([b_forget[0], jnp.take(b_forget[0], perm),
                                jnp.zeros((LANES - 2 * N_HEADS,), F32)]).reshape(1, LANES)
    wvt = jnp.concatenate([by_head(wi[:, 2 * WIDTH:3 * WIDTH], 1), wi[:, 2568:3080]], axis=1).T.astype(BF16)
    ogf_p = by_head(out_g_fox[0], 0).reshape(1, WIDTH)
    wo_p = jnp.concatenate([by_head(w_out[0][:WIDTH], 0), w_out[0][WIDTH:]], axis=0).astype(BF16)

    n_c = bsz + dbs
    rows = -(-n_c // 8) * 8
    c_all = jnp.concatenate([c_prompt, c_sample, jnp.zeros((rows - n_c, d), F32)], axis=0)
    mod = _ada(c_all, w_ada[0], b_ada[0].reshape(1, -1))

    def mods(lo, hi):
        return [mod[lo:hi, j * d:(j + 1) * d].reshape(hi - lo, 1, d) for j in range(6)]

    shift1_p, scale1_p, gate1_p, shift2_p, scale2_p, gate2_p = mods(0, bsz)
    shift1_s, scale1_s, gate1_s, shift2_s, scale2_s, gate2_s = mods(bsz, n_c)

    TM = BAND_REACH
    (qf, kf, vft, kf32, vf32, lf, qb, kb, vbt, kb32, vb32) = _proj(
        x_prompt, shift1_p, scale1_p, g1, w_all_p, bd, gqf, gkf, gqb, gkb, bf_row_p, wvt, G=1, R=TM,
        band_last_only=True)
    r3 = lambda a: a.reshape(bsz, seq, a.shape[-1])
    ct, ka = _scan_t(r3(lf), T=TM)
    of = _foxt(r3(qf), r3(kf), ka, vft, ct, T=TM)
    assert TM == BAND_STEP
    ob = _bandt(r3(qb), r3(kb), vbt, _band_bias_tile_t(rel_bias[0]))
    assert TM % MOE_TM == 0
    x1_p, h2_p, comb_p, combt_p, cnt_p = _merge(of.reshape(-1, WIDTH), ob.reshape(-1, WIDTH), x_prompt, gate1_p,
                                                shift2_p, scale2_p, ogf_p, ogb, wo_p, g2, wr_hi, wr_lo, br, G=1, R=TM)
    y_p = _moe_sparse(h2_p, comb_p, combt_p, cnt_p, x1_p, gate2_p, wg, wu, wd, sg, su, sd)

    GS = 8
    (qf_s, kf_s, vf_s, kf32_s, vf32_s, lf_s, qb_s, kb_s, vb_s, kb32_s, vb32_s) = _proj(
        x_sample, shift1_s, scale1_s, g1, w_all, bd, gqf, gkf, gqb, gkb, bf_row, wvt, G=GS, R=dseq,
        band_last_only=False)
    s3 = lambda a: a.reshape(dbs, dseq, a.shape[-1])
    sk = past + dseq
    skp = -(-sk // LANES) * LANES
    pad_k = skp - sk
    n_seq = dbs * N_HEADS
    assert n_seq <= LANES
    lf_seq = jnp.concatenate([cache_fox_logf[0], s3(lf_s)[:, :, :N_HEADS]], axis=1)
    lf_seq = jnp.swapaxes(lf_seq, 0, 1).reshape(sk, n_seq)
    lf_seq = jnp.pad(lf_seq, ((0, pad_k), (0, LANES - n_seq)))
    cum_col, cum_row = _scan(lf_seq)
    cq_s = jnp.swapaxes(cum_col[past:past + dseq, :n_seq].reshape(dseq, dbs, N_HEADS), 0, 1)
    cum_s = jnp.pad(cq_s, ((0, 0), (0, 0), (0, LANES - N_HEADS)))
    cumt_s = cum_row[:n_seq].reshape(dbs, 1, N_HEADS, skp)
    zpad = jnp.zeros((dbs, pad_k, WIDTH), BF16)
    k_all = jnp.concatenate([cache_fox_k[0].reshape(dbs, past, WIDTH).astype(BF16), s3(kf_s), zpad], axis=1)
    v_all = jnp.concatenate([cache_fox_v[0].reshape(dbs, past, WIDTH).astype(BF16), s3(vf_s), zpad], axis=1)
    of_s = _fox(s3(qf_s), k_all, v_all, cum_s, cumt_s, TQ=dseq, TK=skp, q_off=past)
    bias_s = _band_bias_tile(rel_bias[0], dseq, BAND_REACH + LANES)
    zb = jnp.zeros((dbs, LANES - dseq, WIDTH), BF16)
    kb_all = jnp.concatenate([cache_band_k[0].reshape(dbs, n_cache, WIDTH).astype(BF16), s3(kb_s), zb], axis=1)
    vb_all = jnp.concatenate([cache_band_v[0].reshape(dbs, n_cache, WIDTH).astype(BF16), s3(vb_s), zb], axis=1)
    ob_s = _band(s3(qb_s), kb_all, vb_all, bias_s, TQ=dseq, n_sub=1, padded=False)
    x1_s, h2_s, comb_s, _, _ = _merge(of_s.reshape(-1, WIDTH), ob_s.reshape(-1, WIDTH), x_sample, gate1_s, shift2_s,
                                      scale2_s, ogf, ogb, wo, g2, wr_hi, wr_lo, br, G=GS, R=dseq)
    y_s = _moe(h2_s, comb_s, x1_s, gate2_s, wg, wu, wd, sg, su, sd, G=dbs, R=dseq)

    hshape = (N_HEADS, HEAD_DIM)
    new_bk_s = jnp.concatenate([cache_band_k[0], s3(kb32_s).reshape(dbs, dseq, *hshape)], axis=1)[:, -n_cache:]
    new_bv_s = jnp.concatenate([cache_band_v[0], s3(vb32_s).reshape(dbs, dseq, *hshape)], axis=1)[:, -n_cache:]
    return (y_p, y_s,
            kf32.reshape(1, bsz, seq, *hshape), vf32.reshape(1, bsz, seq, *hshape),
            lf[:, :N_HEADS].reshape(1, bsz, seq, N_HEADS),
            kb32.reshape(1, bsz, BAND_REACH, *hshape), vb32.reshape(1, bsz, BAND_REACH, *hshape),
            kf32_s.reshape(1, dbs, dseq, *hshape), vf32_s.reshape(1, dbs, dseq, *hshape),
            lf_s[:, :N_HEADS].reshape(1, dbs, dseq, N_HEADS),
            new_bk_s[None], new_bv_s[None])
```

```python
import functools

import jax
import jax.numpy as jnp
import numpy as np
from jax import lax
from jax.experimental import pallas as pl
from jax.experimental.pallas import tpu as pltpu

F32 = jnp.float32
BF16 = jnp.bfloat16

HEAD_DIM = 64
N_HEADS = 8
WIDTH = N_HEADS * HEAD_DIM
PAIR = 2 * HEAD_DIM
N_PAIRS = N_HEADS // 2
LANES = 128
CHUNK = 64
BAND_REACH = 512
REL_CLIP = 256
N_EXPERTS = 64
N_GROUPS = 8
GROUP_SIZE = N_EXPERTS // N_GROUPS
TOPK_GROUPS = 4
TOP_K = 8
ROUTED_SCALE = 2.5
EPS = 1e-6
NEG_INF = -1e30
ATTN_SCALE = HEAD_DIM ** -0.5
LOG2E = 1.4426950408889634
VMEM_LIMIT = 56 * 1024 * 1024


def _cparams(n_axes):
    return pltpu.CompilerParams(dimension_semantics=("arbitrary",) * n_axes,
                                vmem_limit_bytes=VMEM_LIMIT)


def _dot(a, b):
    return jnp.dot(a, b, preferred_element_type=F32)


def _dot_nt(a, b):
    return lax.dot_general(a, b, (((1,), (1,)), ((), ())), preferred_element_type=F32)


def _split2(a):
    hi = a.astype(BF16)
    lo = (a - hi.astype(F32)).astype(BF16)
    return hi, lo


def _split3(a):
    hi = a.astype(BF16)
    r = a - hi.astype(F32)
    mid = r.astype(BF16)
    lo = (r - mid.astype(F32)).astype(BF16)
    return hi, mid, lo


def _ada_body(c_ref, w_ref, b_ref, o_ref):
    c = c_ref[...]
    a = c * jax.nn.sigmoid(c)
    a_hi, a_lo = _split2(a)
    w_hi, w_lo = _split2(w_ref[...])
    o_ref[...] = _dot(a_hi, w_hi) + _dot(a_hi, w_lo) + _dot(a_lo, w_hi) + b_ref[...]


def _ada(c_all, w_ada, b_ada):
    rows, d = c_all.shape
    n = w_ada.shape[1]
    tn = 1024
    return pl.pallas_call(
        _ada_body,
        grid=(n // tn,),
        in_specs=[pl.BlockSpec((rows, d), lambda j: (0, 0)),
                  pl.BlockSpec((d, tn), lambda j: (0, j)),
                  pl.BlockSpec((1, tn), lambda j: (0, j))],
        out_specs=pl.BlockSpec((rows, tn), lambda j: (0, j)),
        out_shape=jax.ShapeDtypeStruct((rows, n), F32),
        compiler_params=_cparams(1),
        name="ada",
    )(c_all, w_ada, b_ada)


def _log_sigmoid(z):
    return jnp.minimum(z, 0.0) - jnp.log(1.0 + jnp.exp(-jnp.abs(z)))


def _proj_body(x_ref, sh_ref, sc_ref, g1_ref, w_ref, bd_ref, gqf_ref, gkf_ref, gqb_ref, gkb_ref, bf_ref,
               qf_ref, kf_ref, vf_ref, kf32_ref, vf32_ref, lf_ref, qb_ref, kb_ref, vb_ref, kb32_ref, vb32_ref,
               *, band_last_only):
    x = x_ref[...]
    g, r, d = x.shape
    ms = jnp.mean(x * x, axis=-1, keepdims=True)
    h = x * lax.rsqrt(ms + EPS) * g1_ref[...] * (1.0 + sc_ref[...]) + sh_ref[...]
    hb = h.reshape(g * r, d).astype(BF16)

    def seg(i):
        return _dot(hb, w_ref[:, i * WIDTH:(i + 1) * WIDTH])

    def head_norm(t, gain_ref):
        ssq = _dot((t * t).astype(BF16), bd_ref[...])
        return t * lax.rsqrt(ssq + EPS) * gain_ref[...]

    qf_ref[...] = head_norm(seg(0), gqf_ref).astype(BF16)
    kf = head_norm(seg(1), gkf_ref)
    kf32_ref[...] = kf
    kf_ref[...] = kf.astype(BF16)
    vf = seg(2)
    vf32_ref[...] = vf
    if band_last_only:
        vf_ref[...] = vf.T.astype(BF16)
    else:
        vf_ref[...] = vf.astype(BF16)
    z = _dot(hb, w_ref[:, 6 * WIDTH:6 * WIDTH + LANES]) + bf_ref[...]
    lf_ref[...] = _log_sigmoid(z)
    qb_ref[...] = head_norm(seg(3), gqb_ref).astype(BF16)
    kb = head_norm(seg(4), gkb_ref)
    kb_ref[...] = kb.astype(BF16)
    vb = seg(5)
    if band_last_only:
        vb_ref[...] = vb.T.astype(BF16)
    else:
        vb_ref[...] = vb.astype(BF16)

    if band_last_only:
        @pl.when(pl.program_id(1) == pl.num_programs(1) - 1)
        def _():
            kb32_ref[...] = kb
            vb32_ref[...] = vb
    else:
        kb32_ref[...] = kb
        vb32_ref[...] = vb


def _proj(x, shift, scale, g1, w_all, bd, gqf, gkf, gqb, gkb, bf_row, *, G, R, band_last_only):
    nb, s, d = x.shape
    n = nb * s
    tm = G * R
    nbi, nsi = nb // G, s // R
    grid = (nbi, nsi)
    row = lambda b, i: (b * nsi + i, 0)
    const = lambda b, i: (0, 0)
    mod_spec = pl.BlockSpec((G, 1, d), lambda b, i: (b, 0, 0))
    out_bf = jax.ShapeDtypeStruct((n, WIDTH), BF16)
    out_f32 = jax.ShapeDtypeStruct((n, WIDTH), F32)
    tile = pl.BlockSpec((tm, WIDTH), row)
    if band_last_only:
        assert G == 1 and R == BAND_REACH
        band_shape = jax.ShapeDtypeStruct((nb, BAND_REACH, WIDTH), F32)
        band_spec = pl.BlockSpec((None, BAND_REACH, WIDTH), lambda b, i: (b, 0, 0))
        v_shape = jax.ShapeDtypeStruct((nb, nsi, WIDTH, tm), BF16)
        v_spec = pl.BlockSpec((None, None, WIDTH, tm), lambda b, i: (b, i, 0, 0))
    else:
        band_shape, band_spec = out_f32, tile
        v_shape, v_spec = out_bf, tile
    return pl.pallas_call(
        functools.partial(_proj_body, band_last_only=band_last_only),
        grid=grid,
        in_specs=[pl.BlockSpec((G, R, d), lambda b, i: (b, i, 0)), mod_spec, mod_spec,
                  pl.BlockSpec((1, d), const), pl.BlockSpec(w_all.shape, const), pl.BlockSpec(bd.shape, const),
                  pl.BlockSpec((1, WIDTH), const), pl.BlockSpec((1, WIDTH), const),
                  pl.BlockSpec((1, WIDTH), const), pl.BlockSpec((1, WIDTH), const),
                  pl.BlockSpec((1, LANES), const)],
        out_specs=[tile, tile, v_spec, tile, tile, pl.BlockSpec((tm, LANES), row), tile, tile, v_spec,
                   band_spec, band_spec],
        out_shape=[out_bf, out_bf, v_shape, out_f32, out_f32, jax.ShapeDtypeStruct((n, LANES), F32),
                   out_bf, out_bf, v_shape, band_shape, band_shape],
        compiler_params=_cparams(2),
        name="proj",
    )(x, shift, scale, g1, w_all, bd, gqf, gkf, gqb, gkb, bf_row)


def _scan_body(lf_ref, cum_ref, cumt_ref):
    lf = lf_ref[...]
    s = lf.shape[0]
    hi, mid, lo = _split3(lf)
    rr = lax.broadcasted_iota(jnp.int32, (s, s), 0)
    cc = lax.broadcasted_iota(jnp.int32, (s, s), 1)
    tri = jnp.where(cc <= rr, 1.0, 0.0).astype(BF16)
    cum2 = (_dot(tri, hi) + _dot(tri, mid) + _dot(tri, lo)) * LOG2E
    cum_ref[...] = cum2
    cumt_ref[...] = cum2.T


def _scan(lf):
    s, _ = lf.shape
    return pl.pallas_call(
        _scan_body,
        out_shape=[jax.ShapeDtypeStruct((s, LANES), F32), jax.ShapeDtypeStruct((LANES, s), F32)],
        compiler_params=pltpu.CompilerParams(vmem_limit_bytes=VMEM_LIMIT),
        name="scan",
    )(lf)


def _fox_body(q_ref, k_ref, v_ref, cq_ref, ck_ref, o_ref, *, TQ, TK, q_off):
    p = pl.program_id(1)
    i = pl.program_id(2)
    q = q_ref[...]
    cq_blk = cq_ref[...]
    lane = lax.broadcasted_iota(jnp.int32, (TQ, PAIR), 1)
    ones_blk = jnp.where(lax.broadcasted_iota(jnp.int32, (TK, LANES), 1) == 0, 1.0, 0.0).astype(BF16)
    q0 = q_off + i * TQ
    n_full = q0 // TK
    qpos = q0 + lax.broadcasted_iota(jnp.int32, (TQ, TK), 0)
    kcol = lax.broadcasted_iota(jnp.int32, (TQ, TK), 1)

    outs = []
    for par in range(2):
        h = 2 * p + par
        qm = jnp.where((lane >= HEAD_DIM) == (par == 1), q, jnp.zeros_like(q))
        cq_col = jnp.sum(jnp.where(lane == h, cq_blk, 0.0), axis=1, keepdims=True)
        ref0 = cq_col[0:1, :]
        cqr = cq_col - ref0

        def step(j, carry, masked):
            m, l, acc = carry
            k0 = pl.multiple_of(j * TK, TK)
            kb = k_ref[pl.ds(k0, TK), :]
            vb = v_ref[pl.ds(k0, TK), :]
            s = _dot_nt(qm, kb)
            ck = ck_ref[j, pl.ds(h, 1), :]
            u = s - (ck - ref0)
            if masked:
                u = jnp.where(k0 + kcol <= qpos, u, NEG_INF)
            m_new = jnp.maximum(m, jnp.max(u, axis=1, keepdims=True) + cqr)
            alpha = jnp.exp2(m - m_new)
            pexp = jnp.exp2(u + (cqr - m_new))
            pv = _dot(pexp.astype(BF16), jnp.concatenate([vb, ones_blk], axis=1))
            return m_new, alpha * l + pv[:, LANES:LANES + 1], alpha * acc + pv[:, :LANES]

        init = (jnp.full((TQ, 1), NEG_INF, F32), jnp.zeros((TQ, 1), F32), jnp.zeros((TQ, LANES), F32))
        carry = lax.fori_loop(0, n_full, lambda j, c: step(j, c, False), init)
        _, l, acc = step(n_full, carry, True)
        outs.append(acc / l)
    o_ref[...] = jnp.where(lane < HEAD_DIM, outs[0], outs[1]).astype(o_ref.dtype)


def _fox(q, k, v, cum, cumt, *, TQ, TK, q_off):
    b, sq, _ = q.shape
    sk = k.shape[1]
    return pl.pallas_call(
        functools.partial(_fox_body, TQ=TQ, TK=TK, q_off=q_off),
        grid=(b, N_PAIRS, sq // TQ),
        in_specs=[pl.BlockSpec((None, TQ, PAIR), lambda bi, p, i: (bi, i, p)),
                  pl.BlockSpec((None, sk, PAIR), lambda bi, p, i: (bi, 0, p)),
                  pl.BlockSpec((None, sk, PAIR), lambda bi, p, i: (bi, 0, p)),
                  pl.BlockSpec((None, TQ, LANES), lambda bi, p, i: (bi, i, 0)),
                  pl.BlockSpec((None, sk // TK, N_HEADS, TK), lambda bi, p, i: (bi, 0, 0, 0))],
        out_specs=pl.BlockSpec((None, TQ, PAIR), lambda bi, p, i: (bi, i, p)),
        out_shape=jax.ShapeDtypeStruct((b, sq, WIDTH), BF16),
        compiler_params=_cparams(3),
        name="fox",
    )(q, k, v, cum, cumt)


AUG_PIECES = 3
FOX_UNDERFLOW = 160.0
FOX_NORM_SLACK = 1.02
FOX_BOUND_SLACK = 2.0


def _scan_t_body(lf_ref, place_ref, ct_ref, ka_ref, carry_ref):
    @pl.when(pl.program_id(1) == 0)
    def _():
        carry_ref[...] = jnp.zeros_like(carry_ref)

    lf = lf_ref[...]
    ts = lf.shape[0]
    lane = lax.broadcasted_iota(jnp.int32, lf.shape, 1)
    lf = jnp.where(lane < N_HEADS, lf, 0.0)
    hi, mid, lo = _split3(lf)
    rr = lax.broadcasted_iota(jnp.int32, (ts, ts), 0)
    cc = lax.broadcasted_iota(jnp.int32, (ts, ts), 1)
    tri = jnp.where(cc <= rr, 1.0, 0.0).astype(BF16)
    cum = _dot(tri, hi) + _dot(tri, mid) + _dot(tri, lo) + carry_ref[0:1, :]
    carry_ref[...] = jnp.broadcast_to(cum[ts - 1:ts, :], carry_ref.shape)
    cum2 = cum * LOG2E
    ct_ref[...] = cum2.T[0:N_HEADS, :]
    pieces = _split3(cum2 - cum2[0:1, :])
    ka = _dot(pieces[0], place_ref[0]) + _dot(pieces[1], place_ref[1]) + _dot(pieces[2], place_ref[2])
    ka_ref[...] = ka.astype(BF16)


def _aug_placement():
    h = jnp.arange(LANES)[:, None]
    col = jnp.arange(WIDTH)[None, :]
    mats = []
    for x in range(AUG_PIECES):
        tgt = PAIR * (h // 2) + AUG_PIECES * (h % 2) + x
        mats.append(jnp.where((h < N_HEADS) & (col == tgt), 1.0, 0.0))
    return jnp.stack(mats).astype(BF16)


def _scan_t(lf, *, T):
    b, s, _ = lf.shape
    place = _aug_placement()
    return pl.pallas_call(
        _scan_t_body,
        grid=(b, s // T),
        in_specs=[pl.BlockSpec((None, T, LANES), lambda bi, i: (bi, i, 0)),
                  pl.BlockSpec(place.shape, lambda bi, i: (0, 0, 0))],
        out_specs=[pl.BlockSpec((None, None, N_HEADS, T), lambda bi, i: (bi, i, 0, 0)),
                   pl.BlockSpec((None, T, WIDTH), lambda bi, i: (bi, i, 0))],
        out_shape=[jax.ShapeDtypeStruct((b, s // T, N_HEADS, T), F32),
                   jax.ShapeDtypeStruct((b, s, WIDTH), BF16)],
        scratch_shapes=[pltpu.VMEM((8, LANES), F32)],
        compiler_params=_cparams(2),
        name="scan_t",
    )(lf, place)


def _foxt_body(q_ref, k_ref, ka_ref, vt_ref, ct_ref, o_ref, kn_ref, *, T):
    p = pl.program_id(1)
    i = pl.program_id(2)
    q = q_ref[...]
    lane = lax.broadcasted_iota(jnp.int32, (T, PAIR), 1)
    halves = []
    for par in range(2):
        qm = jnp.where((lane >= HEAD_DIM) == (par == 1), q, jnp.zeros_like(q))
        lo_lane = AUG_PIECES * par
        qa = jnp.where((lane >= lo_lane) & (lane < lo_lane + AUG_PIECES), -1.0, 0.0).astype(BF16)
        halves.append(jnp.concatenate([qm, qa], axis=1))
    qcat = jnp.concatenate(halves, axis=0)
    h_even = 2 * p
    cq = jnp.concatenate([ct_ref[i, pl.ds(h_even, 1), :], ct_ref[i, pl.ds(h_even + 1, 1), :]], axis=1)
    ones_rows = jnp.ones((16, T), BF16)
    krow = lax.broadcasted_iota(jnp.int32, (T, 2 * T), 0)
    qcol = lax.broadcasted_iota(jnp.int32, (T, 2 * T), 1) % T

    def step(j, carry, masked):
        m, acc_e, acc_o = carry
        k0 = pl.multiple_of(j * T, T)
        kcat = jnp.concatenate([k_ref[pl.ds(k0, T), :], ka_ref[pl.ds(k0, T), :]], axis=1)
        st = _dot_nt(kcat, qcat)
        c0 = jnp.concatenate([jnp.broadcast_to(ct_ref[j, pl.ds(h_even, 1), :][:, 0:1], (1, T)),
                              jnp.broadcast_to(ct_ref[j, pl.ds(h_even + 1, 1), :][:, 0:1], (1, T))], axis=1)
        rb = cq - c0
        if masked:
            st = jnp.where(krow <= qcol, st, NEG_INF)
        m_new = jnp.maximum(m, jnp.max(st, axis=0, keepdims=True) + rb)
        alpha = jnp.exp2(m - m_new)
        pt = jnp.exp2(st + (rb - m_new)).astype(BF16)
        vt = vt_ref[j]
        pv_e = _dot(jnp.concatenate([vt[0:HEAD_DIM], ones_rows], axis=0), pt[:, 0:T])
        pv_o = _dot(jnp.concatenate([vt[HEAD_DIM:PAIR], ones_rows], axis=0), pt[:, T:2 * T])
        return m_new, alpha[:, 0:T] * acc_e + pv_e, alpha[:, T:2 * T] * acc_o + pv_o

    @pl.when(i == 0)
    def _():
        ones = jnp.ones((PAIR, LANES), BF16)
        kmax = jnp.zeros((1, LANES), F32)
        for c in range(k_ref.shape[0] // T):
            kc = k_ref[c * T:(c + 1) * T, :].astype(F32)
            kmax = jnp.maximum(kmax, jnp.max(_dot((kc * kc).astype(BF16), ones), axis=0, keepdims=True))
        kn_ref[...] = jnp.broadcast_to(kmax, kn_ref.shape)

    rows = HEAD_DIM + 16
    init = (jnp.full((1, 2 * T), NEG_INF, F32), jnp.zeros((rows, T), F32), jnp.zeros((rows, T), F32))
    carry = step(i, init, True)

    qf = q.astype(F32)
    qsq = qf * qf
    kn2 = kn_ref[0:1, 0:1] * FOX_NORM_SLACK
    need = jnp.zeros((1, 1), jnp.int32)
    blk = lax.broadcasted_iota(jnp.int32, (ct_ref.shape[0], 1, 1), 0)
    for par in range(2):
        head_lanes = (lane >= HEAD_DIM) == (par == 1)
        qn2 = jnp.max(jnp.sum(jnp.where(head_lanes, qsq, 0.0), axis=1, keepdims=True), axis=0, keepdims=True)
        reach = jnp.sqrt(qn2 * kn2) + FOX_BOUND_SLACK
        m_min = jnp.min(carry[0][:, par * T:(par + 1) * T], axis=1, keepdims=True)
        cq_first = cq[:, par * T:par * T + 1]
        ck_end = ct_ref[:, pl.ds(h_even + par, 1), :][:, :, T - 1:T]
        live = (reach + cq_first - m_min)[None, :, :] - ck_end > -FOX_UNDERFLOW
        count = jnp.sum(jnp.where(live & (blk < i), 1, 0), axis=0)
        need = jnp.maximum(need, count)
    n_keep = need[0, 0]

    n_pairs = n_keep // 2
    carry = lax.fori_loop(0, n_pairs, lambda u, c: step(i - 2 - 2 * u, step(i - 1 - 2 * u, c, False), False), carry)
    carry = lax.fori_loop(2 * n_pairs, n_keep, lambda u, c: step(i - 1 - u, c, False), carry)
    _, acc_e, acc_o = carry
    o_t = jnp.concatenate([acc_e[0:HEAD_DIM] / acc_e[HEAD_DIM:HEAD_DIM + 1],
                           acc_o[0:HEAD_DIM] / acc_o[HEAD_DIM:HEAD_DIM + 1]], axis=0)
    o_ref[...] = o_t.T.astype(o_ref.dtype)


def _foxt(q, k, ka, vt, ct, *, T):
    b, s, _ = q.shape
    nt = s // T
    return pl.pallas_call(
        functools.partial(_foxt_body, T=T),
        grid=(b, N_PAIRS, nt),
        in_specs=[pl.BlockSpec((None, T, PAIR), lambda bi, p, i: (bi, i, p)),
                  pl.BlockSpec((None, s, PAIR), lambda bi, p, i: (bi, 0, p)),
                  pl.BlockSpec((None, s, PAIR), lambda bi, p, i: (bi, 0, p)),
                  pl.BlockSpec((None, nt, PAIR, T), lambda bi, p, i: (bi, 0, p, 0)),
                  pl.BlockSpec((None, nt, N_HEADS, T), lambda bi, p, i: (bi, 0, 0, 0))],
        out_specs=pl.BlockSpec((None, T, PAIR), lambda bi, p, i: (bi, i, p)),
        out_shape=jax.ShapeDtypeStruct((b, s, WIDTH), BF16),
        scratch_shapes=[pltpu.VMEM((8, LANES), F32)],
        compiler_params=_cparams(3),
        name="foxt",
    )(q, k, ka, vt, ct)


def _band_body(q_ref, k_ref, v_ref, bias_ref, o_ref, *scratch, TQ, W, n_sub, padded):
    p = pl.program_id(1)
    i = pl.program_id(2)
    if padded:
        kpad_ref, vpad_ref = scratch
        s_len = k_ref.shape[0]

        @pl.when(i == 0)
        def _():
            zeros = jnp.zeros((BAND_REACH, PAIR), BF16)
            kpad_ref[pl.ds(0, BAND_REACH), :] = zeros
            vpad_ref[pl.ds(0, BAND_REACH), :] = zeros
            kpad_ref[pl.ds(BAND_REACH, s_len), :] = k_ref[...]
            vpad_ref[pl.ds(BAND_REACH, s_len), :] = v_ref[...]
    else:
        kpad_ref, vpad_ref = k_ref, v_ref

    lane = lax.broadcasted_iota(jnp.int32, (TQ, PAIR), 1)
    ones_blk = jnp.where(lax.broadcasted_iota(jnp.int32, (W, LANES), 1) == 0, 1.0, 0.0).astype(BF16)
    kcol = lax.broadcasted_iota(jnp.int32, (TQ, W), 1)

    def sub_block(sub, carry):
        r0 = pl.multiple_of(sub * TQ, TQ)
        q0 = i * (n_sub * TQ) + r0
        q = q_ref[pl.ds(r0, TQ), :]
        kw = kpad_ref[pl.ds(pl.multiple_of(q0, TQ), W), :] if padded else kpad_ref[...]
        vw = vpad_ref[pl.ds(pl.multiple_of(q0, TQ), W), :] if padded else vpad_ref[...]
        vcat = jnp.concatenate([vw, ones_blk], axis=1)
        outs = []
        for par in range(2):
            h = 2 * p + par
            qm = jnp.where((lane >= HEAD_DIM) == (par == 1), q, jnp.zeros_like(q))
            s = _dot_nt(qm, kw) + bias_ref[h]
            if padded:
                s = jnp.where(kcol >= BAND_REACH - q0, s, NEG_INF)
            m = jnp.max(s, axis=1, keepdims=True)
            pexp = jnp.exp2(s - m)
            pv = _dot(pexp.astype(BF16), vcat)
            outs.append(pv[:, :LANES] / pv[:, LANES:LANES + 1])
        o_ref[pl.ds(r0, TQ), :] = jnp.where(lane < HEAD_DIM, outs[0], outs[1]).astype(o_ref.dtype)
        return carry

    lax.fori_loop(0, n_sub, sub_block, 0)


def _band(q, k, v, bias, *, TQ, n_sub, padded):
    b, sq, _ = q.shape
    sk = k.shape[1]
    w = BAND_REACH + TQ if padded else sk
    tqb = TQ * n_sub
    scratch = [pltpu.VMEM((sk + BAND_REACH, PAIR), BF16)] * 2 if padded else []
    return pl.pallas_call(
        functools.partial(_band_body, TQ=TQ, W=w, n_sub=n_sub, padded=padded),
        grid=(b, N_PAIRS, sq // tqb),
        in_specs=[pl.BlockSpec((None, tqb, PAIR), lambda bi, p, i: (bi, i, p)),
                  pl.BlockSpec((None, sk, PAIR), lambda bi, p, i: (bi, 0, p)),
                  pl.BlockSpec((None, sk, PAIR), lambda bi, p, i: (bi, 0, p)),
                  pl.BlockSpec(bias.shape, lambda bi, p, i: (0, 0, 0))],
        out_specs=pl.BlockSpec((None, tqb, PAIR), lambda bi, p, i: (bi, i, p)),
        out_shape=jax.ShapeDtypeStruct((b, sq, WIDTH), BF16),
        scratch_shapes=scratch,
        compiler_params=_cparams(3),
        name="band",
    )(q, k, v, bias)


BAND_TQ = 128
BAND_STEP = 512


def _bandt_body(q_ref, kp_ref, kc_ref, vp_ref, vc_ref, bias_ref, o_ref):
    i = pl.program_id(2)
    w = BAND_REACH + BAND_TQ
    k2 = jnp.concatenate([kp_ref[...], kc_ref[...]], axis=0)
    vt2 = jnp.concatenate([vp_ref[...], vc_ref[...]], axis=1)
    lane = lax.broadcasted_iota(jnp.int32, (BAND_TQ, PAIR), 1)
    ones_rows = jnp.ones((16, w), BF16)
    krow = lax.broadcasted_iota(jnp.int32, (w, 2 * BAND_TQ), 0)
    bias = bias_ref[...]
    for sub in range(BAND_STEP // BAND_TQ):
        r0 = sub * BAND_TQ
        q = q_ref[r0:r0 + BAND_TQ, :]
        qcat = jnp.concatenate([jnp.where(lane < HEAD_DIM, q, jnp.zeros_like(q)),
                                jnp.where(lane >= HEAD_DIM, q, jnp.zeros_like(q))], axis=0)
        st = _dot_nt(k2[r0:r0 + w], qcat) + bias
        st = jnp.where(krow >= (1 - i) * BAND_STEP - r0, st, NEG_INF)
        m = jnp.max(st, axis=0, keepdims=True)
        pt = jnp.exp2(st - m).astype(BF16)
        vwin = vt2[:, r0:r0 + w]
        outs = []
        for par in range(2):
            vcat = jnp.concatenate([vwin[par * HEAD_DIM:(par + 1) * HEAD_DIM], ones_rows], axis=0)
            pv = _dot(vcat, pt[:, par * BAND_TQ:(par + 1) * BAND_TQ])
            outs.append(pv[0:HEAD_DIM] / pv[HEAD_DIM:HEAD_DIM + 1])
        o_ref[r0:r0 + BAND_TQ, :] = jnp.concatenate(outs, axis=0).T.astype(o_ref.dtype)


def _bandt(q, k, vt, bias_t):
    b, s, _ = q.shape
    prev = lambda i: jnp.maximum(i - 1, 0)
    return pl.pallas_call(
        _bandt_body,
        grid=(b, N_PAIRS, s // BAND_STEP),
        in_specs=[pl.BlockSpec((None, BAND_STEP, PAIR), lambda bi, p, i: (bi, i, p)),
                  pl.BlockSpec((None, BAND_STEP, PAIR), lambda bi, p, i: (bi, prev(i), p)),
                  pl.BlockSpec((None, BAND_STEP, PAIR), lambda bi, p, i: (bi, i, p)),
                  pl.BlockSpec((None, None, PAIR, BAND_STEP), lambda bi, p, i: (bi, prev(i), p, 0)),
                  pl.BlockSpec((None, None, PAIR, BAND_STEP), lambda bi, p, i: (bi, i, p, 0)),
                  pl.BlockSpec((None,) + bias_t.shape[1:], lambda bi, p, i: (p, 0, 0))],
        out_specs=pl.BlockSpec((None, BAND_STEP, PAIR), lambda bi, p, i: (bi, i, p)),
        out_shape=jax.ShapeDtypeStruct((b, s, WIDTH), BF16),
        compiler_params=_cparams(3),
        name="bandt",
    )(q, k, k, vt, vt, bias_t)


def _band_bias_tile_t(rel_bias):
    tile = _band_bias_tile(rel_bias, BAND_TQ, BAND_REACH + BAND_TQ)
    t = jnp.swapaxes(tile, 1, 2)
    return jnp.concatenate([t[0::2], t[1::2]], axis=2)


def _band_bias_tile(rel_bias, tq, w):
    span = w + tq - 1
    period = span + 1
    v = np.arange(period)
    d = np.where(v < w, v, v - period)
    table_idx = np.clip(BAND_REACH - d, -REL_CLIP, REL_CLIP) + REL_CLIP
    table = rel_bias[:, table_idx] * LOG2E
    n_h = rel_bias.shape[0]
    vals = jnp.tile(table, (1, tq))[:, :tq * span].reshape(n_h, tq, span)[:, :, :w]
    r = np.arange(tq)[:, None]
    c = np.arange(w)[None, :]
    in_band = (c // CHUNK >= r // CHUNK) & (c // CHUNK <= r // CHUNK + BAND_REACH // CHUNK)
    return jnp.where(jnp.asarray(in_band)[None], vals, NEG_INF).astype(F32)


def _first_index(is_max, idx, axis, big):
    return jnp.min(jnp.where(is_max, idx, big), axis=axis, keepdims=True)


def _route(scores, choice):
    t = scores.shape[1]
    c3 = choice.reshape(N_GROUPS, GROUP_SIZE, t)
    j_idx = lax.broadcasted_iota(jnp.int32, c3.shape, 1)
    top1 = jnp.max(c3, axis=1, keepdims=True)
    first = _first_index(c3 == top1, j_idx, 1, GROUP_SIZE)
    top2 = jnp.max(jnp.where(j_idx == first, -jnp.inf, c3), axis=1, keepdims=True)
    gscore = (top1 + top2).reshape(N_GROUPS, t)

    g_idx = lax.broadcasted_iota(jnp.int32, gscore.shape, 0)
    gsel = jnp.zeros(gscore.shape, F32)
    work = gscore
    for _ in range(TOPK_GROUPS):
        gm = jnp.max(work, axis=0, keepdims=True)
        pick = g_idx == _first_index(work == gm, g_idx, 0, N_GROUPS)
        gsel = jnp.where(pick, 1.0, gsel)
        work = jnp.where(pick, -jnp.inf, work)

    emask = jnp.broadcast_to(gsel.reshape(N_GROUPS, 1, t), c3.shape) > 0.0
    work = jnp.where(emask, c3, NEG_INF)
    e_idx = lax.broadcasted_iota(jnp.int32, c3.shape, 0) * GROUP_SIZE + j_idx
    esel = jnp.zeros(c3.shape, F32)
    for _ in range(TOP_K):
        em = jnp.max(jnp.max(work, axis=1, keepdims=True), axis=0, keepdims=True)
        cand = jnp.where(work == em, e_idx, N_EXPERTS)
        first = jnp.min(jnp.min(cand, axis=1, keepdims=True), axis=0, keepdims=True)
        pick = e_idx == first
        esel = jnp.where(pick, 1.0, esel)
        work = jnp.where(pick, -jnp.inf, work)

    w = esel * scores.reshape(c3.shape)
    denom = jnp.sum(jnp.sum(w, axis=1, keepdims=True), axis=0, keepdims=True)
    return (w / denom * ROUTED_SCALE).reshape(N_EXPERTS, t)


def _merge_body(of_ref, ob_ref, x_ref, gate_ref, sh_ref, sc_ref, ogf_ref, ogb_ref, wo_ref, g2_ref,
                wrh_ref, wrl_ref, br_ref, x1_ref, h2_ref, comb_ref, combt_ref, cnt_ref):
    def group_norm(t_ref, gain_ref):
        t = t_ref[...].astype(F32)
        ms = jnp.mean(t * t, axis=-1, keepdims=True)
        return (t * lax.rsqrt(ms + EPS) * gain_ref[...]).astype(BF16)

    y = _dot(group_norm(of_ref, ogf_ref), wo_ref[0:WIDTH, :]) + _dot(group_norm(ob_ref, ogb_ref), wo_ref[WIDTH:, :])
    x = x_ref[...]
    g, r, d = x.shape
    x1 = x + gate_ref[...] * y.reshape(g, r, d)
    x1_ref[...] = x1
    ms = jnp.mean(x1 * x1, axis=-1, keepdims=True)
    h2 = (x1 * lax.rsqrt(ms + EPS) * g2_ref[...] * (1.0 + sc_ref[...]) + sh_ref[...]).reshape(g * r, d)
    h_hi, h_lo = _split2(h2)
    h2_ref[...] = h_hi
    logits = _dot_nt(wrh_ref[...], h_hi) + _dot_nt(wrh_ref[...], h_lo) + _dot_nt(wrl_ref[...], h_hi)
    scores = jax.nn.sigmoid(logits)
    t = scores.shape[1]
    bias = jnp.concatenate([br_ref[...]] * (t // LANES), axis=1)
    comb = _route(scores, scores + bias)
    comb_pad = jnp.concatenate([comb, jnp.zeros((LANES - N_EXPERTS, t), F32)], axis=0)
    comb_ref[...] = comb_pad.T
    combt_ref[...] = comb
    picked = jnp.where(comb > 0.0, 1.0, 0.0)
    for sub in range(cnt_ref.shape[0]):
        cnt = jnp.sum(picked[:, sub * MOE_TM:(sub + 1) * MOE_TM], axis=1, keepdims=True)
        cnt_ref[sub] = jnp.broadcast_to(cnt, (N_EXPERTS, LANES))


def _merge(of, ob, x, gate, shift, scale, ogf, ogb, wo, g2, wr_hi, wr_lo, br, *, G, R):
    nb, s, d = x.shape
    n = nb * s
    tm = G * R
    nbi, nsi = nb // G, s // R
    row = lambda b, i: (b * nsi + i, 0)
    const = lambda b, i: (0, 0)
    mod_spec = pl.BlockSpec((G, 1, d), lambda b, i: (b, 0, 0))
    x_spec = pl.BlockSpec((G, R, d), lambda b, i: (b, i, 0))
    return pl.pallas_call(
        _merge_body,
        grid=(nbi, nsi),
        in_specs=[pl.BlockSpec((tm, WIDTH), row), pl.BlockSpec((tm, WIDTH), row), x_spec,
                  mod_spec, mod_spec, mod_spec,
                  pl.BlockSpec((1, WIDTH), const), pl.BlockSpec((1, WIDTH), const),
                  pl.BlockSpec(wo.shape, const), pl.BlockSpec((1, d), const),
                  pl.BlockSpec(wr_hi.shape, const), pl.BlockSpec(wr_lo.shape, const),
                  pl.BlockSpec(br.shape, const)],
        out_specs=[x_spec, pl.BlockSpec((tm, d), row), pl.BlockSpec((tm, LANES), row),
                   pl.BlockSpec((N_EXPERTS, tm), lambda b, i: (0, b * nsi + i)),
                   pl.BlockSpec((tm // MOE_TM, N_EXPERTS, LANES), lambda b, i: (b * nsi + i, 0, 0))],
        out_shape=[jax.ShapeDtypeStruct((nb, s, d), F32), jax.ShapeDtypeStruct((n, d), BF16),
                   jax.ShapeDtypeStruct((n, LANES), F32), jax.ShapeDtypeStruct((N_EXPERTS, n), F32),
                   jax.ShapeDtypeStruct((n // MOE_TM, N_EXPERTS, LANES), F32)],
        compiler_params=_cparams(2),
        name="merge",
    )(of, ob, x, gate, shift, scale, ogf, ogb, wo, g2, wr_hi, wr_lo, br)


def _silu(g):
    return g * jax.nn.sigmoid(g)


def _moe_body(h_ref, comb_ref, x1_ref, gate_ref, wg_ref, wu_ref, wd_ref, sg_ref, su_ref, sd_ref, y_ref, acc_ref):
    e = pl.program_id(2)
    hb = h_ref[...]

    @pl.when(e == 0)
    def _():
        a = _silu(_dot(hb, sg_ref[...])) * _dot(hb, su_ref[...])
        acc_ref[...] = _dot(a.astype(BF16), sd_ref[...])

    comb = comb_ref[...]
    lane = lax.broadcasted_iota(jnp.int32, comb.shape, 1)
    c_e = jnp.sum(jnp.where(lane == e, comb, 0.0), axis=1, keepdims=True)
    a = _silu(_dot(hb, wg_ref[...])) * _dot(hb, wu_ref[...]) * c_e
    acc_ref[...] += _dot(a.astype(BF16), wd_ref[...])

    @pl.when(e == pl.num_programs(2) - 1)
    def _():
        x1 = x1_ref[...]
        g, r, d = x1.shape
        y_ref[...] = x1 + gate_ref[...] * acc_ref[...].reshape(g, r, d)


def _moe(h2, comb, x1, gate, wg, wu, wd, sg, su, sd, *, G, R):
    nb, s, d = x1.shape
    tm = G * R
    nbi, nsi = nb // G, s // R
    ff = wg.shape[2]
    row = lambda b, i, e: (b * nsi + i, 0)
    const = lambda b, i, e: (0, 0)
    x_spec = pl.BlockSpec((G, R, d), lambda b, i, e: (b, i, 0))
    return pl.pallas_call(
        _moe_body,
        grid=(nbi, nsi, N_EXPERTS),
        in_specs=[pl.BlockSpec((tm, d), row), pl.BlockSpec((tm, LANES), row), x_spec,
                  pl.BlockSpec((G, 1, d), lambda b, i, e: (b, 0, 0)),
                  pl.BlockSpec((None, d, ff), lambda b, i, e: (e, 0, 0)),
                  pl.BlockSpec((None, d, ff), lambda b, i, e: (e, 0, 0)),
                  pl.BlockSpec((None, ff, d), lambda b, i, e: (e, 0, 0)),
                  pl.BlockSpec(sg.shape, const), pl.BlockSpec(su.shape, const), pl.BlockSpec(sd.shape, const)],
        out_specs=x_spec,
        out_shape=jax.ShapeDtypeStruct((nb, s, d), F32),
        scratch_shapes=[pltpu.VMEM((tm, d), F32)],
        compiler_params=_cparams(3),
        name="moe",
    )(h2, comb, x1, gate, wg, wu, wd, sg, su, sd)


MOE_TM = 256
MOE_CH = 16
MOE_SLOTS = TOP_K * MOE_TM + N_EXPERTS * MOE_CH
MOE_NCHUNK = MOE_SLOTS // MOE_CH
MOE_PIECE = 512
MOE_NPIECE = MOE_SLOTS // MOE_PIECE
MOE_CPP = MOE_PIECE // MOE_CH
MOE_RB = 1024
TAU_RADIX = 64.0
assert MOE_SLOTS % MOE_PIECE == 0


def _moe_plan(cnt, n_tiles):
    pc = (cnt + MOE_CH - 1) // MOE_CH * MOE_CH
    off = jnp.cumsum(pc, axis=1) - pc
    end = off + pc
    n_used = (jnp.sum(pc, axis=1) // MOE_CH).astype(jnp.int32)
    tot = jnp.sum(pc, axis=0)
    reg = (tot + MOE_RB - 1) // MOE_RB * MOE_RB
    reg_end = jnp.cumsum(reg)
    reg_start = reg_end - reg
    dest_base = reg_start[None, :] + jnp.cumsum(pc, axis=0) - pc
    chunk_row = jnp.arange(MOE_NCHUNK, dtype=jnp.int32)[None, :] * MOE_CH
    in_group = (chunk_row[:, :, None] >= off[:, None, :]) & (chunk_row[:, :, None] < end[:, None, :])
    used = jnp.any(in_group, axis=2)
    cdest = chunk_row + jnp.sum(jnp.where(in_group, (dest_base - off)[:, None, :], 0), axis=2)
    worst_rows = TOP_K * MOE_TM * n_tiles + n_tiles * N_EXPERTS * (MOE_CH - 1) + N_EXPERTS * (MOE_RB - MOE_CH)
    r_max = -(-worst_rows // MOE_RB)
    cdest_d = jnp.where(used, cdest, r_max * MOE_RB + chunk_row).astype(jnp.int32)
    cdest_c = jnp.where(used, cdest, chunk_row).astype(jnp.int32)
    n_active = (reg_end[-1] // MOE_RB).astype(jnp.int32).reshape(1)
    tile_row = jnp.arange(r_max, dtype=jnp.int32) * MOE_RB
    tile_expert = jnp.minimum(jnp.sum((tile_row[:, None] >= reg_end[None, :]).astype(jnp.int32), axis=1),
                              N_EXPERTS - 1).astype(jnp.int32)
    in_region = (tile_row[:, None] >= reg_start[None, :]) & (tile_row[:, None] < reg_end[None, :])
    rows_end = jnp.sum(jnp.where(in_region, (reg_start + tot)[None, :], 0), axis=1)
    valid = jnp.clip(rows_end - tile_row, 0, MOE_RB).astype(jnp.int32)
    f = lambda a: a.astype(F32)
    zeros64 = jnp.zeros((n_tiles, N_EXPERTS), F32)
    row2 = lambda a: jnp.broadcast_to(jnp.concatenate([f(a), f(a)], axis=1)[:, None, :], (n_tiles, 8, LANES))
    col = lambda a: jnp.broadcast_to(f(a)[:, :, None], (n_tiles, N_EXPERTS, LANES))
    col128 = lambda a: jnp.broadcast_to(jnp.concatenate([f(a), zeros64], axis=1)[:, :, None], (n_tiles, LANES, LANES))
    row1 = lambda a: jnp.broadcast_to(jnp.concatenate([f(a), zeros64], axis=1)[:, None, :], (n_tiles, 8, LANES))
    return dict(n_used=n_used, cdest_d=cdest_d.reshape(n_tiles, 1, MOE_NCHUNK),
                cdest_c=cdest_c.reshape(n_tiles, 1, MOE_NCHUNK), r_max=r_max, n_active=n_active,
                tile_expert=tile_expert, valid=valid,
                off_row2=row2(off), end_row2=row2(end), off_col=col(off),
                off_row1=row1(off), off_col128=col128(off), end_col128=col128(end))


def _tau_pieces(sel, tau):
    tau = jnp.where(sel, tau, -1.0)
    hi = jnp.floor(tau * (1.0 / TAU_RADIX)) * TAU_RADIX
    return hi.astype(BF16), (tau - hi).astype(BF16)


def _dispatch_body(nused_ref, cdest_ref, h_ref, combt_ref, offcol_ref, offrow_ref, endrow_ref, sorted_ref,
                   buf_ref, sem):
    t = pl.program_id(0)
    n_used = nused_ref[t]
    tm = h_ref.shape[0]
    sel = combt_ref[...] > 0.0
    rr = lax.broadcasted_iota(jnp.int32, (tm, tm), 0)
    cc = lax.broadcasted_iota(jnp.int32, (tm, tm), 1)
    upper = jnp.where(rr < cc, 1.0, 0.0).astype(BF16)
    rank = _dot(jnp.where(sel, 1.0, 0.0).astype(BF16), upper)
    cols = jnp.concatenate([offcol_ref[...]] * (tm // LANES), axis=1)
    tau_hi, tau_lo = _tau_pieces(sel, cols + rank)
    taucat = jnp.concatenate([tau_hi, tau_lo], axis=0)
    off_row = offrow_ref[0:1, :]
    end_row = endrow_ref[0:1, :]
    hb = h_ref[...]

    def start_piece(piece):
        chunks = range(piece * MOE_CPP, (piece + 1) * MOE_CPP)
        dests = [cdest_ref[0, c] for c in chunks]
        for c, row in zip(chunks, dests):
            dst = sorted_ref.at[pl.ds(pl.multiple_of(row, MOE_CH), MOE_CH), :]
            pltpu.make_async_copy(buf_ref.at[pl.ds(c * MOE_CH, MOE_CH), :], dst, sem).start()

    def wait_piece(piece):
        rows = pl.ds(piece * MOE_PIECE, MOE_PIECE)
        pltpu.make_async_copy(buf_ref.at[rows, :], sorted_ref.at[rows, :], sem).wait()

    for piece in range(MOE_NPIECE):
        @pl.when(piece * MOE_CPP < n_used)
        def _():
            if piece > 0:
                start_piece(piece - 1)
            base = piece * MOE_PIECE
            s_col = (base + lax.broadcasted_iota(jnp.int32, (MOE_PIECE, LANES), 0)).astype(F32)
            onehot = jnp.where((s_col >= off_row) & (s_col < end_row), 1.0, 0.0).astype(BF16)
            q = _dot(onehot, taucat)
            s_mat = (base + lax.broadcasted_iota(jnp.int32, (MOE_PIECE, tm), 0)).astype(F32)
            g = jnp.where(q == s_mat, 1.0, 0.0).astype(BF16)
            buf_ref[pl.ds(base, MOE_PIECE), :] = _dot(g, hb).astype(BF16)

    last = (n_used - 1) // MOE_CPP
    for piece in range(MOE_NPIECE):
        @pl.when(piece == last)
        def _():
            start_piece(piece)

    for piece in range(MOE_NPIECE):
        @pl.when(piece * MOE_CPP < n_used)
        def _():
            wait_piece(piece)


def _dispatch(h2, combt, plan, n_tiles):
    n, d = h2.shape
    r_total = plan["r_max"] * MOE_RB + MOE_SLOTS
    grid_spec = pltpu.PrefetchScalarGridSpec(
        num_scalar_prefetch=1,
        grid=(n_tiles,),
        in_specs=[pl.BlockSpec((None, 1, MOE_NCHUNK), lambda t, nu: (t, 0, 0), memory_space=pltpu.SMEM),
                  pl.BlockSpec((MOE_TM, d), lambda t, nu: (t, 0)),
                  pl.BlockSpec((N_EXPERTS, MOE_TM), lambda t, nu: (0, t)),
                  pl.BlockSpec((None, N_EXPERTS, LANES), lambda t, nu: (t, 0, 0)),
                  pl.BlockSpec((None, 8, LANES), lambda t, nu: (t, 0, 0)),
                  pl.BlockSpec((None, 8, LANES), lambda t, nu: (t, 0, 0))],
        out_specs=pl.BlockSpec(memory_space=pl.ANY),
        scratch_shapes=[pltpu.VMEM((MOE_SLOTS, d), BF16), pltpu.SemaphoreType.DMA(())],
    )
    return pl.pallas_call(
        _dispatch_body,
        grid_spec=grid_spec,
        out_shape=jax.ShapeDtypeStruct((r_total, d), BF16),
        compiler_params=_cparams(1),
        name="moe_dispatch",
    )(plan["n_used"], plan["cdest_d"], h2, combt, plan["off_col"], plan["off_row2"], plan["end_row2"])


def _ffn_body(texp_ref, nact_ref, valid_ref, x_ref, wg_ref, wu_ref, wd_ref, o_ref):
    r = pl.program_id(0)

    @pl.when(r < nact_ref[0])
    def _():
        x = x_ref[...]
        rows = lax.broadcasted_iota(jnp.int32, x.shape, 0)
        x = jnp.where(rows < valid_ref[r], x, jnp.zeros_like(x))
        a = _silu(_dot(x, wg_ref[...])) * _dot(x, wu_ref[...])
        o_ref[...] = _dot(a.astype(BF16), wd_ref[...]).astype(o_ref.dtype)


def _ffn(xs, wg, wu, wd, plan):
    r_total, d = xs.shape
    ff = wg.shape[2]
    last = lambda r, te, na, va: (jnp.minimum(r, na[0] - 1), 0)
    wmap = lambda r, te, na, va: (te[r], 0, 0)
    grid_spec = pltpu.PrefetchScalarGridSpec(
        num_scalar_prefetch=3,
        grid=(plan["r_max"],),
        in_specs=[pl.BlockSpec((MOE_RB, d), last),
                  pl.BlockSpec((None, d, ff), wmap), pl.BlockSpec((None, d, ff), wmap),
                  pl.BlockSpec((None, ff, d), wmap)],
        out_specs=pl.BlockSpec((MOE_RB, d), last),
    )
    return pl.pallas_call(
        _ffn_body,
        grid_spec=grid_spec,
        out_shape=jax.ShapeDtypeStruct((r_total, d), BF16),
        compiler_params=_cparams(1),
        name="moe_ffn",
    )(plan["tile_expert"], plan["n_active"], plan["valid"], xs, wg, wu, wd)


def _combine_body(nused_ref, cdest_ref, h_ref, comb_ref, x1_ref, gate_ref, offrow_ref, offcol_ref, endcol_ref,
                  sg_ref, su_ref, sd_ref, ys_ref, y_ref, buf_ref, acc_ref, sem):
    t = pl.program_id(0)
    n_used = nused_ref[t]
    tm = h_ref.shape[0]

    @pl.when(t == 0)
    def _():
        buf_ref[...] = jnp.zeros_like(buf_ref)

    def wait_piece(piece):
        rows = pl.ds(piece * MOE_PIECE, MOE_PIECE)
        pltpu.make_async_copy(ys_ref.at[rows, :], buf_ref.at[rows, :], sem.at[piece]).wait()

    for piece in range(MOE_NPIECE):
        @pl.when(piece * MOE_CPP < n_used)
        def _():
            chunks = range(piece * MOE_CPP, (piece + 1) * MOE_CPP)
            srcs = [cdest_ref[0, c] for c in chunks]
            for c, row in zip(chunks, srcs):
                src = ys_ref.at[pl.ds(pl.multiple_of(row, MOE_CH), MOE_CH), :]
                pltpu.make_async_copy(src, buf_ref.at[pl.ds(c * MOE_CH, MOE_CH), :], sem.at[piece]).start()

    hb = h_ref[...]
    acc = _dot((_silu(_dot(hb, sg_ref[...])) * _dot(hb, su_ref[...])).astype(BF16), sd_ref[...])
    comb = comb_ref[...]
    sel = comb > 0.0
    rr = lax.broadcasted_iota(jnp.int32, (tm, tm), 0)
    cc = lax.broadcasted_iota(jnp.int32, (tm, tm), 1)
    lower = jnp.where(cc < rr, 1.0, 0.0).astype(BF16)
    rank = _dot(lower, jnp.where(sel, 1.0, 0.0).astype(BF16))
    tau_hi, tau_lo = _tau_pieces(sel, offrow_ref[0:1, :] + rank)
    taucat = jnp.concatenate([tau_hi, tau_lo], axis=1)
    lhs = jnp.concatenate([taucat, jnp.concatenate([comb.astype(BF16), jnp.zeros((tm, LANES), BF16)], axis=1)],
                          axis=0)
    off_col = offcol_ref[...]
    end_col = endcol_ref[...]
    acc_ref[...] = acc

    for piece in range(MOE_NPIECE):
        @pl.when(piece * MOE_CPP < n_used)
        def _():
            base = piece * MOE_PIECE
            s_row = (base + lax.broadcasted_iota(jnp.int32, (LANES, MOE_PIECE), 1)).astype(F32)
            off_b = jnp.concatenate([off_col] * (MOE_PIECE // LANES), axis=1)
            end_b = jnp.concatenate([end_col] * (MOE_PIECE // LANES), axis=1)
            onehot = jnp.where((s_row >= off_b) & (s_row < end_b), 1.0, 0.0).astype(BF16)
            qw = _dot(lhs, jnp.concatenate([onehot, onehot], axis=0))
            s_mat = (base + lax.broadcasted_iota(jnp.int32, (tm, MOE_PIECE), 1)).astype(F32)
            gw = jnp.where(qw[0:tm] == s_mat, qw[tm:2 * tm], 0.0).astype(BF16)
            wait_piece(piece)
            acc_ref[...] += _dot(gw, buf_ref[pl.ds(base, MOE_PIECE), :])
    y_ref[...] = x1_ref[...] + gate_ref[...] * acc_ref[...]


def _combine(h2, comb, x1, gate, ys, sg, su, sd, plan, n_tiles):
    nb, s, d = x1.shape
    n = nb * s
    per_b = s // MOE_TM
    x1f = x1.reshape(n, d)
    const2 = lambda t, nu: (0, 0)
    grid_spec = pltpu.PrefetchScalarGridSpec(
        num_scalar_prefetch=1,
        grid=(n_tiles,),
        in_specs=[pl.BlockSpec((None, 1, MOE_NCHUNK), lambda t, nu: (t, 0, 0), memory_space=pltpu.SMEM),
                  pl.BlockSpec((MOE_TM, d), lambda t, nu: (t, 0)),
                  pl.BlockSpec((MOE_TM, LANES), lambda t, nu: (t, 0)),
                  pl.BlockSpec((MOE_TM, d), lambda t, nu: (t, 0)),
                  pl.BlockSpec((None, 1, d), lambda t, nu: (t // per_b, 0, 0)),
                  pl.BlockSpec((None, 8, LANES), lambda t, nu: (t, 0, 0)),
                  pl.BlockSpec((None, LANES, LANES), lambda t, nu: (t, 0, 0)),
                  pl.BlockSpec((None, LANES, LANES), lambda t, nu: (t, 0, 0)),
                  pl.BlockSpec(sg.shape, const2), pl.BlockSpec(su.shape, const2), pl.BlockSpec(sd.shape, const2),
                  pl.BlockSpec(memory_space=pl.ANY)],
        out_specs=pl.BlockSpec((MOE_TM, d), lambda t, nu: (t, 0)),
        scratch_shapes=[pltpu.VMEM((MOE_SLOTS, d), BF16), pltpu.VMEM((MOE_TM, d), F32),
                        pltpu.SemaphoreType.DMA((MOE_NPIECE,))],
    )
    y = pl.pallas_call(
        _combine_body,
        grid_spec=grid_spec,
        out_shape=jax.ShapeDtypeStruct((n, d), F32),
        compiler_params=_cparams(1),
        name="moe_combine",
    )(plan["n_used"], plan["cdest_c"], h2, comb, x1f, gate, plan["off_row1"], plan["off_col128"], plan["end_col128"],
      sg, su, sd, ys)
    return y.reshape(nb, s, d)


def _moe_sparse(h2, comb, combt, cnt, x1, gate, wg, wu, wd, sg, su, sd):
    n = h2.shape[0]
    n_tiles = n // MOE_TM
    plan = _moe_plan(cnt[:, :, 0].astype(jnp.int32), n_tiles)
    xs = _dispatch(h2, combt, plan, n_tiles)
    ys = _ffn(xs, wg, wu, wd, plan)
    return _combine(h2, comb, x1, gate, ys, sg, su, sd, plan, n_tiles)


def _tile_heads(g, mult=1.0):
    return (jnp.tile(g.astype(F32), N_HEADS) * mult).reshape(1, WIDTH)


def kernel(x_prompt, x_sample, cache_fox_k, cache_fox_v, cache_fox_logf, cache_band_k, cache_band_v, c_prompt, c_sample, w_ada, b_ada, norm1_g, norm2_g, w_in, b_forget, g_q_fox, g_k_fox, g_q_band, g_k_band, rel_bias, out_g_fox, out_g_band, w_out, w_router, b_router, w_gate, w_up, w_down, ws_gate, ws_up, ws_down):
    depth = w_ada.shape[0]
    assert depth == 1
    bsz, seq, d = x_prompt.shape
    dbs, dseq, _ = x_sample.shape
    past = cache_fox_k.shape[2]
    n_cache = cache_band_k.shape[2]
    assert n_cache == BAND_REACH and dseq == CHUNK and seq % BAND_REACH == 0

    wi = w_in[0]
    cols = [wi[:, 0:512], wi[:, 512:1024], wi[:, 1024:1536], wi[:, 1544:2056], wi[:, 2056:2568], wi[:, 2568:3080],
            wi[:, 1536:1544], jnp.zeros((d, LANES - N_HEADS), F32)]
    w_all = jnp.concatenate(cols, axis=1).astype(BF16)
    hd = jnp.arange(WIDTH) // HEAD_DIM
    bd = jnp.where(hd[:, None] == hd[None, :], 1.0 / HEAD_DIM, 0.0).astype(BF16)
    qscale = ATTN_SCALE * LOG2E
    gqf, gkf = _tile_heads(g_q_fox[0], qscale), _tile_heads(g_k_fox[0])
    gqb, gkb = _tile_heads(g_q_band[0], qscale), _tile_heads(g_k_band[0])
    bf_row = jnp.concatenate([b_forget[0], jnp.zeros((LANES - N_HEADS,), F32)]).reshape(1, LANES)
    g1 = norm1_g[0].reshape(1, d)
    g2 = norm2_g[0].reshape(1, d)
    ogf = out_g_fox[0].reshape(1, WIDTH)
    ogb = out_g_band[0].reshape(1, WIDTH)
    wo = w_out[0].astype(BF16)
    wr_t = w_router[0].T
    wr_hi = wr_t.astype(BF16)
    wr_lo = (wr_t - wr_hi.astype(F32)).astype(BF16)
    br = jnp.broadcast_to(b_router[0].reshape(N_EXPERTS, 1), (N_EXPERTS, LANES)).astype(F32)
    wg, wu, wd = w_gate[0].astype(BF16), w_up[0].astype(BF16), w_down[0].astype(BF16)
    sg, su, sd = ws_gate[0].astype(BF16), ws_up[0].astype(BF16), ws_down[0].astype(BF16)

    n_c = bsz + dbs
    rows = -(-n_c // 8) * 8
    c_all = jnp.concatenate([c_prompt, c_sample, jnp.zeros((rows - n_c, d), F32)], axis=0)
    mod = _ada(c_all, w_ada[0], b_ada[0].reshape(1, -1))

    def mods(lo, hi):
        return [mod[lo:hi, j * d:(j + 1) * d].reshape(hi - lo, 1, d) for j in range(6)]

    shift1_p, scale1_p, gate1_p, shift2_p, scale2_p, gate2_p = mods(0, bsz)
    shift1_s, scale1_s, gate1_s, shift2_s, scale2_s, gate2_s = mods(bsz, n_c)

    TM = BAND_REACH
    (qf, kf, vft, kf32, vf32, lf, qb, kb, vbt, kb32, vb32) = _proj(
        x_prompt, shift1_p, scale1_p, g1, w_all, bd, gqf, gkf, gqb, gkb, bf_row, G=1, R=TM, band_last_only=True)
    r3 = lambda a: a.reshape(bsz, seq, a.shape[-1])
    ct, ka = _scan_t(r3(lf), T=TM)
    of = _foxt(r3(qf), r3(kf), ka, vft, ct, T=TM)
    assert TM == BAND_STEP
    ob = _bandt(r3(qb), r3(kb), vbt, _band_bias_tile_t(rel_bias[0]))
    assert TM % MOE_TM == 0
    x1_p, h2_p, comb_p, combt_p, cnt_p = _merge(of.reshape(-1, WIDTH), ob.reshape(-1, WIDTH), x_prompt, gate1_p,
                                                shift2_p, scale2_p, ogf, ogb, wo, g2, wr_hi, wr_lo, br, G=1, R=TM)
    y_p = _moe_sparse(h2_p, comb_p, combt_p, cnt_p, x1_p, gate2_p, wg, wu, wd, sg, su, sd)

    GS = 8
    (qf_s, kf_s, vf_s, kf32_s, vf32_s, lf_s, qb_s, kb_s, vb_s, kb32_s, vb32_s) = _proj(
        x_sample, shift1_s, scale1_s, g1, w_all, bd, gqf, gkf, gqb, gkb, bf_row, G=GS, R=dseq,
        band_last_only=False)
    s3 = lambda a: a.reshape(dbs, dseq, a.shape[-1])
    sk = past + dseq
    skp = -(-sk // LANES) * LANES
    pad_k = skp - sk
    n_seq = dbs * N_HEADS
    assert n_seq <= LANES
    lf_seq = jnp.concatenate([cache_fox_logf[0], s3(lf_s)[:, :, :N_HEADS]], axis=1)
    lf_seq = jnp.swapaxes(lf_seq, 0, 1).reshape(sk, n_seq)
    lf_seq = jnp.pad(lf_seq, ((0, pad_k), (0, LANES - n_seq)))
    cum_col, cum_row = _scan(lf_seq)
    cq_s = jnp.swapaxes(cum_col[past:past + dseq, :n_seq].reshape(dseq, dbs, N_HEADS), 0, 1)
    cum_s = jnp.pad(cq_s, ((0, 0), (0, 0), (0, LANES - N_HEADS)))
    cumt_s = cum_row[:n_seq].reshape(dbs, 1, N_HEADS, skp)
    zpad = jnp.zeros((dbs, pad_k, WIDTH), BF16)
    k_all = jnp.concatenate([cache_fox_k[0].reshape(dbs, past, WIDTH).astype(BF16), s3(kf_s), zpad], axis=1)
    v_all = jnp.concatenate([cache_fox_v[0].reshape(dbs, past, WIDTH).astype(BF16), s3(vf_s), zpad], axis=1)
    of_s = _fox(s3(qf_s), k_all, v_all, cum_s, cumt_s, TQ=dseq, TK=skp, q_off=past)
    bias_s = _band_bias_tile(rel_bias[0], dseq, BAND_REACH + LANES)
    zb = jnp.zeros((dbs, LANES - dseq, WIDTH), BF16)
    kb_all = jnp.concatenate([cache_band_k[0].reshape(dbs, n_cache, WIDTH).astype(BF16), s3(kb_s), zb], axis=1)
    vb_all = jnp.concatenate([cache_band_v[0].reshape(dbs, n_cache, WIDTH).astype(BF16), s3(vb_s), zb], axis=1)
    ob_s = _band(s3(qb_s), kb_all, vb_all, bias_s, TQ=dseq, n_sub=1, padded=False)
    x1_s, h2_s, comb_s, _, _ = _merge(of_s.reshape(-1, WIDTH), ob_s.reshape(-1, WIDTH), x_sample, gate1_s, shift2_s,
                                      scale2_s, ogf, ogb, wo, g2, wr_hi, wr_lo, br, G=GS, R=dseq)
    y_s = _moe(h2_s, comb_s, x1_s, gate2_s, wg, wu, wd, sg, su, sd, G=dbs, R=dseq)

    hshape = (N_HEADS, HEAD_DIM)
    new_bk_s = jnp.concatenate([cache_band_k[0], s3(kb32_s).reshape(dbs, dseq, *hshape)], axis=1)[:, -n_cache:]
    new_bv_s = jnp.concatenate([cache_band_v[0], s3(vb32_s).reshape(dbs, dseq, *hshape)], axis=1)[:, -n_cache:]
    return (y_p, y_s,
            kf32.reshape(1, bsz, seq, *hshape), vf32.reshape(1, bsz, seq, *hshape),
            lf[:, :N_HEADS].reshape(1, bsz, seq, N_HEADS),
            kb32.reshape(1, bsz, BAND_REACH, *hshape), vb32.reshape(1, bsz, BAND_REACH, *hshape),
            kf32_s.reshape(1, dbs, dseq, *hshape), vf32_s.reshape(1, dbs, dseq, *hshape),
            lf_s[:, :N_HEADS].reshape(1, dbs, dseq, N_HEADS),
            new_bk_s[None], new_bv_s[None])
```

```python
import functools

import jax
import jax.numpy as jnp
import numpy as np
from jax import lax
from jax.experimental import pallas as pl
from jax.experimental.pallas import tpu as pltpu

F32 = jnp.float32
BF16 = jnp.bfloat16

HEAD_DIM = 64
N_HEADS = 8
WIDTH = N_HEADS * HEAD_DIM
PAIR = 2 * HEAD_DIM
N_PAIRS = N_HEADS // 2
LANES = 128
CHUNK = 64
BAND_REACH = 512
REL_CLIP = 256
N_EXPERTS = 64
N_GROUPS = 8
GROUP_SIZE = N_EXPERTS // N_GROUPS
TOPK_GROUPS = 4
TOP_K = 8
ROUTED_SCALE = 2.5
EPS = 1e-6
NEG_INF = -1e30
ATTN_SCALE = HEAD_DIM ** -0.5
LOG2E = 1.4426950408889634
VMEM_LIMIT = 56 * 1024 * 1024


def _cparams(n_axes):
    return pltpu.CompilerParams(dimension_semantics=("arbitrary",) * n_axes,
                                vmem_limit_bytes=VMEM_LIMIT)


def _dot(a, b):
    return jnp.dot(a, b, preferred_element_type=F32)


def _dot_nt(a, b):
    return lax.dot_general(a, b, (((1,), (1,)), ((), ())), preferred_element_type=F32)


def _split2(a):
    hi = a.astype(BF16)
    lo = (a - hi.astype(F32)).astype(BF16)
    return hi, lo


def _split3(a):
    hi = a.astype(BF16)
    r = a - hi.astype(F32)
    mid = r.astype(BF16)
    lo = (r - mid.astype(F32)).astype(BF16)
    return hi, mid, lo


def _ada_body(c_ref, w_ref, b_ref, o_ref):
    c = c_ref[...]
    a = c * jax.nn.sigmoid(c)
    a_hi, a_lo = _split2(a)
    w_hi, w_lo = _split2(w_ref[...])
    o_ref[...] = _dot(a_hi, w_hi) + _dot(a_hi, w_lo) + _dot(a_lo, w_hi) + b_ref[...]


def _ada(c_all, w_ada, b_ada):
    rows, d = c_all.shape
    n = w_ada.shape[1]
    tn = 1024
    return pl.pallas_call(
        _ada_body,
        grid=(n // tn,),
        in_specs=[pl.BlockSpec((rows, d), lambda j: (0, 0)),
                  pl.BlockSpec((d, tn), lambda j: (0, j)),
                  pl.BlockSpec((1, tn), lambda j: (0, j))],
        out_specs=pl.BlockSpec((rows, tn), lambda j: (0, j)),
        out_shape=jax.ShapeDtypeStruct((rows, n), F32),
        compiler_params=_cparams(1),
        name="ada",
    )(c_all, w_ada, b_ada)


def _log_sigmoid(z):
    return jnp.minimum(z, 0.0) - jnp.log(1.0 + jnp.exp(-jnp.abs(z)))


def _proj_body(x_ref, sh_ref, sc_ref, g1_ref, w_ref, bd_ref, gqf_ref, gkf_ref, gqb_ref, gkb_ref, bf_ref,
               qf_ref, kf_ref, vf_ref, kf32_ref, vf32_ref, lf_ref, qb_ref, kb_ref, vb_ref, kb32_ref, vb32_ref,
               *, band_last_only):
    x = x_ref[...]
    g, r, d = x.shape
    ms = jnp.mean(x * x, axis=-1, keepdims=True)
    h = x * lax.rsqrt(ms + EPS) * g1_ref[...] * (1.0 + sc_ref[...]) + sh_ref[...]
    hb = h.reshape(g * r, d).astype(BF16)

    def seg(i):
        return _dot(hb, w_ref[:, i * WIDTH:(i + 1) * WIDTH])

    def head_norm(t, gain_ref):
        ssq = _dot((t * t).astype(BF16), bd_ref[...])
        return t * lax.rsqrt(ssq + EPS) * gain_ref[...]

    qf_ref[...] = head_norm(seg(0), gqf_ref).astype(BF16)
    kf = head_norm(seg(1), gkf_ref)
    kf32_ref[...] = kf
    kf_ref[...] = kf.astype(BF16)
    vf = seg(2)
    vf32_ref[...] = vf
    if band_last_only:
        vf_ref[...] = vf.T.astype(BF16)
    else:
        vf_ref[...] = vf.astype(BF16)
    z = _dot(hb, w_ref[:, 6 * WIDTH:6 * WIDTH + LANES]) + bf_ref[...]
    lf_ref[...] = _log_sigmoid(z)
    qb_ref[...] = head_norm(seg(3), gqb_ref).astype(BF16)
    kb = head_norm(seg(4), gkb_ref)
    kb_ref[...] = kb.astype(BF16)
    vb = seg(5)
    if band_last_only:
        vb_ref[...] = vb.T.astype(BF16)
    else:
        vb_ref[...] = vb.astype(BF16)

    if band_last_only:
        @pl.when(pl.program_id(1) == pl.num_programs(1) - 1)
        def _():
            kb32_ref[...] = kb
            vb32_ref[...] = vb
    else:
        kb32_ref[...] = kb
        vb32_ref[...] = vb


def _proj(x, shift, scale, g1, w_all, bd, gqf, gkf, gqb, gkb, bf_row, *, G, R, band_last_only):
    nb, s, d = x.shape
    n = nb * s
    tm = G * R
    nbi, nsi = nb // G, s // R
    grid = (nbi, nsi)
    row = lambda b, i: (b * nsi + i, 0)
    const = lambda b, i: (0, 0)
    mod_spec = pl.BlockSpec((G, 1, d), lambda b, i: (b, 0, 0))
    out_bf = jax.ShapeDtypeStruct((n, WIDTH), BF16)
    out_f32 = jax.ShapeDtypeStruct((n, WIDTH), F32)
    tile = pl.BlockSpec((tm, WIDTH), row)
    if band_last_only:
        assert G == 1 and R == BAND_REACH
        band_shape = jax.ShapeDtypeStruct((nb, BAND_REACH, WIDTH), F32)
        band_spec = pl.BlockSpec((None, BAND_REACH, WIDTH), lambda b, i: (b, 0, 0))
        v_shape = jax.ShapeDtypeStruct((nb, nsi, WIDTH, tm), BF16)
        v_spec = pl.BlockSpec((None, None, WIDTH, tm), lambda b, i: (b, i, 0, 0))
    else:
        band_shape, band_spec = out_f32, tile
        v_shape, v_spec = out_bf, tile
    return pl.pallas_call(
        functools.partial(_proj_body, band_last_only=band_last_only),
        grid=grid,
        in_specs=[pl.BlockSpec((G, R, d), lambda b, i: (b, i, 0)), mod_spec, mod_spec,
                  pl.BlockSpec((1, d), const), pl.BlockSpec(w_all.shape, const), pl.BlockSpec(bd.shape, const),
                  pl.BlockSpec((1, WIDTH), const), pl.BlockSpec((1, WIDTH), const),
                  pl.BlockSpec((1, WIDTH), const), pl.BlockSpec((1, WIDTH), const),
                  pl.BlockSpec((1, LANES), const)],
        out_specs=[tile, tile, v_spec, tile, tile, pl.BlockSpec((tm, LANES), row), tile, tile, v_spec,
                   band_spec, band_spec],
        out_shape=[out_bf, out_bf, v_shape, out_f32, out_f32, jax.ShapeDtypeStruct((n, LANES), F32),
                   out_bf, out_bf, v_shape, band_shape, band_shape],
        compiler_params=_cparams(2),
        name="proj",
    )(x, shift, scale, g1, w_all, bd, gqf, gkf, gqb, gkb, bf_row)


def _scan_body(lf_ref, cum_ref, cumt_ref):
    lf = lf_ref[...]
    s = lf.shape[0]
    hi, mid, lo = _split3(lf)
    rr = lax.broadcasted_iota(jnp.int32, (s, s), 0)
    cc = lax.broadcasted_iota(jnp.int32, (s, s), 1)
    tri = jnp.where(cc <= rr, 1.0, 0.0).astype(BF16)
    cum2 = (_dot(tri, hi) + _dot(tri, mid) + _dot(tri, lo)) * LOG2E
    cum_ref[...] = cum2
    cumt_ref[...] = cum2.T


def _scan(lf):
    s, _ = lf.shape
    return pl.pallas_call(
        _scan_body,
        out_shape=[jax.ShapeDtypeStruct((s, LANES), F32), jax.ShapeDtypeStruct((LANES, s), F32)],
        compiler_params=pltpu.CompilerParams(vmem_limit_bytes=VMEM_LIMIT),
        name="scan",
    )(lf)


def _fox_body(q_ref, k_ref, v_ref, cq_ref, ck_ref, o_ref, *, TQ, TK, q_off):
    p = pl.program_id(1)
    i = pl.program_id(2)
    q = q_ref[...]
    cq_blk = cq_ref[...]
    lane = lax.broadcasted_iota(jnp.int32, (TQ, PAIR), 1)
    ones_blk = jnp.where(lax.broadcasted_iota(jnp.int32, (TK, LANES), 1) == 0, 1.0, 0.0).astype(BF16)
    q0 = q_off + i * TQ
    n_full = q0 // TK
    qpos = q0 + lax.broadcasted_iota(jnp.int32, (TQ, TK), 0)
    kcol = lax.broadcasted_iota(jnp.int32, (TQ, TK), 1)

    outs = []
    for par in range(2):
        h = 2 * p + par
        qm = jnp.where((lane >= HEAD_DIM) == (par == 1), q, jnp.zeros_like(q))
        cq_col = jnp.sum(jnp.where(lane == h, cq_blk, 0.0), axis=1, keepdims=True)
        ref0 = cq_col[0:1, :]
        cqr = cq_col - ref0

        def step(j, carry, masked):
            m, l, acc = carry
            k0 = pl.multiple_of(j * TK, TK)
            kb = k_ref[pl.ds(k0, TK), :]
            vb = v_ref[pl.ds(k0, TK), :]
            s = _dot_nt(qm, kb)
            ck = ck_ref[j, pl.ds(h, 1), :]
            u = s - (ck - ref0)
            if masked:
                u = jnp.where(k0 + kcol <= qpos, u, NEG_INF)
            m_new = jnp.maximum(m, jnp.max(u, axis=1, keepdims=True) + cqr)
            alpha = jnp.exp2(m - m_new)
            pexp = jnp.exp2(u + (cqr - m_new))
            pv = _dot(pexp.astype(BF16), jnp.concatenate([vb, ones_blk], axis=1))
            return m_new, alpha * l + pv[:, LANES:LANES + 1], alpha * acc + pv[:, :LANES]

        init = (jnp.full((TQ, 1), NEG_INF, F32), jnp.zeros((TQ, 1), F32), jnp.zeros((TQ, LANES), F32))
        carry = lax.fori_loop(0, n_full, lambda j, c: step(j, c, False), init)
        _, l, acc = step(n_full, carry, True)
        outs.append(acc / l)
    o_ref[...] = jnp.where(lane < HEAD_DIM, outs[0], outs[1]).astype(o_ref.dtype)


def _fox(q, k, v, cum, cumt, *, TQ, TK, q_off):
    b, sq, _ = q.shape
    sk = k.shape[1]
    return pl.pallas_call(
        functools.partial(_fox_body, TQ=TQ, TK=TK, q_off=q_off),
        grid=(b, N_PAIRS, sq // TQ),
        in_specs=[pl.BlockSpec((None, TQ, PAIR), lambda bi, p, i: (bi, i, p)),
                  pl.BlockSpec((None, sk, PAIR), lambda bi, p, i: (bi, 0, p)),
                  pl.BlockSpec((None, sk, PAIR), lambda bi, p, i: (bi, 0, p)),
                  pl.BlockSpec((None, TQ, LANES), lambda bi, p, i: (bi, i, 0)),
                  pl.BlockSpec((None, sk // TK, N_HEADS, TK), lambda bi, p, i: (bi, 0, 0, 0))],
        out_specs=pl.BlockSpec((None, TQ, PAIR), lambda bi, p, i: (bi, i, p)),
        out_shape=jax.ShapeDtypeStruct((b, sq, WIDTH), BF16),
        compiler_params=_cparams(3),
        name="fox",
    )(q, k, v, cum, cumt)


AUG_PIECES = 3
FOX_UNDERFLOW = 160.0
FOX_NORM_SLACK = 1.02
FOX_BOUND_SLACK = 2.0


def _scan_t_body(lf_ref, place_ref, ct_ref, ka_ref, carry_ref):
    @pl.when(pl.program_id(1) == 0)
    def _():
        carry_ref[...] = jnp.zeros_like(carry_ref)

    lf = lf_ref[...]
    ts = lf.shape[0]
    lane = lax.broadcasted_iota(jnp.int32, lf.shape, 1)
    lf = jnp.where(lane < N_HEADS, lf, 0.0)
    hi, mid, lo = _split3(lf)
    rr = lax.broadcasted_iota(jnp.int32, (ts, ts), 0)
    cc = lax.broadcasted_iota(jnp.int32, (ts, ts), 1)
    tri = jnp.where(cc <= rr, 1.0, 0.0).astype(BF16)
    cum = _dot(tri, hi) + _dot(tri, mid) + _dot(tri, lo) + carry_ref[0:1, :]
    carry_ref[...] = jnp.broadcast_to(cum[ts - 1:ts, :], carry_ref.shape)
    cum2 = cum * LOG2E
    ct_ref[...] = cum2.T[0:N_HEADS, :]
    pieces = _split3(cum2 - cum2[0:1, :])
    ka = _dot(pieces[0], place_ref[0]) + _dot(pieces[1], place_ref[1]) + _dot(pieces[2], place_ref[2])
    ka_ref[...] = ka.astype(BF16)


def _aug_placement():
    h = jnp.arange(LANES)[:, None]
    col = jnp.arange(WIDTH)[None, :]
    mats = []
    for x in range(AUG_PIECES):
        tgt = PAIR * (h // 2) + AUG_PIECES * (h % 2) + x
        mats.append(jnp.where((h < N_HEADS) & (col == tgt), 1.0, 0.0))
    return jnp.stack(mats).astype(BF16)


def _scan_t(lf, *, T):
    b, s, _ = lf.shape
    place = _aug_placement()
    return pl.pallas_call(
        _scan_t_body,
        grid=(b, s // T),
        in_specs=[pl.BlockSpec((None, T, LANES), lambda bi, i: (bi, i, 0)),
                  pl.BlockSpec(place.shape, lambda bi, i: (0, 0, 0))],
        out_specs=[pl.BlockSpec((None, None, N_HEADS, T), lambda bi, i: (bi, i, 0, 0)),
                   pl.BlockSpec((None, T, WIDTH), lambda bi, i: (bi, i, 0))],
        out_shape=[jax.ShapeDtypeStruct((b, s // T, N_HEADS, T), F32),
                   jax.ShapeDtypeStruct((b, s, WIDTH), BF16)],
        scratch_shapes=[pltpu.VMEM((8, LANES), F32)],
        compiler_params=_cparams(2),
        name="scan_t",
    )(lf, place)


def _foxt_body(q_ref, k_ref, ka_ref, vt_ref, ct_ref, o_ref, kn_ref, *, T):
    p = pl.program_id(1)
    i = pl.program_id(2)
    q = q_ref[...]
    lane = lax.broadcasted_iota(jnp.int32, (T, PAIR), 1)
    halves = []
    for par in range(2):
        qm = jnp.where((lane >= HEAD_DIM) == (par == 1), q, jnp.zeros_like(q))
        lo_lane = AUG_PIECES * par
        qa = jnp.where((lane >= lo_lane) & (lane < lo_lane + AUG_PIECES), -1.0, 0.0).astype(BF16)
        halves.append(jnp.concatenate([qm, qa], axis=1))
    qcat = jnp.concatenate(halves, axis=0)
    h_even = 2 * p
    cq = jnp.concatenate([ct_ref[i, pl.ds(h_even, 1), :], ct_ref[i, pl.ds(h_even + 1, 1), :]], axis=1)
    ones_rows = jnp.ones((16, T), BF16)
    krow = lax.broadcasted_iota(jnp.int32, (T, 2 * T), 0)
    qcol = lax.broadcasted_iota(jnp.int32, (T, 2 * T), 1) % T

    def step(j, carry, masked):
        m, acc_e, acc_o = carry
        k0 = pl.multiple_of(j * T, T)
        kcat = jnp.concatenate([k_ref[pl.ds(k0, T), :], ka_ref[pl.ds(k0, T), :]], axis=1)
        st = _dot_nt(kcat, qcat)
        c0 = jnp.concatenate([jnp.broadcast_to(ct_ref[j, pl.ds(h_even, 1), :][:, 0:1], (1, T)),
                              jnp.broadcast_to(ct_ref[j, pl.ds(h_even + 1, 1), :][:, 0:1], (1, T))], axis=1)
        rb = cq - c0
        if masked:
            st = jnp.where(krow <= qcol, st, NEG_INF)
        m_new = jnp.maximum(m, jnp.max(st, axis=0, keepdims=True) + rb)
        alpha = jnp.exp2(m - m_new)
        pt = jnp.exp2(st + (rb - m_new)).astype(BF16)
        vt = vt_ref[j]
        pv_e = _dot(jnp.concatenate([vt[0:HEAD_DIM], ones_rows], axis=0), pt[:, 0:T])
        pv_o = _dot(jnp.concatenate([vt[HEAD_DIM:PAIR], ones_rows], axis=0), pt[:, T:2 * T])
        return m_new, alpha[:, 0:T] * acc_e + pv_e, alpha[:, T:2 * T] * acc_o + pv_o

    @pl.when(i == 0)
    def _():
        ones = jnp.ones((PAIR, LANES), BF16)
        kmax = jnp.zeros((1, LANES), F32)
        for c in range(k_ref.shape[0] // T):
            kc = k_ref[c * T:(c + 1) * T, :].astype(F32)
            kmax = jnp.maximum(kmax, jnp.max(_dot((kc * kc).astype(BF16), ones), axis=0, keepdims=True))
        kn_ref[...] = jnp.broadcast_to(kmax, kn_ref.shape)

    rows = HEAD_DIM + 16
    init = (jnp.full((1, 2 * T), NEG_INF, F32), jnp.zeros((rows, T), F32), jnp.zeros((rows, T), F32))
    carry = step(i, init, True)

    qf = q.astype(F32)
    qsq = qf * qf
    kn2 = kn_ref[0:1, 0:1] * FOX_NORM_SLACK
    need = jnp.zeros((1, 1), jnp.int32)
    blk = lax.broadcasted_iota(jnp.int32, (ct_ref.shape[0], 1, 1), 0)
    for par in range(2):
        head_lanes = (lane >= HEAD_DIM) == (par == 1)
        qn2 = jnp.max(jnp.sum(jnp.where(head_lanes, qsq, 0.0), axis=1, keepdims=True), axis=0, keepdims=True)
        reach = jnp.sqrt(qn2 * kn2) + FOX_BOUND_SLACK
        m_min = jnp.min(carry[0][:, par * T:(par + 1) * T], axis=1, keepdims=True)
        cq_first = cq[:, par * T:par * T + 1]
        ck_end = ct_ref[:, pl.ds(h_even + par, 1), :][:, :, T - 1:T]
        live = (reach + cq_first - m_min)[None, :, :] - ck_end > -FOX_UNDERFLOW
        count = jnp.sum(jnp.where(live & (blk < i), 1, 0), axis=0)
        need = jnp.maximum(need, count)
    n_keep = need[0, 0]

    n_pairs = n_keep // 2
    carry = lax.fori_loop(0, n_pairs, lambda u, c: step(i - 2 - 2 * u, step(i - 1 - 2 * u, c, False), False), carry)
    carry = lax.fori_loop(2 * n_pairs, n_keep, lambda u, c: step(i - 1 - u, c, False), carry)
    _, acc_e, acc_o = carry
    o_t = jnp.concatenate([acc_e[0:HEAD_DIM] / acc_e[HEAD_DIM:HEAD_DIM + 1],
                           acc_o[0:HEAD_DIM] / acc_o[HEAD_DIM:HEAD_DIM + 1]], axis=0)
    o_ref[...] = o_t.T.astype(o_ref.dtype)


def _foxt(q, k, ka, vt, ct, *, T):
    b, s, _ = q.shape
    nt = s // T
    return pl.pallas_call(
        functools.partial(_foxt_body, T=T),
        grid=(b, N_PAIRS, nt),
        in_specs=[pl.BlockSpec((None, T, PAIR), lambda bi, p, i: (bi, i, p)),
                  pl.BlockSpec((None, s, PAIR), lambda bi, p, i: (bi, 0, p)),
                  pl.BlockSpec((None, s, PAIR), lambda bi, p, i: (bi, 0, p)),
                  pl.BlockSpec((None, nt, PAIR, T), lambda bi, p, i: (bi, 0, p, 0)),
                  pl.BlockSpec((None, nt, N_HEADS, T), lambda bi, p, i: (bi, 0, 0, 0))],
        out_specs=pl.BlockSpec((None, T, PAIR), lambda bi, p, i: (bi, i, p)),
        out_shape=jax.ShapeDtypeStruct((b, s, WIDTH), BF16),
        scratch_shapes=[pltpu.VMEM((8, LANES), F32)],
        compiler_params=_cparams(3),
        name="foxt",
    )(q, k, ka, vt, ct)


def _band_body(q_ref, k_ref, v_ref, bias_ref, o_ref, *scratch, TQ, W, n_sub, padded):
    p = pl.program_id(1)
    i = pl.program_id(2)
    if padded:
        kpad_ref, vpad_ref = scratch
        s_len = k_ref.shape[0]

        @pl.when(i == 0)
        def _():
            zeros = jnp.zeros((BAND_REACH, PAIR), BF16)
            kpad_ref[pl.ds(0, BAND_REACH), :] = zeros
            vpad_ref[pl.ds(0, BAND_REACH), :] = zeros
            kpad_ref[pl.ds(BAND_REACH, s_len), :] = k_ref[...]
            vpad_ref[pl.ds(BAND_REACH, s_len), :] = v_ref[...]
    else:
        kpad_ref, vpad_ref = k_ref, v_ref

    lane = lax.broadcasted_iota(jnp.int32, (TQ, PAIR), 1)
    ones_blk = jnp.where(lax.broadcasted_iota(jnp.int32, (W, LANES), 1) == 0, 1.0, 0.0).astype(BF16)
    kcol = lax.broadcasted_iota(jnp.int32, (TQ, W), 1)

    def sub_block(sub, carry):
        r0 = pl.multiple_of(sub * TQ, TQ)
        q0 = i * (n_sub * TQ) + r0
        q = q_ref[pl.ds(r0, TQ), :]
        kw = kpad_ref[pl.ds(pl.multiple_of(q0, TQ), W), :] if padded else kpad_ref[...]
        vw = vpad_ref[pl.ds(pl.multiple_of(q0, TQ), W), :] if padded else vpad_ref[...]
        vcat = jnp.concatenate([vw, ones_blk], axis=1)
        outs = []
        for par in range(2):
            h = 2 * p + par
            qm = jnp.where((lane >= HEAD_DIM) == (par == 1), q, jnp.zeros_like(q))
            s = _dot_nt(qm, kw) + bias_ref[h]
            if padded:
                s = jnp.where(kcol >= BAND_REACH - q0, s, NEG_INF)
            m = jnp.max(s, axis=1, keepdims=True)
            pexp = jnp.exp2(s - m)
            pv = _dot(pexp.astype(BF16), vcat)
            outs.append(pv[:, :LANES] / pv[:, LANES:LANES + 1])
        o_ref[pl.ds(r0, TQ), :] = jnp.where(lane < HEAD_DIM, outs[0], outs[1]).astype(o_ref.dtype)
        return carry

    lax.fori_loop(0, n_sub, sub_block, 0)


def _band(q, k, v, bias, *, TQ, n_sub, padded):
    b, sq, _ = q.shape
    sk = k.shape[1]
    w = BAND_REACH + TQ if padded else sk
    tqb = TQ * n_sub
    scratch = [pltpu.VMEM((sk + BAND_REACH, PAIR), BF16)] * 2 if padded else []
    return pl.pallas_call(
        functools.partial(_band_body, TQ=TQ, W=w, n_sub=n_sub, padded=padded),
        grid=(b, N_PAIRS, sq // tqb),
        in_specs=[pl.BlockSpec((None, tqb, PAIR), lambda bi, p, i: (bi, i, p)),
                  pl.BlockSpec((None, sk, PAIR), lambda bi, p, i: (bi, 0, p)),
                  pl.BlockSpec((None, sk, PAIR), lambda bi, p, i: (bi, 0, p)),
                  pl.BlockSpec(bias.shape, lambda bi, p, i: (0, 0, 0))],
        out_specs=pl.BlockSpec((None, tqb, PAIR), lambda bi, p, i: (bi, i, p)),
        out_shape=jax.ShapeDtypeStruct((b, sq, WIDTH), BF16),
        scratch_shapes=scratch,
        compiler_params=_cparams(3),
        name="band",
    )(q, k, v, bias)


BAND_TQ = 256
BAND_STEP = 512


def _bandt_body(q_ref, kp_ref, kc_ref, vp_ref, vc_ref, bias_ref, o_ref):
    i = pl.program_id(2)
    w = BAND_REACH + BAND_TQ
    k2 = jnp.concatenate([kp_ref[...], kc_ref[...]], axis=0)
    vt2 = jnp.concatenate([vp_ref[...], vc_ref[...]], axis=1)
    lane = lax.broadcasted_iota(jnp.int32, (BAND_TQ, PAIR), 1)
    ones_rows = jnp.ones((16, w), BF16)
    krow = lax.broadcasted_iota(jnp.int32, (w, 2 * BAND_TQ), 0)
    bias = bias_ref[...]
    for sub in range(BAND_STEP // BAND_TQ):
        r0 = sub * BAND_TQ
        q = q_ref[r0:r0 + BAND_TQ, :]
        qcat = jnp.concatenate([jnp.where(lane < HEAD_DIM, q, jnp.zeros_like(q)),
                                jnp.where(lane >= HEAD_DIM, q, jnp.zeros_like(q))], axis=0)
        st = _dot_nt(k2[r0:r0 + w], qcat) + bias
        st = jnp.where(krow >= (1 - i) * BAND_STEP - r0, st, NEG_INF)
        m = jnp.max(st, axis=0, keepdims=True)
        pt = jnp.exp2(st - m).astype(BF16)
        vwin = vt2[:, r0:r0 + w]
        outs = []
        for par in range(2):
            vcat = jnp.concatenate([vwin[par * HEAD_DIM:(par + 1) * HEAD_DIM], ones_rows], axis=0)
            pv = _dot(vcat, pt[:, par * BAND_TQ:(par + 1) * BAND_TQ])
            outs.append(pv[0:HEAD_DIM] / pv[HEAD_DIM:HEAD_DIM + 1])
        o_ref[r0:r0 + BAND_TQ, :] = jnp.concatenate(outs, axis=0).T.astype(o_ref.dtype)


def _bandt(q, k, vt, bias_t):
    b, s, _ = q.shape
    prev = lambda i: jnp.maximum(i - 1, 0)
    return pl.pallas_call(
        _bandt_body,
        grid=(b, N_PAIRS, s // BAND_STEP),
        in_specs=[pl.BlockSpec((None, BAND_STEP, PAIR), lambda bi, p, i: (bi, i, p)),
                  pl.BlockSpec((None, BAND_STEP, PAIR), lambda bi, p, i: (bi, prev(i), p)),
                  pl.BlockSpec((None, BAND_STEP, PAIR), lambda bi, p, i: (bi, i, p)),
                  pl.BlockSpec((None, None, PAIR, BAND_STEP), lambda bi, p, i: (bi, prev(i), p, 0)),
                  pl.BlockSpec((None, None, PAIR, BAND_STEP), lambda bi, p, i: (bi, i, p, 0)),
                  pl.BlockSpec((None,) + bias_t.shape[1:], lambda bi, p, i: (p, 0, 0))],
        out_specs=pl.BlockSpec((None, BAND_STEP, PAIR), lambda bi, p, i: (bi, i, p)),
        out_shape=jax.ShapeDtypeStruct((b, s, WIDTH), BF16),
        compiler_params=_cparams(3),
        name="bandt",
    )(q, k, k, vt, vt, bias_t)


def _band_bias_tile_t(rel_bias):
    tile = _band_bias_tile(rel_bias, BAND_TQ, BAND_REACH + BAND_TQ)
    t = jnp.swapaxes(tile, 1, 2)
    return jnp.concatenate([t[0::2], t[1::2]], axis=2)


def _band_bias_tile(rel_bias, tq, w):
    span = w + tq - 1
    period = span + 1
    v = np.arange(period)
    d = np.where(v < w, v, v - period)
    table_idx = np.clip(BAND_REACH - d, -REL_CLIP, REL_CLIP) + REL_CLIP
    table = rel_bias[:, table_idx] * LOG2E
    n_h = rel_bias.shape[0]
    vals = jnp.tile(table, (1, tq))[:, :tq * span].reshape(n_h, tq, span)[:, :, :w]
    r = np.arange(tq)[:, None]
    c = np.arange(w)[None, :]
    in_band = (c // CHUNK >= r // CHUNK) & (c // CHUNK <= r // CHUNK + BAND_REACH // CHUNK)
    return jnp.where(jnp.asarray(in_band)[None], vals, NEG_INF).astype(F32)


def _sample_attn_body(q_ref, kc_ref, vc_ref, kn_ref, vn_ref, *rest, fox):
    p = pl.program_id(1)
    t = q_ref.shape[0]
    past = kc_ref.shape[1]
    lane = lax.broadcasted_iota(jnp.int32, (t, PAIR), 1)
    q = q_ref[...]
    qs = jnp.concatenate([jnp.where(lane < HEAD_DIM, q, jnp.zeros_like(q)),
                          jnp.where(lane >= HEAD_DIM, q, jnp.zeros_like(q))], axis=0)
    kc = kc_ref[...].astype(BF16)
    vc = vc_ref[...].astype(BF16)
    s_c = _dot(qs, kc)
    s_n = _dot_nt(qs, kn_ref[...])
    row = lax.broadcasted_iota(jnp.int32, (t, t), 0)
    col = lax.broadcasted_iota(jnp.int32, (t, t), 1)
    if fox:
        cq_ref, ck_ref, o_ref = rest
    else:
        bias_ref, o_ref = rest
    pcs, pns = [], []
    for par in range(2):
        h = 2 * p + par
        sc = s_c[par * t:(par + 1) * t]
        sn = s_n[par * t:(par + 1) * t]
        if fox:
            cq_col = jnp.sum(jnp.where(lane == h, cq_ref[...], 0.0), axis=1, keepdims=True)
            ck = ck_ref[pl.ds(h, 1), :]
            uc = sc + (cq_col - ck[:, 0:past])
            un = jnp.where(col <= row, sn + (cq_col - ck[:, past:past + t]), NEG_INF)
        else:
            bias = bias_ref[h]
            uc = sc + bias[:, 0:past]
            un = sn + bias[:, past:past + t]
        m = jnp.maximum(jnp.max(uc, axis=1, keepdims=True), jnp.max(un, axis=1, keepdims=True))
        pc = jnp.exp2(uc - m)
        pn = jnp.exp2(un - m)
        inv = 1.0 / (jnp.sum(pc, axis=1, keepdims=True) + jnp.sum(pn, axis=1, keepdims=True))
        pcs.append((pc * inv).astype(BF16))
        pns.append((pn * inv).astype(BF16))
    o2 = _dot_nt(jnp.concatenate(pcs, axis=0), vc) + _dot(jnp.concatenate(pns, axis=0), vn_ref[...])
    o_ref[...] = jnp.where(lane < HEAD_DIM, o2[0:t], o2[t:2 * t]).astype(o_ref.dtype)


def _sample_attn(q, kc_t, vc_t, k_new, v_new, *extra, fox):
    b, t, _ = q.shape
    past = kc_t.shape[3]
    pair_rows = pl.BlockSpec((None, t, PAIR), lambda bi, p: (bi, 0, p))
    cache = pl.BlockSpec((None, None, PAIR, past), lambda bi, p: (bi, p, 0, 0))
    if fox:
        cq, ck = extra
        extra_specs = [pl.BlockSpec((None, t, LANES), lambda bi, p: (bi, 0, 0)),
                       pl.BlockSpec((None, None, N_HEADS, ck.shape[3]), lambda bi, p: (bi, 0, 0, 0))]
    else:
        extra_specs = [pl.BlockSpec(extra[0].shape, lambda bi, p: (0, 0, 0))]
    return pl.pallas_call(
        functools.partial(_sample_attn_body, fox=fox),
        grid=(b, N_PAIRS),
        in_specs=[pair_rows, cache, cache, pair_rows, pair_rows] + extra_specs,
        out_specs=pair_rows,
        out_shape=jax.ShapeDtypeStruct((b, t, WIDTH), BF16),
        compiler_params=_cparams(2),
        name="sample_fox" if fox else "sample_band",
    )(q, kc_t, vc_t, k_new, v_new, *extra)


def _first_index(is_max, idx, axis, big):
    return jnp.min(jnp.where(is_max, idx, big), axis=axis, keepdims=True)


def _route(scores, choice):
    t = scores.shape[1]
    c3 = choice.reshape(N_GROUPS, GROUP_SIZE, t)
    j_idx = lax.broadcasted_iota(jnp.int32, c3.shape, 1)
    top1 = jnp.max(c3, axis=1, keepdims=True)
    first = _first_index(c3 == top1, j_idx, 1, GROUP_SIZE)
    top2 = jnp.max(jnp.where(j_idx == first, -jnp.inf, c3), axis=1, keepdims=True)
    gscore = (top1 + top2).reshape(N_GROUPS, t)

    g_idx = lax.broadcasted_iota(jnp.int32, gscore.shape, 0)
    gsel = jnp.zeros(gscore.shape, F32)
    work = gscore
    for _ in range(TOPK_GROUPS):
        gm = jnp.max(work, axis=0, keepdims=True)
        pick = g_idx == _first_index(work == gm, g_idx, 0, N_GROUPS)
        gsel = jnp.where(pick, 1.0, gsel)
        work = jnp.where(pick, -jnp.inf, work)

    emask = jnp.broadcast_to(gsel.reshape(N_GROUPS, 1, t), c3.shape) > 0.0
    work = jnp.where(emask, c3, NEG_INF)
    e_idx = lax.broadcasted_iota(jnp.int32, c3.shape, 0) * GROUP_SIZE + j_idx
    esel = jnp.zeros(c3.shape, F32)
    for _ in range(TOP_K):
        em = jnp.max(jnp.max(work, axis=1, keepdims=True), axis=0, keepdims=True)
        cand = jnp.where(work == em, e_idx, N_EXPERTS)
        first = jnp.min(jnp.min(cand, axis=1, keepdims=True), axis=0, keepdims=True)
        pick = e_idx == first
        esel = jnp.where(pick, 1.0, esel)
        work = jnp.where(pick, -jnp.inf, work)

    w = esel * scores.reshape(c3.shape)
    denom = jnp.sum(jnp.sum(w, axis=1, keepdims=True), axis=0, keepdims=True)
    return (w / denom * ROUTED_SCALE).reshape(N_EXPERTS, t)


def _merge_body(of_ref, ob_ref, x_ref, gate_ref, sh_ref, sc_ref, ogf_ref, ogb_ref, wo_ref, g2_ref,
                wrh_ref, wrl_ref, br_ref, x1_ref, h2_ref, comb_ref, combt_ref, cnt_ref):
    def group_norm(t_ref, gain_ref):
        t = t_ref[...].astype(F32)
        ms = jnp.mean(t * t, axis=-1, keepdims=True)
        return (t * lax.rsqrt(ms + EPS) * gain_ref[...]).astype(BF16)

    y = _dot(group_norm(of_ref, ogf_ref), wo_ref[0:WIDTH, :]) + _dot(group_norm(ob_ref, ogb_ref), wo_ref[WIDTH:, :])
    x = x_ref[...]
    g, r, d = x.shape
    x1 = x + gate_ref[...] * y.reshape(g, r, d)
    x1_ref[...] = x1
    ms = jnp.mean(x1 * x1, axis=-1, keepdims=True)
    h2 = (x1 * lax.rsqrt(ms + EPS) * g2_ref[...] * (1.0 + sc_ref[...]) + sh_ref[...]).reshape(g * r, d)
    h_hi, h_lo = _split2(h2)
    h2_ref[...] = h_hi
    logits = _dot_nt(wrh_ref[...], h_hi) + _dot_nt(wrh_ref[...], h_lo) + _dot_nt(wrl_ref[...], h_hi)
    scores = jax.nn.sigmoid(logits)
    t = scores.shape[1]
    bias = jnp.concatenate([br_ref[...]] * (t // LANES), axis=1)
    comb = _route(scores, scores + bias)
    comb_pad = jnp.concatenate([comb, jnp.zeros((LANES - N_EXPERTS, t), F32)], axis=0)
    comb_ref[...] = comb_pad.T
    combt_ref[...] = comb
    picked = jnp.where(comb > 0.0, 1.0, 0.0)
    for sub in range(cnt_ref.shape[0]):
        cnt = jnp.sum(picked[:, sub * MOE_TM:(sub + 1) * MOE_TM], axis=1, keepdims=True)
        cnt_ref[sub] = jnp.broadcast_to(cnt, (N_EXPERTS, LANES))


def _merge(of, ob, x, gate, shift, scale, ogf, ogb, wo, g2, wr_hi, wr_lo, br, *, G, R):
    nb, s, d = x.shape
    n = nb * s
    tm = G * R
    nbi, nsi = nb // G, s // R
    row = lambda b, i: (b * nsi + i, 0)
    const = lambda b, i: (0, 0)
    mod_spec = pl.BlockSpec((G, 1, d), lambda b, i: (b, 0, 0))
    x_spec = pl.BlockSpec((G, R, d), lambda b, i: (b, i, 0))
    return pl.pallas_call(
        _merge_body,
        grid=(nbi, nsi),
        in_specs=[pl.BlockSpec((tm, WIDTH), row), pl.BlockSpec((tm, WIDTH), row), x_spec,
                  mod_spec, mod_spec, mod_spec,
                  pl.BlockSpec((1, WIDTH), const), pl.BlockSpec((1, WIDTH), const),
                  pl.BlockSpec(wo.shape, const), pl.BlockSpec((1, d), const),
                  pl.BlockSpec(wr_hi.shape, const), pl.BlockSpec(wr_lo.shape, const),
                  pl.BlockSpec(br.shape, const)],
        out_specs=[x_spec, pl.BlockSpec((tm, d), row), pl.BlockSpec((tm, LANES), row),
                   pl.BlockSpec((N_EXPERTS, tm), lambda b, i: (0, b * nsi + i)),
                   pl.BlockSpec((tm // MOE_TM, N_EXPERTS, LANES), lambda b, i: (b * nsi + i, 0, 0))],
        out_shape=[jax.ShapeDtypeStruct((nb, s, d), F32), jax.ShapeDtypeStruct((n, d), BF16),
                   jax.ShapeDtypeStruct((n, LANES), F32), jax.ShapeDtypeStruct((N_EXPERTS, n), F32),
                   jax.ShapeDtypeStruct((n // MOE_TM, N_EXPERTS, LANES), F32)],
        compiler_params=_cparams(2),
        name="merge",
    )(of, ob, x, gate, shift, scale, ogf, ogb, wo, g2, wr_hi, wr_lo, br)


def _silu(g):
    return g * jax.nn.sigmoid(g)


def _moe_body(h_ref, comb_ref, x1_ref, gate_ref, wg_ref, wu_ref, wd_ref, sg_ref, su_ref, sd_ref, y_ref, acc_ref):
    e = pl.program_id(2)
    hb = h_ref[...]

    @pl.when(e == 0)
    def _():
        a = _silu(_dot(hb, sg_ref[...])) * _dot(hb, su_ref[...])
        acc_ref[...] = _dot(a.astype(BF16), sd_ref[...])

    comb = comb_ref[...]
    lane = lax.broadcasted_iota(jnp.int32, comb.shape, 1)
    c_e = jnp.sum(jnp.where(lane == e, comb, 0.0), axis=1, keepdims=True)
    a = _silu(_dot(hb, wg_ref[...])) * _dot(hb, wu_ref[...]) * c_e
    acc_ref[...] += _dot(a.astype(BF16), wd_ref[...])

    @pl.when(e == pl.num_programs(2) - 1)
    def _():
        x1 = x1_ref[...]
        g, r, d = x1.shape
        y_ref[...] = x1 + gate_ref[...] * acc_ref[...].reshape(g, r, d)


def _moe(h2, comb, x1, gate, wg, wu, wd, sg, su, sd, *, G, R):
    nb, s, d = x1.shape
    tm = G * R
    nbi, nsi = nb // G, s // R
    ff = wg.shape[2]
    row = lambda b, i, e: (b * nsi + i, 0)
    const = lambda b, i, e: (0, 0)
    x_spec = pl.BlockSpec((G, R, d), lambda b, i, e: (b, i, 0))
    return pl.pallas_call(
        _moe_body,
        grid=(nbi, nsi, N_EXPERTS),
        in_specs=[pl.BlockSpec((tm, d), row), pl.BlockSpec((tm, LANES), row), x_spec,
                  pl.BlockSpec((G, 1, d), lambda b, i, e: (b, 0, 0)),
                  pl.BlockSpec((None, d, ff), lambda b, i, e: (e, 0, 0)),
                  pl.BlockSpec((None, d, ff), lambda b, i, e: (e, 0, 0)),
                  pl.BlockSpec((None, ff, d), lambda b, i, e: (e, 0, 0)),
                  pl.BlockSpec(sg.shape, const), pl.BlockSpec(su.shape, const), pl.BlockSpec(sd.shape, const)],
        out_specs=x_spec,
        out_shape=jax.ShapeDtypeStruct((nb, s, d), F32),
        scratch_shapes=[pltpu.VMEM((tm, d), F32)],
        compiler_params=_cparams(3),
        name="moe",
    )(h2, comb, x1, gate, wg, wu, wd, sg, su, sd)


MOE_TM = 256
MOE_CH = 16
MOE_SLOTS = TOP_K * MOE_TM + N_EXPERTS * MOE_CH
MOE_NCHUNK = MOE_SLOTS // MOE_CH
MOE_PIECE = 512
MOE_NPIECE = MOE_SLOTS // MOE_PIECE
MOE_CPP = MOE_PIECE // MOE_CH
MOE_RB = 1024
TAU_RADIX = 64.0
assert MOE_SLOTS % MOE_PIECE == 0


def _moe_plan(cnt, n_tiles):
    pc = (cnt + MOE_CH - 1) // MOE_CH * MOE_CH
    off = jnp.cumsum(pc, axis=1) - pc
    end = off + pc
    n_used = (jnp.sum(pc, axis=1) // MOE_CH).astype(jnp.int32)
    tot = jnp.sum(pc, axis=0)
    reg = (tot + MOE_RB - 1) // MOE_RB * MOE_RB
    reg_end = jnp.cumsum(reg)
    reg_start = reg_end - reg
    dest_base = reg_start[None, :] + jnp.cumsum(pc, axis=0) - pc
    chunk_row = jnp.arange(MOE_NCHUNK, dtype=jnp.int32)[None, :] * MOE_CH
    in_group = (chunk_row[:, :, None] >= off[:, None, :]) & (chunk_row[:, :, None] < end[:, None, :])
    used = jnp.any(in_group, axis=2)
    cdest = chunk_row + jnp.sum(jnp.where(in_group, (dest_base - off)[:, None, :], 0), axis=2)
    worst_rows = TOP_K * MOE_TM * n_tiles + n_tiles * N_EXPERTS * (MOE_CH - 1) + N_EXPERTS * (MOE_RB - MOE_CH)
    r_max = -(-worst_rows // MOE_RB)
    cdest_d = jnp.where(used, cdest, r_max * MOE_RB + chunk_row).astype(jnp.int32)
    cdest_c = jnp.where(used, cdest, chunk_row).astype(jnp.int32)
    n_active = (reg_end[-1] // MOE_RB).astype(jnp.int32).reshape(1)
    tile_row = jnp.arange(r_max, dtype=jnp.int32) * MOE_RB
    tile_expert = jnp.minimum(jnp.sum((tile_row[:, None] >= reg_end[None, :]).astype(jnp.int32), axis=1),
                              N_EXPERTS - 1).astype(jnp.int32)
    in_region = (tile_row[:, None] >= reg_start[None, :]) & (tile_row[:, None] < reg_end[None, :])
    rows_end = jnp.sum(jnp.where(in_region, (reg_start + tot)[None, :], 0), axis=1)
    valid = jnp.clip(rows_end - tile_row, 0, MOE_RB).astype(jnp.int32)
    f = lambda a: a.astype(F32)
    zeros64 = jnp.zeros((n_tiles, N_EXPERTS), F32)
    row2 = lambda a: jnp.broadcast_to(jnp.concatenate([f(a), f(a)], axis=1)[:, None, :], (n_tiles, 8, LANES))
    col = lambda a: jnp.broadcast_to(f(a)[:, :, None], (n_tiles, N_EXPERTS, LANES))
    col128 = lambda a: jnp.broadcast_to(jnp.concatenate([f(a), zeros64], axis=1)[:, :, None], (n_tiles, LANES, LANES))
    row1 = lambda a: jnp.broadcast_to(jnp.concatenate([f(a), zeros64], axis=1)[:, None, :], (n_tiles, 8, LANES))
    return dict(n_used=n_used, cdest_d=cdest_d.reshape(n_tiles, 1, MOE_NCHUNK),
                cdest_c=cdest_c.reshape(n_tiles, 1, MOE_NCHUNK), r_max=r_max, n_active=n_active,
                tile_expert=tile_expert, valid=valid,
                off_row2=row2(off), end_row2=row2(end), off_col=col(off),
                off_row1=row1(off), off_col128=col128(off), end_col128=col128(end))


def _tau_pieces(sel, tau):
    tau = jnp.where(sel, tau, -1.0)
    hi = jnp.floor(tau * (1.0 / TAU_RADIX)) * TAU_RADIX
    return hi.astype(BF16), (tau - hi).astype(BF16)


def _dispatch_body(nused_ref, cdest_ref, h_ref, combt_ref, offcol_ref, offrow_ref, endrow_ref, sorted_ref,
                   buf_ref, sem):
    t = pl.program_id(0)
    n_used = nused_ref[t]
    tm = h_ref.shape[0]
    sel = combt_ref[...] > 0.0
    rr = lax.broadcasted_iota(jnp.int32, (tm, tm), 0)
    cc = lax.broadcasted_iota(jnp.int32, (tm, tm), 1)
    upper = jnp.where(rr < cc, 1.0, 0.0).astype(BF16)
    rank = _dot(jnp.where(sel, 1.0, 0.0).astype(BF16), upper)
    cols = jnp.concatenate([offcol_ref[...]] * (tm // LANES), axis=1)
    tau_hi, tau_lo = _tau_pieces(sel, cols + rank)
    taucat = jnp.concatenate([tau_hi, tau_lo], axis=0)
    off_row = offrow_ref[0:1, :]
    end_row = endrow_ref[0:1, :]
    hb = h_ref[...]

    def start_piece(piece):
        chunks = range(piece * MOE_CPP, (piece + 1) * MOE_CPP)
        dests = [cdest_ref[0, c] for c in chunks]
        for c, row in zip(chunks, dests):
            dst = sorted_ref.at[pl.ds(pl.multiple_of(row, MOE_CH), MOE_CH), :]
            pltpu.make_async_copy(buf_ref.at[pl.ds(c * MOE_CH, MOE_CH), :], dst, sem).start()

    def wait_piece(piece):
        rows = pl.ds(piece * MOE_PIECE, MOE_PIECE)
        pltpu.make_async_copy(buf_ref.at[rows, :], sorted_ref.at[rows, :], sem).wait()

    for piece in range(MOE_NPIECE):
        @pl.when(piece * MOE_CPP < n_used)
        def _():
            if piece > 0:
                start_piece(piece - 1)
            base = piece * MOE_PIECE
            s_col = (base + lax.broadcasted_iota(jnp.int32, (MOE_PIECE, LANES), 0)).astype(F32)
            onehot = jnp.where((s_col >= off_row) & (s_col < end_row), 1.0, 0.0).astype(BF16)
            q = _dot(onehot, taucat)
            s_mat = (base + lax.broadcasted_iota(jnp.int32, (MOE_PIECE, tm), 0)).astype(F32)
            g = jnp.where(q == s_mat, 1.0, 0.0).astype(BF16)
            buf_ref[pl.ds(base, MOE_PIECE), :] = _dot(g, hb).astype(BF16)

    last = (n_used - 1) // MOE_CPP
    for piece in range(MOE_NPIECE):
        @pl.when(piece == last)
        def _():
            start_piece(piece)

    for piece in range(MOE_NPIECE):
        @pl.when(piece * MOE_CPP < n_used)
        def _():
            wait_piece(piece)


def _dispatch(h2, combt, plan, n_tiles):
    n, d = h2.shape
    r_total = plan["r_max"] * MOE_RB + MOE_SLOTS
    grid_spec = pltpu.PrefetchScalarGridSpec(
        num_scalar_prefetch=1,
        grid=(n_tiles,),
        in_specs=[pl.BlockSpec((None, 1, MOE_NCHUNK), lambda t, nu: (t, 0, 0), memory_space=pltpu.SMEM),
                  pl.BlockSpec((MOE_TM, d), lambda t, nu: (t, 0)),
                  pl.BlockSpec((N_EXPERTS, MOE_TM), lambda t, nu: (0, t)),
                  pl.BlockSpec((None, N_EXPERTS, LANES), lambda t, nu: (t, 0, 0)),
                  pl.BlockSpec((None, 8, LANES), lambda t, nu: (t, 0, 0)),
                  pl.BlockSpec((None, 8, LANES), lambda t, nu: (t, 0, 0))],
        out_specs=pl.BlockSpec(memory_space=pl.ANY),
        scratch_shapes=[pltpu.VMEM((MOE_SLOTS, d), BF16), pltpu.SemaphoreType.DMA(())],
    )
    return pl.pallas_call(
        _dispatch_body,
        grid_spec=grid_spec,
        out_shape=jax.ShapeDtypeStruct((r_total, d), BF16),
        compiler_params=_cparams(1),
        name="moe_dispatch",
    )(plan["n_used"], plan["cdest_d"], h2, combt, plan["off_col"], plan["off_row2"], plan["end_row2"])


def _ffn_body(texp_ref, nact_ref, valid_ref, x_ref, wg_ref, wu_ref, wd_ref, o_ref):
    r = pl.program_id(0)

    @pl.when(r < nact_ref[0])
    def _():
        x = x_ref[...]
        rows = lax.broadcasted_iota(jnp.int32, x.shape, 0)
        x = jnp.where(rows < valid_ref[r], x, jnp.zeros_like(x))
        a = _silu(_dot(x, wg_ref[...])) * _dot(x, wu_ref[...])
        o_ref[...] = _dot(a.astype(BF16), wd_ref[...]).astype(o_ref.dtype)


def _ffn(xs, wg, wu, wd, plan):
    r_total, d = xs.shape
    ff = wg.shape[2]
    last = lambda r, te, na, va: (jnp.minimum(r, na[0] - 1), 0)
    wmap = lambda r, te, na, va: (te[r], 0, 0)
    grid_spec = pltpu.PrefetchScalarGridSpec(
        num_scalar_prefetch=3,
        grid=(plan["r_max"],),
        in_specs=[pl.BlockSpec((MOE_RB, d), last),
                  pl.BlockSpec((None, d, ff), wmap), pl.BlockSpec((None, d, ff), wmap),
                  pl.BlockSpec((None, ff, d), wmap)],
        out_specs=pl.BlockSpec((MOE_RB, d), last),
    )
    return pl.pallas_call(
        _ffn_body,
        grid_spec=grid_spec,
        out_shape=jax.ShapeDtypeStruct((r_total, d), BF16),
        compiler_params=_cparams(1),
        name="moe_ffn",
    )(plan["tile_expert"], plan["n_active"], plan["valid"], xs, wg, wu, wd)


def _combine_body(nused_ref, cdest_ref, h_ref, comb_ref, x1_ref, gate_ref, offrow_ref, offcol_ref, endcol_ref,
                  sg_ref, su_ref, sd_ref, ys_ref, y_ref, buf_ref, acc_ref, sem):
    t = pl.program_id(0)
    n_used = nused_ref[t]
    tm = h_ref.shape[0]

    @pl.when(t == 0)
    def _():
        buf_ref[...] = jnp.zeros_like(buf_ref)

    def wait_piece(piece):
        rows = pl.ds(piece * MOE_PIECE, MOE_PIECE)
        pltpu.make_async_copy(ys_ref.at[rows, :], buf_ref.at[rows, :], sem.at[piece]).wait()

    for piece in range(MOE_NPIECE):
        @pl.when(piece * MOE_CPP < n_used)
        def _():
            chunks = range(piece * MOE_CPP, (piece + 1) * MOE_CPP)
            srcs = [cdest_ref[0, c] for c in chunks]
            for c, row in zip(chunks, srcs):
                src = ys_ref.at[pl.ds(pl.multiple_of(row, MOE_CH), MOE_CH), :]
                pltpu.make_async_copy(src, buf_ref.at[pl.ds(c * MOE_CH, MOE_CH), :], sem.at[piece]).start()

    hb = h_ref[...]
    acc = _dot((_silu(_dot(hb, sg_ref[...])) * _dot(hb, su_ref[...])).astype(BF16), sd_ref[...])
    comb = comb_ref[...]
    sel = comb > 0.0
    rr = lax.broadcasted_iota(jnp.int32, (tm, tm), 0)
    cc = lax.broadcasted_iota(jnp.int32, (tm, tm), 1)
    lower = jnp.where(cc < rr, 1.0, 0.0).astype(BF16)
    rank = _dot(lower, jnp.where(sel, 1.0, 0.0).astype(BF16))
    tau_hi, tau_lo = _tau_pieces(sel, offrow_ref[0:1, :] + rank)
    taucat = jnp.concatenate([tau_hi, tau_lo], axis=1)
    lhs = jnp.concatenate([taucat, jnp.concatenate([comb.astype(BF16), jnp.zeros((tm, LANES), BF16)], axis=1)],
                          axis=0)
    off_col = offcol_ref[...]
    end_col = endcol_ref[...]
    acc_ref[...] = acc

    for piece in range(MOE_NPIECE):
        @pl.when(piece * MOE_CPP < n_used)
        def _():
            base = piece * MOE_PIECE
            s_row = (base + lax.broadcasted_iota(jnp.int32, (LANES, MOE_PIECE), 1)).astype(F32)
            off_b = jnp.concatenate([off_col] * (MOE_PIECE // LANES), axis=1)
            end_b = jnp.concatenate([end_col] * (MOE_PIECE // LANES), axis=1)
            onehot = jnp.where((s_row >= off_b) & (s_row < end_b), 1.0, 0.0).astype(BF16)
            qw = _dot(lhs, jnp.concatenate([onehot, onehot], axis=0))
            s_mat = (base + lax.broadcasted_iota(jnp.int32, (tm, MOE_PIECE), 1)).astype(F32)
            gw = jnp.where(qw[0:tm] == s_mat, qw[tm:2 * tm], 0.0).astype(BF16)
            wait_piece(piece)
            acc_ref[...] += _dot(gw, buf_ref[pl.ds(base, MOE_PIECE), :])
    y_ref[...] = x1_ref[...] + gate_ref[...] * acc_ref[...]


def _combine(h2, comb, x1, gate, ys, sg, su, sd, plan, n_tiles):
    nb, s, d = x1.shape
    n = nb * s
    per_b = s // MOE_TM
    x1f = x1.reshape(n, d)
    const2 = lambda t, nu: (0, 0)
    grid_spec = pltpu.PrefetchScalarGridSpec(
        num_scalar_prefetch=1,
        grid=(n_tiles,),
        in_specs=[pl.BlockSpec((None, 1, MOE_NCHUNK), lambda t, nu: (t, 0, 0), memory_space=pltpu.SMEM),
                  pl.BlockSpec((MOE_TM, d), lambda t, nu: (t, 0)),
                  pl.BlockSpec((MOE_TM, LANES), lambda t, nu: (t, 0)),
                  pl.BlockSpec((MOE_TM, d), lambda t, nu: (t, 0)),
                  pl.BlockSpec((None, 1, d), lambda t, nu: (t // per_b, 0, 0)),
                  pl.BlockSpec((None, 8, LANES), lambda t, nu: (t, 0, 0)),
                  pl.BlockSpec((None, LANES, LANES), lambda t, nu: (t, 0, 0)),
                  pl.BlockSpec((None, LANES, LANES), lambda t, nu: (t, 0, 0)),
                  pl.BlockSpec(sg.shape, const2), pl.BlockSpec(su.shape, const2), pl.BlockSpec(sd.shape, const2),
                  pl.BlockSpec(memory_space=pl.ANY)],
        out_specs=pl.BlockSpec((MOE_TM, d), lambda t, nu: (t, 0)),
        scratch_shapes=[pltpu.VMEM((MOE_SLOTS, d), BF16), pltpu.VMEM((MOE_TM, d), F32),
                        pltpu.SemaphoreType.DMA((MOE_NPIECE,))],
    )
    y = pl.pallas_call(
        _combine_body,
        grid_spec=grid_spec,
        out_shape=jax.ShapeDtypeStruct((n, d), F32),
        compiler_params=_cparams(1),
        name="moe_combine",
    )(plan["n_used"], plan["cdest_c"], h2, comb, x1f, gate, plan["off_row1"], plan["off_col128"], plan["end_col128"],
      sg, su, sd, ys)
    return y.reshape(nb, s, d)


def _moe_sparse(h2, comb, combt, cnt, x1, gate, wg, wu, wd, sg, su, sd):
    n = h2.shape[0]
    n_tiles = n // MOE_TM
    plan = _moe_plan(cnt[:, :, 0].astype(jnp.int32), n_tiles)
    xs = _dispatch(h2, combt, plan, n_tiles)
    ys = _ffn(xs, wg, wu, wd, plan)
    return _combine(h2, comb, x1, gate, ys, sg, su, sd, plan, n_tiles)


def _tile_heads(g, mult=1.0):
    return (jnp.tile(g.astype(F32), N_HEADS) * mult).reshape(1, WIDTH)


def kernel(x_prompt, x_sample, cache_fox_k, cache_fox_v, cache_fox_logf, cache_band_k, cache_band_v, c_prompt, c_sample, w_ada, b_ada, norm1_g, norm2_g, w_in, b_forget, g_q_fox, g_k_fox, g_q_band, g_k_band, rel_bias, out_g_fox, out_g_band, w_out, w_router, b_router, w_gate, w_up, w_down, ws_gate, ws_up, ws_down):
    depth = w_ada.shape[0]
    assert depth == 1
    bsz, seq, d = x_prompt.shape
    dbs, dseq, _ = x_sample.shape
    past = cache_fox_k.shape[2]
    n_cache = cache_band_k.shape[2]
    assert n_cache == BAND_REACH and dseq == CHUNK and seq % BAND_REACH == 0

    wi = w_in[0]
    cols = [wi[:, 0:512], wi[:, 512:1024], wi[:, 1024:1536], wi[:, 1544:2056], wi[:, 2056:2568], wi[:, 2568:3080],
            wi[:, 1536:1544], jnp.zeros((d, LANES - N_HEADS), F32)]
    w_all = jnp.concatenate(cols, axis=1).astype(BF16)
    hd = jnp.arange(WIDTH) // HEAD_DIM
    bd = jnp.where(hd[:, None] == hd[None, :], 1.0 / HEAD_DIM, 0.0).astype(BF16)
    qscale = ATTN_SCALE * LOG2E
    gqf, gkf = _tile_heads(g_q_fox[0], qscale), _tile_heads(g_k_fox[0])
    gqb, gkb = _tile_heads(g_q_band[0], qscale), _tile_heads(g_k_band[0])
    bf_row = jnp.concatenate([b_forget[0], jnp.zeros((LANES - N_HEADS,), F32)]).reshape(1, LANES)
    g1 = norm1_g[0].reshape(1, d)
    g2 = norm2_g[0].reshape(1, d)
    ogf = out_g_fox[0].reshape(1, WIDTH)
    ogb = out_g_band[0].reshape(1, WIDTH)
    wo = w_out[0].astype(BF16)
    wr_t = w_router[0].T
    wr_hi = wr_t.astype(BF16)
    wr_lo = (wr_t - wr_hi.astype(F32)).astype(BF16)
    br = jnp.broadcast_to(b_router[0].reshape(N_EXPERTS, 1), (N_EXPERTS, LANES)).astype(F32)
    wg, wu, wd = w_gate[0].astype(BF16), w_up[0].astype(BF16), w_down[0].astype(BF16)
    sg, su, sd = ws_gate[0].astype(BF16), ws_up[0].astype(BF16), ws_down[0].astype(BF16)

    n_c = bsz + dbs
    rows = -(-n_c // 8) * 8
    c_all = jnp.concatenate([c_prompt, c_sample, jnp.zeros((rows - n_c, d), F32)], axis=0)
    mod = _ada(c_all, w_ada[0], b_ada[0].reshape(1, -1))

    def mods(lo, hi):
        return [mod[lo:hi, j * d:(j + 1) * d].reshape(hi - lo, 1, d) for j in range(6)]

    shift1_p, scale1_p, gate1_p, shift2_p, scale2_p, gate2_p = mods(0, bsz)
    shift1_s, scale1_s, gate1_s, shift2_s, scale2_s, gate2_s = mods(bsz, n_c)

    TM = BAND_REACH
    (qf, kf, vft, kf32, vf32, lf, qb, kb, vbt, kb32, vb32) = _proj(
        x_prompt, shift1_p, scale1_p, g1, w_all, bd, gqf, gkf, gqb, gkb, bf_row, G=1, R=TM, band_last_only=True)
    r3 = lambda a: a.reshape(bsz, seq, a.shape[-1])
    ct, ka = _scan_t(r3(lf), T=TM)
    of = _foxt(r3(qf), r3(kf), ka, vft, ct, T=TM)
    assert TM == BAND_STEP
    ob = _bandt(r3(qb), r3(kb), vbt, _band_bias_tile_t(rel_bias[0]))
    assert TM % MOE_TM == 0
    x1_p, h2_p, comb_p, combt_p, cnt_p = _merge(of.reshape(-1, WIDTH), ob.reshape(-1, WIDTH), x_prompt, gate1_p,
                                                shift2_p, scale2_p, ogf, ogb, wo, g2, wr_hi, wr_lo, br, G=1, R=TM)
    y_p = _moe_sparse(h2_p, comb_p, combt_p, cnt_p, x1_p, gate2_p, wg, wu, wd, sg, su, sd)

    GS = 8
    (qf_s, kf_s, vf_s, kf32_s, vf32_s, lf_s, qb_s, kb_s, vb_s, kb32_s, vb32_s) = _proj(
        x_sample, shift1_s, scale1_s, g1, w_all, bd, gqf, gkf, gqb, gkb, bf_row, G=GS, R=dseq,
        band_last_only=False)
    s3 = lambda a: a.reshape(dbs, dseq, a.shape[-1])
    sk = past + dseq
    skp = -(-sk // LANES) * LANES
    pad_k = skp - sk
    n_seq = dbs * N_HEADS
    assert n_seq <= LANES
    lf_seq = jnp.concatenate([cache_fox_logf[0], s3(lf_s)[:, :, :N_HEADS]], axis=1)
    lf_seq = jnp.swapaxes(lf_seq, 0, 1).reshape(sk, n_seq)
    lf_seq = jnp.pad(lf_seq, ((0, pad_k), (0, LANES - n_seq)))
    cum_col, cum_row = _scan(lf_seq)
    cq_s = jnp.swapaxes(cum_col[past:past + dseq, :n_seq].reshape(dseq, dbs, N_HEADS), 0, 1)
    cum_s = jnp.pad(cq_s, ((0, 0), (0, 0), (0, LANES - N_HEADS)))
    cumt_s = cum_row[:n_seq].reshape(dbs, 1, N_HEADS, skp)

    def cache_t(c):
        return jnp.transpose(c, (0, 2, 3, 1)).reshape(dbs, N_PAIRS, PAIR, c.shape[1])

    of_s = _sample_attn(s3(qf_s), cache_t(cache_fox_k[0]), cache_t(cache_fox_v[0]), s3(kf_s), s3(vf_s),
                        cum_s, cumt_s, fox=True)
    bias_s = _band_bias_tile(rel_bias[0], dseq, BAND_REACH + LANES)
    ob_s = _sample_attn(s3(qb_s), cache_t(cache_band_k[0]), cache_t(cache_band_v[0]), s3(kb_s), s3(vb_s),
                        bias_s, fox=False)
    x1_s, h2_s, comb_s, _, _ = _merge(of_s.reshape(-1, WIDTH), ob_s.reshape(-1, WIDTH), x_sample, gate1_s, shift2_s,
                                      scale2_s, ogf, ogb, wo, g2, wr_hi, wr_lo, br, G=GS, R=dseq)
    y_s = _moe(h2_s, comb_s, x1_s, gate2_s, wg, wu, wd, sg, su, sd, G=dbs, R=dseq)

    hshape = (N_HEADS, HEAD_DIM)
    new_bk_s = jnp.concatenate([cache_band_k[0], s3(kb32_s).reshape(dbs, dseq, *hshape)], axis=1)[:, -n_cache:]
    new_bv_s = jnp.concatenate([cache_band_v[0], s3(vb32_s).reshape(dbs, dseq, *hshape)], axis=1)[:, -n_cache:]
    return (y_p, y_s,
            kf32.reshape(1, bsz, seq, *hshape), vf32.reshape(1, bsz, seq, *hshape),
            lf[:, :N_HEADS].reshape(1, bsz, seq, N_HEADS),
            kb32.reshape(1, bsz, BAND_REACH, *hshape), vb32.reshape(1, bsz, BAND_REACH, *hshape),
            kf32_s.reshape(1, dbs, dseq, *hshape), vf32_s.reshape(1, dbs, dseq, *hshape),
            lf_s[:, :N_HEADS].reshape(1, dbs, dseq, N_HEADS),
            new_bk_s[None], new_bv_s[None])
```

```python
import functools

import jax
import jax.numpy as jnp
import numpy as np
from jax import lax
from jax.experimental import pallas as pl
from jax.experimental.pallas import tpu as pltpu

F32 = jnp.float32
BF16 = jnp.bfloat16

HEAD_DIM = 64
N_HEADS = 8
WIDTH = N_HEADS * HEAD_DIM
PAIR = 2 * HEAD_DIM
N_PAIRS = N_HEADS // 2
LANES = 128
CHUNK = 64
BAND_REACH = 512
REL_CLIP = 256
N_EXPERTS = 64
N_GROUPS = 8
GROUP_SIZE = N_EXPERTS // N_GROUPS
TOPK_GROUPS = 4
TOP_K = 8
ROUTED_SCALE = 2.5
EPS = 1e-6
NEG_INF = -1e30
ATTN_SCALE = HEAD_DIM ** -0.5
LOG2E = 1.4426950408889634
VMEM_LIMIT = 56 * 1024 * 1024


def _cparams(n_axes):
    return pltpu.CompilerParams(dimension_semantics=("arbitrary",) * n_axes,
                                vmem_limit_bytes=VMEM_LIMIT)


def _dot(a, b):
    return jnp.dot(a, b, preferred_element_type=F32)


def _dot_nt(a, b):
    return lax.dot_general(a, b, (((1,), (1,)), ((), ())), preferred_element_type=F32)


def _split2(a):
    hi = a.astype(BF16)
    lo = (a - hi.astype(F32)).astype(BF16)
    return hi, lo


def _split3(a):
    hi = a.astype(BF16)
    r = a - hi.astype(F32)
    mid = r.astype(BF16)
    lo = (r - mid.astype(F32)).astype(BF16)
    return hi, mid, lo


def _ada_body(c_ref, w_ref, b_ref, o_ref):
    c = c_ref[...]
    a = c * jax.nn.sigmoid(c)
    a_hi, a_lo = _split2(a)
    w_hi, w_lo = _split2(w_ref[...])
    o_ref[...] = _dot(a_hi, w_hi) + _dot(a_hi, w_lo) + _dot(a_lo, w_hi) + b_ref[...]


def _ada(c_all, w_ada, b_ada):
    rows, d = c_all.shape
    n = w_ada.shape[1]
    tn = 1024
    return pl.pallas_call(
        _ada_body,
        grid=(n // tn,),
        in_specs=[pl.BlockSpec((rows, d), lambda j: (0, 0)),
                  pl.BlockSpec((d, tn), lambda j: (0, j)),
                  pl.BlockSpec((1, tn), lambda j: (0, j))],
        out_specs=pl.BlockSpec((rows, tn), lambda j: (0, j)),
        out_shape=jax.ShapeDtypeStruct((rows, n), F32),
        compiler_params=_cparams(1),
        name="ada",
    )(c_all, w_ada, b_ada)


def _log_sigmoid(z):
    return jnp.minimum(z, 0.0) - jnp.log(1.0 + jnp.exp(-jnp.abs(z)))


def _proj_body(x_ref, sh_ref, sc_ref, g1_ref, w_ref, bd_ref, gqf_ref, gkf_ref, gqb_ref, gkb_ref, bf_ref,
               qf_ref, kf_ref, vf_ref, kf32_ref, vf32_ref, lf_ref, qb_ref, kb_ref, vb_ref, kb32_ref, vb32_ref,
               *, band_last_only):
    x = x_ref[...]
    g, r, d = x.shape
    ms = jnp.mean(x * x, axis=-1, keepdims=True)
    h = x * lax.rsqrt(ms + EPS) * g1_ref[...] * (1.0 + sc_ref[...]) + sh_ref[...]
    hb = h.reshape(g * r, d).astype(BF16)

    def seg(i):
        return _dot(hb, w_ref[:, i * WIDTH:(i + 1) * WIDTH])

    def head_norm(t, gain_ref):
        ssq = _dot((t * t).astype(BF16), bd_ref[...])
        return t * lax.rsqrt(ssq + EPS) * gain_ref[...]

    qf_ref[...] = head_norm(seg(0), gqf_ref).astype(BF16)
    kf = head_norm(seg(1), gkf_ref)
    kf32_ref[...] = kf
    kf_ref[...] = kf.astype(BF16)
    vf = seg(2)
    vf32_ref[...] = vf
    if band_last_only:
        vf_ref[...] = vf.T.astype(BF16)
    else:
        vf_ref[...] = vf.astype(BF16)
    z = _dot(hb, w_ref[:, 6 * WIDTH:6 * WIDTH + LANES]) + bf_ref[...]
    lf_ref[...] = _log_sigmoid(z)
    qb_ref[...] = head_norm(seg(3), gqb_ref).astype(BF16)
    kb = head_norm(seg(4), gkb_ref)
    kb_ref[...] = kb.astype(BF16)
    vb = seg(5)
    if band_last_only:
        vb_ref[...] = vb.T.astype(BF16)
    else:
        vb_ref[...] = vb.astype(BF16)

    if band_last_only:
        @pl.when(pl.program_id(1) == pl.num_programs(1) - 1)
        def _():
            kb32_ref[...] = kb
            vb32_ref[...] = vb
    else:
        kb32_ref[...] = kb
        vb32_ref[...] = vb


def _proj(x, shift, scale, g1, w_all, bd, gqf, gkf, gqb, gkb, bf_row, *, G, R, band_last_only):
    nb, s, d = x.shape
    n = nb * s
    tm = G * R
    nbi, nsi = nb // G, s // R
    grid = (nbi, nsi)
    row = lambda b, i: (b * nsi + i, 0)
    const = lambda b, i: (0, 0)
    mod_spec = pl.BlockSpec((G, 1, d), lambda b, i: (b, 0, 0))
    out_bf = jax.ShapeDtypeStruct((n, WIDTH), BF16)
    out_f32 = jax.ShapeDtypeStruct((n, WIDTH), F32)
    tile = pl.BlockSpec((tm, WIDTH), row)
    if band_last_only:
        assert G == 1 and R == BAND_REACH
        band_shape = jax.ShapeDtypeStruct((nb, BAND_REACH, WIDTH), F32)
        band_spec = pl.BlockSpec((None, BAND_REACH, WIDTH), lambda b, i: (b, 0, 0))
        v_shape = jax.ShapeDtypeStruct((nb, nsi, WIDTH, tm), BF16)
        v_spec = pl.BlockSpec((None, None, WIDTH, tm), lambda b, i: (b, i, 0, 0))
    else:
        band_shape, band_spec = out_f32, tile
        v_shape, v_spec = out_bf, tile
    return pl.pallas_call(
        functools.partial(_proj_body, band_last_only=band_last_only),
        grid=grid,
        in_specs=[pl.BlockSpec((G, R, d), lambda b, i: (b, i, 0)), mod_spec, mod_spec,
                  pl.BlockSpec((1, d), const), pl.BlockSpec(w_all.shape, const), pl.BlockSpec(bd.shape, const),
                  pl.BlockSpec((1, WIDTH), const), pl.BlockSpec((1, WIDTH), const),
                  pl.BlockSpec((1, WIDTH), const), pl.BlockSpec((1, WIDTH), const),
                  pl.BlockSpec((1, LANES), const)],
        out_specs=[tile, tile, v_spec, tile, tile, pl.BlockSpec((tm, LANES), row), tile, tile, v_spec,
                   band_spec, band_spec],
        out_shape=[out_bf, out_bf, v_shape, out_f32, out_f32, jax.ShapeDtypeStruct((n, LANES), F32),
                   out_bf, out_bf, v_shape, band_shape, band_shape],
        compiler_params=_cparams(2),
        name="proj",
    )(x, shift, scale, g1, w_all, bd, gqf, gkf, gqb, gkb, bf_row)


def _scan_body(lf_ref, cum_ref, cumt_ref):
    lf = lf_ref[...]
    s = lf.shape[0]
    hi, mid, lo = _split3(lf)
    rr = lax.broadcasted_iota(jnp.int32, (s, s), 0)
    cc = lax.broadcasted_iota(jnp.int32, (s, s), 1)
    tri = jnp.where(cc <= rr, 1.0, 0.0).astype(BF16)
    cum2 = (_dot(tri, hi) + _dot(tri, mid) + _dot(tri, lo)) * LOG2E
    cum_ref[...] = cum2
    cumt_ref[...] = cum2.T


def _scan(lf):
    s, _ = lf.shape
    return pl.pallas_call(
        _scan_body,
        out_shape=[jax.ShapeDtypeStruct((s, LANES), F32), jax.ShapeDtypeStruct((LANES, s), F32)],
        compiler_params=pltpu.CompilerParams(vmem_limit_bytes=VMEM_LIMIT),
        name="scan",
    )(lf)


def _fox_body(q_ref, k_ref, v_ref, cq_ref, ck_ref, o_ref, *, TQ, TK, q_off):
    p = pl.program_id(1)
    i = pl.program_id(2)
    q = q_ref[...]
    cq_blk = cq_ref[...]
    lane = lax.broadcasted_iota(jnp.int32, (TQ, PAIR), 1)
    ones_blk = jnp.where(lax.broadcasted_iota(jnp.int32, (TK, LANES), 1) == 0, 1.0, 0.0).astype(BF16)
    q0 = q_off + i * TQ
    n_full = q0 // TK
    qpos = q0 + lax.broadcasted_iota(jnp.int32, (TQ, TK), 0)
    kcol = lax.broadcasted_iota(jnp.int32, (TQ, TK), 1)

    outs = []
    for par in range(2):
        h = 2 * p + par
        qm = jnp.where((lane >= HEAD_DIM) == (par == 1), q, jnp.zeros_like(q))
        cq_col = jnp.sum(jnp.where(lane == h, cq_blk, 0.0), axis=1, keepdims=True)
        ref0 = cq_col[0:1, :]
        cqr = cq_col - ref0

        def step(j, carry, masked):
            m, l, acc = carry
            k0 = pl.multiple_of(j * TK, TK)
            kb = k_ref[pl.ds(k0, TK), :]
            vb = v_ref[pl.ds(k0, TK), :]
            s = _dot_nt(qm, kb)
            ck = ck_ref[j, pl.ds(h, 1), :]
            u = s - (ck - ref0)
            if masked:
                u = jnp.where(k0 + kcol <= qpos, u, NEG_INF)
            m_new = jnp.maximum(m, jnp.max(u, axis=1, keepdims=True) + cqr)
            alpha = jnp.exp2(m - m_new)
            pexp = jnp.exp2(u + (cqr - m_new))
            pv = _dot(pexp.astype(BF16), jnp.concatenate([vb, ones_blk], axis=1))
            return m_new, alpha * l + pv[:, LANES:LANES + 1], alpha * acc + pv[:, :LANES]

        init = (jnp.full((TQ, 1), NEG_INF, F32), jnp.zeros((TQ, 1), F32), jnp.zeros((TQ, LANES), F32))
        carry = lax.fori_loop(0, n_full, lambda j, c: step(j, c, False), init)
        _, l, acc = step(n_full, carry, True)
        outs.append(acc / l)
    o_ref[...] = jnp.where(lane < HEAD_DIM, outs[0], outs[1]).astype(o_ref.dtype)


def _fox(q, k, v, cum, cumt, *, TQ, TK, q_off):
    b, sq, _ = q.shape
    sk = k.shape[1]
    return pl.pallas_call(
        functools.partial(_fox_body, TQ=TQ, TK=TK, q_off=q_off),
        grid=(b, N_PAIRS, sq // TQ),
        in_specs=[pl.BlockSpec((None, TQ, PAIR), lambda bi, p, i: (bi, i, p)),
                  pl.BlockSpec((None, sk, PAIR), lambda bi, p, i: (bi, 0, p)),
                  pl.BlockSpec((None, sk, PAIR), lambda bi, p, i: (bi, 0, p)),
                  pl.BlockSpec((None, TQ, LANES), lambda bi, p, i: (bi, i, 0)),
                  pl.BlockSpec((None, sk // TK, N_HEADS, TK), lambda bi, p, i: (bi, 0, 0, 0))],
        out_specs=pl.BlockSpec((None, TQ, PAIR), lambda bi, p, i: (bi, i, p)),
        out_shape=jax.ShapeDtypeStruct((b, sq, WIDTH), BF16),
        compiler_params=_cparams(3),
        name="fox",
    )(q, k, v, cum, cumt)


AUG_PIECES = 3
FOX_UNDERFLOW = 160.0
FOX_NORM_SLACK = 1.02
FOX_BOUND_SLACK = 2.0


def _scan_t_body(lf_ref, place_ref, ct_ref, ka_ref, carry_ref):
    @pl.when(pl.program_id(1) == 0)
    def _():
        carry_ref[...] = jnp.zeros_like(carry_ref)

    lf = lf_ref[...]
    ts = lf.shape[0]
    lane = lax.broadcasted_iota(jnp.int32, lf.shape, 1)
    lf = jnp.where(lane < N_HEADS, lf, 0.0)
    hi, mid, lo = _split3(lf)
    rr = lax.broadcasted_iota(jnp.int32, (ts, ts), 0)
    cc = lax.broadcasted_iota(jnp.int32, (ts, ts), 1)
    tri = jnp.where(cc <= rr, 1.0, 0.0).astype(BF16)
    cum = _dot(tri, hi) + _dot(tri, mid) + _dot(tri, lo) + carry_ref[0:1, :]
    carry_ref[...] = jnp.broadcast_to(cum[ts - 1:ts, :], carry_ref.shape)
    cum2 = cum * LOG2E
    ct_ref[...] = cum2.T[0:N_HEADS, :]
    pieces = _split3(cum2 - cum2[0:1, :])
    ka = _dot(pieces[0], place_ref[0]) + _dot(pieces[1], place_ref[1]) + _dot(pieces[2], place_ref[2])
    ka_ref[...] = ka.astype(BF16)


def _aug_placement():
    h = jnp.arange(LANES)[:, None]
    col = jnp.arange(WIDTH)[None, :]
    mats = []
    for x in range(AUG_PIECES):
        tgt = PAIR * (h // 2) + AUG_PIECES * (h % 2) + x
        mats.append(jnp.where((h < N_HEADS) & (col == tgt), 1.0, 0.0))
    return jnp.stack(mats).astype(BF16)


def _scan_t(lf, *, T):
    b, s, _ = lf.shape
    place = _aug_placement()
    return pl.pallas_call(
        _scan_t_body,
        grid=(b, s // T),
        in_specs=[pl.BlockSpec((None, T, LANES), lambda bi, i: (bi, i, 0)),
                  pl.BlockSpec(place.shape, lambda bi, i: (0, 0, 0))],
        out_specs=[pl.BlockSpec((None, None, N_HEADS, T), lambda bi, i: (bi, i, 0, 0)),
                   pl.BlockSpec((None, T, WIDTH), lambda bi, i: (bi, i, 0))],
        out_shape=[jax.ShapeDtypeStruct((b, s // T, N_HEADS, T), F32),
                   jax.ShapeDtypeStruct((b, s, WIDTH), BF16)],
        scratch_shapes=[pltpu.VMEM((8, LANES), F32)],
        compiler_params=_cparams(2),
        name="scan_t",
    )(lf, place)


def _foxt_body(q_ref, k_ref, ka_ref, vt_ref, ct_ref, o_ref, kn_ref, *, T):
    p = pl.program_id(1)
    i = pl.program_id(2)
    q = q_ref[...]
    lane = lax.broadcasted_iota(jnp.int32, (T, PAIR), 1)
    halves = []
    for par in range(2):
        qm = jnp.where((lane >= HEAD_DIM) == (par == 1), q, jnp.zeros_like(q))
        lo_lane = AUG_PIECES * par
        qa = jnp.where((lane >= lo_lane) & (lane < lo_lane + AUG_PIECES), -1.0, 0.0).astype(BF16)
        halves.append(jnp.concatenate([qm, qa], axis=1))
    qcat = jnp.concatenate(halves, axis=0)
    h_even = 2 * p
    cq = jnp.concatenate([ct_ref[i, pl.ds(h_even, 1), :], ct_ref[i, pl.ds(h_even + 1, 1), :]], axis=1)
    ones_rows = jnp.ones((16, T), BF16)
    krow = lax.broadcasted_iota(jnp.int32, (T, 2 * T), 0)
    qcol = lax.broadcasted_iota(jnp.int32, (T, 2 * T), 1) % T

    def step(j, carry, masked):
        m, acc_e, acc_o = carry
        k0 = pl.multiple_of(j * T, T)
        kcat = jnp.concatenate([k_ref[pl.ds(k0, T), :], ka_ref[pl.ds(k0, T), :]], axis=1)
        st = _dot_nt(kcat, qcat)
        c0 = jnp.concatenate([jnp.broadcast_to(ct_ref[j, pl.ds(h_even, 1), :][:, 0:1], (1, T)),
                              jnp.broadcast_to(ct_ref[j, pl.ds(h_even + 1, 1), :][:, 0:1], (1, T))], axis=1)
        rb = cq - c0
        if masked:
            st = jnp.where(krow <= qcol, st, NEG_INF)
        m_new = jnp.maximum(m, jnp.max(st, axis=0, keepdims=True) + rb)
        alpha = jnp.exp2(m - m_new)
        pt = jnp.exp2(st + (rb - m_new)).astype(BF16)
        vt = vt_ref[j]
        pv_e = _dot(jnp.concatenate([vt[0:HEAD_DIM], ones_rows], axis=0), pt[:, 0:T])
        pv_o = _dot(jnp.concatenate([vt[HEAD_DIM:PAIR], ones_rows], axis=0), pt[:, T:2 * T])
        return m_new, alpha[:, 0:T] * acc_e + pv_e, alpha[:, T:2 * T] * acc_o + pv_o

    @pl.when(i == 0)
    def _():
        ones = jnp.ones((PAIR, LANES), BF16)
        kmax = jnp.zeros((1, LANES), F32)
        for c in range(k_ref.shape[0] // T):
            kc = k_ref[c * T:(c + 1) * T, :].astype(F32)
            kmax = jnp.maximum(kmax, jnp.max(_dot((kc * kc).astype(BF16), ones), axis=0, keepdims=True))
        kn_ref[...] = jnp.broadcast_to(kmax, kn_ref.shape)

    rows = HEAD_DIM + 16
    init = (jnp.full((1, 2 * T), NEG_INF, F32), jnp.zeros((rows, T), F32), jnp.zeros((rows, T), F32))
    carry = step(i, init, True)

    qf = q.astype(F32)
    qsq = qf * qf
    kn2 = kn_ref[0:1, 0:1] * FOX_NORM_SLACK
    need = jnp.zeros((1, 1), jnp.int32)
    blk = lax.broadcasted_iota(jnp.int32, (ct_ref.shape[0], 1, 1), 0)
    for par in range(2):
        head_lanes = (lane >= HEAD_DIM) == (par == 1)
        qn2 = jnp.max(jnp.sum(jnp.where(head_lanes, qsq, 0.0), axis=1, keepdims=True), axis=0, keepdims=True)
        reach = jnp.sqrt(qn2 * kn2) + FOX_BOUND_SLACK
        m_min = jnp.min(carry[0][:, par * T:(par + 1) * T], axis=1, keepdims=True)
        cq_first = cq[:, par * T:par * T + 1]
        ck_end = ct_ref[:, pl.ds(h_even + par, 1), :][:, :, T - 1:T]
        live = (reach + cq_first - m_min)[None, :, :] - ck_end > -FOX_UNDERFLOW
        count = jnp.sum(jnp.where(live & (blk < i), 1, 0), axis=0)
        need = jnp.maximum(need, count)
    n_keep = need[0, 0]

    n_pairs = n_keep // 2
    carry = lax.fori_loop(0, n_pairs, lambda u, c: step(i - 2 - 2 * u, step(i - 1 - 2 * u, c, False), False), carry)
    carry = lax.fori_loop(2 * n_pairs, n_keep, lambda u, c: step(i - 1 - u, c, False), carry)
    _, acc_e, acc_o = carry
    o_t = jnp.concatenate([acc_e[0:HEAD_DIM] / acc_e[HEAD_DIM:HEAD_DIM + 1],
                           acc_o[0:HEAD_DIM] / acc_o[HEAD_DIM:HEAD_DIM + 1]], axis=0)
    o_ref[...] = o_t.T.astype(o_ref.dtype)


def _foxt(q, k, ka, vt, ct, *, T):
    b, s, _ = q.shape
    nt = s // T
    return pl.pallas_call(
        functools.partial(_foxt_body, T=T),
        grid=(b, N_PAIRS, nt),
        in_specs=[pl.BlockSpec((None, T, PAIR), lambda bi, p, i: (bi, i, p)),
                  pl.BlockSpec((None, s, PAIR), lambda bi, p, i: (bi, 0, p)),
                  pl.BlockSpec((None, s, PAIR), lambda bi, p, i: (bi, 0, p)),
                  pl.BlockSpec((None, nt, PAIR, T), lambda bi, p, i: (bi, 0, p, 0)),
                  pl.BlockSpec((None, nt, N_HEADS, T), lambda bi, p, i: (bi, 0, 0, 0))],
        out_specs=pl.BlockSpec((None, T, PAIR), lambda bi, p, i: (bi, i, p)),
        out_shape=jax.ShapeDtypeStruct((b, s, WIDTH), BF16),
        scratch_shapes=[pltpu.VMEM((8, LANES), F32)],
        compiler_params=_cparams(3),
        name="foxt",
    )(q, k, ka, vt, ct)


def _band_body(q_ref, k_ref, v_ref, bias_ref, o_ref, *scratch, TQ, W, n_sub, padded):
    p = pl.program_id(1)
    i = pl.program_id(2)
    if padded:
        kpad_ref, vpad_ref = scratch
        s_len = k_ref.shape[0]

        @pl.when(i == 0)
        def _():
            zeros = jnp.zeros((BAND_REACH, PAIR), BF16)
            kpad_ref[pl.ds(0, BAND_REACH), :] = zeros
            vpad_ref[pl.ds(0, BAND_REACH), :] = zeros
            kpad_ref[pl.ds(BAND_REACH, s_len), :] = k_ref[...]
            vpad_ref[pl.ds(BAND_REACH, s_len), :] = v_ref[...]
    else:
        kpad_ref, vpad_ref = k_ref, v_ref

    lane = lax.broadcasted_iota(jnp.int32, (TQ, PAIR), 1)
    ones_blk = jnp.where(lax.broadcasted_iota(jnp.int32, (W, LANES), 1) == 0, 1.0, 0.0).astype(BF16)
    kcol = lax.broadcasted_iota(jnp.int32, (TQ, W), 1)

    def sub_block(sub, carry):
        r0 = pl.multiple_of(sub * TQ, TQ)
        q0 = i * (n_sub * TQ) + r0
        q = q_ref[pl.ds(r0, TQ), :]
        kw = kpad_ref[pl.ds(pl.multiple_of(q0, TQ), W), :] if padded else kpad_ref[...]
        vw = vpad_ref[pl.ds(pl.multiple_of(q0, TQ), W), :] if padded else vpad_ref[...]
        vcat = jnp.concatenate([vw, ones_blk], axis=1)
        outs = []
        for par in range(2):
            h = 2 * p + par
            qm = jnp.where((lane >= HEAD_DIM) == (par == 1), q, jnp.zeros_like(q))
            s = _dot_nt(qm, kw) + bias_ref[h]
            if padded:
                s = jnp.where(kcol >= BAND_REACH - q0, s, NEG_INF)
            m = jnp.max(s, axis=1, keepdims=True)
            pexp = jnp.exp2(s - m)
            pv = _dot(pexp.astype(BF16), vcat)
            outs.append(pv[:, :LANES] / pv[:, LANES:LANES + 1])
        o_ref[pl.ds(r0, TQ), :] = jnp.where(lane < HEAD_DIM, outs[0], outs[1]).astype(o_ref.dtype)
        return carry

    lax.fori_loop(0, n_sub, sub_block, 0)


def _band(q, k, v, bias, *, TQ, n_sub, padded):
    b, sq, _ = q.shape
    sk = k.shape[1]
    w = BAND_REACH + TQ if padded else sk
    tqb = TQ * n_sub
    scratch = [pltpu.VMEM((sk + BAND_REACH, PAIR), BF16)] * 2 if padded else []
    return pl.pallas_call(
        functools.partial(_band_body, TQ=TQ, W=w, n_sub=n_sub, padded=padded),
        grid=(b, N_PAIRS, sq // tqb),
        in_specs=[pl.BlockSpec((None, tqb, PAIR), lambda bi, p, i: (bi, i, p)),
                  pl.BlockSpec((None, sk, PAIR), lambda bi, p, i: (bi, 0, p)),
                  pl.BlockSpec((None, sk, PAIR), lambda bi, p, i: (bi, 0, p)),
                  pl.BlockSpec(bias.shape, lambda bi, p, i: (0, 0, 0))],
        out_specs=pl.BlockSpec((None, tqb, PAIR), lambda bi, p, i: (bi, i, p)),
        out_shape=jax.ShapeDtypeStruct((b, sq, WIDTH), BF16),
        scratch_shapes=scratch,
        compiler_params=_cparams(3),
        name="band",
    )(q, k, v, bias)


BAND_TQ = 256
BAND_STEP = 512


def _bandt_body(q_ref, kp_ref, kc_ref, vp_ref, vc_ref, bias_ref, o_ref):
    i = pl.program_id(2)
    w = BAND_REACH + BAND_TQ
    k2 = jnp.concatenate([kp_ref[...], kc_ref[...]], axis=0)
    vt2 = jnp.concatenate([vp_ref[...], vc_ref[...]], axis=1)
    lane = lax.broadcasted_iota(jnp.int32, (BAND_TQ, PAIR), 1)
    ones_rows = jnp.ones((16, w), BF16)
    krow = lax.broadcasted_iota(jnp.int32, (w, 2 * BAND_TQ), 0)
    bias = bias_ref[...]
    for sub in range(BAND_STEP // BAND_TQ):
        r0 = sub * BAND_TQ
        q = q_ref[r0:r0 + BAND_TQ, :]
        qcat = jnp.concatenate([jnp.where(lane < HEAD_DIM, q, jnp.zeros_like(q)),
                                jnp.where(lane >= HEAD_DIM, q, jnp.zeros_like(q))], axis=0)
        st = _dot_nt(k2[r0:r0 + w], qcat) + bias
        st = jnp.where(krow >= (1 - i) * BAND_STEP - r0, st, NEG_INF)
        m = jnp.max(st, axis=0, keepdims=True)
        pt = jnp.exp2(st - m).astype(BF16)
        vwin = vt2[:, r0:r0 + w]
        outs = []
        for par in range(2):
            vcat = jnp.concatenate([vwin[par * HEAD_DIM:(par + 1) * HEAD_DIM], ones_rows], axis=0)
            pv = _dot(vcat, pt[:, par * BAND_TQ:(par + 1) * BAND_TQ])
            outs.append(pv[0:HEAD_DIM] / pv[HEAD_DIM:HEAD_DIM + 1])
        o_ref[r0:r0 + BAND_TQ, :] = jnp.concatenate(outs, axis=0).T.astype(o_ref.dtype)


def _bandt(q, k, vt, bias_t):
    b, s, _ = q.shape
    prev = lambda i: jnp.maximum(i - 1, 0)
    return pl.pallas_call(
        _bandt_body,
        grid=(b, N_PAIRS, s // BAND_STEP),
        in_specs=[pl.BlockSpec((None, BAND_STEP, PAIR), lambda bi, p, i: (bi, i, p)),
                  pl.BlockSpec((None, BAND_STEP, PAIR), lambda bi, p, i: (bi, prev(i), p)),
                  pl.BlockSpec((None, BAND_STEP, PAIR), lambda bi, p, i: (bi, i, p)),
                  pl.BlockSpec((None, None, PAIR, BAND_STEP), lambda bi, p, i: (bi, prev(i), p, 0)),
                  pl.BlockSpec((None, None, PAIR, BAND_STEP), lambda bi, p, i: (bi, i, p, 0)),
                  pl.BlockSpec((None,) + bias_t.shape[1:], lambda bi, p, i: (p, 0, 0))],
        out_specs=pl.BlockSpec((None, BAND_STEP, PAIR), lambda bi, p, i: (bi, i, p)),
        out_shape=jax.ShapeDtypeStruct((b, s, WIDTH), BF16),
        compiler_params=_cparams(3),
        name="bandt",
    )(q, k, k, vt, vt, bias_t)


def _band_bias_tile_t(rel_bias):
    tile = _band_bias_tile(rel_bias, BAND_TQ, BAND_REACH + BAND_TQ)
    t = jnp.swapaxes(tile, 1, 2)
    return jnp.concatenate([t[0::2], t[1::2]], axis=2)


def _band_bias_tile(rel_bias, tq, w):
    span = w + tq - 1
    period = span + 1
    v = np.arange(period)
    d = np.where(v < w, v, v - period)
    table_idx = np.clip(BAND_REACH - d, -REL_CLIP, REL_CLIP) + REL_CLIP
    table = rel_bias[:, table_idx] * LOG2E
    n_h = rel_bias.shape[0]
    vals = jnp.tile(table, (1, tq))[:, :tq * span].reshape(n_h, tq, span)[:, :, :w]
    r = np.arange(tq)[:, None]
    c = np.arange(w)[None, :]
    in_band = (c // CHUNK >= r // CHUNK) & (c // CHUNK <= r // CHUNK + BAND_REACH // CHUNK)
    return jnp.where(jnp.asarray(in_band)[None], vals, NEG_INF).astype(F32)


def _sample_attn_body(q_ref, kc_ref, vc_ref, kn_ref, vn_ref, *rest, fox):
    p = pl.program_id(1)
    t = q_ref.shape[0]
    past = kc_ref.shape[1]
    lane = lax.broadcasted_iota(jnp.int32, (t, PAIR), 1)
    q = q_ref[...]
    qs = jnp.concatenate([jnp.where(lane < HEAD_DIM, q, jnp.zeros_like(q)),
                          jnp.where(lane >= HEAD_DIM, q, jnp.zeros_like(q))], axis=0)
    kc = kc_ref[...].astype(BF16)
    vc = vc_ref[...].astype(BF16)
    s_c = _dot(qs, kc)
    s_n = _dot_nt(qs, kn_ref[...])
    row = lax.broadcasted_iota(jnp.int32, (t, t), 0)
    col = lax.broadcasted_iota(jnp.int32, (t, t), 1)
    if fox:
        cq_ref, ck_ref, o_ref = rest
    else:
        bias_ref, o_ref = rest
    pcs, pns = [], []
    for par in range(2):
        h = 2 * p + par
        sc = s_c[par * t:(par + 1) * t]
        sn = s_n[par * t:(par + 1) * t]
        if fox:
            cq_col = jnp.sum(jnp.where(lane == h, cq_ref[...], 0.0), axis=1, keepdims=True)
            ck = ck_ref[pl.ds(h, 1), :]
            uc = sc + (cq_col - ck[:, 0:past])
            un = jnp.where(col <= row, sn + (cq_col - ck[:, past:past + t]), NEG_INF)
        else:
            bias = bias_ref[h]
            uc = sc + bias[:, 0:past]
            un = sn + bias[:, past:past + t]
        m = jnp.maximum(jnp.max(uc, axis=1, keepdims=True), jnp.max(un, axis=1, keepdims=True))
        pc = jnp.exp2(uc - m)
        pn = jnp.exp2(un - m)
        inv = 1.0 / (jnp.sum(pc, axis=1, keepdims=True) + jnp.sum(pn, axis=1, keepdims=True))
        pcs.append((pc * inv).astype(BF16))
        pns.append((pn * inv).astype(BF16))
    o2 = _dot_nt(jnp.concatenate(pcs, axis=0), vc) + _dot(jnp.concatenate(pns, axis=0), vn_ref[...])
    o_ref[...] = jnp.where(lane < HEAD_DIM, o2[0:t], o2[t:2 * t]).astype(o_ref.dtype)


def _sample_attn(q, kc_t, vc_t, k_new, v_new, *extra, fox):
    b, t, _ = q.shape
    past = kc_t.shape[3]
    pair_rows = pl.BlockSpec((None, t, PAIR), lambda bi, p: (bi, 0, p))
    cache = pl.BlockSpec((None, None, PAIR, past), lambda bi, p: (bi, p, 0, 0))
    if fox:
        cq, ck = extra
        extra_specs = [pl.BlockSpec((None, t, LANES), lambda bi, p: (bi, 0, 0)),
                       pl.BlockSpec((None, None, N_HEADS, ck.shape[3]), lambda bi, p: (bi, 0, 0, 0))]
    else:
        extra_specs = [pl.BlockSpec(extra[0].shape, lambda bi, p: (0, 0, 0))]
    return pl.pallas_call(
        functools.partial(_sample_attn_body, fox=fox),
        grid=(b, N_PAIRS),
        in_specs=[pair_rows, cache, cache, pair_rows, pair_rows] + extra_specs,
        out_specs=pair_rows,
        out_shape=jax.ShapeDtypeStruct((b, t, WIDTH), BF16),
        compiler_params=_cparams(2),
        name="sample_fox" if fox else "sample_band",
    )(q, kc_t, vc_t, k_new, v_new, *extra)


def _first_index(is_max, idx, axis, big):
    return jnp.min(jnp.where(is_max, idx, big), axis=axis, keepdims=True)


def _route(scores, choice):
    t = scores.shape[1]
    c3 = choice.reshape(N_GROUPS, GROUP_SIZE, t)
    j_idx = lax.broadcasted_iota(jnp.int32, c3.shape, 1)
    top1 = jnp.max(c3, axis=1, keepdims=True)
    first = _first_index(c3 == top1, j_idx, 1, GROUP_SIZE)
    top2 = jnp.max(jnp.where(j_idx == first, -jnp.inf, c3), axis=1, keepdims=True)
    gscore = (top1 + top2).reshape(N_GROUPS, t)

    g_idx = lax.broadcasted_iota(jnp.int32, gscore.shape, 0)
    gsel = jnp.zeros(gscore.shape, F32)
    work = gscore
    for _ in range(TOPK_GROUPS):
        gm = jnp.max(work, axis=0, keepdims=True)
        pick = g_idx == _first_index(work == gm, g_idx, 0, N_GROUPS)
        gsel = jnp.where(pick, 1.0, gsel)
        work = jnp.where(pick, -jnp.inf, work)

    emask = jnp.broadcast_to(gsel.reshape(N_GROUPS, 1, t), c3.shape) > 0.0
    work = jnp.where(emask, c3, NEG_INF)
    e_idx = lax.broadcasted_iota(jnp.int32, c3.shape, 0) * GROUP_SIZE + j_idx
    esel = jnp.zeros(c3.shape, F32)
    for _ in range(TOP_K):
        em = jnp.max(jnp.max(work, axis=1, keepdims=True), axis=0, keepdims=True)
        cand = jnp.where(work == em, e_idx, N_EXPERTS)
        first = jnp.min(jnp.min(cand, axis=1, keepdims=True), axis=0, keepdims=True)
        pick = e_idx == first
        esel = jnp.where(pick, 1.0, esel)
        work = jnp.where(pick, -jnp.inf, work)

    w = esel * scores.reshape(c3.shape)
    denom = jnp.sum(jnp.sum(w, axis=1, keepdims=True), axis=0, keepdims=True)
    return (w / denom * ROUTED_SCALE).reshape(N_EXPERTS, t)


def _merge_body(of_ref, ob_ref, x_ref, gate_ref, sh_ref, sc_ref, ogf_ref, ogb_ref, wo_ref, g2_ref,
                wrh_ref, wrl_ref, br_ref, x1_ref, h2_ref, comb_ref, combt_ref, cnt_ref):
    def group_norm(t_ref, gain_ref):
        t = t_ref[...].astype(F32)
        ms = jnp.mean(t * t, axis=-1, keepdims=True)
        return (t * lax.rsqrt(ms + EPS) * gain_ref[...]).astype(BF16)

    y = _dot(group_norm(of_ref, ogf_ref), wo_ref[0:WIDTH, :]) + _dot(group_norm(ob_ref, ogb_ref), wo_ref[WIDTH:, :])
    x = x_ref[...]
    g, r, d = x.shape
    x1 = x + gate_ref[...] * y.reshape(g, r, d)
    x1_ref[...] = x1
    ms = jnp.mean(x1 * x1, axis=-1, keepdims=True)
    h2 = (x1 * lax.rsqrt(ms + EPS) * g2_ref[...] * (1.0 + sc_ref[...]) + sh_ref[...]).reshape(g * r, d)
    h_hi, h_lo = _split2(h2)
    h2_ref[...] = h_hi
    logits = _dot_nt(wrh_ref[...], h_hi) + _dot_nt(wrh_ref[...], h_lo) + _dot_nt(wrl_ref[...], h_hi)
    scores = jax.nn.sigmoid(logits)
    t = scores.shape[1]
    bias = jnp.concatenate([br_ref[...]] * (t // LANES), axis=1)
    comb = _route(scores, scores + bias)
    comb_pad = jnp.concatenate([comb, jnp.zeros((LANES - N_EXPERTS, t), F32)], axis=0)
    comb_ref[...] = comb_pad.T
    combt_ref[...] = comb
    picked = jnp.where(comb > 0.0, 1.0, 0.0)
    for sub in range(cnt_ref.shape[0]):
        cnt = jnp.sum(picked[:, sub * MOE_TM:(sub + 1) * MOE_TM], axis=1, keepdims=True)
        cnt_ref[sub] = jnp.broadcast_to(cnt, (N_EXPERTS, LANES))


def _merge(of, ob, x, gate, shift, scale, ogf, ogb, wo, g2, wr_hi, wr_lo, br, *, G, R):
    nb, s, d = x.shape
    n = nb * s
    tm = G * R
    nbi, nsi = nb // G, s // R
    row = lambda b, i: (b * nsi + i, 0)
    const = lambda b, i: (0, 0)
    mod_spec = pl.BlockSpec((G, 1, d), lambda b, i: (b, 0, 0))
    x_spec = pl.BlockSpec((G, R, d), lambda b, i: (b, i, 0))
    return pl.pallas_call(
        _merge_body,
        grid=(nbi, nsi),
        in_specs=[pl.BlockSpec((tm, WIDTH), row), pl.BlockSpec((tm, WIDTH), row), x_spec,
                  mod_spec, mod_spec, mod_spec,
                  pl.BlockSpec((1, WIDTH), const), pl.BlockSpec((1, WIDTH), const),
                  pl.BlockSpec(wo.shape, const), pl.BlockSpec((1, d), const),
                  pl.BlockSpec(wr_hi.shape, const), pl.BlockSpec(wr_lo.shape, const),
                  pl.BlockSpec(br.shape, const)],
        out_specs=[x_spec, pl.BlockSpec((tm, d), row), pl.BlockSpec((tm, LANES), row),
                   pl.BlockSpec((N_EXPERTS, tm), lambda b, i: (0, b * nsi + i)),
                   pl.BlockSpec((tm // MOE_TM, N_EXPERTS, LANES), lambda b, i: (b * nsi + i, 0, 0))],
        out_shape=[jax.ShapeDtypeStruct((nb, s, d), F32), jax.ShapeDtypeStruct((n, d), BF16),
                   jax.ShapeDtypeStruct((n, LANES), F32), jax.ShapeDtypeStruct((N_EXPERTS, n), F32),
                   jax.ShapeDtypeStruct((n // MOE_TM, N_EXPERTS, LANES), F32)],
        compiler_params=_cparams(2),
        name="merge",
    )(of, ob, x, gate, shift, scale, ogf, ogb, wo, g2, wr_hi, wr_lo, br)


def _silu(g):
    return g * jax.nn.sigmoid(g)


def _moe_body(h_ref, comb_ref, x1_ref, gate_ref, wg_ref, wu_ref, wd_ref, sg_ref, su_ref, sd_ref, y_ref, acc_ref):
    e = pl.program_id(2)
    hb = h_ref[...]

    @pl.when(e == 0)
    def _():
        a = _silu(_dot(hb, sg_ref[...])) * _dot(hb, su_ref[...])
        acc_ref[...] = _dot(a.astype(BF16), sd_ref[...])

    comb = comb_ref[...]
    lane = lax.broadcasted_iota(jnp.int32, comb.shape, 1)
    c_e = jnp.sum(jnp.where(lane == e, comb, 0.0), axis=1, keepdims=True)
    a = _silu(_dot(hb, wg_ref[...])) * _dot(hb, wu_ref[...]) * c_e
    acc_ref[...] += _dot(a.astype(BF16), wd_ref[...])

    @pl.when(e == pl.num_programs(2) - 1)
    def _():
        x1 = x1_ref[...]
        g, r, d = x1.shape
        y_ref[...] = x1 + gate_ref[...] * acc_ref[...].reshape(g, r, d)


def _moe(h2, comb, x1, gate, wg, wu, wd, sg, su, sd, *, G, R):
    nb, s, d = x1.shape
    tm = G * R
    nbi, nsi = nb // G, s // R
    ff = wg.shape[2]
    row = lambda b, i, e: (b * nsi + i, 0)
    const = lambda b, i, e: (0, 0)
    x_spec = pl.BlockSpec((G, R, d), lambda b, i, e: (b, i, 0))
    return pl.pallas_call(
        _moe_body,
        grid=(nbi, nsi, N_EXPERTS),
        in_specs=[pl.BlockSpec((tm, d), row), pl.BlockSpec((tm, LANES), row), x_spec,
                  pl.BlockSpec((G, 1, d), lambda b, i, e: (b, 0, 0)),
                  pl.BlockSpec((None, d, ff), lambda b, i, e: (e, 0, 0)),
                  pl.BlockSpec((None, d, ff), lambda b, i, e: (e, 0, 0)),
                  pl.BlockSpec((None, ff, d), lambda b, i, e: (e, 0, 0)),
                  pl.BlockSpec(sg.shape, const), pl.BlockSpec(su.shape, const), pl.BlockSpec(sd.shape, const)],
        out_specs=x_spec,
        out_shape=jax.ShapeDtypeStruct((nb, s, d), F32),
        scratch_shapes=[pltpu.VMEM((tm, d), F32)],
        compiler_params=_cparams(3),
        name="moe",
    )(h2, comb, x1, gate, wg, wu, wd, sg, su, sd)


MOE_TM = 256
MOE_CH = 16
MOE_SLOTS = TOP_K * MOE_TM + N_EXPERTS * MOE_CH
MOE_NCHUNK = MOE_SLOTS // MOE_CH
MOE_PIECE = 512
MOE_NPIECE = MOE_SLOTS // MOE_PIECE
MOE_CPP = MOE_PIECE // MOE_CH
MOE_RB = 1024
TAU_RADIX = 64.0
assert MOE_SLOTS % MOE_PIECE == 0


def _moe_plan(cnt, n_tiles):
    pc = (cnt + MOE_CH - 1) // MOE_CH * MOE_CH
    off = jnp.cumsum(pc, axis=1) - pc
    end = off + pc
    n_used = (jnp.sum(pc, axis=1) // MOE_CH).astype(jnp.int32)
    tot = jnp.sum(pc, axis=0)
    reg = (tot + MOE_RB - 1) // MOE_RB * MOE_RB
    reg_end = jnp.cumsum(reg)
    reg_start = reg_end - reg
    dest_base = reg_start[None, :] + jnp.cumsum(pc, axis=0) - pc
    chunk_row = jnp.arange(MOE_NCHUNK, dtype=jnp.int32)[None, :] * MOE_CH
    in_group = (chunk_row[:, :, None] >= off[:, None, :]) & (chunk_row[:, :, None] < end[:, None, :])
    used = jnp.any(in_group, axis=2)
    cdest = chunk_row + jnp.sum(jnp.where(in_group, (dest_base - off)[:, None, :], 0), axis=2)
    worst_rows = TOP_K * MOE_TM * n_tiles + n_tiles * N_EXPERTS * (MOE_CH - 1) + N_EXPERTS * (MOE_RB - MOE_CH)
    r_max = -(-worst_rows // MOE_RB)
    parity = (jnp.arange(n_tiles, dtype=jnp.int32) % 2)[:, None]
    cdest_d = jnp.where(used, cdest, r_max * MOE_RB + parity * MOE_SLOTS + chunk_row).astype(jnp.int32)
    cdest_c = jnp.where(used, cdest, chunk_row).astype(jnp.int32)
    n_active = (reg_end[-1] // MOE_RB).astype(jnp.int32).reshape(1)
    tile_row = jnp.arange(r_max, dtype=jnp.int32) * MOE_RB
    tile_expert = jnp.minimum(jnp.sum((tile_row[:, None] >= reg_end[None, :]).astype(jnp.int32), axis=1),
                              N_EXPERTS - 1).astype(jnp.int32)
    in_region = (tile_row[:, None] >= reg_start[None, :]) & (tile_row[:, None] < reg_end[None, :])
    rows_end = jnp.sum(jnp.where(in_region, (reg_start + tot)[None, :], 0), axis=1)
    valid = jnp.clip(rows_end - tile_row, 0, MOE_RB).astype(jnp.int32)
    f = lambda a: a.astype(F32)
    zeros64 = jnp.zeros((n_tiles, N_EXPERTS), F32)
    row2 = lambda a: jnp.broadcast_to(jnp.concatenate([f(a), f(a)], axis=1)[:, None, :], (n_tiles, 8, LANES))
    col = lambda a: jnp.broadcast_to(f(a)[:, :, None], (n_tiles, N_EXPERTS, LANES))
    col128 = lambda a: jnp.broadcast_to(jnp.concatenate([f(a), zeros64], axis=1)[:, :, None], (n_tiles, LANES, LANES))
    row1 = lambda a: jnp.broadcast_to(jnp.concatenate([f(a), zeros64], axis=1)[:, None, :], (n_tiles, 8, LANES))
    return dict(n_used=n_used, cdest_d=cdest_d.reshape(n_tiles, 1, MOE_NCHUNK),
                cdest_c=cdest_c.reshape(n_tiles, 1, MOE_NCHUNK), r_max=r_max, n_active=n_active,
                tile_expert=tile_expert, valid=valid,
                off_row2=row2(off), end_row2=row2(end), off_col=col(off),
                off_row1=row1(off), off_col128=col128(off), end_col128=col128(end))


def _tau_pieces(sel, tau):
    tau = jnp.where(sel, tau, -1.0)
    hi = jnp.floor(tau * (1.0 / TAU_RADIX)) * TAU_RADIX
    return hi.astype(BF16), (tau - hi).astype(BF16)


def _dispatch_body(nused_ref, cdest_ref, h_ref, combt_ref, offcol_ref, offrow_ref, endrow_ref, sorted_ref,
                   buf_ref, sem):
    t = pl.program_id(0)
    n_used = nused_ref[t]
    tm = h_ref.shape[0]
    sel = combt_ref[...] > 0.0
    rr = lax.broadcasted_iota(jnp.int32, (tm, tm), 0)
    cc = lax.broadcasted_iota(jnp.int32, (tm, tm), 1)
    upper = jnp.where(rr < cc, 1.0, 0.0).astype(BF16)
    rank = _dot(jnp.where(sel, 1.0, 0.0).astype(BF16), upper)
    cols = jnp.concatenate([offcol_ref[...]] * (tm // LANES), axis=1)
    tau_hi, tau_lo = _tau_pieces(sel, cols + rank)
    taucat = jnp.concatenate([tau_hi, tau_lo], axis=0)
    off_row = offrow_ref[0:1, :]
    end_row = endrow_ref[0:1, :]
    hb = h_ref[...]

    slot = t % 2
    buf = buf_ref.at[slot]
    buf_prev = buf_ref.at[1 - slot]

    def start_piece(piece):
        chunks = range(piece * MOE_CPP, (piece + 1) * MOE_CPP)
        dests = [cdest_ref[0, c] for c in chunks]
        for c, row in zip(chunks, dests):
            dst = sorted_ref.at[pl.ds(pl.multiple_of(row, MOE_CH), MOE_CH), :]
            pltpu.make_async_copy(buf.at[pl.ds(c * MOE_CH, MOE_CH), :], dst, sem.at[slot]).start()

    def wait_piece(piece, which_buf, which_sem):
        rows = pl.ds(piece * MOE_PIECE, MOE_PIECE)
        pltpu.make_async_copy(which_buf.at[rows, :], sorted_ref.at[rows, :], which_sem).wait()

    for piece in range(MOE_NPIECE):
        @pl.when(piece * MOE_CPP < n_used)
        def _():
            if piece > 0:
                start_piece(piece - 1)
            base = piece * MOE_PIECE
            s_col = (base + lax.broadcasted_iota(jnp.int32, (MOE_PIECE, LANES), 0)).astype(F32)
            onehot = jnp.where((s_col >= off_row) & (s_col < end_row), 1.0, 0.0).astype(BF16)
            q = _dot(onehot, taucat)
            s_mat = (base + lax.broadcasted_iota(jnp.int32, (MOE_PIECE, tm), 0)).astype(F32)
            g = jnp.where(q == s_mat, 1.0, 0.0).astype(BF16)
            buf[pl.ds(base, MOE_PIECE), :] = _dot(g, hb).astype(BF16)

    last = (n_used - 1) // MOE_CPP
    for piece in range(MOE_NPIECE):
        @pl.when(piece == last)
        def _():
            start_piece(piece)

    n_prev = nused_ref[jnp.maximum(t - 1, 0)]
    for piece in range(MOE_NPIECE):
        @pl.when((t > 0) & (piece * MOE_CPP < n_prev))
        def _():
            wait_piece(piece, buf_prev, sem.at[1 - slot])

    for piece in range(MOE_NPIECE):
        @pl.when((t == pl.num_programs(0) - 1) & (piece * MOE_CPP < n_used))
        def _():
            wait_piece(piece, buf, sem.at[slot])


def _dispatch(h2, combt, plan, n_tiles):
    n, d = h2.shape
    r_total = plan["r_max"] * MOE_RB + 2 * MOE_SLOTS
    grid_spec = pltpu.PrefetchScalarGridSpec(
        num_scalar_prefetch=1,
        grid=(n_tiles,),
        in_specs=[pl.BlockSpec((None, 1, MOE_NCHUNK), lambda t, nu: (t, 0, 0), memory_space=pltpu.SMEM),
                  pl.BlockSpec((MOE_TM, d), lambda t, nu: (t, 0)),
                  pl.BlockSpec((N_EXPERTS, MOE_TM), lambda t, nu: (0, t)),
                  pl.BlockSpec((None, N_EXPERTS, LANES), lambda t, nu: (t, 0, 0)),
                  pl.BlockSpec((None, 8, LANES), lambda t, nu: (t, 0, 0)),
                  pl.BlockSpec((None, 8, LANES), lambda t, nu: (t, 0, 0))],
        out_specs=pl.BlockSpec(memory_space=pl.ANY),
        scratch_shapes=[pltpu.VMEM((2, MOE_SLOTS, d), BF16), pltpu.SemaphoreType.DMA((2,))],
    )
    return pl.pallas_call(
        _dispatch_body,
        grid_spec=grid_spec,
        out_shape=jax.ShapeDtypeStruct((r_total, d), BF16),
        compiler_params=_cparams(1),
        name="moe_dispatch",
    )(plan["n_used"], plan["cdest_d"], h2, combt, plan["off_col"], plan["off_row2"], plan["end_row2"])


def _ffn_body(texp_ref, nact_ref, valid_ref, x_ref, wg_ref, wu_ref, wd_ref, o_ref):
    r = pl.program_id(0)

    @pl.when(r < nact_ref[0])
    def _():
        x = x_ref[...]
        rows = lax.broadcasted_iota(jnp.int32, x.shape, 0)
        x = jnp.where(rows < valid_ref[r], x, jnp.zeros_like(x))
        a = _silu(_dot(x, wg_ref[...])) * _dot(x, wu_ref[...])
        o_ref[...] = _dot(a.astype(BF16), wd_ref[...]).astype(o_ref.dtype)


def _ffn(xs, wg, wu, wd, plan):
    r_total, d = xs.shape
    ff = wg.shape[2]
    last = lambda r, te, na, va: (jnp.minimum(r, na[0] - 1), 0)
    wmap = lambda r, te, na, va: (te[r], 0, 0)
    grid_spec = pltpu.PrefetchScalarGridSpec(
        num_scalar_prefetch=3,
        grid=(plan["r_max"],),
        in_specs=[pl.BlockSpec((MOE_RB, d), last),
                  pl.BlockSpec((None, d, ff), wmap), pl.BlockSpec((None, d, ff), wmap),
                  pl.BlockSpec((None, ff, d), wmap)],
        out_specs=pl.BlockSpec((MOE_RB, d), last),
    )
    return pl.pallas_call(
        _ffn_body,
        grid_spec=grid_spec,
        out_shape=jax.ShapeDtypeStruct((r_total, d), BF16),
        compiler_params=_cparams(1),
        name="moe_ffn",
    )(plan["tile_expert"], plan["n_active"], plan["valid"], xs, wg, wu, wd)


def _combine_body(nused_ref, cdest_ref, h_ref, comb_ref, x1_ref, gate_ref, offrow_ref, offcol_ref, endcol_ref,
                  sg_ref, su_ref, sd_ref, ys_ref, y_ref, buf_ref, sem):
    t = pl.program_id(0)
    n_used = nused_ref[t]
    tm = h_ref.shape[0]

    @pl.when(t == 0)
    def _():
        buf_ref[...] = jnp.zeros_like(buf_ref)

    def wait_piece(piece):
        rows = pl.ds(piece * MOE_PIECE, MOE_PIECE)
        pltpu.make_async_copy(ys_ref.at[rows, :], buf_ref.at[rows, :], sem.at[piece]).wait()

    for piece in range(MOE_NPIECE):
        @pl.when(piece * MOE_CPP < n_used)
        def _():
            chunks = range(piece * MOE_CPP, (piece + 1) * MOE_CPP)
            srcs = [cdest_ref[0, c] for c in chunks]
            for c, row in zip(chunks, srcs):
                src = ys_ref.at[pl.ds(pl.multiple_of(row, MOE_CH), MOE_CH), :]
                pltpu.make_async_copy(src, buf_ref.at[pl.ds(c * MOE_CH, MOE_CH), :], sem.at[piece]).start()

    hb = h_ref[...]
    acc = _dot((_silu(_dot(hb, sg_ref[...])) * _dot(hb, su_ref[...])).astype(BF16), sd_ref[...])
    comb = comb_ref[...]
    sel = comb > 0.0
    rr = lax.broadcasted_iota(jnp.int32, (tm, tm), 0)
    cc = lax.broadcasted_iota(jnp.int32, (tm, tm), 1)
    lower = jnp.where(cc < rr, 1.0, 0.0).astype(BF16)
    rank = _dot(lower, jnp.where(sel, 1.0, 0.0).astype(BF16))
    tau_hi, tau_lo = _tau_pieces(sel, offrow_ref[0:1, :] + rank)
    taucat = jnp.concatenate([tau_hi, tau_lo], axis=1)
    lhs = jnp.concatenate([taucat, jnp.concatenate([comb.astype(BF16), jnp.zeros((tm, LANES), BF16)], axis=1)],
                          axis=0)
    off_col = offcol_ref[...]
    end_col = endcol_ref[...]
    s_row = lax.broadcasted_iota(jnp.int32, (LANES, MOE_SLOTS), 1).astype(F32)
    off_b = jnp.concatenate([off_col] * (MOE_SLOTS // LANES), axis=1)
    end_b = jnp.concatenate([end_col] * (MOE_SLOTS // LANES), axis=1)
    onehot = jnp.where((s_row >= off_b) & (s_row < end_b), 1.0, 0.0).astype(BF16)
    qw = _dot(lhs, jnp.concatenate([onehot, onehot], axis=0))
    s_mat = lax.broadcasted_iota(jnp.int32, (tm, MOE_SLOTS), 1).astype(F32)
    gw = jnp.where(qw[0:tm] == s_mat, qw[tm:2 * tm], 0.0).astype(BF16)

    for piece in range(MOE_NPIECE):
        @pl.when(piece * MOE_CPP < n_used)
        def _():
            wait_piece(piece)

    y_ref[...] = x1_ref[...] + gate_ref[...] * (acc + _dot(gw, buf_ref[...]))


def _combine(h2, comb, x1, gate, ys, sg, su, sd, plan, n_tiles):
    nb, s, d = x1.shape
    n = nb * s
    per_b = s // MOE_TM
    x1f = x1.reshape(n, d)
    const2 = lambda t, nu: (0, 0)
    grid_spec = pltpu.PrefetchScalarGridSpec(
        num_scalar_prefetch=1,
        grid=(n_tiles,),
        in_specs=[pl.BlockSpec((None, 1, MOE_NCHUNK), lambda t, nu: (t, 0, 0), memory_space=pltpu.SMEM),
                  pl.BlockSpec((MOE_TM, d), lambda t, nu: (t, 0)),
                  pl.BlockSpec((MOE_TM, LANES), lambda t, nu: (t, 0)),
                  pl.BlockSpec((MOE_TM, d), lambda t, nu: (t, 0)),
                  pl.BlockSpec((None, 1, d), lambda t, nu: (t // per_b, 0, 0)),
                  pl.BlockSpec((None, 8, LANES), lambda t, nu: (t, 0, 0)),
                  pl.BlockSpec((None, LANES, LANES), lambda t, nu: (t, 0, 0)),
                  pl.BlockSpec((None, LANES, LANES), lambda t, nu: (t, 0, 0)),
                  pl.BlockSpec(sg.shape, const2), pl.BlockSpec(su.shape, const2), pl.BlockSpec(sd.shape, const2),
                  pl.BlockSpec(memory_space=pl.ANY)],
        out_specs=pl.BlockSpec((MOE_TM, d), lambda t, nu: (t, 0)),
        scratch_shapes=[pltpu.VMEM((MOE_SLOTS, d), BF16),
                        pltpu.SemaphoreType.DMA((MOE_NPIECE,))],
    )
    y = pl.pallas_call(
        _combine_body,
        grid_spec=grid_spec,
        out_shape=jax.ShapeDtypeStruct((n, d), F32),
        compiler_params=_cparams(1),
        name="moe_combine",
    )(plan["n_used"], plan["cdest_c"], h2, comb, x1f, gate, plan["off_row1"], plan["off_col128"], plan["end_col128"],
      sg, su, sd, ys)
    return y.reshape(nb, s, d)


def _moe_sparse(h2, comb, combt, cnt, x1, gate, wg, wu, wd, sg, su, sd):
    n = h2.shape[0]
    n_tiles = n // MOE_TM
    plan = _moe_plan(cnt[:, :, 0].astype(jnp.int32), n_tiles)
    xs = _dispatch(h2, combt, plan, n_tiles)
    ys = _ffn(xs, wg, wu, wd, plan)
    return _combine(h2, comb, x1, gate, ys, sg, su, sd, plan, n_tiles)


def _tile_heads(g, mult=1.0):
    return (jnp.tile(g.astype(F32), N_HEADS) * mult).reshape(1, WIDTH)


def kernel(x_prompt, x_sample, cache_fox_k, cache_fox_v, cache_fox_logf, cache_band_k, cache_band_v, c_prompt, c_sample, w_ada, b_ada, norm1_g, norm2_g, w_in, b_forget, g_q_fox, g_k_fox, g_q_band, g_k_band, rel_bias, out_g_fox, out_g_band, w_out, w_router, b_router, w_gate, w_up, w_down, ws_gate, ws_up, ws_down):
    depth = w_ada.shape[0]
    assert depth == 1
    bsz, seq, d = x_prompt.shape
    dbs, dseq, _ = x_sample.shape
    past = cache_fox_k.shape[2]
    n_cache = cache_band_k.shape[2]
    assert n_cache == BAND_REACH and dseq == CHUNK and seq % BAND_REACH == 0

    wi = w_in[0]
    cols = [wi[:, 0:512], wi[:, 512:1024], wi[:, 1024:1536], wi[:, 1544:2056], wi[:, 2056:2568], wi[:, 2568:3080],
            wi[:, 1536:1544], jnp.zeros((d, LANES - N_HEADS), F32)]
    w_all = jnp.concatenate(cols, axis=1).astype(BF16)
    hd = jnp.arange(WIDTH) // HEAD_DIM
    bd = jnp.where(hd[:, None] == hd[None, :], 1.0 / HEAD_DIM, 0.0).astype(BF16)
    qscale = ATTN_SCALE * LOG2E
    gqf, gkf = _tile_heads(g_q_fox[0], qscale), _tile_heads(g_k_fox[0])
    gqb, gkb = _tile_heads(g_q_band[0], qscale), _tile_heads(g_k_band[0])
    bf_row = jnp.concatenate([b_forget[0], jnp.zeros((LANES - N_HEADS,), F32)]).reshape(1, LANES)
    g1 = norm1_g[0].reshape(1, d)
    g2 = norm2_g[0].reshape(1, d)
    ogf = out_g_fox[0].reshape(1, WIDTH)
    ogb = out_g_band[0].reshape(1, WIDTH)
    wo = w_out[0].astype(BF16)
    wr_t = w_router[0].T
    wr_hi = wr_t.astype(BF16)
    wr_lo = (wr_t - wr_hi.astype(F32)).astype(BF16)
    br = jnp.broadcast_to(b_router[0].reshape(N_EXPERTS, 1), (N_EXPERTS, LANES)).astype(F32)
    wg, wu, wd = w_gate[0].astype(BF16), w_up[0].astype(BF16), w_down[0].astype(BF16)
    sg, su, sd = ws_gate[0].astype(BF16), ws_up[0].astype(BF16), ws_down[0].astype(BF16)

    n_c = bsz + dbs
    rows = -(-n_c // 8) * 8
    c_all = jnp.concatenate([c_prompt, c_sample, jnp.zeros((rows - n_c, d), F32)], axis=0)
    mod = _ada(c_all, w_ada[0], b_ada[0].reshape(1, -1))

    def mods(lo, hi):
        return [mod[lo:hi, j * d:(j + 1) * d].reshape(hi - lo, 1, d) for j in range(6)]

    shift1_p, scale1_p, gate1_p, shift2_p, scale2_p, gate2_p = mods(0, bsz)
    shift1_s, scale1_s, gate1_s, shift2_s, scale2_s, gate2_s = mods(bsz, n_c)

    TM = BAND_REACH
    (qf, kf, vft, kf32, vf32, lf, qb, kb, vbt, kb32, vb32) = _proj(
        x_prompt, shift1_p, scale1_p, g1, w_all, bd, gqf, gkf, gqb, gkb, bf_row, G=1, R=TM, band_last_only=True)
    r3 = lambda a: a.reshape(bsz, seq, a.shape[-1])
    ct, ka = _scan_t(r3(lf), T=TM)
    of = _foxt(r3(qf), r3(kf), ka, vft, ct, T=TM)
    assert TM == BAND_STEP
    ob = _bandt(r3(qb), r3(kb), vbt, _band_bias_tile_t(rel_bias[0]))
    assert TM % MOE_TM == 0
    x1_p, h2_p, comb_p, combt_p, cnt_p = _merge(of.reshape(-1, WIDTH), ob.reshape(-1, WIDTH), x_prompt, gate1_p,
                                                shift2_p, scale2_p, ogf, ogb, wo, g2, wr_hi, wr_lo, br, G=1, R=TM)
    y_p = _moe_sparse(h2_p, comb_p, combt_p, cnt_p, x1_p, gate2_p, wg, wu, wd, sg, su, sd)

    GS = 8
    (qf_s, kf_s, vf_s, kf32_s, vf32_s, lf_s, qb_s, kb_s, vb_s, kb32_s, vb32_s) = _proj(
        x_sample, shift1_s, scale1_s, g1, w_all, bd, gqf, gkf, gqb, gkb, bf_row, G=GS, R=dseq,
        band_last_only=False)
    s3 = lambda a: a.reshape(dbs, dseq, a.shape[-1])
    sk = past + dseq
    skp = -(-sk // LANES) * LANES
    pad_k = skp - sk
    n_seq = dbs * N_HEADS
    assert n_seq <= LANES
    lf_seq = jnp.concatenate([cache_fox_logf[0], s3(lf_s)[:, :, :N_HEADS]], axis=1)
    lf_seq = jnp.swapaxes(lf_seq, 0, 1).reshape(sk, n_seq)
    lf_seq = jnp.pad(lf_seq, ((0, pad_k), (0, LANES - n_seq)))
    cum_col, cum_row = _scan(lf_seq)
    cq_s = jnp.swapaxes(cum_col[past:past + dseq, :n_seq].reshape(dseq, dbs, N_HEADS), 0, 1)
    cum_s = jnp.pad(cq_s, ((0, 0), (0, 0), (0, LANES - N_HEADS)))
    cumt_s = cum_row[:n_seq].reshape(dbs, 1, N_HEADS, skp)

    def cache_t(c):
        return jnp.transpose(c, (0, 2, 3, 1)).reshape(dbs, N_PAIRS, PAIR, c.shape[1])

    of_s = _sample_attn(s3(qf_s), cache_t(cache_fox_k[0]), cache_t(cache_fox_v[0]), s3(kf_s), s3(vf_s),
                        cum_s, cumt_s, fox=True)
    bias_s = _band_bias_tile(rel_bias[0], dseq, BAND_REACH + LANES)
    ob_s = _sample_attn(s3(qb_s), cache_t(cache_band_k[0]), cache_t(cache_band_v[0]), s3(kb_s), s3(vb_s),
                        bias_s, fox=False)
    x1_s, h2_s, comb_s, _, _ = _merge(of_s.reshape(-1, WIDTH), ob_s.reshape(-1, WIDTH), x_sample, gate1_s, shift2_s,
                                      scale2_s, ogf, ogb, wo, g2, wr_hi, wr_lo, br, G=GS, R=dseq)
    y_s = _moe(h2_s, comb_s, x1_s, gate2_s, wg, wu, wd, sg, su, sd, G=dbs, R=dseq)

    hshape = (N_HEADS, HEAD_DIM)
    new_bk_s = jnp.concatenate([cache_band_k[0], s3(kb32_s).reshape(dbs, dseq, *hshape)], axis=1)[:, -n_cache:]
    new_bv_s = jnp.concatenate([cache_band_v[0], s3(vb32_s).reshape(dbs, dseq, *hshape)], axis=1)[:, -n_cache:]
    return (y_p, y_s,
            kf32.reshape(1, bsz, seq, *hshape), vf32.reshape(1, bsz, seq, *hshape),
            lf[:, :N_HEADS].reshape(1, bsz, seq, N_HEADS),
            kb32.reshape(1, bsz, BAND_REACH, *hshape), vb32.reshape(1, bsz, BAND_REACH, *hshape),
            kf32_s.reshape(1, dbs, dseq, *hshape), vf32_s.reshape(1, dbs, dseq, *hshape),
            lf_s[:, :N_HEADS].reshape(1, dbs, dseq, N_HEADS),
            new_bk_s[None], new_bv_s[None])
```

```python
import functools

import jax
import jax.numpy as jnp
import numpy as np
from jax import lax
from jax.experimental import pallas as pl
from jax.experimental.pallas import tpu as pltpu

F32 = jnp.float32
BF16 = jnp.bfloat16

HEAD_DIM = 64
N_HEADS = 8
WIDTH = N_HEADS * HEAD_DIM
PAIR = 2 * HEAD_DIM
N_PAIRS = N_HEADS // 2
LANES = 128
CHUNK = 64
BAND_REACH = 512
REL_CLIP = 256
N_EXPERTS = 64
N_GROUPS = 8
GROUP_SIZE = N_EXPERTS // N_GROUPS
TOPK_GROUPS = 4
TOP_K = 8
ROUTED_SCALE = 2.5
EPS = 1e-6
NEG_INF = -1e30
ATTN_SCALE = HEAD_DIM ** -0.5
LOG2E = 1.4426950408889634
VMEM_LIMIT = 56 * 1024 * 1024


def _cparams(n_axes):
    return pltpu.CompilerParams(dimension_semantics=("arbitrary",) * n_axes,
                                vmem_limit_bytes=VMEM_LIMIT)


def _dot(a, b):
    return jnp.dot(a, b, preferred_element_type=F32)


def _dot_nt(a, b):
    return lax.dot_general(a, b, (((1,), (1,)), ((), ())), preferred_element_type=F32)


def _split2(a):
    hi = a.astype(BF16)
    lo = (a - hi.astype(F32)).astype(BF16)
    return hi, lo


def _split3(a):
    hi = a.astype(BF16)
    r = a - hi.astype(F32)
    mid = r.astype(BF16)
    lo = (r - mid.astype(F32)).astype(BF16)
    return hi, mid, lo


def _ada_body(c_ref, w_ref, b_ref, o_ref):
    c = c_ref[...]
    a = c * jax.nn.sigmoid(c)
    a_hi, a_lo = _split2(a)
    w_hi, w_lo = _split2(w_ref[...])
    o_ref[...] = _dot(a_hi, w_hi) + _dot(a_hi, w_lo) + _dot(a_lo, w_hi) + b_ref[...]


def _ada(c_all, w_ada, b_ada):
    rows, d = c_all.shape
    n = w_ada.shape[1]
    tn = 1024
    return pl.pallas_call(
        _ada_body,
        grid=(n // tn,),
        in_specs=[pl.BlockSpec((rows, d), lambda j: (0, 0)),
                  pl.BlockSpec((d, tn), lambda j: (0, j)),
                  pl.BlockSpec((1, tn), lambda j: (0, j))],
        out_specs=pl.BlockSpec((rows, tn), lambda j: (0, j)),
        out_shape=jax.ShapeDtypeStruct((rows, n), F32),
        compiler_params=_cparams(1),
        name="ada",
    )(c_all, w_ada, b_ada)


def _log_sigmoid(z):
    return jnp.minimum(z, 0.0) - jnp.log(1.0 + jnp.exp(-jnp.abs(z)))


def _proj_body(x_ref, sh_ref, sc_ref, g1_ref, w_ref, bd_ref, gqf_ref, gkf_ref, gqb_ref, gkb_ref, bf_ref,
               qf_ref, kf_ref, vf_ref, kf32_ref, vf32_ref, lf_ref, qb_ref, kb_ref, vb_ref, kb32_ref, vb32_ref,
               *, band_last_only):
    x = x_ref[...]
    g, r, d = x.shape
    ms = jnp.mean(x * x, axis=-1, keepdims=True)
    h = x * lax.rsqrt(ms + EPS) * g1_ref[...] * (1.0 + sc_ref[...]) + sh_ref[...]
    hb = h.reshape(g * r, d).astype(BF16)

    def seg(i):
        return _dot(hb, w_ref[:, i * WIDTH:(i + 1) * WIDTH])

    def head_norm(t, gain_ref):
        ssq = _dot((t * t).astype(BF16), bd_ref[...])
        return t * lax.rsqrt(ssq + EPS) * gain_ref[...]

    qf_ref[...] = head_norm(seg(0), gqf_ref).astype(BF16)
    kf = head_norm(seg(1), gkf_ref)
    kf32_ref[...] = kf
    kf_ref[...] = kf.astype(BF16)
    vf = seg(2)
    vf32_ref[...] = vf
    if band_last_only:
        vf_ref[...] = vf.T.astype(BF16)
    else:
        vf_ref[...] = vf.astype(BF16)
    z = _dot(hb, w_ref[:, 6 * WIDTH:6 * WIDTH + LANES]) + bf_ref[...]
    lf_ref[...] = _log_sigmoid(z)
    qb_ref[...] = head_norm(seg(3), gqb_ref).astype(BF16)
    kb = head_norm(seg(4), gkb_ref)
    kb_ref[...] = kb.astype(BF16)
    vb = seg(5)
    if band_last_only:
        vb_ref[...] = vb.T.astype(BF16)
    else:
        vb_ref[...] = vb.astype(BF16)

    if band_last_only:
        @pl.when(pl.program_id(1) == pl.num_programs(1) - 1)
        def _():
            kb32_ref[...] = kb
            vb32_ref[...] = vb
    else:
        kb32_ref[...] = kb
        vb32_ref[...] = vb


def _proj(x, shift, scale, g1, w_all, bd, gqf, gkf, gqb, gkb, bf_row, *, G, R, band_last_only):
    nb, s, d = x.shape
    n = nb * s
    tm = G * R
    nbi, nsi = nb // G, s // R
    grid = (nbi, nsi)
    row = lambda b, i: (b * nsi + i, 0)
    const = lambda b, i: (0, 0)
    mod_spec = pl.BlockSpec((G, 1, d), lambda b, i: (b, 0, 0))
    out_bf = jax.ShapeDtypeStruct((n, WIDTH), BF16)
    out_f32 = jax.ShapeDtypeStruct((n, WIDTH), F32)
    tile = pl.BlockSpec((tm, WIDTH), row)
    if band_last_only:
        assert G == 1 and R == BAND_REACH
        band_shape = jax.ShapeDtypeStruct((nb, BAND_REACH, WIDTH), F32)
        band_spec = pl.BlockSpec((None, BAND_REACH, WIDTH), lambda b, i: (b, 0, 0))
        v_shape = jax.ShapeDtypeStruct((nb, nsi, WIDTH, tm), BF16)
        v_spec = pl.BlockSpec((None, None, WIDTH, tm), lambda b, i: (b, i, 0, 0))
    else:
        band_shape, band_spec = out_f32, tile
        v_shape, v_spec = out_bf, tile
    return pl.pallas_call(
        functools.partial(_proj_body, band_last_only=band_last_only),
        grid=grid,
        in_specs=[pl.BlockSpec((G, R, d), lambda b, i: (b, i, 0)), mod_spec, mod_spec,
                  pl.BlockSpec((1, d), const), pl.BlockSpec(w_all.shape, const), pl.BlockSpec(bd.shape, const),
                  pl.BlockSpec((1, WIDTH), const), pl.BlockSpec((1, WIDTH), const),
                  pl.BlockSpec((1, WIDTH), const), pl.BlockSpec((1, WIDTH), const),
                  pl.BlockSpec((1, LANES), const)],
        out_specs=[tile, tile, v_spec, tile, tile, pl.BlockSpec((tm, LANES), row), tile, tile, v_spec,
                   band_spec, band_spec],
        out_shape=[out_bf, out_bf, v_shape, out_f32, out_f32, jax.ShapeDtypeStruct((n, LANES), F32),
                   out_bf, out_bf, v_shape, band_shape, band_shape],
        compiler_params=_cparams(2),
        name="proj",
    )(x, shift, scale, g1, w_all, bd, gqf, gkf, gqb, gkb, bf_row)


def _scan_body(lf_ref, cum_ref, cumt_ref):
    lf = lf_ref[...]
    s = lf.shape[0]
    hi, mid, lo = _split3(lf)
    rr = lax.broadcasted_iota(jnp.int32, (s, s), 0)
    cc = lax.broadcasted_iota(jnp.int32, (s, s), 1)
    tri = jnp.where(cc <= rr, 1.0, 0.0).astype(BF16)
    cum2 = (_dot(tri, hi) + _dot(tri, mid) + _dot(tri, lo)) * LOG2E
    cum_ref[...] = cum2
    cumt_ref[...] = cum2.T


def _scan(lf):
    s, _ = lf.shape
    return pl.pallas_call(
        _scan_body,
        out_shape=[jax.ShapeDtypeStruct((s, LANES), F32), jax.ShapeDtypeStruct((LANES, s), F32)],
        compiler_params=pltpu.CompilerParams(vmem_limit_bytes=VMEM_LIMIT),
        name="scan",
    )(lf)


def _fox_body(q_ref, k_ref, v_ref, cq_ref, ck_ref, o_ref, *, TQ, TK, q_off):
    p = pl.program_id(1)
    i = pl.program_id(2)
    q = q_ref[...]
    cq_blk = cq_ref[...]
    lane = lax.broadcasted_iota(jnp.int32, (TQ, PAIR), 1)
    ones_blk = jnp.where(lax.broadcasted_iota(jnp.int32, (TK, LANES), 1) == 0, 1.0, 0.0).astype(BF16)
    q0 = q_off + i * TQ
    n_full = q0 // TK
    qpos = q0 + lax.broadcasted_iota(jnp.int32, (TQ, TK), 0)
    kcol = lax.broadcasted_iota(jnp.int32, (TQ, TK), 1)

    outs = []
    for par in range(2):
        h = 2 * p + par
        qm = jnp.where((lane >= HEAD_DIM) == (par == 1), q, jnp.zeros_like(q))
        cq_col = jnp.sum(jnp.where(lane == h, cq_blk, 0.0), axis=1, keepdims=True)
        ref0 = cq_col[0:1, :]
        cqr = cq_col - ref0

        def step(j, carry, masked):
            m, l, acc = carry
            k0 = pl.multiple_of(j * TK, TK)
            kb = k_ref[pl.ds(k0, TK), :]
            vb = v_ref[pl.ds(k0, TK), :]
            s = _dot_nt(qm, kb)
            ck = ck_ref[j, pl.ds(h, 1), :]
            u = s - (ck - ref0)
            if masked:
                u = jnp.where(k0 + kcol <= qpos, u, NEG_INF)
            m_new = jnp.maximum(m, jnp.max(u, axis=1, keepdims=True) + cqr)
            alpha = jnp.exp2(m - m_new)
            pexp = jnp.exp2(u + (cqr - m_new))
            pv = _dot(pexp.astype(BF16), jnp.concatenate([vb, ones_blk], axis=1))
            return m_new, alpha * l + pv[:, LANES:LANES + 1], alpha * acc + pv[:, :LANES]

        init = (jnp.full((TQ, 1), NEG_INF, F32), jnp.zeros((TQ, 1), F32), jnp.zeros((TQ, LANES), F32))
        carry = lax.fori_loop(0, n_full, lambda j, c: step(j, c, False), init)
        _, l, acc = step(n_full, carry, True)
        outs.append(acc / l)
    o_ref[...] = jnp.where(lane < HEAD_DIM, outs[0], outs[1]).astype(o_ref.dtype)


def _fox(q, k, v, cum, cumt, *, TQ, TK, q_off):
    b, sq, _ = q.shape
    sk = k.shape[1]
    return pl.pallas_call(
        functools.partial(_fox_body, TQ=TQ, TK=TK, q_off=q_off),
        grid=(b, N_PAIRS, sq // TQ),
        in_specs=[pl.BlockSpec((None, TQ, PAIR), lambda bi, p, i: (bi, i, p)),
                  pl.BlockSpec((None, sk, PAIR), lambda bi, p, i: (bi, 0, p)),
                  pl.BlockSpec((None, sk, PAIR), lambda bi, p, i: (bi, 0, p)),
                  pl.BlockSpec((None, TQ, LANES), lambda bi, p, i: (bi, i, 0)),
                  pl.BlockSpec((None, sk // TK, N_HEADS, TK), lambda bi, p, i: (bi, 0, 0, 0))],
        out_specs=pl.BlockSpec((None, TQ, PAIR), lambda bi, p, i: (bi, i, p)),
        out_shape=jax.ShapeDtypeStruct((b, sq, WIDTH), BF16),
        compiler_params=_cparams(3),
        name="fox",
    )(q, k, v, cum, cumt)


AUG_PIECES = 3
FOX_UNDERFLOW = 160.0
FOX_NORM_SLACK = 1.02
FOX_BOUND_SLACK = 2.0


def _scan_t_body(lf_ref, place_ref, ct_ref, ka_ref, carry_ref):
    @pl.when(pl.program_id(1) == 0)
    def _():
        carry_ref[...] = jnp.zeros_like(carry_ref)

    lf = lf_ref[...]
    ts = lf.shape[0]
    lane = lax.broadcasted_iota(jnp.int32, lf.shape, 1)
    lf = jnp.where(lane < N_HEADS, lf, 0.0)
    hi, mid, lo = _split3(lf)
    rr = lax.broadcasted_iota(jnp.int32, (ts, ts), 0)
    cc = lax.broadcasted_iota(jnp.int32, (ts, ts), 1)
    tri = jnp.where(cc <= rr, 1.0, 0.0).astype(BF16)
    cum = _dot(tri, hi) + _dot(tri, mid) + _dot(tri, lo) + carry_ref[0:1, :]
    carry_ref[...] = jnp.broadcast_to(cum[ts - 1:ts, :], carry_ref.shape)
    cum2 = cum * LOG2E
    ct_ref[...] = cum2.T[0:N_HEADS, :]
    pieces = _split3(cum2 - cum2[0:1, :])
    ka = _dot(pieces[0], place_ref[0]) + _dot(pieces[1], place_ref[1]) + _dot(pieces[2], place_ref[2])
    ka_ref[...] = ka.astype(BF16)


def _aug_placement():
    h = jnp.arange(LANES)[:, None]
    col = jnp.arange(WIDTH)[None, :]
    mats = []
    for x in range(AUG_PIECES):
        tgt = PAIR * (h // 2) + AUG_PIECES * (h % 2) + x
        mats.append(jnp.where((h < N_HEADS) & (col == tgt), 1.0, 0.0))
    return jnp.stack(mats).astype(BF16)


def _scan_t(lf, *, T):
    b, s, _ = lf.shape
    place = _aug_placement()
    return pl.pallas_call(
        _scan_t_body,
        grid=(b, s // T),
        in_specs=[pl.BlockSpec((None, T, LANES), lambda bi, i: (bi, i, 0)),
                  pl.BlockSpec(place.shape, lambda bi, i: (0, 0, 0))],
        out_specs=[pl.BlockSpec((None, None, N_HEADS, T), lambda bi, i: (bi, i, 0, 0)),
                   pl.BlockSpec((None, T, WIDTH), lambda bi, i: (bi, i, 0))],
        out_shape=[jax.ShapeDtypeStruct((b, s // T, N_HEADS, T), F32),
                   jax.ShapeDtypeStruct((b, s, WIDTH), BF16)],
        scratch_shapes=[pltpu.VMEM((8, LANES), F32)],
        compiler_params=_cparams(2),
        name="scan_t",
    )(lf, place)


def _foxt_body(q_ref, k_ref, ka_ref, vt_ref, ct_ref, o_ref, kn_ref, *, T):
    p = pl.program_id(1)
    i = pl.program_id(2)
    q = q_ref[...]
    lane = lax.broadcasted_iota(jnp.int32, (T, PAIR), 1)
    halves = []
    for par in range(2):
        qm = jnp.where((lane >= HEAD_DIM) == (par == 1), q, jnp.zeros_like(q))
        lo_lane = AUG_PIECES * par
        qa = jnp.where((lane >= lo_lane) & (lane < lo_lane + AUG_PIECES), -1.0, 0.0).astype(BF16)
        halves.append(jnp.concatenate([qm, qa], axis=1))
    qcat = jnp.concatenate(halves, axis=0)
    h_even = 2 * p
    cq = jnp.concatenate([ct_ref[i, pl.ds(h_even, 1), :], ct_ref[i, pl.ds(h_even + 1, 1), :]], axis=1)
    ones_rows = jnp.ones((16, T), BF16)
    krow = lax.broadcasted_iota(jnp.int32, (T, 2 * T), 0)
    qcol = lax.broadcasted_iota(jnp.int32, (T, 2 * T), 1) % T

    def step(j, carry, masked):
        m, acc_e, acc_o = carry
        k0 = pl.multiple_of(j * T, T)
        kcat = jnp.concatenate([k_ref[pl.ds(k0, T), :], ka_ref[pl.ds(k0, T), :]], axis=1)
        st = _dot_nt(kcat, qcat)
        c0 = jnp.concatenate([jnp.broadcast_to(ct_ref[j, pl.ds(h_even, 1), :][:, 0:1], (1, T)),
                              jnp.broadcast_to(ct_ref[j, pl.ds(h_even + 1, 1), :][:, 0:1], (1, T))], axis=1)
        rb = cq - c0
        if masked:
            st = jnp.where(krow <= qcol, st, NEG_INF)
        m_new = jnp.maximum(m, jnp.max(st, axis=0, keepdims=True) + rb)
        alpha = jnp.exp2(m - m_new)
        pt = jnp.exp2(st + (rb - m_new)).astype(BF16)
        vt = vt_ref[j]
        pv_e = _dot(jnp.concatenate([vt[0:HEAD_DIM], ones_rows], axis=0), pt[:, 0:T])
        pv_o = _dot(jnp.concatenate([vt[HEAD_DIM:PAIR], ones_rows], axis=0), pt[:, T:2 * T])
        return m_new, alpha[:, 0:T] * acc_e + pv_e, alpha[:, T:2 * T] * acc_o + pv_o

    @pl.when(i == 0)
    def _():
        ones = jnp.ones((PAIR, LANES), BF16)
        kmax = jnp.zeros((1, LANES), F32)
        for c in range(k_ref.shape[0] // T):
            kc = k_ref[c * T:(c + 1) * T, :].astype(F32)
            kmax = jnp.maximum(kmax, jnp.max(_dot((kc * kc).astype(BF16), ones), axis=0, keepdims=True))
        kn_ref[...] = jnp.broadcast_to(kmax, kn_ref.shape)

    rows = HEAD_DIM + 16
    init = (jnp.full((1, 2 * T), NEG_INF, F32), jnp.zeros((rows, T), F32), jnp.zeros((rows, T), F32))
    carry = step(i, init, True)

    qf = q.astype(F32)
    qsq = qf * qf
    kn2 = kn_ref[0:1, 0:1] * FOX_NORM_SLACK
    need = jnp.zeros((1, 1), jnp.int32)
    blk = lax.broadcasted_iota(jnp.int32, (ct_ref.shape[0], 1, 1), 0)
    for par in range(2):
        head_lanes = (lane >= HEAD_DIM) == (par == 1)
        qn2 = jnp.max(jnp.sum(jnp.where(head_lanes, qsq, 0.0), axis=1, keepdims=True), axis=0, keepdims=True)
        reach = jnp.sqrt(qn2 * kn2) + FOX_BOUND_SLACK
        m_min = jnp.min(carry[0][:, par * T:(par + 1) * T], axis=1, keepdims=True)
        cq_first = cq[:, par * T:par * T + 1]
        ck_end = ct_ref[:, pl.ds(h_even + par, 1), :][:, :, T - 1:T]
        live = (reach + cq_first - m_min)[None, :, :] - ck_end > -FOX_UNDERFLOW
        count = jnp.sum(jnp.where(live & (blk < i), 1, 0), axis=0)
        need = jnp.maximum(need, count)
    n_keep = need[0, 0]

    n_pairs = n_keep // 2
    carry = lax.fori_loop(0, n_pairs, lambda u, c: step(i - 2 - 2 * u, step(i - 1 - 2 * u, c, False), False), carry)
    carry = lax.fori_loop(2 * n_pairs, n_keep, lambda u, c: step(i - 1 - u, c, False), carry)
    _, acc_e, acc_o = carry
    o_t = jnp.concatenate([acc_e[0:HEAD_DIM] / acc_e[HEAD_DIM:HEAD_DIM + 1],
                           acc_o[0:HEAD_DIM] / acc_o[HEAD_DIM:HEAD_DIM + 1]], axis=0)
    o_ref[...] = o_t.T.astype(o_ref.dtype)


def _foxt(q, k, ka, vt, ct, *, T):
    b, s, _ = q.shape
    nt = s // T
    return pl.pallas_call(
        functools.partial(_foxt_body, T=T),
        grid=(b, N_PAIRS, nt),
        in_specs=[pl.BlockSpec((None, T, PAIR), lambda bi, p, i: (bi, i, p)),
                  pl.BlockSpec((None, s, PAIR), lambda bi, p, i: (bi, 0, p)),
                  pl.BlockSpec((None, s, PAIR), lambda bi, p, i: (bi, 0, p)),
                  pl.BlockSpec((None, nt, PAIR, T), lambda bi, p, i: (bi, 0, p, 0)),
                  pl.BlockSpec((None, nt, N_HEADS, T), lambda bi, p, i: (bi, 0, 0, 0))],
        out_specs=pl.BlockSpec((None, T, PAIR), lambda bi, p, i: (bi, i, p)),
        out_shape=jax.ShapeDtypeStruct((b, s, WIDTH), BF16),
        scratch_shapes=[pltpu.VMEM((8, LANES), F32)],
        compiler_params=_cparams(3),
        name="foxt",
    )(q, k, ka, vt, ct)


def _band_body(q_ref, k_ref, v_ref, bias_ref, o_ref, *scratch, TQ, W, n_sub, padded):
    p = pl.program_id(1)
    i = pl.program_id(2)
    if padded:
        kpad_ref, vpad_ref = scratch
        s_len = k_ref.shape[0]

        @pl.when(i == 0)
        def _():
            zeros = jnp.zeros((BAND_REACH, PAIR), BF16)
            kpad_ref[pl.ds(0, BAND_REACH), :] = zeros
            vpad_ref[pl.ds(0, BAND_REACH), :] = zeros
            kpad_ref[pl.ds(BAND_REACH, s_len), :] = k_ref[...]
            vpad_ref[pl.ds(BAND_REACH, s_len), :] = v_ref[...]
    else:
        kpad_ref, vpad_ref = k_ref, v_ref

    lane = lax.broadcasted_iota(jnp.int32, (TQ, PAIR), 1)
    ones_blk = jnp.where(lax.broadcasted_iota(jnp.int32, (W, LANES), 1) == 0, 1.0, 0.0).astype(BF16)
    kcol = lax.broadcasted_iota(jnp.int32, (TQ, W), 1)

    def sub_block(sub, carry):
        r0 = pl.multiple_of(sub * TQ, TQ)
        q0 = i * (n_sub * TQ) + r0
        q = q_ref[pl.ds(r0, TQ), :]
        kw = kpad_ref[pl.ds(pl.multiple_of(q0, TQ), W), :] if padded else kpad_ref[...]
        vw = vpad_ref[pl.ds(pl.multiple_of(q0, TQ), W), :] if padded else vpad_ref[...]
        vcat = jnp.concatenate([vw, ones_blk], axis=1)
        outs = []
        for par in range(2):
            h = 2 * p + par
            qm = jnp.where((lane >= HEAD_DIM) == (par == 1), q, jnp.zeros_like(q))
            s = _dot_nt(qm, kw) + bias_ref[h]
            if padded:
                s = jnp.where(kcol >= BAND_REACH - q0, s, NEG_INF)
            m = jnp.max(s, axis=1, keepdims=True)
            pexp = jnp.exp2(s - m)
            pv = _dot(pexp.astype(BF16), vcat)
            outs.append(pv[:, :LANES] / pv[:, LANES:LANES + 1])
        o_ref[pl.ds(r0, TQ), :] = jnp.where(lane < HEAD_DIM, outs[0], outs[1]).astype(o_ref.dtype)
        return carry

    lax.fori_loop(0, n_sub, sub_block, 0)


def _band(q, k, v, bias, *, TQ, n_sub, padded):
    b, sq, _ = q.shape
    sk = k.shape[1]
    w = BAND_REACH + TQ if padded else sk
    tqb = TQ * n_sub
    scratch = [pltpu.VMEM((sk + BAND_REACH, PAIR), BF16)] * 2 if padded else []
    return pl.pallas_call(
        functools.partial(_band_body, TQ=TQ, W=w, n_sub=n_sub, padded=padded),
        grid=(b, N_PAIRS, sq // tqb),
        in_specs=[pl.BlockSpec((None, tqb, PAIR), lambda bi, p, i: (bi, i, p)),
                  pl.BlockSpec((None, sk, PAIR), lambda bi, p, i: (bi, 0, p)),
                  pl.BlockSpec((None, sk, PAIR), lambda bi, p, i: (bi, 0, p)),
                  pl.BlockSpec(bias.shape, lambda bi, p, i: (0, 0, 0))],
        out_specs=pl.BlockSpec((None, tqb, PAIR), lambda bi, p, i: (bi, i, p)),
        out_shape=jax.ShapeDtypeStruct((b, sq, WIDTH), BF16),
        scratch_shapes=scratch,
        compiler_params=_cparams(3),
        name="band",
    )(q, k, v, bias)


BAND_TQ = 256
BAND_STEP = 512


def _bandt_body(q_ref, kp_ref, kc_ref, vp_ref, vc_ref, bias_ref, o_ref):
    i = pl.program_id(2)
    w = BAND_REACH + BAND_TQ
    k2 = jnp.concatenate([kp_ref[...], kc_ref[...]], axis=0)
    vt2 = jnp.concatenate([vp_ref[...], vc_ref[...]], axis=1)
    lane = lax.broadcasted_iota(jnp.int32, (BAND_TQ, PAIR), 1)
    ones_rows = jnp.ones((16, w), BF16)
    krow = lax.broadcasted_iota(jnp.int32, (w, 2 * BAND_TQ), 0)
    bias = bias_ref[...]
    for sub in range(BAND_STEP // BAND_TQ):
        r0 = sub * BAND_TQ
        q = q_ref[r0:r0 + BAND_TQ, :]
        qcat = jnp.concatenate([jnp.where(lane < HEAD_DIM, q, jnp.zeros_like(q)),
                                jnp.where(lane >= HEAD_DIM, q, jnp.zeros_like(q))], axis=0)
        st = _dot_nt(k2[r0:r0 + w], qcat) + bias
        st = jnp.where(krow >= (1 - i) * BAND_STEP - r0, st, NEG_INF)
        m = jnp.max(st, axis=0, keepdims=True)
        pt = jnp.exp2(st - m).astype(BF16)
        vwin = vt2[:, r0:r0 + w]
        outs = []
        for par in range(2):
            vcat = jnp.concatenate([vwin[par * HEAD_DIM:(par + 1) * HEAD_DIM], ones_rows], axis=0)
            pv = _dot(vcat, pt[:, par * BAND_TQ:(par + 1) * BAND_TQ])
            outs.append(pv[0:HEAD_DIM] / pv[HEAD_DIM:HEAD_DIM + 1])
        o_ref[r0:r0 + BAND_TQ, :] = jnp.concatenate(outs, axis=0).T.astype(o_ref.dtype)


def _bandt(q, k, vt, bias_t):
    b, s, _ = q.shape
    prev = lambda i: jnp.maximum(i - 1, 0)
    return pl.pallas_call(
        _bandt_body,
        grid=(b, N_PAIRS, s // BAND_STEP),
        in_specs=[pl.BlockSpec((None, BAND_STEP, PAIR), lambda bi, p, i: (bi, i, p)),
                  pl.BlockSpec((None, BAND_STEP, PAIR), lambda bi, p, i: (bi, prev(i), p)),
                  pl.BlockSpec((None, BAND_STEP, PAIR), lambda bi, p, i: (bi, i, p)),
                  pl.BlockSpec((None, None, PAIR, BAND_STEP), lambda bi, p, i: (bi, prev(i), p, 0)),
                  pl.BlockSpec((None, None, PAIR, BAND_STEP), lambda bi, p, i: (bi, i, p, 0)),
                  pl.BlockSpec((None,) + bias_t.shape[1:], lambda bi, p, i: (p, 0, 0))],
        out_specs=pl.BlockSpec((None, BAND_STEP, PAIR), lambda bi, p, i: (bi, i, p)),
        out_shape=jax.ShapeDtypeStruct((b, s, WIDTH), BF16),
        compiler_params=_cparams(3),
        name="bandt",
    )(q, k, k, vt, vt, bias_t)


def _band_bias_tile_t(rel_bias):
    tile = _band_bias_tile(rel_bias, BAND_TQ, BAND_REACH + BAND_TQ)
    t = jnp.swapaxes(tile, 1, 2)
    return jnp.concatenate([t[0::2], t[1::2]], axis=2)


def _band_bias_tile(rel_bias, tq, w):
    span = w + tq - 1
    period = span + 1
    v = np.arange(period)
    d = np.where(v < w, v, v - period)
    table_idx = np.clip(BAND_REACH - d, -REL_CLIP, REL_CLIP) + REL_CLIP
    table = rel_bias[:, table_idx] * LOG2E
    n_h = rel_bias.shape[0]
    vals = jnp.tile(table, (1, tq))[:, :tq * span].reshape(n_h, tq, span)[:, :, :w]
    r = np.arange(tq)[:, None]
    c = np.arange(w)[None, :]
    in_band = (c // CHUNK >= r // CHUNK) & (c // CHUNK <= r // CHUNK + BAND_REACH // CHUNK)
    return jnp.where(jnp.asarray(in_band)[None], vals, NEG_INF).astype(F32)


def _sample_attn_body(q_ref, kc_ref, vc_ref, kn_ref, vn_ref, *rest, fox):
    p = pl.program_id(1)
    t = q_ref.shape[0]
    past = kc_ref.shape[1]
    lane = lax.broadcasted_iota(jnp.int32, (t, PAIR), 1)
    q = q_ref[...]
    qs = jnp.concatenate([jnp.where(lane < HEAD_DIM, q, jnp.zeros_like(q)),
                          jnp.where(lane >= HEAD_DIM, q, jnp.zeros_like(q))], axis=0)
    kc = kc_ref[...].astype(BF16)
    vc = vc_ref[...].astype(BF16)
    s_c = _dot(qs, kc)
    s_n = _dot_nt(qs, kn_ref[...])
    row = lax.broadcasted_iota(jnp.int32, (t, t), 0)
    col = lax.broadcasted_iota(jnp.int32, (t, t), 1)
    if fox:
        cq_ref, ck_ref, o_ref = rest
    else:
        bias_ref, o_ref = rest
    pcs, pns = [], []
    for par in range(2):
        h = 2 * p + par
        sc = s_c[par * t:(par + 1) * t]
        sn = s_n[par * t:(par + 1) * t]
        if fox:
            cq_col = jnp.sum(jnp.where(lane == h, cq_ref[...], 0.0), axis=1, keepdims=True)
            ck = ck_ref[pl.ds(h, 1), :]
            uc = sc + (cq_col - ck[:, 0:past])
            un = jnp.where(col <= row, sn + (cq_col - ck[:, past:past + t]), NEG_INF)
        else:
            bias = bias_ref[h]
            uc = sc + bias[:, 0:past]
            un = sn + bias[:, past:past + t]
        m = jnp.maximum(jnp.max(uc, axis=1, keepdims=True), jnp.max(un, axis=1, keepdims=True))
        pc = jnp.exp2(uc - m)
        pn = jnp.exp2(un - m)
        inv = 1.0 / (jnp.sum(pc, axis=1, keepdims=True) + jnp.sum(pn, axis=1, keepdims=True))
        pcs.append((pc * inv).astype(BF16))
        pns.append((pn * inv).astype(BF16))
    o2 = _dot_nt(jnp.concatenate(pcs, axis=0), vc) + _dot(jnp.concatenate(pns, axis=0), vn_ref[...])
    o_ref[...] = jnp.where(lane < HEAD_DIM, o2[0:t], o2[t:2 * t]).astype(o_ref.dtype)


def _sample_attn(q, kc_t, vc_t, k_new, v_new, *extra, fox):
    b, t, _ = q.shape
    past = kc_t.shape[3]
    pair_rows = pl.BlockSpec((None, t, PAIR), lambda bi, p: (bi, 0, p))
    cache = pl.BlockSpec((None, None, PAIR, past), lambda bi, p: (bi, p, 0, 0))
    if fox:
        cq, ck = extra
        extra_specs = [pl.BlockSpec((None, t, LANES), lambda bi, p: (bi, 0, 0)),
                       pl.BlockSpec((None, None, N_HEADS, ck.shape[3]), lambda bi, p: (bi, 0, 0, 0))]
    else:
        extra_specs = [pl.BlockSpec(extra[0].shape, lambda bi, p: (0, 0, 0))]
    return pl.pallas_call(
        functools.partial(_sample_attn_body, fox=fox),
        grid=(b, N_PAIRS),
        in_specs=[pair_rows, cache, cache, pair_rows, pair_rows] + extra_specs,
        out_specs=pair_rows,
        out_shape=jax.ShapeDtypeStruct((b, t, WIDTH), BF16),
        compiler_params=_cparams(2),
        name="sample_fox" if fox else "sample_band",
    )(q, kc_t, vc_t, k_new, v_new, *extra)


def _first_index(is_max, idx, axis, big):
    return jnp.min(jnp.where(is_max, idx, big), axis=axis, keepdims=True)


def _route(scores, choice):
    t = scores.shape[1]
    c3 = choice.reshape(N_GROUPS, GROUP_SIZE, t)
    j_idx = lax.broadcasted_iota(jnp.int32, c3.shape, 1)
    top1 = jnp.max(c3, axis=1, keepdims=True)
    first = _first_index(c3 == top1, j_idx, 1, GROUP_SIZE)
    top2 = jnp.max(jnp.where(j_idx == first, -jnp.inf, c3), axis=1, keepdims=True)
    gscore = (top1 + top2).reshape(N_GROUPS, t)

    g_idx = lax.broadcasted_iota(jnp.int32, gscore.shape, 0)
    gsel = jnp.zeros(gscore.shape, F32)
    work = gscore
    for _ in range(TOPK_GROUPS):
        gm = jnp.max(work, axis=0, keepdims=True)
        pick = g_idx == _first_index(work == gm, g_idx, 0, N_GROUPS)
        gsel = jnp.where(pick, 1.0, gsel)
        work = jnp.where(pick, -jnp.inf, work)

    emask = jnp.broadcast_to(gsel.reshape(N_GROUPS, 1, t), c3.shape) > 0.0
    work = jnp.where(emask, c3, NEG_INF)
    e_idx = lax.broadcasted_iota(jnp.int32, c3.shape, 0) * GROUP_SIZE + j_idx
    esel = jnp.zeros(c3.shape, F32)
    for _ in range(TOP_K):
        em = jnp.max(jnp.max(work, axis=1, keepdims=True), axis=0, keepdims=True)
        cand = jnp.where(work == em, e_idx, N_EXPERTS)
        first = jnp.min(jnp.min(cand, axis=1, keepdims=True), axis=0, keepdims=True)
        pick = e_idx == first
        esel = jnp.where(pick, 1.0, esel)
        work = jnp.where(pick, -jnp.inf, work)

    w = esel * scores.reshape(c3.shape)
    denom = jnp.sum(jnp.sum(w, axis=1, keepdims=True), axis=0, keepdims=True)
    return (w / denom * ROUTED_SCALE).reshape(N_EXPERTS, t)


def _merge_body(of_ref, ob_ref, x_ref, gate_ref, sh_ref, sc_ref, ogf_ref, ogb_ref, wo_ref, g2_ref,
                wrh_ref, wrl_ref, br_ref, x1_ref, h2_ref, comb_ref, combt_ref, cnt_ref):
    def group_norm(t_ref, gain_ref):
        t = t_ref[...].astype(F32)
        ms = jnp.mean(t * t, axis=-1, keepdims=True)
        return (t * lax.rsqrt(ms + EPS) * gain_ref[...]).astype(BF16)

    y = _dot(group_norm(of_ref, ogf_ref), wo_ref[0:WIDTH, :]) + _dot(group_norm(ob_ref, ogb_ref), wo_ref[WIDTH:, :])
    x = x_ref[...]
    g, r, d = x.shape
    x1 = x + gate_ref[...] * y.reshape(g, r, d)
    x1_ref[...] = x1
    ms = jnp.mean(x1 * x1, axis=-1, keepdims=True)
    h2 = (x1 * lax.rsqrt(ms + EPS) * g2_ref[...] * (1.0 + sc_ref[...]) + sh_ref[...]).reshape(g * r, d)
    h_hi, h_lo = _split2(h2)
    h2_ref[...] = h_hi
    logits = _dot_nt(wrh_ref[...], h_hi) + _dot_nt(wrh_ref[...], h_lo) + _dot_nt(wrl_ref[...], h_hi)
    scores = jax.nn.sigmoid(logits)
    t = scores.shape[1]
    bias = jnp.concatenate([br_ref[...]] * (t // LANES), axis=1)
    comb = _route(scores, scores + bias)
    comb_pad = jnp.concatenate([comb, jnp.zeros((LANES - N_EXPERTS, t), F32)], axis=0)
    comb_ref[...] = comb_pad.T
    combt_ref[...] = comb
    picked = jnp.where(comb > 0.0, 1.0, 0.0)
    for sub in range(cnt_ref.shape[0]):
        cnt = jnp.sum(picked[:, sub * MOE_TM:(sub + 1) * MOE_TM], axis=1, keepdims=True)
        cnt_ref[sub] = jnp.broadcast_to(cnt, (N_EXPERTS, LANES))


def _merge(of, ob, x, gate, shift, scale, ogf, ogb, wo, g2, wr_hi, wr_lo, br, *, G, R):
    nb, s, d = x.shape
    n = nb * s
    tm = G * R
    nbi, nsi = nb // G, s // R
    row = lambda b, i: (b * nsi + i, 0)
    const = lambda b, i: (0, 0)
    mod_spec = pl.BlockSpec((G, 1, d), lambda b, i: (b, 0, 0))
    x_spec = pl.BlockSpec((G, R, d), lambda b, i: (b, i, 0))
    return pl.pallas_call(
        _merge_body,
        grid=(nbi, nsi),
        in_specs=[pl.BlockSpec((tm, WIDTH), row), pl.BlockSpec((tm, WIDTH), row), x_spec,
                  mod_spec, mod_spec, mod_spec,
                  pl.BlockSpec((1, WIDTH), const), pl.BlockSpec((1, WIDTH), const),
                  pl.BlockSpec(wo.shape, const), pl.BlockSpec((1, d), const),
                  pl.BlockSpec(wr_hi.shape, const), pl.BlockSpec(wr_lo.shape, const),
                  pl.BlockSpec(br.shape, const)],
        out_specs=[x_spec, pl.BlockSpec((tm, d), row), pl.BlockSpec((tm, LANES), row),
                   pl.BlockSpec((N_EXPERTS, tm), lambda b, i: (0, b * nsi + i)),
                   pl.BlockSpec((tm // MOE_TM, N_EXPERTS, LANES), lambda b, i: (b * nsi + i, 0, 0))],
        out_shape=[jax.ShapeDtypeStruct((nb, s, d), F32), jax.ShapeDtypeStruct((n, d), BF16),
                   jax.ShapeDtypeStruct((n, LANES), F32), jax.ShapeDtypeStruct((N_EXPERTS, n), F32),
                   jax.ShapeDtypeStruct((n // MOE_TM, N_EXPERTS, LANES), F32)],
        compiler_params=_cparams(2),
        name="merge",
    )(of, ob, x, gate, shift, scale, ogf, ogb, wo, g2, wr_hi, wr_lo, br)


def _silu(g):
    return g * jax.nn.sigmoid(g)


def _moe_body(h_ref, comb_ref, x1_ref, gate_ref, wg_ref, wu_ref, wd_ref, sg_ref, su_ref, sd_ref, y_ref, acc_ref):
    e = pl.program_id(2)
    hb = h_ref[...]

    @pl.when(e == 0)
    def _():
        a = _silu(_dot(hb, sg_ref[...])) * _dot(hb, su_ref[...])
        acc_ref[...] = _dot(a.astype(BF16), sd_ref[...])

    comb = comb_ref[...]
    lane = lax.broadcasted_iota(jnp.int32, comb.shape, 1)
    c_e = jnp.sum(jnp.where(lane == e, comb, 0.0), axis=1, keepdims=True)
    a = _silu(_dot(hb, wg_ref[...])) * _dot(hb, wu_ref[...]) * c_e
    acc_ref[...] += _dot(a.astype(BF16), wd_ref[...])

    @pl.when(e == pl.num_programs(2) - 1)
    def _():
        x1 = x1_ref[...]
        g, r, d = x1.shape
        y_ref[...] = x1 + gate_ref[...] * acc_ref[...].reshape(g, r, d)


def _moe(h2, comb, x1, gate, wg, wu, wd, sg, su, sd, *, G, R):
    nb, s, d = x1.shape
    tm = G * R
    nbi, nsi = nb // G, s // R
    ff = wg.shape[2]
    row = lambda b, i, e: (b * nsi + i, 0)
    const = lambda b, i, e: (0, 0)
    x_spec = pl.BlockSpec((G, R, d), lambda b, i, e: (b, i, 0))
    return pl.pallas_call(
        _moe_body,
        grid=(nbi, nsi, N_EXPERTS),
        in_specs=[pl.BlockSpec((tm, d), row), pl.BlockSpec((tm, LANES), row), x_spec,
                  pl.BlockSpec((G, 1, d), lambda b, i, e: (b, 0, 0)),
                  pl.BlockSpec((None, d, ff), lambda b, i, e: (e, 0, 0)),
                  pl.BlockSpec((None, d, ff), lambda b, i, e: (e, 0, 0)),
                  pl.BlockSpec((None, ff, d), lambda b, i, e: (e, 0, 0)),
                  pl.BlockSpec(sg.shape, const), pl.BlockSpec(su.shape, const), pl.BlockSpec(sd.shape, const)],
        out_specs=x_spec,
        out_shape=jax.ShapeDtypeStruct((nb, s, d), F32),
        scratch_shapes=[pltpu.VMEM((tm, d), F32)],
        compiler_params=_cparams(3),
        name="moe",
    )(h2, comb, x1, gate, wg, wu, wd, sg, su, sd)


MOE_TM = 256
MOE_CH = 16
MOE_SLOTS = TOP_K * MOE_TM + N_EXPERTS * MOE_CH
MOE_NCHUNK = MOE_SLOTS // MOE_CH
MOE_PIECE = 512
MOE_NPIECE = MOE_SLOTS // MOE_PIECE
MOE_CPP = MOE_PIECE // MOE_CH
MOE_RB = 1024
TAU_RADIX = 64.0
assert MOE_SLOTS % MOE_PIECE == 0


def _moe_plan(cnt, n_tiles):
    pc = (cnt + MOE_CH - 1) // MOE_CH * MOE_CH
    off = jnp.cumsum(pc, axis=1) - pc
    end = off + pc
    n_used = (jnp.sum(pc, axis=1) // MOE_CH).astype(jnp.int32)
    tot = jnp.sum(pc, axis=0)
    reg = (tot + MOE_RB - 1) // MOE_RB * MOE_RB
    reg_end = jnp.cumsum(reg)
    reg_start = reg_end - reg
    dest_base = reg_start[None, :] + jnp.cumsum(pc, axis=0) - pc
    chunk_row = jnp.arange(MOE_NCHUNK, dtype=jnp.int32)[None, :] * MOE_CH
    in_group = (chunk_row[:, :, None] >= off[:, None, :]) & (chunk_row[:, :, None] < end[:, None, :])
    used = jnp.any(in_group, axis=2)
    cdest = chunk_row + jnp.sum(jnp.where(in_group, (dest_base - off)[:, None, :], 0), axis=2)
    worst_rows = TOP_K * MOE_TM * n_tiles + n_tiles * N_EXPERTS * (MOE_CH - 1) + N_EXPERTS * (MOE_RB - MOE_CH)
    r_max = -(-worst_rows // MOE_RB)
    parity = (jnp.arange(n_tiles, dtype=jnp.int32) % 2)[:, None]
    cdest_d = jnp.where(used, cdest, r_max * MOE_RB + parity * MOE_SLOTS + chunk_row).astype(jnp.int32)
    cdest_c = jnp.where(used, cdest, chunk_row).astype(jnp.int32)
    n_active = (reg_end[-1] // MOE_RB).astype(jnp.int32).reshape(1)
    tile_row = jnp.arange(r_max, dtype=jnp.int32) * MOE_RB
    tile_expert = jnp.minimum(jnp.sum((tile_row[:, None] >= reg_end[None, :]).astype(jnp.int32), axis=1),
                              N_EXPERTS - 1).astype(jnp.int32)
    in_region = (tile_row[:, None] >= reg_start[None, :]) & (tile_row[:, None] < reg_end[None, :])
    rows_end = jnp.sum(jnp.where(in_region, (reg_start + tot)[None, :], 0), axis=1)
    valid = jnp.clip(rows_end - tile_row, 0, MOE_RB).astype(jnp.int32)
    f = lambda a: a.astype(F32)
    zeros64 = jnp.zeros((n_tiles, N_EXPERTS), F32)
    row2 = lambda a: jnp.broadcast_to(jnp.concatenate([f(a), f(a)], axis=1)[:, None, :], (n_tiles, 8, LANES))
    col = lambda a: jnp.broadcast_to(f(a)[:, :, None], (n_tiles, N_EXPERTS, LANES))
    col128 = lambda a: jnp.broadcast_to(jnp.concatenate([f(a), zeros64], axis=1)[:, :, None], (n_tiles, LANES, LANES))
    row1 = lambda a: jnp.broadcast_to(jnp.concatenate([f(a), zeros64], axis=1)[:, None, :], (n_tiles, 8, LANES))
    return dict(n_used=n_used, cdest_d=cdest_d.reshape(n_tiles, 1, MOE_NCHUNK),
                cdest_c=cdest_c.reshape(n_tiles, 1, MOE_NCHUNK), r_max=r_max, n_active=n_active,
                tile_expert=tile_expert, valid=valid,
                off_row2=row2(off), end_row2=row2(end), off_col=col(off),
                off_row1=row1(off), off_col128=col128(off), end_col128=col128(end))


def _tau_pieces(sel, tau):
    tau = jnp.where(sel, tau, -1.0)
    hi = jnp.floor(tau * (1.0 / TAU_RADIX)) * TAU_RADIX
    return hi.astype(BF16), (tau - hi).astype(BF16)


def _dispatch_body(nused_ref, cdest_ref, h_ref, combt_ref, offcol_ref, offrow_ref, endrow_ref, sorted_ref,
                   buf_ref, sem):
    t = pl.program_id(0)
    n_used = nused_ref[t]
    tm = h_ref.shape[0]
    sel = combt_ref[...] > 0.0
    rr = lax.broadcasted_iota(jnp.int32, (tm, tm), 0)
    cc = lax.broadcasted_iota(jnp.int32, (tm, tm), 1)
    upper = jnp.where(rr < cc, 1.0, 0.0).astype(BF16)
    rank = _dot(jnp.where(sel, 1.0, 0.0).astype(BF16), upper)
    cols = jnp.concatenate([offcol_ref[...]] * (tm // LANES), axis=1)
    tau_hi, tau_lo = _tau_pieces(sel, cols + rank)
    taucat = jnp.concatenate([tau_hi, tau_lo], axis=0)
    off_row = offrow_ref[0:1, :]
    end_row = endrow_ref[0:1, :]
    hb = h_ref[...]

    slot = t % 2
    buf = buf_ref.at[slot]
    buf_prev = buf_ref.at[1 - slot]

    def start_piece(piece):
        chunks = range(piece * MOE_CPP, (piece + 1) * MOE_CPP)
        dests = [cdest_ref[0, c] for c in chunks]
        for c, row in zip(chunks, dests):
            dst = sorted_ref.at[pl.ds(pl.multiple_of(row, MOE_CH), MOE_CH), :]
            pltpu.make_async_copy(buf.at[pl.ds(c * MOE_CH, MOE_CH), :], dst, sem.at[slot]).start()

    def wait_piece(piece, which_buf, which_sem):
        rows = pl.ds(piece * MOE_PIECE, MOE_PIECE)
        pltpu.make_async_copy(which_buf.at[rows, :], sorted_ref.at[rows, :], which_sem).wait()

    for piece in range(MOE_NPIECE):
        @pl.when(piece * MOE_CPP < n_used)
        def _():
            if piece > 0:
                start_piece(piece - 1)
            base = piece * MOE_PIECE
            s_col = (base + lax.broadcasted_iota(jnp.int32, (MOE_PIECE, LANES), 0)).astype(F32)
            onehot = jnp.where((s_col >= off_row) & (s_col < end_row), 1.0, 0.0).astype(BF16)
            q = _dot(onehot, taucat)
            s_mat = (base + lax.broadcasted_iota(jnp.int32, (MOE_PIECE, tm), 0)).astype(F32)
            g = jnp.where(q == s_mat, 1.0, 0.0).astype(BF16)
            buf[pl.ds(base, MOE_PIECE), :] = _dot(g, hb).astype(BF16)

    last = (n_used - 1) // MOE_CPP
    for piece in range(MOE_NPIECE):
        @pl.when(piece == last)
        def _():
            start_piece(piece)

    n_prev = nused_ref[jnp.maximum(t - 1, 0)]
    for piece in range(MOE_NPIECE):
        @pl.when((t > 0) & (piece * MOE_CPP < n_prev))
        def _():
            wait_piece(piece, buf_prev, sem.at[1 - slot])

    for piece in range(MOE_NPIECE):
        @pl.when((t == pl.num_programs(0) - 1) & (piece * MOE_CPP < n_used))
        def _():
            wait_piece(piece, buf, sem.at[slot])


def _dispatch(h2, combt, plan, n_tiles):
    n, d = h2.shape
    r_total = plan["r_max"] * MOE_RB + 2 * MOE_SLOTS
    grid_spec = pltpu.PrefetchScalarGridSpec(
        num_scalar_prefetch=1,
        grid=(n_tiles,),
        in_specs=[pl.BlockSpec((None, 1, MOE_NCHUNK), lambda t, nu: (t, 0, 0), memory_space=pltpu.SMEM),
                  pl.BlockSpec((MOE_TM, d), lambda t, nu: (t, 0)),
                  pl.BlockSpec((N_EXPERTS, MOE_TM), lambda t, nu: (0, t)),
                  pl.BlockSpec((None, N_EXPERTS, LANES), lambda t, nu: (t, 0, 0)),
                  pl.BlockSpec((None, 8, LANES), lambda t, nu: (t, 0, 0)),
                  pl.BlockSpec((None, 8, LANES), lambda t, nu: (t, 0, 0))],
        out_specs=pl.BlockSpec(memory_space=pl.ANY),
        scratch_shapes=[pltpu.VMEM((2, MOE_SLOTS, d), BF16), pltpu.SemaphoreType.DMA((2,))],
    )
    return pl.pallas_call(
        _dispatch_body,
        grid_spec=grid_spec,
        out_shape=jax.ShapeDtypeStruct((r_total, d), BF16),
        compiler_params=_cparams(1),
        name="moe_dispatch",
    )(plan["n_used"], plan["cdest_d"], h2, combt, plan["off_col"], plan["off_row2"], plan["end_row2"])


def _ffn_body(texp_ref, nact_ref, valid_ref, x_ref, wg_ref, wu_ref, wd_ref, o_ref):
    r = pl.program_id(0)

    @pl.when(r < nact_ref[0])
    def _():
        x = x_ref[...]
        rows = lax.broadcasted_iota(jnp.int32, x.shape, 0)
        x = jnp.where(rows < valid_ref[r], x, jnp.zeros_like(x))
        a = _silu(_dot(x, wg_ref[...])) * _dot(x, wu_ref[...])
        o_ref[...] = _dot(a.astype(BF16), wd_ref[...]).astype(o_ref.dtype)


def _ffn(xs, wg, wu, wd, plan):
    r_total, d = xs.shape
    ff = wg.shape[2]
    last = lambda r, te, na, va: (jnp.minimum(r, na[0] - 1), 0)
    wmap = lambda r, te, na, va: (te[r], 0, 0)
    grid_spec = pltpu.PrefetchScalarGridSpec(
        num_scalar_prefetch=3,
        grid=(plan["r_max"],),
        in_specs=[pl.BlockSpec((MOE_RB, d), last),
                  pl.BlockSpec((None, d, ff), wmap), pl.BlockSpec((None, d, ff), wmap),
                  pl.BlockSpec((None, ff, d), wmap)],
        out_specs=pl.BlockSpec((MOE_RB, d), last),
    )
    return pl.pallas_call(
        _ffn_body,
        grid_spec=grid_spec,
        out_shape=jax.ShapeDtypeStruct((r_total, d), BF16),
        compiler_params=_cparams(1),
        name="moe_ffn",
    )(plan["tile_expert"], plan["n_active"], plan["valid"], xs, wg, wu, wd)


def _combine_body(nused_ref, cdest_ref, cnext_ref, h_ref, comb_ref, x1_ref, gate_ref, offrow_ref, offcol_ref,
                  endcol_ref, sg_ref, su_ref, sd_ref, ys_ref, y_ref, buf_ref, sem):
    t = pl.program_id(0)
    n_tiles = pl.num_programs(0)
    n_used = nused_ref[t]
    tm = h_ref.shape[0]
    slot = t % 2
    buf = buf_ref.at[slot]

    def fetch_tile(table_ref, count, which):
        for piece in range(MOE_NPIECE):
            @pl.when(piece * MOE_CPP < count)
            def _():
                chunks = range(piece * MOE_CPP, (piece + 1) * MOE_CPP)
                srcs = [table_ref[0, c] for c in chunks]
                for c, row in zip(chunks, srcs):
                    src = ys_ref.at[pl.ds(pl.multiple_of(row, MOE_CH), MOE_CH), :]
                    dst = buf_ref.at[which, pl.ds(c * MOE_CH, MOE_CH), :]
                    pltpu.make_async_copy(src, dst, sem.at[which, piece]).start()

    def wait_piece(piece):
        rows = pl.ds(piece * MOE_PIECE, MOE_PIECE)
        pltpu.make_async_copy(ys_ref.at[rows, :], buf.at[rows, :], sem.at[slot, piece]).wait()

    @pl.when(t == 0)
    def _():
        buf_ref[...] = jnp.zeros_like(buf_ref)
        fetch_tile(cdest_ref, n_used, 0)

    @pl.when(t + 1 < n_tiles)
    def _():
        fetch_tile(cnext_ref, nused_ref[jnp.minimum(t + 1, n_tiles - 1)], 1 - slot)

    hb = h_ref[...]
    acc = _dot((_silu(_dot(hb, sg_ref[...])) * _dot(hb, su_ref[...])).astype(BF16), sd_ref[...])
    comb = comb_ref[...]
    sel = comb > 0.0
    rr = lax.broadcasted_iota(jnp.int32, (tm, tm), 0)
    cc = lax.broadcasted_iota(jnp.int32, (tm, tm), 1)
    lower = jnp.where(cc < rr, 1.0, 0.0).astype(BF16)
    rank = _dot(lower, jnp.where(sel, 1.0, 0.0).astype(BF16))
    tau_hi, tau_lo = _tau_pieces(sel, offrow_ref[0:1, :] + rank)
    taucat = jnp.concatenate([tau_hi, tau_lo], axis=1)
    lhs = jnp.concatenate([taucat, jnp.concatenate([comb.astype(BF16), jnp.zeros((tm, LANES), BF16)], axis=1)],
                          axis=0)
    off_col = offcol_ref[...]
    end_col = endcol_ref[...]
    s_row = lax.broadcasted_iota(jnp.int32, (LANES, MOE_SLOTS), 1).astype(F32)
    off_b = jnp.concatenate([off_col] * (MOE_SLOTS // LANES), axis=1)
    end_b = jnp.concatenate([end_col] * (MOE_SLOTS // LANES), axis=1)
    onehot = jnp.where((s_row >= off_b) & (s_row < end_b), 1.0, 0.0).astype(BF16)
    qw = _dot(lhs, jnp.concatenate([onehot, onehot], axis=0))
    s_mat = lax.broadcasted_iota(jnp.int32, (tm, MOE_SLOTS), 1).astype(F32)
    gw = jnp.where(qw[0:tm] == s_mat, qw[tm:2 * tm], 0.0).astype(BF16)

    for piece in range(MOE_NPIECE):
        @pl.when(piece * MOE_CPP < n_used)
        def _():
            wait_piece(piece)

    y_ref[...] = x1_ref[...] + gate_ref[...] * (acc + _dot(gw, buf[...]))


def _combine(h2, comb, x1, gate, ys, sg, su, sd, plan, n_tiles):
    nb, s, d = x1.shape
    n = nb * s
    per_b = s // MOE_TM
    x1f = x1.reshape(n, d)
    const2 = lambda t, nu: (0, 0)
    grid_spec = pltpu.PrefetchScalarGridSpec(
        num_scalar_prefetch=1,
        grid=(n_tiles,),
        in_specs=[pl.BlockSpec((None, 1, MOE_NCHUNK), lambda t, nu: (t, 0, 0), memory_space=pltpu.SMEM),
                  pl.BlockSpec((None, 1, MOE_NCHUNK), lambda t, nu: (jnp.minimum(t + 1, n_tiles - 1), 0, 0),
                               memory_space=pltpu.SMEM),
                  pl.BlockSpec((MOE_TM, d), lambda t, nu: (t, 0)),
                  pl.BlockSpec((MOE_TM, LANES), lambda t, nu: (t, 0)),
                  pl.BlockSpec((MOE_TM, d), lambda t, nu: (t, 0)),
                  pl.BlockSpec((None, 1, d), lambda t, nu: (t // per_b, 0, 0)),
                  pl.BlockSpec((None, 8, LANES), lambda t, nu: (t, 0, 0)),
                  pl.BlockSpec((None, LANES, LANES), lambda t, nu: (t, 0, 0)),
                  pl.BlockSpec((None, LANES, LANES), lambda t, nu: (t, 0, 0)),
                  pl.BlockSpec(sg.shape, const2), pl.BlockSpec(su.shape, const2), pl.BlockSpec(sd.shape, const2),
                  pl.BlockSpec(memory_space=pl.ANY)],
        out_specs=pl.BlockSpec((MOE_TM, d), lambda t, nu: (t, 0)),
        scratch_shapes=[pltpu.VMEM((2, MOE_SLOTS, d), BF16),
                        pltpu.SemaphoreType.DMA((2, MOE_NPIECE))],
    )
    y = pl.pallas_call(
        _combine_body,
        grid_spec=grid_spec,
        out_shape=jax.ShapeDtypeStruct((n, d), F32),
        compiler_params=_cparams(1),
        name="moe_combine",
    )(plan["n_used"], plan["cdest_c"], plan["cdest_c"], h2, comb, x1f, gate, plan["off_row1"], plan["off_col128"],
      plan["end_col128"],
      sg, su, sd, ys)
    return y.reshape(nb, s, d)


def _moe_sparse(h2, comb, combt, cnt, x1, gate, wg, wu, wd, sg, su, sd):
    n = h2.shape[0]
    n_tiles = n // MOE_TM
    plan = _moe_plan(cnt[:, :, 0].astype(jnp.int32), n_tiles)
    xs = _dispatch(h2, combt, plan, n_tiles)
    ys = _ffn(xs, wg, wu, wd, plan)
    return _combine(h2, comb, x1, gate, ys, sg, su, sd, plan, n_tiles)


def _tile_heads(g, mult=1.0):
    return (jnp.tile(g.astype(F32), N_HEADS) * mult).reshape(1, WIDTH)


def kernel(x_prompt, x_sample, cache_fox_k, cache_fox_v, cache_fox_logf, cache_band_k, cache_band_v, c_prompt, c_sample, w_ada, b_ada, norm1_g, norm2_g, w_in, b_forget, g_q_fox, g_k_fox, g_q_band, g_k_band, rel_bias, out_g_fox, out_g_band, w_out, w_router, b_router, w_gate, w_up, w_down, ws_gate, ws_up, ws_down):
    depth = w_ada.shape[0]
    assert depth == 1
    bsz, seq, d = x_prompt.shape
    dbs, dseq, _ = x_sample.shape
    past = cache_fox_k.shape[2]
    n_cache = cache_band_k.shape[2]
    assert n_cache == BAND_REACH and dseq == CHUNK and seq % BAND_REACH == 0

    wi = w_in[0]
    cols = [wi[:, 0:512], wi[:, 512:1024], wi[:, 1024:1536], wi[:, 1544:2056], wi[:, 2056:2568], wi[:, 2568:3080],
            wi[:, 1536:1544], jnp.zeros((d, LANES - N_HEADS), F32)]
    w_all = jnp.concatenate(cols, axis=1).astype(BF16)
    hd = jnp.arange(WIDTH) // HEAD_DIM
    bd = jnp.where(hd[:, None] == hd[None, :], 1.0 / HEAD_DIM, 0.0).astype(BF16)
    qscale = ATTN_SCALE * LOG2E
    gqf, gkf = _tile_heads(g_q_fox[0], qscale), _tile_heads(g_k_fox[0])
    gqb, gkb = _tile_heads(g_q_band[0], qscale), _tile_heads(g_k_band[0])
    bf_row = jnp.concatenate([b_forget[0], jnp.zeros((LANES - N_HEADS,), F32)]).reshape(1, LANES)
    g1 = norm1_g[0].reshape(1, d)
    g2 = norm2_g[0].reshape(1, d)
    ogf = out_g_fox[0].reshape(1, WIDTH)
    ogb = out_g_band[0].reshape(1, WIDTH)
    wo = w_out[0].astype(BF16)
    wr_t = w_router[0].T
    wr_hi = wr_t.astype(BF16)
    wr_lo = (wr_t - wr_hi.astype(F32)).astype(BF16)
    br = jnp.broadcast_to(b_router[0].reshape(N_EXPERTS, 1), (N_EXPERTS, LANES)).astype(F32)
    wg, wu, wd = w_gate[0].astype(BF16), w_up[0].astype(BF16), w_down[0].astype(BF16)
    sg, su, sd = ws_gate[0].astype(BF16), ws_up[0].astype(BF16), ws_down[0].astype(BF16)

    n_c = bsz + dbs
    rows = -(-n_c // 8) * 8
    c_all = jnp.concatenate([c_prompt, c_sample, jnp.zeros((rows - n_c, d), F32)], axis=0)
    mod = _ada(c_all, w_ada[0], b_ada[0].reshape(1, -1))

    def mods(lo, hi):
        return [mod[lo:hi, j * d:(j + 1) * d].reshape(hi - lo, 1, d) for j in range(6)]

    shift1_p, scale1_p, gate1_p, shift2_p, scale2_p, gate2_p = mods(0, bsz)
    shift1_s, scale1_s, gate1_s, shift2_s, scale2_s, gate2_s = mods(bsz, n_c)

    TM = BAND_REACH
    (qf, kf, vft, kf32, vf32, lf, qb, kb, vbt, kb32, vb32) = _proj(
        x_prompt, shift1_p, scale1_p, g1, w_all, bd, gqf, gkf, gqb, gkb, bf_row, G=1, R=TM, band_last_only=True)
    r3 = lambda a: a.reshape(bsz, seq, a.shape[-1])
    ct, ka = _scan_t(r3(lf), T=TM)
    of = _foxt(r3(qf), r3(kf), ka, vft, ct, T=TM)
    assert TM == BAND_STEP
    ob = _bandt(r3(qb), r3(kb), vbt, _band_bias_tile_t(rel_bias[0]))
    assert TM % MOE_TM == 0
    x1_p, h2_p, comb_p, combt_p, cnt_p = _merge(of.reshape(-1, WIDTH), ob.reshape(-1, WIDTH), x_prompt, gate1_p,
                                                shift2_p, scale2_p, ogf, ogb, wo, g2, wr_hi, wr_lo, br, G=1, R=TM)
    y_p = _moe_sparse(h2_p, comb_p, combt_p, cnt_p, x1_p, gate2_p, wg, wu, wd, sg, su, sd)

    GS = 8
    (qf_s, kf_s, vf_s, kf32_s, vf32_s, lf_s, qb_s, kb_s, vb_s, kb32_s, vb32_s) = _proj(
        x_sample, shift1_s, scale1_s, g1, w_all, bd, gqf, gkf, gqb, gkb, bf_row, G=GS, R=dseq,
        band_last_only=False)
    s3 = lambda a: a.reshape(dbs, dseq, a.shape[-1])
    sk = past + dseq
    skp = -(-sk // LANES) * LANES
    pad_k = skp - sk
    n_seq = dbs * N_HEADS
    assert n_seq <= LANES
    lf_seq = jnp.concatenate([cache_fox_logf[0], s3(lf_s)[:, :, :N_HEADS]], axis=1)
    lf_seq = jnp.swapaxes(lf_seq, 0, 1).reshape(sk, n_seq)
    lf_seq = jnp.pad(lf_seq, ((0, pad_k), (0, LANES - n_seq)))
    cum_col, cum_row = _scan(lf_seq)
    cq_s = jnp.swapaxes(cum_col[past:past + dseq, :n_seq].reshape(dseq, dbs, N_HEADS), 0, 1)
    cum_s = jnp.pad(cq_s, ((0, 0), (0, 0), (0, LANES - N_HEADS)))
    cumt_s = cum_row[:n_seq].reshape(dbs, 1, N_HEADS, skp)

    def cache_t(c):
        return jnp.transpose(c, (0, 2, 3, 1)).reshape(dbs, N_PAIRS, PAIR, c.shape[1])

    of_s = _sample_attn(s3(qf_s), cache_t(cache_fox_k[0]), cache_t(cache_fox_v[0]), s3(kf_s), s3(vf_s),
                        cum_s, cumt_s, fox=True)
    bias_s = _band_bias_tile(rel_bias[0], dseq, BAND_REACH + LANES)
    ob_s = _sample_attn(s3(qb_s), cache_t(cache_band_k[0]), cache_t(cache_band_v[0]), s3(kb_s), s3(vb_s),
                        bias_s, fox=False)
    x1_s, h2_s, comb_s, _, _ = _merge(of_s.reshape(-1, WIDTH), ob_s.reshape(-1, WIDTH), x_sample, gate1_s, shift2_s,
                                      scale2_s, ogf, ogb, wo, g2, wr_hi, wr_lo, br, G=GS, R=dseq)
    y_s = _moe(h2_s, comb_s, x1_s, gate2_s, wg, wu, wd, sg, su, sd, G=dbs, R=dseq)

    hshape = (N_HEADS, HEAD_DIM)
    new_bk_s = jnp.concatenate([cache_band_k[0], s3(kb32_s).reshape(dbs, dseq, *hshape)], axis=1)[:, -n_cache:]
    new_bv_s = jnp.concatenate([cache_band_v[0], s3(vb32_s).reshape(dbs, dseq, *hshape)], axis=1)[:, -n_cache:]
    return (y_p, y_s,
            kf32.reshape(1, bsz, seq, *hshape), vf32.reshape(1, bsz, seq, *hshape),
            lf[:, :N_HEADS].reshape(1, bsz, seq, N_HEADS),
            kb32.reshape(1, bsz, BAND_REACH, *hshape), vb32.reshape(1, bsz, BAND_REACH, *hshape),
            kf32_s.reshape(1, dbs, dseq, *hshape), vf32_s.reshape(1, dbs, dseq, *hshape),
            lf_s[:, :N_HEADS].reshape(1, dbs, dseq, N_HEADS),
            new_bk_s[None], new_bv_s[None])
```

```python
import functools

import jax
import jax.numpy as jnp
import numpy as np
from jax import lax
from jax.experimental import pallas as pl
from jax.experimental.pallas import tpu as pltpu

F32 = jnp.float32
BF16 = jnp.bfloat16

HEAD_DIM = 64
N_HEADS = 8
WIDTH = N_HEADS * HEAD_DIM
PAIR = 2 * HEAD_DIM
N_PAIRS = N_HEADS // 2
LANES = 128
CHUNK = 64
BAND_REACH = 512
REL_CLIP = 256
N_EXPERTS = 64
N_GROUPS = 8
GROUP_SIZE = N_EXPERTS // N_GROUPS
TOPK_GROUPS = 4
TOP_K = 8
ROUTED_SCALE = 2.5
EPS = 1e-6
NEG_INF = -1e30
ATTN_SCALE = HEAD_DIM ** -0.5
LOG2E = 1.4426950408889634
VMEM_LIMIT = 56 * 1024 * 1024


def _cparams(n_axes):
    return pltpu.CompilerParams(dimension_semantics=("arbitrary",) * n_axes,
                                vmem_limit_bytes=VMEM_LIMIT)


def _dot(a, b):
    return jnp.dot(a, b, preferred_element_type=F32)


def _dot_nt(a, b):
    return lax.dot_general(a, b, (((1,), (1,)), ((), ())), preferred_element_type=F32)


def _split2(a):
    hi = a.astype(BF16)
    lo = (a - hi.astype(F32)).astype(BF16)
    return hi, lo


def _split3(a):
    hi = a.astype(BF16)
    r = a - hi.astype(F32)
    mid = r.astype(BF16)
    lo = (r - mid.astype(F32)).astype(BF16)
    return hi, mid, lo


def _ada_body(c_ref, w_ref, b_ref, o_ref):
    c = c_ref[...]
    a = c * jax.nn.sigmoid(c)
    a_hi, a_lo = _split2(a)
    w_hi, w_lo = _split2(w_ref[...])
    o_ref[...] = _dot(a_hi, w_hi) + _dot(a_hi, w_lo) + _dot(a_lo, w_hi) + b_ref[...]


def _ada(c_all, w_ada, b_ada):
    rows, d = c_all.shape
    n = w_ada.shape[1]
    tn = 1024
    return pl.pallas_call(
        _ada_body,
        grid=(n // tn,),
        in_specs=[pl.BlockSpec((rows, d), lambda j: (0, 0)),
                  pl.BlockSpec((d, tn), lambda j: (0, j)),
                  pl.BlockSpec((1, tn), lambda j: (0, j))],
        out_specs=pl.BlockSpec((rows, tn), lambda j: (0, j)),
        out_shape=jax.ShapeDtypeStruct((rows, n), F32),
        compiler_params=_cparams(1),
        name="ada",
    )(c_all, w_ada, b_ada)


def _log_sigmoid(z):
    return jnp.minimum(z, 0.0) - jnp.log(1.0 + jnp.exp(-jnp.abs(z)))


def _proj_body(x_ref, sh_ref, sc_ref, g1_ref, w_ref, bd_ref, gqf_ref, gkf_ref, gqb_ref, gkb_ref, bf_ref,
               qf_ref, kf_ref, vf_ref, kf32_ref, vf32_ref, lf_ref, qb_ref, kb_ref, vb_ref, kb32_ref, vb32_ref,
               *, band_last_only):
    x = x_ref[...]
    g, r, d = x.shape
    ms = jnp.mean(x * x, axis=-1, keepdims=True)
    h = x * lax.rsqrt(ms + EPS) * g1_ref[...] * (1.0 + sc_ref[...]) + sh_ref[...]
    hb = h.reshape(g * r, d).astype(BF16)

    def seg(i):
        return _dot(hb, w_ref[:, i * WIDTH:(i + 1) * WIDTH])

    def head_norm(t, gain_ref):
        ssq = _dot((t * t).astype(BF16), bd_ref[...])
        return t * lax.rsqrt(ssq + EPS) * gain_ref[...]

    qf_ref[...] = head_norm(seg(0), gqf_ref).astype(BF16)
    kf = head_norm(seg(1), gkf_ref)
    kf_ref[...] = kf.astype(BF16)
    vf = seg(2)
    for hh in range(N_HEADS):
        kf32_ref[:, hh, :] = kf[:, hh * HEAD_DIM:(hh + 1) * HEAD_DIM]
        vf32_ref[:, hh, :] = vf[:, hh * HEAD_DIM:(hh + 1) * HEAD_DIM]
    if band_last_only:
        vf_ref[...] = vf.T.astype(BF16)
    else:
        vf_ref[...] = vf.astype(BF16)
    z = _dot(hb, w_ref[:, 6 * WIDTH:6 * WIDTH + LANES]) + bf_ref[...]
    lf_ref[...] = _log_sigmoid(z)
    qb_ref[...] = head_norm(seg(3), gqb_ref).astype(BF16)
    kb = head_norm(seg(4), gkb_ref)
    kb_ref[...] = kb.astype(BF16)
    vb = seg(5)
    if band_last_only:
        vb_ref[...] = vb.T.astype(BF16)
    else:
        vb_ref[...] = vb.astype(BF16)

    if band_last_only:
        @pl.when(pl.program_id(1) == pl.num_programs(1) - 1)
        def _():
            kb32_ref[...] = kb
            vb32_ref[...] = vb
    else:
        kb32_ref[...] = kb
        vb32_ref[...] = vb


def _proj(x, shift, scale, g1, w_all, bd, gqf, gkf, gqb, gkb, bf_row, *, G, R, band_last_only):
    nb, s, d = x.shape
    n = nb * s
    tm = G * R
    nbi, nsi = nb // G, s // R
    grid = (nbi, nsi)
    row = lambda b, i: (b * nsi + i, 0)
    const = lambda b, i: (0, 0)
    mod_spec = pl.BlockSpec((G, 1, d), lambda b, i: (b, 0, 0))
    out_bf = jax.ShapeDtypeStruct((n, WIDTH), BF16)
    out_f32 = jax.ShapeDtypeStruct((n, WIDTH), F32)
    tile = pl.BlockSpec((tm, WIDTH), row)
    state_shape = jax.ShapeDtypeStruct((n, N_HEADS, HEAD_DIM), F32)
    state_spec = pl.BlockSpec((tm, N_HEADS, HEAD_DIM), lambda b, i: (b * nsi + i, 0, 0))
    if band_last_only:
        assert G == 1 and R == BAND_REACH
        band_shape = jax.ShapeDtypeStruct((nb, BAND_REACH, WIDTH), F32)
        band_spec = pl.BlockSpec((None, BAND_REACH, WIDTH), lambda b, i: (b, 0, 0))
        v_shape = jax.ShapeDtypeStruct((nb, nsi, WIDTH, tm), BF16)
        v_spec = pl.BlockSpec((None, None, WIDTH, tm), lambda b, i: (b, i, 0, 0))
    else:
        band_shape, band_spec = out_f32, tile
        v_shape, v_spec = out_bf, tile
    return pl.pallas_call(
        functools.partial(_proj_body, band_last_only=band_last_only),
        grid=grid,
        in_specs=[pl.BlockSpec((G, R, d), lambda b, i: (b, i, 0)), mod_spec, mod_spec,
                  pl.BlockSpec((1, d), const), pl.BlockSpec(w_all.shape, const), pl.BlockSpec(bd.shape, const),
                  pl.BlockSpec((1, WIDTH), const), pl.BlockSpec((1, WIDTH), const),
                  pl.BlockSpec((1, WIDTH), const), pl.BlockSpec((1, WIDTH), const),
                  pl.BlockSpec((1, LANES), const)],
        out_specs=[tile, tile, v_spec, state_spec, state_spec, pl.BlockSpec((tm, LANES), row), tile, tile, v_spec,
                   band_spec, band_spec],
        out_shape=[out_bf, out_bf, v_shape, state_shape, state_shape, jax.ShapeDtypeStruct((n, LANES), F32),
                   out_bf, out_bf, v_shape, band_shape, band_shape],
        compiler_params=_cparams(2),
        name="proj",
    )(x, shift, scale, g1, w_all, bd, gqf, gkf, gqb, gkb, bf_row)


def _scan_body(lf_ref, cum_ref, cumt_ref):
    lf = lf_ref[...]
    s = lf.shape[0]
    hi, mid, lo = _split3(lf)
    rr = lax.broadcasted_iota(jnp.int32, (s, s), 0)
    cc = lax.broadcasted_iota(jnp.int32, (s, s), 1)
    tri = jnp.where(cc <= rr, 1.0, 0.0).astype(BF16)
    cum2 = (_dot(tri, hi) + _dot(tri, mid) + _dot(tri, lo)) * LOG2E
    cum_ref[...] = cum2
    cumt_ref[...] = cum2.T


def _scan(lf):
    s, _ = lf.shape
    return pl.pallas_call(
        _scan_body,
        out_shape=[jax.ShapeDtypeStruct((s, LANES), F32), jax.ShapeDtypeStruct((LANES, s), F32)],
        compiler_params=pltpu.CompilerParams(vmem_limit_bytes=VMEM_LIMIT),
        name="scan",
    )(lf)


AUG_PIECES = 3
FOX_UNDERFLOW = 160.0
FOX_NORM_SLACK = 1.02
FOX_BOUND_SLACK = 2.0


def _scan_t_body(lf_ref, place_ref, ct_ref, ka_ref, carry_ref):
    @pl.when(pl.program_id(1) == 0)
    def _():
        carry_ref[...] = jnp.zeros_like(carry_ref)

    lf = lf_ref[...]
    ts = lf.shape[0]
    lane = lax.broadcasted_iota(jnp.int32, lf.shape, 1)
    lf = jnp.where(lane < N_HEADS, lf, 0.0)
    hi, mid, lo = _split3(lf)
    rr = lax.broadcasted_iota(jnp.int32, (ts, ts), 0)
    cc = lax.broadcasted_iota(jnp.int32, (ts, ts), 1)
    tri = jnp.where(cc <= rr, 1.0, 0.0).astype(BF16)
    cum = _dot(tri, hi) + _dot(tri, mid) + _dot(tri, lo) + carry_ref[0:1, :]
    carry_ref[...] = jnp.broadcast_to(cum[ts - 1:ts, :], carry_ref.shape)
    cum2 = cum * LOG2E
    ct_ref[...] = cum2.T[0:N_HEADS, :]
    pieces = _split3(cum2 - cum2[0:1, :])
    ka = _dot(pieces[0], place_ref[0]) + _dot(pieces[1], place_ref[1]) + _dot(pieces[2], place_ref[2])
    ka_ref[...] = ka.astype(BF16)


def _aug_placement():
    h = jnp.arange(LANES)[:, None]
    col = jnp.arange(WIDTH)[None, :]
    mats = []
    for x in range(AUG_PIECES):
        tgt = PAIR * (h // 2) + AUG_PIECES * (h % 2) + x
        mats.append(jnp.where((h < N_HEADS) & (col == tgt), 1.0, 0.0))
    return jnp.stack(mats).astype(BF16)


def _scan_t(lf, *, T):
    b, s, _ = lf.shape
    place = _aug_placement()
    return pl.pallas_call(
        _scan_t_body,
        grid=(b, s // T),
        in_specs=[pl.BlockSpec((None, T, LANES), lambda bi, i: (bi, i, 0)),
                  pl.BlockSpec(place.shape, lambda bi, i: (0, 0, 0))],
        out_specs=[pl.BlockSpec((None, None, N_HEADS, T), lambda bi, i: (bi, i, 0, 0)),
                   pl.BlockSpec((None, T, WIDTH), lambda bi, i: (bi, i, 0))],
        out_shape=[jax.ShapeDtypeStruct((b, s // T, N_HEADS, T), F32),
                   jax.ShapeDtypeStruct((b, s, WIDTH), BF16)],
        scratch_shapes=[pltpu.VMEM((8, LANES), F32)],
        compiler_params=_cparams(2),
        name="scan_t",
    )(lf, place)


def _foxt_body(q_ref, k_ref, ka_ref, vt_ref, ct_ref, o_ref, kn_ref, *, T):
    p = pl.program_id(1)
    i = pl.program_id(2)
    q = q_ref[...]
    lane = lax.broadcasted_iota(jnp.int32, (T, PAIR), 1)
    halves = []
    for par in range(2):
        qm = jnp.where((lane >= HEAD_DIM) == (par == 1), q, jnp.zeros_like(q))
        lo_lane = AUG_PIECES * par
        qa = jnp.where((lane >= lo_lane) & (lane < lo_lane + AUG_PIECES), -1.0, 0.0).astype(BF16)
        halves.append(jnp.concatenate([qm, qa], axis=1))
    qcat = jnp.concatenate(halves, axis=0)
    h_even = 2 * p
    cq = jnp.concatenate([ct_ref[i, pl.ds(h_even, 1), :], ct_ref[i, pl.ds(h_even + 1, 1), :]], axis=1)
    ones_rows = jnp.ones((16, T), BF16)
    krow = lax.broadcasted_iota(jnp.int32, (T, 2 * T), 0)
    qcol = lax.broadcasted_iota(jnp.int32, (T, 2 * T), 1) % T

    def step(j, carry, masked):
        m, acc_e, acc_o = carry
        k0 = pl.multiple_of(j * T, T)
        kcat = jnp.concatenate([k_ref[pl.ds(k0, T), :], ka_ref[pl.ds(k0, T), :]], axis=1)
        st = _dot_nt(kcat, qcat)
        c0 = jnp.concatenate([jnp.broadcast_to(ct_ref[j, pl.ds(h_even, 1), :][:, 0:1], (1, T)),
                              jnp.broadcast_to(ct_ref[j, pl.ds(h_even + 1, 1), :][:, 0:1], (1, T))], axis=1)
        rb = cq - c0
        if masked:
            st = jnp.where(krow <= qcol, st, NEG_INF)
        m_new = jnp.maximum(m, jnp.max(st, axis=0, keepdims=True) + rb)
        alpha = jnp.exp2(m - m_new)
        pt = jnp.exp2(st + (rb - m_new)).astype(BF16)
        vt = vt_ref[j]
        pv_e = _dot(jnp.concatenate([vt[0:HEAD_DIM], ones_rows], axis=0), pt[:, 0:T])
        pv_o = _dot(jnp.concatenate([vt[HEAD_DIM:PAIR], ones_rows], axis=0), pt[:, T:2 * T])
        return m_new, alpha[:, 0:T] * acc_e + pv_e, alpha[:, T:2 * T] * acc_o + pv_o

    @pl.when(i == 0)
    def _():
        ones = jnp.ones((PAIR, LANES), BF16)
        kmax = jnp.zeros((1, LANES), F32)
        for c in range(k_ref.shape[0] // T):
            kc = k_ref[c * T:(c + 1) * T, :].astype(F32)
            kmax = jnp.maximum(kmax, jnp.max(_dot((kc * kc).astype(BF16), ones), axis=0, keepdims=True))
        kn_ref[...] = jnp.broadcast_to(kmax, kn_ref.shape)

    rows = HEAD_DIM + 16
    init = (jnp.full((1, 2 * T), NEG_INF, F32), jnp.zeros((rows, T), F32), jnp.zeros((rows, T), F32))
    carry = step(i, init, True)

    qf = q.astype(F32)
    qsq = qf * qf
    kn2 = kn_ref[0:1, 0:1] * FOX_NORM_SLACK
    need = jnp.zeros((1, 1), jnp.int32)
    blk = lax.broadcasted_iota(jnp.int32, (ct_ref.shape[0], 1, 1), 0)
    for par in range(2):
        head_lanes = (lane >= HEAD_DIM) == (par == 1)
        qn2 = jnp.max(jnp.sum(jnp.where(head_lanes, qsq, 0.0), axis=1, keepdims=True), axis=0, keepdims=True)
        reach = jnp.sqrt(qn2 * kn2) + FOX_BOUND_SLACK
        m_min = jnp.min(carry[0][:, par * T:(par + 1) * T], axis=1, keepdims=True)
        cq_first = cq[:, par * T:par * T + 1]
        ck_end = ct_ref[:, pl.ds(h_even + par, 1), :][:, :, T - 1:T]
        live = (reach + cq_first - m_min)[None, :, :] - ck_end > -FOX_UNDERFLOW
        count = jnp.sum(jnp.where(live & (blk < i), 1, 0), axis=0)
        need = jnp.maximum(need, count)
    n_keep = need[0, 0]

    n_pairs = n_keep // 2
    carry = lax.fori_loop(0, n_pairs, lambda u, c: step(i - 2 - 2 * u, step(i - 1 - 2 * u, c, False), False), carry)
    carry = lax.fori_loop(2 * n_pairs, n_keep, lambda u, c: step(i - 1 - u, c, False), carry)
    _, acc_e, acc_o = carry
    o_t = jnp.concatenate([acc_e[0:HEAD_DIM] / acc_e[HEAD_DIM:HEAD_DIM + 1],
                           acc_o[0:HEAD_DIM] / acc_o[HEAD_DIM:HEAD_DIM + 1]], axis=0)
    o_ref[...] = o_t.T.astype(o_ref.dtype)


def _foxt(q, k, ka, vt, ct, *, T):
    b, s, _ = q.shape
    nt = s // T
    return pl.pallas_call(
        functools.partial(_foxt_body, T=T),
        grid=(b, N_PAIRS, nt),
        in_specs=[pl.BlockSpec((None, T, PAIR), lambda bi, p, i: (bi, i, p)),
                  pl.BlockSpec((None, s, PAIR), lambda bi, p, i: (bi, 0, p)),
                  pl.BlockSpec((None, s, PAIR), lambda bi, p, i: (bi, 0, p)),
                  pl.BlockSpec((None, nt, PAIR, T), lambda bi, p, i: (bi, 0, p, 0)),
                  pl.BlockSpec((None, nt, N_HEADS, T), lambda bi, p, i: (bi, 0, 0, 0))],
        out_specs=pl.BlockSpec((None, T, PAIR), lambda bi, p, i: (bi, i, p)),
        out_shape=jax.ShapeDtypeStruct((b, s, WIDTH), BF16),
        scratch_shapes=[pltpu.VMEM((8, LANES), F32)],
        compiler_params=_cparams(3),
        name="foxt",
    )(q, k, ka, vt, ct)


BAND_TQ = 256
BAND_STEP = 512


def _bandt_body(q_ref, kp_ref, kc_ref, vp_ref, vc_ref, bias_ref, o_ref):
    i = pl.program_id(2)
    w = BAND_REACH + BAND_TQ
    k2 = jnp.concatenate([kp_ref[...], kc_ref[...]], axis=0)
    vt2 = jnp.concatenate([vp_ref[...], vc_ref[...]], axis=1)
    lane = lax.broadcasted_iota(jnp.int32, (BAND_TQ, PAIR), 1)
    ones_rows = jnp.ones((16, w), BF16)
    krow = lax.broadcasted_iota(jnp.int32, (w, 2 * BAND_TQ), 0)
    bias = bias_ref[...]
    for sub in range(BAND_STEP // BAND_TQ):
        r0 = sub * BAND_TQ
        q = q_ref[r0:r0 + BAND_TQ, :]
        qcat = jnp.concatenate([jnp.where(lane < HEAD_DIM, q, jnp.zeros_like(q)),
                                jnp.where(lane >= HEAD_DIM, q, jnp.zeros_like(q))], axis=0)
        st = _dot_nt(k2[r0:r0 + w], qcat) + bias
        st = jnp.where(krow >= (1 - i) * BAND_STEP - r0, st, NEG_INF)
        m = jnp.max(st, axis=0, keepdims=True)
        pt = jnp.exp2(st - m).astype(BF16)
        vwin = vt2[:, r0:r0 + w]
        outs = []
        for par in range(2):
            vcat = jnp.concatenate([vwin[par * HEAD_DIM:(par + 1) * HEAD_DIM], ones_rows], axis=0)
            pv = _dot(vcat, pt[:, par * BAND_TQ:(par + 1) * BAND_TQ])
            outs.append(pv[0:HEAD_DIM] / pv[HEAD_DIM:HEAD_DIM + 1])
        o_ref[r0:r0 + BAND_TQ, :] = jnp.concatenate(outs, axis=0).T.astype(o_ref.dtype)


def _bandt(q, k, vt, bias_t):
    b, s, _ = q.shape
    prev = lambda i: jnp.maximum(i - 1, 0)
    return pl.pallas_call(
        _bandt_body,
        grid=(b, N_PAIRS, s // BAND_STEP),
        in_specs=[pl.BlockSpec((None, BAND_STEP, PAIR), lambda bi, p, i: (bi, i, p)),
                  pl.BlockSpec((None, BAND_STEP, PAIR), lambda bi, p, i: (bi, prev(i), p)),
                  pl.BlockSpec((None, BAND_STEP, PAIR), lambda bi, p, i: (bi, i, p)),
                  pl.BlockSpec((None, None, PAIR, BAND_STEP), lambda bi, p, i: (bi, prev(i), p, 0)),
                  pl.BlockSpec((None, None, PAIR, BAND_STEP), lambda bi, p, i: (bi, i, p, 0)),
                  pl.BlockSpec((None,) + bias_t.shape[1:], lambda bi, p, i: (p, 0, 0))],
        out_specs=pl.BlockSpec((None, BAND_STEP, PAIR), lambda bi, p, i: (bi, i, p)),
        out_shape=jax.ShapeDtypeStruct((b, s, WIDTH), BF16),
        compiler_params=_cparams(3),
        name="bandt",
    )(q, k, k, vt, vt, bias_t)


def _band_bias_tile_t(rel_bias):
    tile = _band_bias_tile(rel_bias, BAND_TQ, BAND_REACH + BAND_TQ)
    t = jnp.swapaxes(tile, 1, 2)
    return jnp.concatenate([t[0::2], t[1::2]], axis=2)


def _band_bias_tile(rel_bias, tq, w):
    span = w + tq - 1
    period = span + 1
    v = np.arange(period)
    d = np.where(v < w, v, v - period)
    table_idx = np.clip(BAND_REACH - d, -REL_CLIP, REL_CLIP) + REL_CLIP
    table = rel_bias[:, table_idx] * LOG2E
    n_h = rel_bias.shape[0]
    vals = jnp.tile(table, (1, tq))[:, :tq * span].reshape(n_h, tq, span)[:, :, :w]
    r = np.arange(tq)[:, None]
    c = np.arange(w)[None, :]
    in_band = (c // CHUNK >= r // CHUNK) & (c // CHUNK <= r // CHUNK + BAND_REACH // CHUNK)
    return jnp.where(jnp.asarray(in_band)[None], vals, NEG_INF).astype(F32)


def _sample_attn_body(q_ref, kc_ref, vc_ref, kn_ref, vn_ref, *rest, fox):
    p = pl.program_id(1)
    t = q_ref.shape[0]
    past = kc_ref.shape[1]
    lane = lax.broadcasted_iota(jnp.int32, (t, PAIR), 1)
    q = q_ref[...]
    qs = jnp.concatenate([jnp.where(lane < HEAD_DIM, q, jnp.zeros_like(q)),
                          jnp.where(lane >= HEAD_DIM, q, jnp.zeros_like(q))], axis=0)
    kc = kc_ref[...].astype(BF16)
    vc = vc_ref[...].astype(BF16)
    s_c = _dot(qs, kc)
    s_n = _dot_nt(qs, kn_ref[...])
    row = lax.broadcasted_iota(jnp.int32, (t, t), 0)
    col = lax.broadcasted_iota(jnp.int32, (t, t), 1)
    if fox:
        cq_ref, ck_ref, o_ref = rest
    else:
        bias_ref, o_ref = rest
    pcs, pns = [], []
    for par in range(2):
        h = 2 * p + par
        sc = s_c[par * t:(par + 1) * t]
        sn = s_n[par * t:(par + 1) * t]
        if fox:
            cq_col = jnp.sum(jnp.where(lane == h, cq_ref[...], 0.0), axis=1, keepdims=True)
            ck = ck_ref[pl.ds(h, 1), :]
            uc = sc + (cq_col - ck[:, 0:past])
            un = jnp.where(col <= row, sn + (cq_col - ck[:, past:past + t]), NEG_INF)
        else:
            bias = bias_ref[h]
            uc = sc + bias[:, 0:past]
            un = sn + bias[:, past:past + t]
        m = jnp.maximum(jnp.max(uc, axis=1, keepdims=True), jnp.max(un, axis=1, keepdims=True))
        pc = jnp.exp2(uc - m)
        pn = jnp.exp2(un - m)
        inv = 1.0 / (jnp.sum(pc, axis=1, keepdims=True) + jnp.sum(pn, axis=1, keepdims=True))
        pcs.append((pc * inv).astype(BF16))
        pns.append((pn * inv).astype(BF16))
    o2 = _dot_nt(jnp.concatenate(pcs, axis=0), vc) + _dot(jnp.concatenate(pns, axis=0), vn_ref[...])
    o_ref[...] = jnp.where(lane < HEAD_DIM, o2[0:t], o2[t:2 * t]).astype(o_ref.dtype)


def _sample_attn(q, kc_t, vc_t, k_new, v_new, *extra, fox):
    b, t, _ = q.shape
    past = kc_t.shape[3]
    pair_rows = pl.BlockSpec((None, t, PAIR), lambda bi, p: (bi, 0, p))
    cache = pl.BlockSpec((None, None, PAIR, past), lambda bi, p: (bi, p, 0, 0))
    if fox:
        cq, ck = extra
        extra_specs = [pl.BlockSpec((None, t, LANES), lambda bi, p: (bi, 0, 0)),
                       pl.BlockSpec((None, None, N_HEADS, ck.shape[3]), lambda bi, p: (bi, 0, 0, 0))]
    else:
        extra_specs = [pl.BlockSpec(extra[0].shape, lambda bi, p: (0, 0, 0))]
    return pl.pallas_call(
        functools.partial(_sample_attn_body, fox=fox),
        grid=(b, N_PAIRS),
        in_specs=[pair_rows, cache, cache, pair_rows, pair_rows] + extra_specs,
        out_specs=pair_rows,
        out_shape=jax.ShapeDtypeStruct((b, t, WIDTH), BF16),
        compiler_params=_cparams(2),
        name="sample_fox" if fox else "sample_band",
    )(q, kc_t, vc_t, k_new, v_new, *extra)


def _first_index(is_max, idx, axis, big):
    return jnp.min(jnp.where(is_max, idx, big), axis=axis, keepdims=True)


def _route(scores, choice):
    t = scores.shape[1]
    c3 = choice.reshape(N_GROUPS, GROUP_SIZE, t)
    j_idx = lax.broadcasted_iota(jnp.int32, c3.shape, 1)
    top1 = jnp.max(c3, axis=1, keepdims=True)
    first = _first_index(c3 == top1, j_idx, 1, GROUP_SIZE)
    top2 = jnp.max(jnp.where(j_idx == first, -jnp.inf, c3), axis=1, keepdims=True)
    gscore = (top1 + top2).reshape(N_GROUPS, t)

    g_idx = lax.broadcasted_iota(jnp.int32, gscore.shape, 0)
    gsel = jnp.zeros(gscore.shape, F32)
    work = gscore
    for _ in range(TOPK_GROUPS):
        gm = jnp.max(work, axis=0, keepdims=True)
        pick = g_idx == _first_index(work == gm, g_idx, 0, N_GROUPS)
        gsel = jnp.where(pick, 1.0, gsel)
        work = jnp.where(pick, -jnp.inf, work)

    emask = jnp.broadcast_to(gsel.reshape(N_GROUPS, 1, t), c3.shape) > 0.0
    work = jnp.where(emask, c3, NEG_INF)
    e_idx = lax.broadcasted_iota(jnp.int32, c3.shape, 0) * GROUP_SIZE + j_idx
    esel = jnp.zeros(c3.shape, F32)
    for _ in range(TOP_K):
        em = jnp.max(jnp.max(work, axis=1, keepdims=True), axis=0, keepdims=True)
        cand = jnp.where(work == em, e_idx, N_EXPERTS)
        first = jnp.min(jnp.min(cand, axis=1, keepdims=True), axis=0, keepdims=True)
        pick = e_idx == first
        esel = jnp.where(pick, 1.0, esel)
        work = jnp.where(pick, -jnp.inf, work)

    w = esel * scores.reshape(c3.shape)
    denom = jnp.sum(jnp.sum(w, axis=1, keepdims=True), axis=0, keepdims=True)
    return (w / denom * ROUTED_SCALE).reshape(N_EXPERTS, t)


def _merge_body(of_ref, ob_ref, x_ref, gate_ref, sh_ref, sc_ref, ogf_ref, ogb_ref, wo_ref, g2_ref,
                wrh_ref, wrl_ref, br_ref, x1_ref, h2_ref, comb_ref, combt_ref, cnt_ref):
    def group_norm(t_ref, gain_ref):
        t = t_ref[...].astype(F32)
        ms = jnp.mean(t * t, axis=-1, keepdims=True)
        return (t * lax.rsqrt(ms + EPS) * gain_ref[...]).astype(BF16)

    y = _dot(group_norm(of_ref, ogf_ref), wo_ref[0:WIDTH, :]) + _dot(group_norm(ob_ref, ogb_ref), wo_ref[WIDTH:, :])
    x = x_ref[...]
    g, r, d = x.shape
    x1 = x + gate_ref[...] * y.reshape(g, r, d)
    x1_ref[...] = x1
    ms = jnp.mean(x1 * x1, axis=-1, keepdims=True)
    h2 = (x1 * lax.rsqrt(ms + EPS) * g2_ref[...] * (1.0 + sc_ref[...]) + sh_ref[...]).reshape(g * r, d)
    h_hi, h_lo = _split2(h2)
    h2_ref[...] = h_hi
    logits = _dot_nt(wrh_ref[...], h_hi) + _dot_nt(wrh_ref[...], h_lo) + _dot_nt(wrl_ref[...], h_hi)
    scores = jax.nn.sigmoid(logits)
    t = scores.shape[1]
    bias = jnp.concatenate([br_ref[...]] * (t // LANES), axis=1)
    comb = _route(scores, scores + bias)
    comb_pad = jnp.concatenate([comb, jnp.zeros((LANES - N_EXPERTS, t), F32)], axis=0)
    comb_ref[...] = comb_pad.T
    combt_ref[...] = comb
    picked = jnp.where(comb > 0.0, 1.0, 0.0)
    for sub in range(cnt_ref.shape[0]):
        cnt = jnp.sum(picked[:, sub * MOE_TM:(sub + 1) * MOE_TM], axis=1, keepdims=True)
        cnt_ref[sub] = jnp.broadcast_to(cnt, (N_EXPERTS, LANES))


def _merge(of, ob, x, gate, shift, scale, ogf, ogb, wo, g2, wr_hi, wr_lo, br, *, G, R):
    nb, s, d = x.shape
    n = nb * s
    tm = G * R
    nbi, nsi = nb // G, s // R
    row = lambda b, i: (b * nsi + i, 0)
    const = lambda b, i: (0, 0)
    mod_spec = pl.BlockSpec((G, 1, d), lambda b, i: (b, 0, 0))
    x_spec = pl.BlockSpec((G, R, d), lambda b, i: (b, i, 0))
    return pl.pallas_call(
        _merge_body,
        grid=(nbi, nsi),
        in_specs=[pl.BlockSpec((tm, WIDTH), row), pl.BlockSpec((tm, WIDTH), row), x_spec,
                  mod_spec, mod_spec, mod_spec,
                  pl.BlockSpec((1, WIDTH), const), pl.BlockSpec((1, WIDTH), const),
                  pl.BlockSpec(wo.shape, const), pl.BlockSpec((1, d), const),
                  pl.BlockSpec(wr_hi.shape, const), pl.BlockSpec(wr_lo.shape, const),
                  pl.BlockSpec(br.shape, const)],
        out_specs=[x_spec, pl.BlockSpec((tm, d), row), pl.BlockSpec((tm, LANES), row),
                   pl.BlockSpec((N_EXPERTS, tm), lambda b, i: (0, b * nsi + i)),
                   pl.BlockSpec((tm // MOE_TM, N_EXPERTS, LANES), lambda b, i: (b * nsi + i, 0, 0))],
        out_shape=[jax.ShapeDtypeStruct((nb, s, d), F32), jax.ShapeDtypeStruct((n, d), BF16),
                   jax.ShapeDtypeStruct((n, LANES), F32), jax.ShapeDtypeStruct((N_EXPERTS, n), F32),
                   jax.ShapeDtypeStruct((n // MOE_TM, N_EXPERTS, LANES), F32)],
        compiler_params=_cparams(2),
        name="merge",
    )(of, ob, x, gate, shift, scale, ogf, ogb, wo, g2, wr_hi, wr_lo, br)


def _silu(g):
    return g * jax.nn.sigmoid(g)


def _moe_body(h_ref, comb_ref, x1_ref, gate_ref, wg_ref, wu_ref, wd_ref, sg_ref, su_ref, sd_ref, y_ref, acc_ref):
    e = pl.program_id(2)
    hb = h_ref[...]

    @pl.when(e == 0)
    def _():
        a = _silu(_dot(hb, sg_ref[...])) * _dot(hb, su_ref[...])
        acc_ref[...] = _dot(a.astype(BF16), sd_ref[...])

    comb = comb_ref[...]
    lane = lax.broadcasted_iota(jnp.int32, comb.shape, 1)
    c_e = jnp.sum(jnp.where(lane == e, comb, 0.0), axis=1, keepdims=True)
    a = _silu(_dot(hb, wg_ref[...])) * _dot(hb, wu_ref[...]) * c_e
    acc_ref[...] += _dot(a.astype(BF16), wd_ref[...])

    @pl.when(e == pl.num_programs(2) - 1)
    def _():
        x1 = x1_ref[...]
        g, r, d = x1.shape
        y_ref[...] = x1 + gate_ref[...] * acc_ref[...].reshape(g, r, d)


def _moe(h2, comb, x1, gate, wg, wu, wd, sg, su, sd, *, G, R):
    nb, s, d = x1.shape
    tm = G * R
    nbi, nsi = nb // G, s // R
    ff = wg.shape[2]
    row = lambda b, i, e: (b * nsi + i, 0)
    const = lambda b, i, e: (0, 0)
    x_spec = pl.BlockSpec((G, R, d), lambda b, i, e: (b, i, 0))
    return pl.pallas_call(
        _moe_body,
        grid=(nbi, nsi, N_EXPERTS),
        in_specs=[pl.BlockSpec((tm, d), row), pl.BlockSpec((tm, LANES), row), x_spec,
                  pl.BlockSpec((G, 1, d), lambda b, i, e: (b, 0, 0)),
                  pl.BlockSpec((None, d, ff), lambda b, i, e: (e, 0, 0)),
                  pl.BlockSpec((None, d, ff), lambda b, i, e: (e, 0, 0)),
                  pl.BlockSpec((None, ff, d), lambda b, i, e: (e, 0, 0)),
                  pl.BlockSpec(sg.shape, const), pl.BlockSpec(su.shape, const), pl.BlockSpec(sd.shape, const)],
        out_specs=x_spec,
        out_shape=jax.ShapeDtypeStruct((nb, s, d), F32),
        scratch_shapes=[pltpu.VMEM((tm, d), F32)],
        compiler_params=_cparams(3),
        name="moe",
    )(h2, comb, x1, gate, wg, wu, wd, sg, su, sd)


MOE_TM = 256
MOE_CH = 16
MOE_SLOTS = TOP_K * MOE_TM + N_EXPERTS * MOE_CH
MOE_NCHUNK = MOE_SLOTS // MOE_CH
MOE_PIECE = 512
MOE_NPIECE = MOE_SLOTS // MOE_PIECE
MOE_CPP = MOE_PIECE // MOE_CH
MOE_RB = 1024
TAU_RADIX = 64.0
assert MOE_SLOTS % MOE_PIECE == 0


def _moe_plan(cnt, n_tiles):
    pc = (cnt + MOE_CH - 1) // MOE_CH * MOE_CH
    off = jnp.cumsum(pc, axis=1) - pc
    end = off + pc
    n_used = (jnp.sum(pc, axis=1) // MOE_CH).astype(jnp.int32)
    tot = jnp.sum(pc, axis=0)
    reg = (tot + MOE_RB - 1) // MOE_RB * MOE_RB
    reg_end = jnp.cumsum(reg)
    reg_start = reg_end - reg
    dest_base = reg_start[None, :] + jnp.cumsum(pc, axis=0) - pc
    chunk_row = jnp.arange(MOE_NCHUNK, dtype=jnp.int32)[None, :] * MOE_CH
    in_group = (chunk_row[:, :, None] >= off[:, None, :]) & (chunk_row[:, :, None] < end[:, None, :])
    used = jnp.any(in_group, axis=2)
    cdest = chunk_row + jnp.sum(jnp.where(in_group, (dest_base - off)[:, None, :], 0), axis=2)
    worst_rows = TOP_K * MOE_TM * n_tiles + n_tiles * N_EXPERTS * (MOE_CH - 1) + N_EXPERTS * (MOE_RB - MOE_CH)
    r_max = -(-worst_rows // MOE_RB)
    parity = (jnp.arange(n_tiles, dtype=jnp.int32) % 2)[:, None]
    cdest_d = jnp.where(used, cdest, r_max * MOE_RB + parity * MOE_SLOTS + chunk_row).astype(jnp.int32)
    cdest_c = jnp.where(used, cdest, chunk_row).astype(jnp.int32)
    n_active = (reg_end[-1] // MOE_RB).astype(jnp.int32).reshape(1)
    tile_row = jnp.arange(r_max, dtype=jnp.int32) * MOE_RB
    tile_expert = jnp.minimum(jnp.sum((tile_row[:, None] >= reg_end[None, :]).astype(jnp.int32), axis=1),
                              N_EXPERTS - 1).astype(jnp.int32)
    in_region = (tile_row[:, None] >= reg_start[None, :]) & (tile_row[:, None] < reg_end[None, :])
    rows_end = jnp.sum(jnp.where(in_region, (reg_start + tot)[None, :], 0), axis=1)
    valid = jnp.clip(rows_end - tile_row, 0, MOE_RB).astype(jnp.int32)
    f = lambda a: a.astype(F32)
    zeros64 = jnp.zeros((n_tiles, N_EXPERTS), F32)
    row2 = lambda a: jnp.broadcast_to(jnp.concatenate([f(a), f(a)], axis=1)[:, None, :], (n_tiles, 8, LANES))
    col = lambda a: jnp.broadcast_to(f(a)[:, :, None], (n_tiles, N_EXPERTS, LANES))
    col128 = lambda a: jnp.broadcast_to(jnp.concatenate([f(a), zeros64], axis=1)[:, :, None], (n_tiles, LANES, LANES))
    row1 = lambda a: jnp.broadcast_to(jnp.concatenate([f(a), zeros64], axis=1)[:, None, :], (n_tiles, 8, LANES))
    return dict(n_used=n_used, cdest_d=cdest_d.reshape(n_tiles, 1, MOE_NCHUNK),
                cdest_c=cdest_c.reshape(n_tiles, 1, MOE_NCHUNK), r_max=r_max, n_active=n_active,
                tile_expert=tile_expert, valid=valid,
                off_row2=row2(off), end_row2=row2(end), off_col=col(off),
                off_row1=row1(off), off_col128=col128(off), end_col128=col128(end))


def _tau_pieces(sel, tau):
    tau = jnp.where(sel, tau, -1.0)
    hi = jnp.floor(tau * (1.0 / TAU_RADIX)) * TAU_RADIX
    return hi.astype(BF16), (tau - hi).astype(BF16)


def _dispatch_body(nused_ref, cdest_ref, h_ref, combt_ref, offcol_ref, offrow_ref, endrow_ref, sorted_ref,
                   buf_ref, sem):
    t = pl.program_id(0)
    n_used = nused_ref[t]
    tm = h_ref.shape[0]
    sel = combt_ref[...] > 0.0
    rr = lax.broadcasted_iota(jnp.int32, (tm, tm), 0)
    cc = lax.broadcasted_iota(jnp.int32, (tm, tm), 1)
    upper = jnp.where(rr < cc, 1.0, 0.0).astype(BF16)
    rank = _dot(jnp.where(sel, 1.0, 0.0).astype(BF16), upper)
    cols = jnp.concatenate([offcol_ref[...]] * (tm // LANES), axis=1)
    tau_hi, tau_lo = _tau_pieces(sel, cols + rank)
    taucat = jnp.concatenate([tau_hi, tau_lo], axis=0)
    off_row = offrow_ref[0:1, :]
    end_row = endrow_ref[0:1, :]
    hb = h_ref[...]

    slot = t % 2
    buf = buf_ref.at[slot]
    buf_prev = buf_ref.at[1 - slot]

    def start_piece(piece):
        chunks = range(piece * MOE_CPP, (piece + 1) * MOE_CPP)
        dests = [cdest_ref[0, c] for c in chunks]
        for c, row in zip(chunks, dests):
            dst = sorted_ref.at[pl.ds(pl.multiple_of(row, MOE_CH), MOE_CH), :]
            pltpu.make_async_copy(buf.at[pl.ds(c * MOE_CH, MOE_CH), :], dst, sem.at[slot]).start()

    def wait_piece(piece, which_buf, which_sem):
        rows = pl.ds(piece * MOE_PIECE, MOE_PIECE)
        pltpu.make_async_copy(which_buf.at[rows, :], sorted_ref.at[rows, :], which_sem).wait()

    for piece in range(MOE_NPIECE):
        @pl.when(piece * MOE_CPP < n_used)
        def _():
            if piece > 0:
                start_piece(piece - 1)
            base = piece * MOE_PIECE
            s_col = (base + lax.broadcasted_iota(jnp.int32, (MOE_PIECE, LANES), 0)).astype(F32)
            onehot = jnp.where((s_col >= off_row) & (s_col < end_row), 1.0, 0.0).astype(BF16)
            q = _dot(onehot, taucat)
            s_mat = (base + lax.broadcasted_iota(jnp.int32, (MOE_PIECE, tm), 0)).astype(F32)
            g = jnp.where(q == s_mat, 1.0, 0.0).astype(BF16)
            buf[pl.ds(base, MOE_PIECE), :] = _dot(g, hb).astype(BF16)

    last = (n_used - 1) // MOE_CPP
    for piece in range(MOE_NPIECE):
        @pl.when(piece == last)
        def _():
            start_piece(piece)

    n_prev = nused_ref[jnp.maximum(t - 1, 0)]
    for piece in range(MOE_NPIECE):
        @pl.when((t > 0) & (piece * MOE_CPP < n_prev))
        def _():
            wait_piece(piece, buf_prev, sem.at[1 - slot])

    for piece in range(MOE_NPIECE):
        @pl.when((t == pl.num_programs(0) - 1) & (piece * MOE_CPP < n_used))
        def _():
            wait_piece(piece, buf, sem.at[slot])


def _dispatch(h2, combt, plan, n_tiles):
    n, d = h2.shape
    r_total = plan["r_max"] * MOE_RB + 2 * MOE_SLOTS
    grid_spec = pltpu.PrefetchScalarGridSpec(
        num_scalar_prefetch=1,
        grid=(n_tiles,),
        in_specs=[pl.BlockSpec((None, 1, MOE_NCHUNK), lambda t, nu: (t, 0, 0), memory_space=pltpu.SMEM),
                  pl.BlockSpec((MOE_TM, d), lambda t, nu: (t, 0)),
                  pl.BlockSpec((N_EXPERTS, MOE_TM), lambda t, nu: (0, t)),
                  pl.BlockSpec((None, N_EXPERTS, LANES), lambda t, nu: (t, 0, 0)),
                  pl.BlockSpec((None, 8, LANES), lambda t, nu: (t, 0, 0)),
                  pl.BlockSpec((None, 8, LANES), lambda t, nu: (t, 0, 0))],
        out_specs=pl.BlockSpec(memory_space=pl.ANY),
        scratch_shapes=[pltpu.VMEM((2, MOE_SLOTS, d), BF16), pltpu.SemaphoreType.DMA((2,))],
    )
    return pl.pallas_call(
        _dispatch_body,
        grid_spec=grid_spec,
        out_shape=jax.ShapeDtypeStruct((r_total, d), BF16),
        compiler_params=_cparams(1),
        name="moe_dispatch",
    )(plan["n_used"], plan["cdest_d"], h2, combt, plan["off_col"], plan["off_row2"], plan["end_row2"])


def _ffn_body(texp_ref, nact_ref, valid_ref, x_ref, wg_ref, wu_ref, wd_ref, o_ref):
    r = pl.program_id(0)

    @pl.when(r < nact_ref[0])
    def _():
        x = x_ref[...]
        rows = lax.broadcasted_iota(jnp.int32, x.shape, 0)
        x = jnp.where(rows < valid_ref[r], x, jnp.zeros_like(x))
        a = _silu(_dot(x, wg_ref[...])) * _dot(x, wu_ref[...])
        o_ref[...] = _dot(a.astype(BF16), wd_ref[...]).astype(o_ref.dtype)


def _ffn(xs, wg, wu, wd, plan):
    r_total, d = xs.shape
    ff = wg.shape[2]
    last = lambda r, te, na, va: (jnp.minimum(r, na[0] - 1), 0)
    wmap = lambda r, te, na, va: (te[r], 0, 0)
    grid_spec = pltpu.PrefetchScalarGridSpec(
        num_scalar_prefetch=3,
        grid=(plan["r_max"],),
        in_specs=[pl.BlockSpec((MOE_RB, d), last),
                  pl.BlockSpec((None, d, ff), wmap), pl.BlockSpec((None, d, ff), wmap),
                  pl.BlockSpec((None, ff, d), wmap)],
        out_specs=pl.BlockSpec((MOE_RB, d), last),
    )
    return pl.pallas_call(
        _ffn_body,
        grid_spec=grid_spec,
        out_shape=jax.ShapeDtypeStruct((r_total, d), BF16),
        compiler_params=_cparams(1),
        name="moe_ffn",
    )(plan["tile_expert"], plan["n_active"], plan["valid"], xs, wg, wu, wd)


def _combine_body(nused_ref, cdest_ref, cnext_ref, h_ref, comb_ref, x1_ref, gate_ref, offrow_ref, offcol_ref,
                  endcol_ref, sg_ref, su_ref, sd_ref, ys_ref, y_ref, buf_ref, sem):
    t = pl.program_id(0)
    n_tiles = pl.num_programs(0)
    n_used = nused_ref[t]
    tm = h_ref.shape[0]
    slot = t % 2
    buf = buf_ref.at[slot]

    def fetch_tile(table_ref, count, which):
        for piece in range(MOE_NPIECE):
            @pl.when(piece * MOE_CPP < count)
            def _():
                chunks = range(piece * MOE_CPP, (piece + 1) * MOE_CPP)
                srcs = [table_ref[0, c] for c in chunks]
                for c, row in zip(chunks, srcs):
                    src = ys_ref.at[pl.ds(pl.multiple_of(row, MOE_CH), MOE_CH), :]
                    dst = buf_ref.at[which, pl.ds(c * MOE_CH, MOE_CH), :]
                    pltpu.make_async_copy(src, dst, sem.at[which, piece]).start()

    def wait_piece(piece):
        rows = pl.ds(piece * MOE_PIECE, MOE_PIECE)
        pltpu.make_async_copy(ys_ref.at[rows, :], buf.at[rows, :], sem.at[slot, piece]).wait()

    @pl.when(t == 0)
    def _():
        buf_ref[...] = jnp.zeros_like(buf_ref)
        fetch_tile(cdest_ref, n_used, 0)

    @pl.when(t + 1 < n_tiles)
    def _():
        fetch_tile(cnext_ref, nused_ref[jnp.minimum(t + 1, n_tiles - 1)], 1 - slot)

    hb = h_ref[...]
    acc = _dot((_silu(_dot(hb, sg_ref[...])) * _dot(hb, su_ref[...])).astype(BF16), sd_ref[...])
    comb = comb_ref[...]
    sel = comb > 0.0
    rr = lax.broadcasted_iota(jnp.int32, (tm, tm), 0)
    cc = lax.broadcasted_iota(jnp.int32, (tm, tm), 1)
    lower = jnp.where(cc < rr, 1.0, 0.0).astype(BF16)
    rank = _dot(lower, jnp.where(sel, 1.0, 0.0).astype(BF16))
    tau_hi, tau_lo = _tau_pieces(sel, offrow_ref[0:1, :] + rank)
    taucat = jnp.concatenate([tau_hi, tau_lo], axis=1)
    lhs = jnp.concatenate([taucat, jnp.concatenate([comb.astype(BF16), jnp.zeros((tm, LANES), BF16)], axis=1)],
                          axis=0)
    off_col = offcol_ref[...]
    end_col = endcol_ref[...]
    s_row = lax.broadcasted_iota(jnp.int32, (LANES, MOE_SLOTS), 1).astype(F32)
    off_b = jnp.concatenate([off_col] * (MOE_SLOTS // LANES), axis=1)
    end_b = jnp.concatenate([end_col] * (MOE_SLOTS // LANES), axis=1)
    onehot = jnp.where((s_row >= off_b) & (s_row < end_b), 1.0, 0.0).astype(BF16)
    qw = _dot(lhs, jnp.concatenate([onehot, onehot], axis=0))
    s_mat = lax.broadcasted_iota(jnp.int32, (tm, MOE_SLOTS), 1).astype(F32)
    gw = jnp.where(qw[0:tm] == s_mat, qw[tm:2 * tm], 0.0).astype(BF16)

    for piece in range(MOE_NPIECE):
        @pl.when(piece * MOE_CPP < n_used)
        def _():
            wait_piece(piece)

    y_ref[...] = x1_ref[...] + gate_ref[...] * (acc + _dot(gw, buf[...]))


def _combine(h2, comb, x1, gate, ys, sg, su, sd, plan, n_tiles):
    nb, s, d = x1.shape
    n = nb * s
    per_b = s // MOE_TM
    x1f = x1.reshape(n, d)
    const2 = lambda t, nu: (0, 0)
    grid_spec = pltpu.PrefetchScalarGridSpec(
        num_scalar_prefetch=1,
        grid=(n_tiles,),
        in_specs=[pl.BlockSpec((None, 1, MOE_NCHUNK), lambda t, nu: (t, 0, 0), memory_space=pltpu.SMEM),
                  pl.BlockSpec((None, 1, MOE_NCHUNK), lambda t, nu: (jnp.minimum(t + 1, n_tiles - 1), 0, 0),
                               memory_space=pltpu.SMEM),
                  pl.BlockSpec((MOE_TM, d), lambda t, nu: (t, 0)),
                  pl.BlockSpec((MOE_TM, LANES), lambda t, nu: (t, 0)),
                  pl.BlockSpec((MOE_TM, d), lambda t, nu: (t, 0)),
                  pl.BlockSpec((None, 1, d), lambda t, nu: (t // per_b, 0, 0)),
                  pl.BlockSpec((None, 8, LANES), lambda t, nu: (t, 0, 0)),
                  pl.BlockSpec((None, LANES, LANES), lambda t, nu: (t, 0, 0)),
                  pl.BlockSpec((None, LANES, LANES), lambda t, nu: (t, 0, 0)),
                  pl.BlockSpec(sg.shape, const2), pl.BlockSpec(su.shape, const2), pl.BlockSpec(sd.shape, const2),
                  pl.BlockSpec(memory_space=pl.ANY)],
        out_specs=pl.BlockSpec((MOE_TM, d), lambda t, nu: (t, 0)),
        scratch_shapes=[pltpu.VMEM((2, MOE_SLOTS, d), BF16),
                        pltpu.SemaphoreType.DMA((2, MOE_NPIECE))],
    )
    y = pl.pallas_call(
        _combine_body,
        grid_spec=grid_spec,
        out_shape=jax.ShapeDtypeStruct((n, d), F32),
        compiler_params=_cparams(1),
        name="moe_combine",
    )(plan["n_used"], plan["cdest_c"], plan["cdest_c"], h2, comb, x1f, gate, plan["off_row1"], plan["off_col128"],
      plan["end_col128"],
      sg, su, sd, ys)
    return y.reshape(nb, s, d)


def _moe_sparse(h2, comb, combt, cnt, x1, gate, wg, wu, wd, sg, su, sd):
    n = h2.shape[0]
    n_tiles = n // MOE_TM
    plan = _moe_plan(cnt[:, :, 0].astype(jnp.int32), n_tiles)
    xs = _dispatch(h2, combt, plan, n_tiles)
    ys = _ffn(xs, wg, wu, wd, plan)
    return _combine(h2, comb, x1, gate, ys, sg, su, sd, plan, n_tiles)


def _tile_heads(g, mult=1.0):
    return (jnp.tile(g.astype(F32), N_HEADS) * mult).reshape(1, WIDTH)


def kernel(x_prompt, x_sample, cache_fox_k, cache_fox_v, cache_fox_logf, cache_band_k, cache_band_v, c_prompt, c_sample, w_ada, b_ada, norm1_g, norm2_g, w_in, b_forget, g_q_fox, g_k_fox, g_q_band, g_k_band, rel_bias, out_g_fox, out_g_band, w_out, w_router, b_router, w_gate, w_up, w_down, ws_gate, ws_up, ws_down):
    depth = w_ada.shape[0]
    assert depth == 1
    bsz, seq, d = x_prompt.shape
    dbs, dseq, _ = x_sample.shape
    past = cache_fox_k.shape[2]
    n_cache = cache_band_k.shape[2]
    assert n_cache == BAND_REACH and dseq == CHUNK and seq % BAND_REACH == 0

    wi = w_in[0]
    cols = [wi[:, 0:512], wi[:, 512:1024], wi[:, 1024:1536], wi[:, 1544:2056], wi[:, 2056:2568], wi[:, 2568:3080],
            wi[:, 1536:1544], jnp.zeros((d, LANES - N_HEADS), F32)]
    w_all = jnp.concatenate(cols, axis=1).astype(BF16)
    hd = jnp.arange(WIDTH) // HEAD_DIM
    bd = jnp.where(hd[:, None] == hd[None, :], 1.0 / HEAD_DIM, 0.0).astype(BF16)
    qscale = ATTN_SCALE * LOG2E
    gqf, gkf = _tile_heads(g_q_fox[0], qscale), _tile_heads(g_k_fox[0])
    gqb, gkb = _tile_heads(g_q_band[0], qscale), _tile_heads(g_k_band[0])
    bf_row = jnp.concatenate([b_forget[0], jnp.zeros((LANES - N_HEADS,), F32)]).reshape(1, LANES)
    g1 = norm1_g[0].reshape(1, d)
    g2 = norm2_g[0].reshape(1, d)
    ogf = out_g_fox[0].reshape(1, WIDTH)
    ogb = out_g_band[0].reshape(1, WIDTH)
    wo = w_out[0].astype(BF16)
    wr_t = w_router[0].T
    wr_hi = wr_t.astype(BF16)
    wr_lo = (wr_t - wr_hi.astype(F32)).astype(BF16)
    br = jnp.broadcast_to(b_router[0].reshape(N_EXPERTS, 1), (N_EXPERTS, LANES)).astype(F32)
    wg, wu, wd = w_gate[0].astype(BF16), w_up[0].astype(BF16), w_down[0].astype(BF16)
    sg, su, sd = ws_gate[0].astype(BF16), ws_up[0].astype(BF16), ws_down[0].astype(BF16)

    n_c = bsz + dbs
    rows = -(-n_c // 8) * 8
    c_all = jnp.concatenate([c_prompt, c_sample, jnp.zeros((rows - n_c, d), F32)], axis=0)
    mod = _ada(c_all, w_ada[0], b_ada[0].reshape(1, -1))

    def mods(lo, hi):
        return [mod[lo:hi, j * d:(j + 1) * d].reshape(hi - lo, 1, d) for j in range(6)]

    shift1_p, scale1_p, gate1_p, shift2_p, scale2_p, gate2_p = mods(0, bsz)
    shift1_s, scale1_s, gate1_s, shift2_s, scale2_s, gate2_s = mods(bsz, n_c)

    TM = BAND_REACH
    (qf, kf, vft, kf32, vf32, lf, qb, kb, vbt, kb32, vb32) = _proj(
        x_prompt, shift1_p, scale1_p, g1, w_all, bd, gqf, gkf, gqb, gkb, bf_row, G=1, R=TM, band_last_only=True)
    r3 = lambda a: a.reshape(bsz, seq, a.shape[-1])
    ct, ka = _scan_t(r3(lf), T=TM)
    of = _foxt(r3(qf), r3(kf), ka, vft, ct, T=TM)
    assert TM == BAND_STEP
    ob = _bandt(r3(qb), r3(kb), vbt, _band_bias_tile_t(rel_bias[0]))
    assert TM % MOE_TM == 0
    x1_p, h2_p, comb_p, combt_p, cnt_p = _merge(of.reshape(-1, WIDTH), ob.reshape(-1, WIDTH), x_prompt, gate1_p,
                                                shift2_p, scale2_p, ogf, ogb, wo, g2, wr_hi, wr_lo, br, G=1, R=TM)
    y_p = _moe_sparse(h2_p, comb_p, combt_p, cnt_p, x1_p, gate2_p, wg, wu, wd, sg, su, sd)

    GS = 8
    (qf_s, kf_s, vf_s, kf32_s, vf32_s, lf_s, qb_s, kb_s, vb_s, kb32_s, vb32_s) = _proj(
        x_sample, shift1_s, scale1_s, g1, w_all, bd, gqf, gkf, gqb, gkb, bf_row, G=GS, R=dseq,
        band_last_only=False)
    s3 = lambda a: a.reshape(dbs, dseq, a.shape[-1])
    sk = past + dseq
    skp = -(-sk // LANES) * LANES
    pad_k = skp - sk
    n_seq = dbs * N_HEADS
    assert n_seq <= LANES
    lf_seq = jnp.concatenate([cache_fox_logf[0], s3(lf_s)[:, :, :N_HEADS]], axis=1)
    lf_seq = jnp.swapaxes(lf_seq, 0, 1).reshape(sk, n_seq)
    lf_seq = jnp.pad(lf_seq, ((0, pad_k), (0, LANES - n_seq)))
    cum_col, cum_row = _scan(lf_seq)
    cq_s = jnp.swapaxes(cum_col[past:past + dseq, :n_seq].reshape(dseq, dbs, N_HEADS), 0, 1)
    cum_s = jnp.pad(cq_s, ((0, 0), (0, 0), (0, LANES - N_HEADS)))
    cumt_s = cum_row[:n_seq].reshape(dbs, 1, N_HEADS, skp)

    def cache_t(c):
        return jnp.transpose(c, (0, 2, 3, 1)).reshape(dbs, N_PAIRS, PAIR, c.shape[1])

    of_s = _sample_attn(s3(qf_s), cache_t(cache_fox_k[0]), cache_t(cache_fox_v[0]), s3(kf_s), s3(vf_s),
                        cum_s, cumt_s, fox=True)
    bias_s = _band_bias_tile(rel_bias[0], dseq, BAND_REACH + LANES)
    ob_s = _sample_attn(s3(qb_s), cache_t(cache_band_k[0]), cache_t(cache_band_v[0]), s3(kb_s), s3(vb_s),
                        bias_s, fox=False)
    x1_s, h2_s, comb_s, _, _ = _merge(of_s.reshape(-1, WIDTH), ob_s.reshape(-1, WIDTH), x_sample, gate1_s, shift2_s,
                                      scale2_s, ogf, ogb, wo, g2, wr_hi, wr_lo, br, G=GS, R=dseq)
    y_s = _moe(h2_s, comb_s, x1_s, gate2_s, wg, wu, wd, sg, su, sd, G=dbs, R=dseq)

    hshape = (N_HEADS, HEAD_DIM)
    new_bk_s = jnp.concatenate([cache_band_k[0], s3(kb32_s).reshape(dbs, dseq, *hshape)], axis=1)[:, -n_cache:]
    new_bv_s = jnp.concatenate([cache_band_v[0], s3(vb32_s).reshape(dbs, dseq, *hshape)], axis=1)[:, -n_cache:]
    return (y_p, y_s,
            kf32.reshape(1, bsz, seq, *hshape), vf32.reshape(1, bsz, seq, *hshape),
            lf[:, :N_HEADS].reshape(1, bsz, seq, N_HEADS),
            kb32.reshape(1, bsz, BAND_REACH, *hshape), vb32.reshape(1, bsz, BAND_REACH, *hshape),
            kf32_s.reshape(1, dbs, dseq, *hshape), vf32_s.reshape(1, dbs, dseq, *hshape),
            lf_s[:, :N_HEADS].reshape(1, dbs, dseq, N_HEADS),
            new_bk_s[None], new_bv_s[None])
```

```python
import functools

import jax
import jax.numpy as jnp
import numpy as np
from jax import lax
from jax.experimental import pallas as pl
from jax.experimental.pallas import tpu as pltpu

F32 = jnp.float32
BF16 = jnp.bfloat16

HEAD_DIM = 64
N_HEADS = 8
WIDTH = N_HEADS * HEAD_DIM
PAIR = 2 * HEAD_DIM
N_PAIRS = N_HEADS // 2
LANES = 128
CHUNK = 64
BAND_REACH = 512
REL_CLIP = 256
N_EXPERTS = 64
N_GROUPS = 8
GROUP_SIZE = N_EXPERTS // N_GROUPS
TOPK_GROUPS = 4
TOP_K = 8
ROUTED_SCALE = 2.5
EPS = 1e-6
NEG_INF = -1e30
ATTN_SCALE = HEAD_DIM ** -0.5
LOG2E = 1.4426950408889634
VMEM_LIMIT = 56 * 1024 * 1024


def _cparams(n_axes):
    return pltpu.CompilerParams(dimension_semantics=("arbitrary",) * n_axes,
                                vmem_limit_bytes=VMEM_LIMIT)


def _dot(a, b):
    return jnp.dot(a, b, preferred_element_type=F32)


def _dot_nt(a, b):
    return lax.dot_general(a, b, (((1,), (1,)), ((), ())), preferred_element_type=F32)


def _split2(a):
    hi = a.astype(BF16)
    lo = (a - hi.astype(F32)).astype(BF16)
    return hi, lo


def _split3(a):
    hi = a.astype(BF16)
    r = a - hi.astype(F32)
    mid = r.astype(BF16)
    lo = (r - mid.astype(F32)).astype(BF16)
    return hi, mid, lo


def _ada_body(c_ref, w_ref, b_ref, o_ref):
    c = c_ref[...]
    a = c * jax.nn.sigmoid(c)
    a_hi, a_lo = _split2(a)
    w_hi, w_lo = _split2(w_ref[...])
    o_ref[...] = _dot(a_hi, w_hi) + _dot(a_hi, w_lo) + _dot(a_lo, w_hi) + b_ref[...]


def _ada(c_all, w_ada, b_ada):
    rows, d = c_all.shape
    n = w_ada.shape[1]
    tn = 1024
    return pl.pallas_call(
        _ada_body,
        grid=(n // tn,),
        in_specs=[pl.BlockSpec((rows, d), lambda j: (0, 0)),
                  pl.BlockSpec((d, tn), lambda j: (0, j)),
                  pl.BlockSpec((1, tn), lambda j: (0, j))],
        out_specs=pl.BlockSpec((rows, tn), lambda j: (0, j)),
        out_shape=jax.ShapeDtypeStruct((rows, n), F32),
        compiler_params=_cparams(1),
        name="ada",
    )(c_all, w_ada, b_ada)


def _log_sigmoid(z):
    return jnp.minimum(z, 0.0) - jnp.log(1.0 + jnp.exp(-jnp.abs(z)))


def _proj_body(x_ref, sh_ref, sc_ref, g1_ref, w_ref, bd_ref, gqf_ref, gkf_ref, gqb_ref, gkb_ref, bf_ref,
               qf_ref, kf_ref, vf_ref, kf32_ref, vf32_ref, lf_ref, qb_ref, kb_ref, vb_ref, kb32_ref, vb32_ref,
               *, band_last_only):
    x = x_ref[...]
    g, r, d = x.shape
    ms = jnp.mean(x * x, axis=-1, keepdims=True)
    h = x * lax.rsqrt(ms + EPS) * g1_ref[...] * (1.0 + sc_ref[...]) + sh_ref[...]
    hb = h.reshape(g * r, d).astype(BF16)

    def seg(i):
        return _dot(hb, w_ref[:, i * WIDTH:(i + 1) * WIDTH])

    def head_norm(t, gain_ref):
        ssq = _dot((t * t).astype(BF16), bd_ref[...])
        return t * lax.rsqrt(ssq + EPS) * gain_ref[...]

    qf_ref[...] = head_norm(seg(0), gqf_ref).astype(BF16)
    kf = head_norm(seg(1), gkf_ref)
    kf_ref[...] = kf.astype(BF16)
    vf = seg(2)
    for hh in range(N_HEADS):
        kf32_ref[:, hh, :] = kf[:, hh * HEAD_DIM:(hh + 1) * HEAD_DIM]
        vf32_ref[:, hh, :] = vf[:, hh * HEAD_DIM:(hh + 1) * HEAD_DIM]
    if band_last_only:
        vf_ref[...] = vf.T.astype(BF16)
    else:
        vf_ref[...] = vf.astype(BF16)
    z = _dot(hb, w_ref[:, 6 * WIDTH:6 * WIDTH + LANES]) + bf_ref[...]
    lf_ref[...] = _log_sigmoid(z)
    qb_ref[...] = head_norm(seg(3), gqb_ref).astype(BF16)
    kb = head_norm(seg(4), gkb_ref)
    kb_ref[...] = kb.astype(BF16)
    vb = seg(5)
    if band_last_only:
        vb_ref[...] = vb.T.astype(BF16)
    else:
        vb_ref[...] = vb.astype(BF16)

    if band_last_only:
        @pl.when(pl.program_id(1) == pl.num_programs(1) - 1)
        def _():
            kb32_ref[...] = kb
            vb32_ref[...] = vb
    else:
        kb32_ref[...] = kb
        vb32_ref[...] = vb


def _proj(x, shift, scale, g1, w_all, bd, gqf, gkf, gqb, gkb, bf_row, *, G, R, band_last_only):
    nb, s, d = x.shape
    n = nb * s
    tm = G * R
    nbi, nsi = nb // G, s // R
    grid = (nbi, nsi)
    row = lambda b, i: (b * nsi + i, 0)
    const = lambda b, i: (0, 0)
    mod_spec = pl.BlockSpec((G, 1, d), lambda b, i: (b, 0, 0))
    out_bf = jax.ShapeDtypeStruct((n, WIDTH), BF16)
    out_f32 = jax.ShapeDtypeStruct((n, WIDTH), F32)
    tile = pl.BlockSpec((tm, WIDTH), row)
    state_shape = jax.ShapeDtypeStruct((n, N_HEADS, HEAD_DIM), F32)
    state_spec = pl.BlockSpec((tm, N_HEADS, HEAD_DIM), lambda b, i: (b * nsi + i, 0, 0))
    if band_last_only:
        assert G == 1 and R == BAND_REACH
        band_shape = jax.ShapeDtypeStruct((nb, BAND_REACH, WIDTH), F32)
        band_spec = pl.BlockSpec((None, BAND_REACH, WIDTH), lambda b, i: (b, 0, 0))
        v_shape = jax.ShapeDtypeStruct((nb, nsi, WIDTH, tm), BF16)
        v_spec = pl.BlockSpec((None, None, WIDTH, tm), lambda b, i: (b, i, 0, 0))
    else:
        band_shape, band_spec = out_f32, tile
        v_shape, v_spec = out_bf, tile
    return pl.pallas_call(
        functools.partial(_proj_body, band_last_only=band_last_only),
        grid=grid,
        in_specs=[pl.BlockSpec((G, R, d), lambda b, i: (b, i, 0)), mod_spec, mod_spec,
                  pl.BlockSpec((1, d), const), pl.BlockSpec(w_all.shape, const), pl.BlockSpec(bd.shape, const),
                  pl.BlockSpec((1, WIDTH), const), pl.BlockSpec((1, WIDTH), const),
                  pl.BlockSpec((1, WIDTH), const), pl.BlockSpec((1, WIDTH), const),
                  pl.BlockSpec((1, LANES), const)],
        out_specs=[tile, tile, v_spec, state_spec, state_spec, pl.BlockSpec((tm, LANES), row), tile, tile, v_spec,
                   band_spec, band_spec],
        out_shape=[out_bf, out_bf, v_shape, state_shape, state_shape, jax.ShapeDtypeStruct((n, LANES), F32),
                   out_bf, out_bf, v_shape, band_shape, band_shape],
        compiler_params=_cparams(2),
        name="proj",
    )(x, shift, scale, g1, w_all, bd, gqf, gkf, gqb, gkb, bf_row)


def _scan_body(lf_ref, cum_ref, cumt_ref):
    lf = lf_ref[...]
    s = lf.shape[0]
    hi, mid, lo = _split3(lf)
    rr = lax.broadcasted_iota(jnp.int32, (s, s), 0)
    cc = lax.broadcasted_iota(jnp.int32, (s, s), 1)
    tri = jnp.where(cc <= rr, 1.0, 0.0).astype(BF16)
    cum2 = (_dot(tri, hi) + _dot(tri, mid) + _dot(tri, lo)) * LOG2E
    cum_ref[...] = cum2
    cumt_ref[...] = cum2.T


def _scan(lf):
    s, _ = lf.shape
    return pl.pallas_call(
        _scan_body,
        out_shape=[jax.ShapeDtypeStruct((s, LANES), F32), jax.ShapeDtypeStruct((LANES, s), F32)],
        compiler_params=pltpu.CompilerParams(vmem_limit_bytes=VMEM_LIMIT),
        name="scan",
    )(lf)


AUG_PIECES = 3
FOX_UNDERFLOW = 160.0
FOX_NORM_SLACK = 1.02
FOX_BOUND_SLACK = 2.0


def _scan_t_body(lf_ref, place_ref, ct_ref, ka_ref, carry_ref):
    @pl.when(pl.program_id(1) == 0)
    def _():
        carry_ref[...] = jnp.zeros_like(carry_ref)

    lf = lf_ref[...]
    ts = lf.shape[0]
    lane = lax.broadcasted_iota(jnp.int32, lf.shape, 1)
    lf = jnp.where(lane < N_HEADS, lf, 0.0)
    hi, mid, lo = _split3(lf)
    rr = lax.broadcasted_iota(jnp.int32, (ts, ts), 0)
    cc = lax.broadcasted_iota(jnp.int32, (ts, ts), 1)
    tri = jnp.where(cc <= rr, 1.0, 0.0).astype(BF16)
    cum = _dot(tri, hi) + _dot(tri, mid) + _dot(tri, lo) + carry_ref[0:1, :]
    carry_ref[...] = jnp.broadcast_to(cum[ts - 1:ts, :], carry_ref.shape)
    cum2 = cum * LOG2E
    ct_ref[...] = cum2.T[0:N_HEADS, :]
    pieces = _split3(cum2 - cum2[0:1, :])
    ka = _dot(pieces[0], place_ref[0]) + _dot(pieces[1], place_ref[1]) + _dot(pieces[2], place_ref[2])
    ka_ref[...] = ka.astype(BF16)


def _aug_placement():
    h = jnp.arange(LANES)[:, None]
    col = jnp.arange(WIDTH)[None, :]
    mats = []
    for x in range(AUG_PIECES):
        tgt = PAIR * (h // 2) + AUG_PIECES * (h % 2) + x
        mats.append(jnp.where((h < N_HEADS) & (col == tgt), 1.0, 0.0))
    return jnp.stack(mats).astype(BF16)


def _scan_t(lf, *, T):
    b, s, _ = lf.shape
    place = _aug_placement()
    return pl.pallas_call(
        _scan_t_body,
        grid=(b, s // T),
        in_specs=[pl.BlockSpec((None, T, LANES), lambda bi, i: (bi, i, 0)),
                  pl.BlockSpec(place.shape, lambda bi, i: (0, 0, 0))],
        out_specs=[pl.BlockSpec((None, None, N_HEADS, T), lambda bi, i: (bi, i, 0, 0)),
                   pl.BlockSpec((None, T, WIDTH), lambda bi, i: (bi, i, 0))],
        out_shape=[jax.ShapeDtypeStruct((b, s // T, N_HEADS, T), F32),
                   jax.ShapeDtypeStruct((b, s, WIDTH), BF16)],
        scratch_shapes=[pltpu.VMEM((8, LANES), F32)],
        compiler_params=_cparams(2),
        name="scan_t",
    )(lf, place)


def _foxt_body(q_ref, k_ref, ka_ref, vt_ref, ct_ref, o_ref, kn_ref, *, T):
    p = pl.program_id(1)
    i = pl.program_id(2)
    q = q_ref[...]
    lane = lax.broadcasted_iota(jnp.int32, (T, PAIR), 1)
    halves = []
    for par in range(2):
        qm = jnp.where((lane >= HEAD_DIM) == (par == 1), q, jnp.zeros_like(q))
        lo_lane = AUG_PIECES * par
        qa = jnp.where((lane >= lo_lane) & (lane < lo_lane + AUG_PIECES), -1.0, 0.0).astype(BF16)
        halves.append(jnp.concatenate([qm, qa], axis=1))
    qcat = jnp.concatenate(halves, axis=0)
    h_even = 2 * p
    cq = jnp.concatenate([ct_ref[i, pl.ds(h_even, 1), :], ct_ref[i, pl.ds(h_even + 1, 1), :]], axis=1)
    ones_rows = jnp.ones((16, T), BF16)
    krow = lax.broadcasted_iota(jnp.int32, (T, 2 * T), 0)
    qcol = lax.broadcasted_iota(jnp.int32, (T, 2 * T), 1) % T

    def step(j, carry, masked):
        m, acc_e, acc_o = carry
        k0 = pl.multiple_of(j * T, T)
        kcat = jnp.concatenate([k_ref[pl.ds(k0, T), :], ka_ref[pl.ds(k0, T), :]], axis=1)
        st = _dot_nt(kcat, qcat)
        c0 = jnp.concatenate([jnp.broadcast_to(ct_ref[j, pl.ds(h_even, 1), :][:, 0:1], (1, T)),
                              jnp.broadcast_to(ct_ref[j, pl.ds(h_even + 1, 1), :][:, 0:1], (1, T))], axis=1)
        rb = cq - c0
        if masked:
            st = jnp.where(krow <= qcol, st, NEG_INF)
        m_new = jnp.maximum(m, jnp.max(st, axis=0, keepdims=True) + rb)
        alpha = jnp.exp2(m - m_new)
        pt = jnp.exp2(st + (rb - m_new)).astype(BF16)
        vt = vt_ref[j]
        pv_e = _dot(jnp.concatenate([vt[0:HEAD_DIM], ones_rows], axis=0), pt[:, 0:T])
        pv_o = _dot(jnp.concatenate([vt[HEAD_DIM:PAIR], ones_rows], axis=0), pt[:, T:2 * T])
        return m_new, alpha[:, 0:T] * acc_e + pv_e, alpha[:, T:2 * T] * acc_o + pv_o

    @pl.when(i == 0)
    def _():
        ones = jnp.ones((PAIR, LANES), BF16)
        kmax = jnp.zeros((1, LANES), F32)
        for c in range(k_ref.shape[0] // T):
            kc = k_ref[c * T:(c + 1) * T, :].astype(F32)
            kmax = jnp.maximum(kmax, jnp.max(_dot((kc * kc).astype(BF16), ones), axis=0, keepdims=True))
        kn_ref[...] = jnp.broadcast_to(kmax, kn_ref.shape)

    rows = HEAD_DIM + 16
    init = (jnp.full((1, 2 * T), NEG_INF, F32), jnp.zeros((rows, T), F32), jnp.zeros((rows, T), F32))
    carry = step(i, init, True)

    qf = q.astype(F32)
    qsq = qf * qf
    kn2 = kn_ref[0:1, 0:1] * FOX_NORM_SLACK
    need = jnp.zeros((1, 1), jnp.int32)
    blk = lax.broadcasted_iota(jnp.int32, (ct_ref.shape[0], 1, 1), 0)
    for par in range(2):
        head_lanes = (lane >= HEAD_DIM) == (par == 1)
        qn2 = jnp.max(jnp.sum(jnp.where(head_lanes, qsq, 0.0), axis=1, keepdims=True), axis=0, keepdims=True)
        reach = jnp.sqrt(qn2 * kn2) + FOX_BOUND_SLACK
        m_min = jnp.min(carry[0][:, par * T:(par + 1) * T], axis=1, keepdims=True)
        cq_first = cq[:, par * T:par * T + 1]
        ck_end = ct_ref[:, pl.ds(h_even + par, 1), :][:, :, T - 1:T]
        live = (reach + cq_first - m_min)[None, :, :] - ck_end > -FOX_UNDERFLOW
        count = jnp.sum(jnp.where(live & (blk < i), 1, 0), axis=0)
        need = jnp.maximum(need, count)
    n_keep = need[0, 0]

    n_pairs = n_keep // 2
    carry = lax.fori_loop(0, n_pairs, lambda u, c: step(i - 2 - 2 * u, step(i - 1 - 2 * u, c, False), False), carry)
    carry = lax.fori_loop(2 * n_pairs, n_keep, lambda u, c: step(i - 1 - u, c, False), carry)
    _, acc_e, acc_o = carry
    o_t = jnp.concatenate([acc_e[0:HEAD_DIM] / acc_e[HEAD_DIM:HEAD_DIM + 1],
                           acc_o[0:HEAD_DIM] / acc_o[HEAD_DIM:HEAD_DIM + 1]], axis=0)
    o_ref[...] = o_t.T.astype(o_ref.dtype)


def _foxt(q, k, ka, vt, ct, *, T):
    b, s, _ = q.shape
    nt = s // T
    return pl.pallas_call(
        functools.partial(_foxt_body, T=T),
        grid=(b, N_PAIRS, nt),
        in_specs=[pl.BlockSpec((None, T, PAIR), lambda bi, p, i: (bi, i, p)),
                  pl.BlockSpec((None, s, PAIR), lambda bi, p, i: (bi, 0, p)),
                  pl.BlockSpec((None, s, PAIR), lambda bi, p, i: (bi, 0, p)),
                  pl.BlockSpec((None, nt, PAIR, T), lambda bi, p, i: (bi, 0, p, 0)),
                  pl.BlockSpec((None, nt, N_HEADS, T), lambda bi, p, i: (bi, 0, 0, 0))],
        out_specs=pl.BlockSpec((None, T, PAIR), lambda bi, p, i: (bi, i, p)),
        out_shape=jax.ShapeDtypeStruct((b, s, WIDTH), BF16),
        scratch_shapes=[pltpu.VMEM((8, LANES), F32)],
        compiler_params=_cparams(3),
        name="foxt",
    )(q, k, ka, vt, ct)


BAND_TQ = 256
BAND_KBLK = 512
BAND_STEP = 1024


def _bandt_body(q_ref, *refs):
    n_blk = 1 + BAND_STEP // BAND_KBLK
    k_refs, v_refs, (bias_ref, o_ref) = refs[:n_blk], refs[n_blk:2 * n_blk], refs[2 * n_blk:]
    i = pl.program_id(2)
    w = BAND_REACH + BAND_TQ
    k2 = jnp.concatenate([r[...] for r in k_refs], axis=0)
    vt2 = jnp.concatenate([r[...] for r in v_refs], axis=1)
    lane = lax.broadcasted_iota(jnp.int32, (BAND_TQ, PAIR), 1)
    ones_rows = jnp.ones((16, w), BF16)
    krow = lax.broadcasted_iota(jnp.int32, (w, 2 * BAND_TQ), 0)
    bias = bias_ref[...]
    for sub in range(BAND_STEP // BAND_TQ):
        r0 = sub * BAND_TQ
        q = q_ref[r0:r0 + BAND_TQ, :]
        qcat = jnp.concatenate([jnp.where(lane < HEAD_DIM, q, jnp.zeros_like(q)),
                                jnp.where(lane >= HEAD_DIM, q, jnp.zeros_like(q))], axis=0)
        st = _dot_nt(k2[r0:r0 + w], qcat) + bias
        st = jnp.where(krow >= BAND_KBLK - i * BAND_STEP - r0, st, NEG_INF)
        m = jnp.max(st, axis=0, keepdims=True)
        pt = jnp.exp2(st - m).astype(BF16)
        vwin = vt2[:, r0:r0 + w]
        outs = []
        for par in range(2):
            vcat = jnp.concatenate([vwin[par * HEAD_DIM:(par + 1) * HEAD_DIM], ones_rows], axis=0)
            pv = _dot(vcat, pt[:, par * BAND_TQ:(par + 1) * BAND_TQ])
            outs.append(pv[0:HEAD_DIM] / pv[HEAD_DIM:HEAD_DIM + 1])
        o_ref[r0:r0 + BAND_TQ, :] = jnp.concatenate(outs, axis=0).T.astype(o_ref.dtype)


def _bandt(q, k, vt, bias_t):
    b, s, _ = q.shape
    per = BAND_STEP // BAND_KBLK
    n_blk = 1 + per
    blk = lambda d: (lambda i: jnp.maximum(per * i + d, 0))
    offs = [blk(d) for d in range(-1, per)]
    k_specs = [pl.BlockSpec((None, BAND_KBLK, PAIR), lambda bi, p, i, f=f: (bi, f(i), p)) for f in offs]
    v_specs = [pl.BlockSpec((None, None, PAIR, BAND_KBLK), lambda bi, p, i, f=f: (bi, f(i), p, 0)) for f in offs]
    return pl.pallas_call(
        _bandt_body,
        grid=(b, N_PAIRS, s // BAND_STEP),
        in_specs=[pl.BlockSpec((None, BAND_STEP, PAIR), lambda bi, p, i: (bi, i, p))] + k_specs + v_specs
                 + [pl.BlockSpec((None,) + bias_t.shape[1:], lambda bi, p, i: (p, 0, 0))],
        out_specs=pl.BlockSpec((None, BAND_STEP, PAIR), lambda bi, p, i: (bi, i, p)),
        out_shape=jax.ShapeDtypeStruct((b, s, WIDTH), BF16),
        compiler_params=_cparams(3),
        name="bandt",
    )(q, *([k] * n_blk), *([vt] * n_blk), bias_t)


def _band_bias_tile_t(rel_bias):
    tile = _band_bias_tile(rel_bias, BAND_TQ, BAND_REACH + BAND_TQ)
    t = jnp.swapaxes(tile, 1, 2)
    return jnp.concatenate([t[0::2], t[1::2]], axis=2)


def _band_bias_tile(rel_bias, tq, w):
    span = w + tq - 1
    period = span + 1
    v = np.arange(period)
    d = np.where(v < w, v, v - period)
    table_idx = np.clip(BAND_REACH - d, -REL_CLIP, REL_CLIP) + REL_CLIP
    table = rel_bias[:, table_idx] * LOG2E
    n_h = rel_bias.shape[0]
    vals = jnp.tile(table, (1, tq))[:, :tq * span].reshape(n_h, tq, span)[:, :, :w]
    r = np.arange(tq)[:, None]
    c = np.arange(w)[None, :]
    in_band = (c // CHUNK >= r // CHUNK) & (c // CHUNK <= r // CHUNK + BAND_REACH // CHUNK)
    return jnp.where(jnp.asarray(in_band)[None], vals, NEG_INF).astype(F32)


def _sample_attn_body(q_ref, kc_ref, vc_ref, kn_ref, vn_ref, *rest, fox):
    p = pl.program_id(1)
    t = q_ref.shape[0]
    past = kc_ref.shape[1]
    lane = lax.broadcasted_iota(jnp.int32, (t, PAIR), 1)
    q = q_ref[...]
    qs = jnp.concatenate([jnp.where(lane < HEAD_DIM, q, jnp.zeros_like(q)),
                          jnp.where(lane >= HEAD_DIM, q, jnp.zeros_like(q))], axis=0)
    kc = kc_ref[...].astype(BF16)
    vc = vc_ref[...].astype(BF16)
    s_c = _dot(qs, kc)
    s_n = _dot_nt(qs, kn_ref[...])
    row = lax.broadcasted_iota(jnp.int32, (t, t), 0)
    col = lax.broadcasted_iota(jnp.int32, (t, t), 1)
    if fox:
        cq_ref, ck_ref, o_ref = rest
    else:
        bias_ref, o_ref = rest
    pcs, pns = [], []
    for par in range(2):
        h = 2 * p + par
        sc = s_c[par * t:(par + 1) * t]
        sn = s_n[par * t:(par + 1) * t]
        if fox:
            cq_col = jnp.sum(jnp.where(lane == h, cq_ref[...], 0.0), axis=1, keepdims=True)
            ck = ck_ref[pl.ds(h, 1), :]
            uc = sc + (cq_col - ck[:, 0:past])
            un = jnp.where(col <= row, sn + (cq_col - ck[:, past:past + t]), NEG_INF)
        else:
            bias = bias_ref[h]
            uc = sc + bias[:, 0:past]
            un = sn + bias[:, past:past + t]
        m = jnp.maximum(jnp.max(uc, axis=1, keepdims=True), jnp.max(un, axis=1, keepdims=True))
        pc = jnp.exp2(uc - m)
        pn = jnp.exp2(un - m)
        inv = 1.0 / (jnp.sum(pc, axis=1, keepdims=True) + jnp.sum(pn, axis=1, keepdims=True))
        pcs.append((pc * inv).astype(BF16))
        pns.append((pn * inv).astype(BF16))
    o2 = _dot_nt(jnp.concatenate(pcs, axis=0), vc) + _dot(jnp.concatenate(pns, axis=0), vn_ref[...])
    o_ref[...] = jnp.where(lane < HEAD_DIM, o2[0:t], o2[t:2 * t]).astype(o_ref.dtype)


def _sample_attn(q, kc_t, vc_t, k_new, v_new, *extra, fox):
    b, t, _ = q.shape
    past = kc_t.shape[3]
    pair_rows = pl.BlockSpec((None, t, PAIR), lambda bi, p: (bi, 0, p))
    cache = pl.BlockSpec((None, None, PAIR, past), lambda bi, p: (bi, p, 0, 0))
    if fox:
        cq, ck = extra
        extra_specs = [pl.BlockSpec((None, t, LANES), lambda bi, p: (bi, 0, 0)),
                       pl.BlockSpec((None, None, N_HEADS, ck.shape[3]), lambda bi, p: (bi, 0, 0, 0))]
    else:
        extra_specs = [pl.BlockSpec(extra[0].shape, lambda bi, p: (0, 0, 0))]
    return pl.pallas_call(
        functools.partial(_sample_attn_body, fox=fox),
        grid=(b, N_PAIRS),
        in_specs=[pair_rows, cache, cache, pair_rows, pair_rows] + extra_specs,
        out_specs=pair_rows,
        out_shape=jax.ShapeDtypeStruct((b, t, WIDTH), BF16),
        compiler_params=_cparams(2),
        name="sample_fox" if fox else "sample_band",
    )(q, kc_t, vc_t, k_new, v_new, *extra)


def _first_index(is_max, idx, axis, big):
    return jnp.min(jnp.where(is_max, idx, big), axis=axis, keepdims=True)


def _route(scores, choice):
    t = scores.shape[1]
    c3 = choice.reshape(N_GROUPS, GROUP_SIZE, t)
    j_idx = lax.broadcasted_iota(jnp.int32, c3.shape, 1)
    top1 = jnp.max(c3, axis=1, keepdims=True)
    first = _first_index(c3 == top1, j_idx, 1, GROUP_SIZE)
    top2 = jnp.max(jnp.where(j_idx == first, -jnp.inf, c3), axis=1, keepdims=True)
    gscore = (top1 + top2).reshape(N_GROUPS, t)

    g_idx = lax.broadcasted_iota(jnp.int32, gscore.shape, 0)
    gsel = jnp.zeros(gscore.shape, F32)
    work = gscore
    for _ in range(TOPK_GROUPS):
        gm = jnp.max(work, axis=0, keepdims=True)
        pick = g_idx == _first_index(work == gm, g_idx, 0, N_GROUPS)
        gsel = jnp.where(pick, 1.0, gsel)
        work = jnp.where(pick, -jnp.inf, work)

    emask = jnp.broadcast_to(gsel.reshape(N_GROUPS, 1, t), c3.shape) > 0.0
    work = jnp.where(emask, c3, NEG_INF)
    e_idx = lax.broadcasted_iota(jnp.int32, c3.shape, 0) * GROUP_SIZE + j_idx
    esel = jnp.zeros(c3.shape, F32)
    for _ in range(TOP_K):
        em = jnp.max(jnp.max(work, axis=1, keepdims=True), axis=0, keepdims=True)
        cand = jnp.where(work == em, e_idx, N_EXPERTS)
        first = jnp.min(jnp.min(cand, axis=1, keepdims=True), axis=0, keepdims=True)
        pick = e_idx == first
        esel = jnp.where(pick, 1.0, esel)
        work = jnp.where(pick, -jnp.inf, work)

    w = esel * scores.reshape(c3.shape)
    denom = jnp.sum(jnp.sum(w, axis=1, keepdims=True), axis=0, keepdims=True)
    return (w / denom * ROUTED_SCALE).reshape(N_EXPERTS, t)


def _merge_body(of_ref, ob_ref, x_ref, gate_ref, sh_ref, sc_ref, ogf_ref, ogb_ref, wo_ref, g2_ref,
                wrh_ref, wrl_ref, br_ref, x1_ref, h2_ref, comb_ref, combt_ref, cnt_ref):
    def group_norm(t_ref, gain_ref):
        t = t_ref[...].astype(F32)
        ms = jnp.mean(t * t, axis=-1, keepdims=True)
        return (t * lax.rsqrt(ms + EPS) * gain_ref[...]).astype(BF16)

    y = _dot(group_norm(of_ref, ogf_ref), wo_ref[0:WIDTH, :]) + _dot(group_norm(ob_ref, ogb_ref), wo_ref[WIDTH:, :])
    x = x_ref[...]
    g, r, d = x.shape
    x1 = x + gate_ref[...] * y.reshape(g, r, d)
    x1_ref[...] = x1
    ms = jnp.mean(x1 * x1, axis=-1, keepdims=True)
    h2 = (x1 * lax.rsqrt(ms + EPS) * g2_ref[...] * (1.0 + sc_ref[...]) + sh_ref[...]).reshape(g * r, d)
    h_hi, h_lo = _split2(h2)
    h2_ref[...] = h_hi
    logits = _dot_nt(wrh_ref[...], h_hi) + _dot_nt(wrh_ref[...], h_lo) + _dot_nt(wrl_ref[...], h_hi)
    scores = jax.nn.sigmoid(logits)
    t = scores.shape[1]
    bias = jnp.concatenate([br_ref[...]] * (t // LANES), axis=1)
    comb = _route(scores, scores + bias)
    comb_pad = jnp.concatenate([comb, jnp.zeros((LANES - N_EXPERTS, t), F32)], axis=0)
    comb_ref[...] = comb_pad.T
    combt_ref[...] = comb
    picked = jnp.where(comb > 0.0, 1.0, 0.0)
    for sub in range(cnt_ref.shape[0]):
        cnt = jnp.sum(picked[:, sub * MOE_TM:(sub + 1) * MOE_TM], axis=1, keepdims=True)
        cnt_ref[sub] = jnp.broadcast_to(cnt, (N_EXPERTS, LANES))


def _merge(of, ob, x, gate, shift, scale, ogf, ogb, wo, g2, wr_hi, wr_lo, br, *, G, R):
    nb, s, d = x.shape
    n = nb * s
    tm = G * R
    nbi, nsi = nb // G, s // R
    row = lambda b, i: (b * nsi + i, 0)
    const = lambda b, i: (0, 0)
    mod_spec = pl.BlockSpec((G, 1, d), lambda b, i: (b, 0, 0))
    x_spec = pl.BlockSpec((G, R, d), lambda b, i: (b, i, 0))
    return pl.pallas_call(
        _merge_body,
        grid=(nbi, nsi),
        in_specs=[pl.BlockSpec((tm, WIDTH), row), pl.BlockSpec((tm, WIDTH), row), x_spec,
                  mod_spec, mod_spec, mod_spec,
                  pl.BlockSpec((1, WIDTH), const), pl.BlockSpec((1, WIDTH), const),
                  pl.BlockSpec(wo.shape, const), pl.BlockSpec((1, d), const),
                  pl.BlockSpec(wr_hi.shape, const), pl.BlockSpec(wr_lo.shape, const),
                  pl.BlockSpec(br.shape, const)],
        out_specs=[x_spec, pl.BlockSpec((tm, d), row), pl.BlockSpec((tm, LANES), row),
                   pl.BlockSpec((N_EXPERTS, tm), lambda b, i: (0, b * nsi + i)),
                   pl.BlockSpec((tm // MOE_TM, N_EXPERTS, LANES), lambda b, i: (b * nsi + i, 0, 0))],
        out_shape=[jax.ShapeDtypeStruct((nb, s, d), F32), jax.ShapeDtypeStruct((n, d), BF16),
                   jax.ShapeDtypeStruct((n, LANES), F32), jax.ShapeDtypeStruct((N_EXPERTS, n), F32),
                   jax.ShapeDtypeStruct((n // MOE_TM, N_EXPERTS, LANES), F32)],
        compiler_params=_cparams(2),
        name="merge",
    )(of, ob, x, gate, shift, scale, ogf, ogb, wo, g2, wr_hi, wr_lo, br)


def _silu(g):
    return g * jax.nn.sigmoid(g)


def _moe_body(h_ref, comb_ref, x1_ref, gate_ref, wg_ref, wu_ref, wd_ref, sg_ref, su_ref, sd_ref, y_ref, acc_ref):
    e = pl.program_id(2)
    hb = h_ref[...]

    @pl.when(e == 0)
    def _():
        a = _silu(_dot(hb, sg_ref[...])) * _dot(hb, su_ref[...])
        acc_ref[...] = _dot(a.astype(BF16), sd_ref[...])

    comb = comb_ref[...]
    lane = lax.broadcasted_iota(jnp.int32, comb.shape, 1)
    c_e = jnp.sum(jnp.where(lane == e, comb, 0.0), axis=1, keepdims=True)
    a = _silu(_dot(hb, wg_ref[...])) * _dot(hb, wu_ref[...]) * c_e
    acc_ref[...] += _dot(a.astype(BF16), wd_ref[...])

    @pl.when(e == pl.num_programs(2) - 1)
    def _():
        x1 = x1_ref[...]
        g, r, d = x1.shape
        y_ref[...] = x1 + gate_ref[...] * acc_ref[...].reshape(g, r, d)


def _moe(h2, comb, x1, gate, wg, wu, wd, sg, su, sd, *, G, R):
    nb, s, d = x1.shape
    tm = G * R
    nbi, nsi = nb // G, s // R
    ff = wg.shape[2]
    row = lambda b, i, e: (b * nsi + i, 0)
    const = lambda b, i, e: (0, 0)
    x_spec = pl.BlockSpec((G, R, d), lambda b, i, e: (b, i, 0))
    return pl.pallas_call(
        _moe_body,
        grid=(nbi, nsi, N_EXPERTS),
        in_specs=[pl.BlockSpec((tm, d), row), pl.BlockSpec((tm, LANES), row), x_spec,
                  pl.BlockSpec((G, 1, d), lambda b, i, e: (b, 0, 0)),
                  pl.BlockSpec((None, d, ff), lambda b, i, e: (e, 0, 0)),
                  pl.BlockSpec((None, d, ff), lambda b, i, e: (e, 0, 0)),
                  pl.BlockSpec((None, ff, d), lambda b, i, e: (e, 0, 0)),
                  pl.BlockSpec(sg.shape, const), pl.BlockSpec(su.shape, const), pl.BlockSpec(sd.shape, const)],
        out_specs=x_spec,
        out_shape=jax.ShapeDtypeStruct((nb, s, d), F32),
        scratch_shapes=[pltpu.VMEM((tm, d), F32)],
        compiler_params=_cparams(3),
        name="moe",
    )(h2, comb, x1, gate, wg, wu, wd, sg, su, sd)


MOE_TM = 256
MOE_CH = 16
MOE_SLOTS = TOP_K * MOE_TM + N_EXPERTS * MOE_CH
MOE_NCHUNK = MOE_SLOTS // MOE_CH
MOE_PIECE = 512
MOE_NPIECE = MOE_SLOTS // MOE_PIECE
MOE_CPP = MOE_PIECE // MOE_CH
MOE_RB = 1024
TAU_RADIX = 64.0
assert MOE_SLOTS % MOE_PIECE == 0


def _moe_plan(cnt, n_tiles):
    pc = (cnt + MOE_CH - 1) // MOE_CH * MOE_CH
    off = jnp.cumsum(pc, axis=1) - pc
    end = off + pc
    n_used = (jnp.sum(pc, axis=1) // MOE_CH).astype(jnp.int32)
    tot = jnp.sum(pc, axis=0)
    reg = (tot + MOE_RB - 1) // MOE_RB * MOE_RB
    reg_end = jnp.cumsum(reg)
    reg_start = reg_end - reg
    dest_base = reg_start[None, :] + jnp.cumsum(pc, axis=0) - pc
    chunk_row = jnp.arange(MOE_NCHUNK, dtype=jnp.int32)[None, :] * MOE_CH
    in_group = (chunk_row[:, :, None] >= off[:, None, :]) & (chunk_row[:, :, None] < end[:, None, :])
    used = jnp.any(in_group, axis=2)
    cdest = chunk_row + jnp.sum(jnp.where(in_group, (dest_base - off)[:, None, :], 0), axis=2)
    worst_rows = TOP_K * MOE_TM * n_tiles + n_tiles * N_EXPERTS * (MOE_CH - 1) + N_EXPERTS * (MOE_RB - MOE_CH)
    r_max = -(-worst_rows // MOE_RB)
    parity = (jnp.arange(n_tiles, dtype=jnp.int32) % 2)[:, None]
    cdest_d = jnp.where(used, cdest, r_max * MOE_RB + parity * MOE_SLOTS + chunk_row).astype(jnp.int32)
    cdest_c = jnp.where(used, cdest, chunk_row).astype(jnp.int32)
    n_active = (reg_end[-1] // MOE_RB).astype(jnp.int32).reshape(1)
    tile_row = jnp.arange(r_max, dtype=jnp.int32) * MOE_RB
    tile_expert = jnp.minimum(jnp.sum((tile_row[:, None] >= reg_end[None, :]).astype(jnp.int32), axis=1),
                              N_EXPERTS - 1).astype(jnp.int32)
    in_region = (tile_row[:, None] >= reg_start[None, :]) & (tile_row[:, None] < reg_end[None, :])
    rows_end = jnp.sum(jnp.where(in_region, (reg_start + tot)[None, :], 0), axis=1)
    valid = jnp.clip(rows_end - tile_row, 0, MOE_RB).astype(jnp.int32)
    f = lambda a: a.astype(F32)
    zeros64 = jnp.zeros((n_tiles, N_EXPERTS), F32)
    row2 = lambda a: jnp.broadcast_to(jnp.concatenate([f(a), f(a)], axis=1)[:, None, :], (n_tiles, 8, LANES))
    col = lambda a: jnp.broadcast_to(f(a)[:, :, None], (n_tiles, N_EXPERTS, LANES))
    col128 = lambda a: jnp.broadcast_to(jnp.concatenate([f(a), zeros64], axis=1)[:, :, None], (n_tiles, LANES, LANES))
    row1 = lambda a: jnp.broadcast_to(jnp.concatenate([f(a), zeros64], axis=1)[:, None, :], (n_tiles, 8, LANES))
    return dict(n_used=n_used, cdest_d=cdest_d.reshape(n_tiles, 1, MOE_NCHUNK),
                cdest_c=cdest_c.reshape(n_tiles, 1, MOE_NCHUNK), r_max=r_max, n_active=n_active,
                tile_expert=tile_expert, valid=valid,
                off_row2=row2(off), end_row2=row2(end), off_col=col(off),
                off_row1=row1(off), off_col128=col128(off), end_col128=col128(end))


def _tau_pieces(sel, tau):
    tau = jnp.where(sel, tau, -1.0)
    hi = jnp.floor(tau * (1.0 / TAU_RADIX)) * TAU_RADIX
    return hi.astype(BF16), (tau - hi).astype(BF16)


def _dispatch_body(nused_ref, cdest_ref, h_ref, combt_ref, offcol_ref, offrow_ref, endrow_ref, sorted_ref,
                   buf_ref, sem):
    t = pl.program_id(0)
    n_used = nused_ref[t]
    tm = h_ref.shape[0]
    sel = combt_ref[...] > 0.0
    rr = lax.broadcasted_iota(jnp.int32, (tm, tm), 0)
    cc = lax.broadcasted_iota(jnp.int32, (tm, tm), 1)
    upper = jnp.where(rr < cc, 1.0, 0.0).astype(BF16)
    rank = _dot(jnp.where(sel, 1.0, 0.0).astype(BF16), upper)
    cols = jnp.concatenate([offcol_ref[...]] * (tm // LANES), axis=1)
    tau_hi, tau_lo = _tau_pieces(sel, cols + rank)
    taucat = jnp.concatenate([tau_hi, tau_lo], axis=0)
    off_row = offrow_ref[0:1, :]
    end_row = endrow_ref[0:1, :]
    hb = h_ref[...]

    slot = t % 2
    buf = buf_ref.at[slot]
    buf_prev = buf_ref.at[1 - slot]

    def start_piece(piece):
        chunks = range(piece * MOE_CPP, (piece + 1) * MOE_CPP)
        dests = [cdest_ref[0, c] for c in chunks]
        for c, row in zip(chunks, dests):
            dst = sorted_ref.at[pl.ds(pl.multiple_of(row, MOE_CH), MOE_CH), :]
            pltpu.make_async_copy(buf.at[pl.ds(c * MOE_CH, MOE_CH), :], dst, sem.at[slot]).start()

    def wait_piece(piece, which_buf, which_sem):
        rows = pl.ds(piece * MOE_PIECE, MOE_PIECE)
        pltpu.make_async_copy(which_buf.at[rows, :], sorted_ref.at[rows, :], which_sem).wait()

    for piece in range(MOE_NPIECE):
        @pl.when(piece * MOE_CPP < n_used)
        def _():
            if piece > 0:
                start_piece(piece - 1)
            base = piece * MOE_PIECE
            s_col = (base + lax.broadcasted_iota(jnp.int32, (MOE_PIECE, LANES), 0)).astype(F32)
            onehot = jnp.where((s_col >= off_row) & (s_col < end_row), 1.0, 0.0).astype(BF16)
            q = _dot(onehot, taucat)
            s_mat = (base + lax.broadcasted_iota(jnp.int32, (MOE_PIECE, tm), 0)).astype(F32)
            g = jnp.where(q == s_mat, 1.0, 0.0).astype(BF16)
            buf[pl.ds(base, MOE_PIECE), :] = _dot(g, hb).astype(BF16)

    last = (n_used - 1) // MOE_CPP
    for piece in range(MOE_NPIECE):
        @pl.when(piece == last)
        def _():
            start_piece(piece)

    n_prev = nused_ref[jnp.maximum(t - 1, 0)]
    for piece in range(MOE_NPIECE):
        @pl.when((t > 0) & (piece * MOE_CPP < n_prev))
        def _():
            wait_piece(piece, buf_prev, sem.at[1 - slot])

    for piece in range(MOE_NPIECE):
        @pl.when((t == pl.num_programs(0) - 1) & (piece * MOE_CPP < n_used))
        def _():
            wait_piece(piece, buf, sem.at[slot])


def _dispatch(h2, combt, plan, n_tiles):
    n, d = h2.shape
    r_total = plan["r_max"] * MOE_RB + 2 * MOE_SLOTS
    grid_spec = pltpu.PrefetchScalarGridSpec(
        num_scalar_prefetch=1,
        grid=(n_tiles,),
        in_specs=[pl.BlockSpec((None, 1, MOE_NCHUNK), lambda t, nu: (t, 0, 0), memory_space=pltpu.SMEM),
                  pl.BlockSpec((MOE_TM, d), lambda t, nu: (t, 0)),
                  pl.BlockSpec((N_EXPERTS, MOE_TM), lambda t, nu: (0, t)),
                  pl.BlockSpec((None, N_EXPERTS, LANES), lambda t, nu: (t, 0, 0)),
                  pl.BlockSpec((None, 8, LANES), lambda t, nu: (t, 0, 0)),
                  pl.BlockSpec((None, 8, LANES), lambda t, nu: (t, 0, 0))],
        out_specs=pl.BlockSpec(memory_space=pl.ANY),
        scratch_shapes=[pltpu.VMEM((2, MOE_SLOTS, d), BF16), pltpu.SemaphoreType.DMA((2,))],
    )
    return pl.pallas_call(
        _dispatch_body,
        grid_spec=grid_spec,
        out_shape=jax.ShapeDtypeStruct((r_total, d), BF16),
        compiler_params=_cparams(1),
        name="moe_dispatch",
    )(plan["n_used"], plan["cdest_d"], h2, combt, plan["off_col"], plan["off_row2"], plan["end_row2"])


def _ffn_body(texp_ref, nact_ref, valid_ref, x_ref, wg_ref, wu_ref, wd_ref, o_ref):
    r = pl.program_id(0)

    @pl.when(r < nact_ref[0])
    def _():
        x = x_ref[...]
        rows = lax.broadcasted_iota(jnp.int32, x.shape, 0)
        x = jnp.where(rows < valid_ref[r], x, jnp.zeros_like(x))
        a = _silu(_dot(x, wg_ref[...])) * _dot(x, wu_ref[...])
        o_ref[...] = _dot(a.astype(BF16), wd_ref[...]).astype(o_ref.dtype)


def _ffn(xs, wg, wu, wd, plan):
    r_total, d = xs.shape
    ff = wg.shape[2]
    last = lambda r, te, na, va: (jnp.minimum(r, na[0] - 1), 0)
    wmap = lambda r, te, na, va: (te[r], 0, 0)
    grid_spec = pltpu.PrefetchScalarGridSpec(
        num_scalar_prefetch=3,
        grid=(plan["r_max"],),
        in_specs=[pl.BlockSpec((MOE_RB, d), last),
                  pl.BlockSpec((None, d, ff), wmap), pl.BlockSpec((None, d, ff), wmap),
                  pl.BlockSpec((None, ff, d), wmap)],
        out_specs=pl.BlockSpec((MOE_RB, d), last),
    )
    return pl.pallas_call(
        _ffn_body,
        grid_spec=grid_spec,
        out_shape=jax.ShapeDtypeStruct((r_total, d), BF16),
        compiler_params=_cparams(1),
        name="moe_ffn",
    )(plan["tile_expert"], plan["n_active"], plan["valid"], xs, wg, wu, wd)


def _combine_body(nused_ref, cdest_ref, cnext_ref, h_ref, comb_ref, x1_ref, gate_ref, offrow_ref, offcol_ref,
                  endcol_ref, sg_ref, su_ref, sd_ref, ys_ref, y_ref, buf_ref, sem):
    t = pl.program_id(0)
    n_tiles = pl.num_programs(0)
    n_used = nused_ref[t]
    tm = h_ref.shape[0]
    slot = t % 2
    buf = buf_ref.at[slot]

    def fetch_tile(table_ref, count, which):
        for piece in range(MOE_NPIECE):
            @pl.when(piece * MOE_CPP < count)
            def _():
                chunks = range(piece * MOE_CPP, (piece + 1) * MOE_CPP)
                srcs = [table_ref[0, c] for c in chunks]
                for c, row in zip(chunks, srcs):
                    src = ys_ref.at[pl.ds(pl.multiple_of(row, MOE_CH), MOE_CH), :]
                    dst = buf_ref.at[which, pl.ds(c * MOE_CH, MOE_CH), :]
                    pltpu.make_async_copy(src, dst, sem.at[which, piece]).start()

    def wait_piece(piece):
        rows = pl.ds(piece * MOE_PIECE, MOE_PIECE)
        pltpu.make_async_copy(ys_ref.at[rows, :], buf.at[rows, :], sem.at[slot, piece]).wait()

    @pl.when(t == 0)
    def _():
        buf_ref[...] = jnp.zeros_like(buf_ref)
        fetch_tile(cdest_ref, n_used, 0)

    @pl.when(t + 1 < n_tiles)
    def _():
        fetch_tile(cnext_ref, nused_ref[jnp.minimum(t + 1, n_tiles - 1)], 1 - slot)

    hb = h_ref[...]
    acc = _dot((_silu(_dot(hb, sg_ref[...])) * _dot(hb, su_ref[...])).astype(BF16), sd_ref[...])
    comb = comb_ref[...]
    sel = comb > 0.0
    rr = lax.broadcasted_iota(jnp.int32, (tm, tm), 0)
    cc = lax.broadcasted_iota(jnp.int32, (tm, tm), 1)
    lower = jnp.where(cc < rr, 1.0, 0.0).astype(BF16)
    rank = _dot(lower, jnp.where(sel, 1.0, 0.0).astype(BF16))
    tau_hi, tau_lo = _tau_pieces(sel, offrow_ref[0:1, :] + rank)
    taucat = jnp.concatenate([tau_hi, tau_lo], axis=1)
    lhs = jnp.concatenate([taucat, jnp.concatenate([comb.astype(BF16), jnp.zeros((tm, LANES), BF16)], axis=1)],
                          axis=0)
    off_col = offcol_ref[...]
    end_col = endcol_ref[...]
    s_row = lax.broadcasted_iota(jnp.int32, (LANES, MOE_SLOTS), 1).astype(F32)
    off_b = jnp.concatenate([off_col] * (MOE_SLOTS // LANES), axis=1)
    end_b = jnp.concatenate([end_col] * (MOE_SLOTS // LANES), axis=1)
    onehot = jnp.where((s_row >= off_b) & (s_row < end_b), 1.0, 0.0).astype(BF16)
    qw = _dot(lhs, jnp.concatenate([onehot, onehot], axis=0))
    s_mat = lax.broadcasted_iota(jnp.int32, (tm, MOE_SLOTS), 1).astype(F32)
    gw = jnp.where(qw[0:tm] == s_mat, qw[tm:2 * tm], 0.0).astype(BF16)

    for piece in range(MOE_NPIECE):
        @pl.when(piece * MOE_CPP < n_used)
        def _():
            wait_piece(piece)

    y_ref[...] = x1_ref[...] + gate_ref[...] * (acc + _dot(gw, buf[...]))


def _combine(h2, comb, x1, gate, ys, sg, su, sd, plan, n_tiles):
    nb, s, d = x1.shape
    n = nb * s
    per_b = s // MOE_TM
    x1f = x1.reshape(n, d)
    const2 = lambda t, nu: (0, 0)
    grid_spec = pltpu.PrefetchScalarGridSpec(
        num_scalar_prefetch=1,
        grid=(n_tiles,),
        in_specs=[pl.BlockSpec((None, 1, MOE_NCHUNK), lambda t, nu: (t, 0, 0), memory_space=pltpu.SMEM),
                  pl.BlockSpec((None, 1, MOE_NCHUNK), lambda t, nu: (jnp.minimum(t + 1, n_tiles - 1), 0, 0),
                               memory_space=pltpu.SMEM),
                  pl.BlockSpec((MOE_TM, d), lambda t, nu: (t, 0)),
                  pl.BlockSpec((MOE_TM, LANES), lambda t, nu: (t, 0)),
                  pl.BlockSpec((MOE_TM, d), lambda t, nu: (t, 0)),
                  pl.BlockSpec((None, 1, d), lambda t, nu: (t // per_b, 0, 0)),
                  pl.BlockSpec((None, 8, LANES), lambda t, nu: (t, 0, 0)),
                  pl.BlockSpec((None, LANES, LANES), lambda t, nu: (t, 0, 0)),
                  pl.BlockSpec((None, LANES, LANES), lambda t, nu: (t, 0, 0)),
                  pl.BlockSpec(sg.shape, const2), pl.BlockSpec(su.shape, const2), pl.BlockSpec(sd.shape, const2),
                  pl.BlockSpec(memory_space=pl.ANY)],
        out_specs=pl.BlockSpec((MOE_TM, d), lambda t, nu: (t, 0)),
        scratch_shapes=[pltpu.VMEM((2, MOE_SLOTS, d), BF16),
                        pltpu.SemaphoreType.DMA((2, MOE_NPIECE))],
    )
    y = pl.pallas_call(
        _combine_body,
        grid_spec=grid_spec,
        out_shape=jax.ShapeDtypeStruct((n, d), F32),
        compiler_params=_cparams(1),
        name="moe_combine",
    )(plan["n_used"], plan["cdest_c"], plan["cdest_c"], h2, comb, x1f, gate, plan["off_row1"], plan["off_col128"],
      plan["end_col128"],
      sg, su, sd, ys)
    return y.reshape(nb, s, d)


def _moe_sparse(h2, comb, combt, cnt, x1, gate, wg, wu, wd, sg, su, sd):
    n = h2.shape[0]
    n_tiles = n // MOE_TM
    plan = _moe_plan(cnt[:, :, 0].astype(jnp.int32), n_tiles)
    xs = _dispatch(h2, combt, plan, n_tiles)
    ys = _ffn(xs, wg, wu, wd, plan)
    return _combine(h2, comb, x1, gate, ys, sg, su, sd, plan, n_tiles)


def _tile_heads(g, mult=1.0):
    return (jnp.tile(g.astype(F32), N_HEADS) * mult).reshape(1, WIDTH)


def kernel(x_prompt, x_sample, cache_fox_k, cache_fox_v, cache_fox_logf, cache_band_k, cache_band_v, c_prompt, c_sample, w_ada, b_ada, norm1_g, norm2_g, w_in, b_forget, g_q_fox, g_k_fox, g_q_band, g_k_band, rel_bias, out_g_fox, out_g_band, w_out, w_router, b_router, w_gate, w_up, w_down, ws_gate, ws_up, ws_down):
    depth = w_ada.shape[0]
    assert depth == 1
    bsz, seq, d = x_prompt.shape
    dbs, dseq, _ = x_sample.shape
    past = cache_fox_k.shape[2]
    n_cache = cache_band_k.shape[2]
    assert n_cache == BAND_REACH and dseq == CHUNK and seq % BAND_REACH == 0

    wi = w_in[0]
    cols = [wi[:, 0:512], wi[:, 512:1024], wi[:, 1024:1536], wi[:, 1544:2056], wi[:, 2056:2568], wi[:, 2568:3080],
            wi[:, 1536:1544], jnp.zeros((d, LANES - N_HEADS), F32)]
    w_all = jnp.concatenate(cols, axis=1).astype(BF16)
    hd = jnp.arange(WIDTH) // HEAD_DIM
    bd = jnp.where(hd[:, None] == hd[None, :], 1.0 / HEAD_DIM, 0.0).astype(BF16)
    qscale = ATTN_SCALE * LOG2E
    gqf, gkf = _tile_heads(g_q_fox[0], qscale), _tile_heads(g_k_fox[0])
    gqb, gkb = _tile_heads(g_q_band[0], qscale), _tile_heads(g_k_band[0])
    bf_row = jnp.concatenate([b_forget[0], jnp.zeros((LANES - N_HEADS,), F32)]).reshape(1, LANES)
    g1 = norm1_g[0].reshape(1, d)
    g2 = norm2_g[0].reshape(1, d)
    ogf = out_g_fox[0].reshape(1, WIDTH)
    ogb = out_g_band[0].reshape(1, WIDTH)
    wo = w_out[0].astype(BF16)
    wr_t = w_router[0].T
    wr_hi = wr_t.astype(BF16)
    wr_lo = (wr_t - wr_hi.astype(F32)).astype(BF16)
    br = jnp.broadcast_to(b_router[0].reshape(N_EXPERTS, 1), (N_EXPERTS, LANES)).astype(F32)
    wg, wu, wd = w_gate[0].astype(BF16), w_up[0].astype(BF16), w_down[0].astype(BF16)
    sg, su, sd = ws_gate[0].astype(BF16), ws_up[0].astype(BF16), ws_down[0].astype(BF16)

    n_c = bsz + dbs
    rows = -(-n_c // 8) * 8
    c_all = jnp.concatenate([c_prompt, c_sample, jnp.zeros((rows - n_c, d), F32)], axis=0)
    mod = _ada(c_all, w_ada[0], b_ada[0].reshape(1, -1))

    def mods(lo, hi):
        return [mod[lo:hi, j * d:(j + 1) * d].reshape(hi - lo, 1, d) for j in range(6)]

    shift1_p, scale1_p, gate1_p, shift2_p, scale2_p, gate2_p = mods(0, bsz)
    shift1_s, scale1_s, gate1_s, shift2_s, scale2_s, gate2_s = mods(bsz, n_c)

    TM = BAND_REACH
    (qf, kf, vft, kf32, vf32, lf, qb, kb, vbt, kb32, vb32) = _proj(
        x_prompt, shift1_p, scale1_p, g1, w_all, bd, gqf, gkf, gqb, gkb, bf_row, G=1, R=TM, band_last_only=True)
    r3 = lambda a: a.reshape(bsz, seq, a.shape[-1])
    ct, ka = _scan_t(r3(lf), T=TM)
    of = _foxt(r3(qf), r3(kf), ka, vft, ct, T=TM)
    assert TM == BAND_KBLK == BAND_REACH and seq % BAND_STEP == 0
    ob = _bandt(r3(qb), r3(kb), vbt, _band_bias_tile_t(rel_bias[0]))
    assert TM % MOE_TM == 0
    x1_p, h2_p, comb_p, combt_p, cnt_p = _merge(of.reshape(-1, WIDTH), ob.reshape(-1, WIDTH), x_prompt, gate1_p,
                                                shift2_p, scale2_p, ogf, ogb, wo, g2, wr_hi, wr_lo, br, G=1, R=TM)
    y_p = _moe_sparse(h2_p, comb_p, combt_p, cnt_p, x1_p, gate2_p, wg, wu, wd, sg, su, sd)

    GS = 8
    (qf_s, kf_s, vf_s, kf32_s, vf32_s, lf_s, qb_s, kb_s, vb_s, kb32_s, vb32_s) = _proj(
        x_sample, shift1_s, scale1_s, g1, w_all, bd, gqf, gkf, gqb, gkb, bf_row, G=GS, R=dseq,
        band_last_only=False)
    s3 = lambda a: a.reshape(dbs, dseq, a.shape[-1])
    sk = past + dseq
    skp = -(-sk // LANES) * LANES
    pad_k = skp - sk
    n_seq = dbs * N_HEADS
    assert n_seq <= LANES
    lf_seq = jnp.concatenate([cache_fox_logf[0], s3(lf_s)[:, :, :N_HEADS]], axis=1)
    lf_seq = jnp.swapaxes(lf_seq, 0, 1).reshape(sk, n_seq)
    lf_seq = jnp.pad(lf_seq, ((0, pad_k), (0, LANES - n_seq)))
    cum_col, cum_row = _scan(lf_seq)
    cq_s = jnp.swapaxes(cum_col[past:past + dseq, :n_seq].reshape(dseq, dbs, N_HEADS), 0, 1)
    cum_s = jnp.pad(cq_s, ((0, 0), (0, 0), (0, LANES - N_HEADS)))
    cumt_s = cum_row[:n_seq].reshape(dbs, 1, N_HEADS, skp)

    def cache_t(c):
        return jnp.transpose(c, (0, 2, 3, 1)).reshape(dbs, N_PAIRS, PAIR, c.shape[1])

    of_s = _sample_attn(s3(qf_s), cache_t(cache_fox_k[0]), cache_t(cache_fox_v[0]), s3(kf_s), s3(vf_s),
                        cum_s, cumt_s, fox=True)
    bias_s = _band_bias_tile(rel_bias[0], dseq, BAND_REACH + LANES)
    ob_s = _sample_attn(s3(qb_s), cache_t(cache_band_k[0]), cache_t(cache_band_v[0]), s3(kb_s), s3(vb_s),
                        bias_s, fox=False)
    x1_s, h2_s, comb_s, _, _ = _merge(of_s.reshape(-1, WIDTH), ob_s.reshape(-1, WIDTH), x_sample, gate1_s, shift2_s,
                                      scale2_s, ogf, ogb, wo, g2, wr_hi, wr_lo, br, G=GS, R=dseq)
    y_s = _moe(h2_s, comb_s, x1_s, gate2_s, wg, wu, wd, sg, su, sd, G=dbs, R=dseq)

    hshape = (N_HEADS, HEAD_DIM)
    new_bk_s = jnp.concatenate([cache_band_k[0], s3(kb32_s).reshape(dbs, dseq, *hshape)], axis=1)[:, -n_cache:]
    new_bv_s = jnp.concatenate([cache_band_v[0], s3(vb32_s).reshape(dbs, dseq, *hshape)], axis=1)[:, -n_cache:]
    return (y_p, y_s,
            kf32.reshape(1, bsz, seq, *hshape), vf32.reshape(1, bsz, seq, *hshape),
            lf[:, :N_HEADS].reshape(1, bsz, seq, N_HEADS),
            kb32.reshape(1, bsz, BAND_REACH, *hshape), vb32.reshape(1, bsz, BAND_REACH, *hshape),
            kf32_s.reshape(1, dbs, dseq, *hshape), vf32_s.reshape(1, dbs, dseq, *hshape),
            lf_s[:, :N_HEADS].reshape(1, dbs, dseq, N_HEADS),
            new_bk_s[None], new_bv_s[None])
```

```python
import functools

import jax
import jax.numpy as jnp
import numpy as np
from jax import lax
from jax.experimental import pallas as pl
from jax.experimental.pallas import tpu as pltpu

F32 = jnp.float32
BF16 = jnp.bfloat16

HEAD_DIM = 64
N_HEADS = 8
WIDTH = N_HEADS * HEAD_DIM
PAIR = 2 * HEAD_DIM
N_PAIRS = N_HEADS // 2
LANES = 128
CHUNK = 64
BAND_REACH = 512
REL_CLIP = 256
N_EXPERTS = 64
N_GROUPS = 8
GROUP_SIZE = N_EXPERTS // N_GROUPS
TOPK_GROUPS = 4
TOP_K = 8
ROUTED_SCALE = 2.5
EPS = 1e-6
NEG_INF = -1e30
ATTN_SCALE = HEAD_DIM ** -0.5
LOG2E = 1.4426950408889634
VMEM_LIMIT = 56 * 1024 * 1024
MERGE_ROWS = 1024


def _cparams(n_axes):
    return pltpu.CompilerParams(dimension_semantics=("arbitrary",) * n_axes,
                                vmem_limit_bytes=VMEM_LIMIT)


def _dot(a, b):
    return jnp.dot(a, b, preferred_element_type=F32)


def _dot_nt(a, b):
    return lax.dot_general(a, b, (((1,), (1,)), ((), ())), preferred_element_type=F32)


def _split2(a):
    hi = a.astype(BF16)
    lo = (a - hi.astype(F32)).astype(BF16)
    return hi, lo


def _split3(a):
    hi = a.astype(BF16)
    r = a - hi.astype(F32)
    mid = r.astype(BF16)
    lo = (r - mid.astype(F32)).astype(BF16)
    return hi, mid, lo


def _ada_body(c_ref, w_ref, b_ref, o_ref):
    c = c_ref[...]
    a = c * jax.nn.sigmoid(c)
    a_hi, a_lo = _split2(a)
    w_hi, w_lo = _split2(w_ref[...])
    o_ref[...] = _dot(a_hi, w_hi) + _dot(a_hi, w_lo) + _dot(a_lo, w_hi) + b_ref[...]


def _ada(c_all, w_ada, b_ada):
    rows, d = c_all.shape
    n = w_ada.shape[1]
    tn = 1024
    return pl.pallas_call(
        _ada_body,
        grid=(n // tn,),
        in_specs=[pl.BlockSpec((rows, d), lambda j: (0, 0)),
                  pl.BlockSpec((d, tn), lambda j: (0, j)),
                  pl.BlockSpec((1, tn), lambda j: (0, j))],
        out_specs=pl.BlockSpec((rows, tn), lambda j: (0, j)),
        out_shape=jax.ShapeDtypeStruct((rows, n), F32),
        compiler_params=_cparams(1),
        name="ada",
    )(c_all, w_ada, b_ada)


def _log_sigmoid(z):
    return jnp.minimum(z, 0.0) - jnp.log(1.0 + jnp.exp(-jnp.abs(z)))


def _proj_body(x_ref, sh_ref, sc_ref, g1_ref, w_ref, bd_ref, gqf_ref, gkf_ref, gqb_ref, gkb_ref, bf_ref,
               qf_ref, kf_ref, vf_ref, kf32_ref, vf32_ref, lf_ref, qb_ref, kb_ref, vb_ref, kb32_ref, vb32_ref,
               *, band_last_only):
    x = x_ref[...]
    g, r, d = x.shape
    ms = jnp.mean(x * x, axis=-1, keepdims=True)
    h = x * lax.rsqrt(ms + EPS) * g1_ref[...] * (1.0 + sc_ref[...]) + sh_ref[...]
    hb = h.reshape(g * r, d).astype(BF16)

    def seg(i):
        return _dot(hb, w_ref[:, i * WIDTH:(i + 1) * WIDTH])

    def head_norm(t, gain_ref):
        ssq = _dot((t * t).astype(BF16), bd_ref[...])
        return t * lax.rsqrt(ssq + EPS) * gain_ref[...]

    qf_ref[...] = head_norm(seg(0), gqf_ref).astype(BF16)
    kf = head_norm(seg(1), gkf_ref)
    kf_ref[...] = kf.astype(BF16)
    vf = seg(2)
    for hh in range(N_HEADS):
        kf32_ref[:, hh, :] = kf[:, hh * HEAD_DIM:(hh + 1) * HEAD_DIM]
        vf32_ref[:, hh, :] = vf[:, hh * HEAD_DIM:(hh + 1) * HEAD_DIM]
    if band_last_only:
        vf_ref[...] = vf.T.astype(BF16)
    else:
        vf_ref[...] = vf.astype(BF16)
    z = _dot(hb, w_ref[:, 6 * WIDTH:6 * WIDTH + LANES]) + bf_ref[...]
    lf_ref[...] = _log_sigmoid(z)
    qb_ref[...] = head_norm(seg(3), gqb_ref).astype(BF16)
    kb = head_norm(seg(4), gkb_ref)
    kb_ref[...] = kb.astype(BF16)
    vb = seg(5)
    if band_last_only:
        vb_ref[...] = vb.T.astype(BF16)
    else:
        vb_ref[...] = vb.astype(BF16)

    if band_last_only:
        @pl.when(pl.program_id(1) == pl.num_programs(1) - 1)
        def _():
            kb32_ref[...] = kb
            vb32_ref[...] = vb
    else:
        kb32_ref[...] = kb
        vb32_ref[...] = vb


def _proj(x, shift, scale, g1, w_all, bd, gqf, gkf, gqb, gkb, bf_row, *, G, R, band_last_only):
    nb, s, d = x.shape
    n = nb * s
    tm = G * R
    nbi, nsi = nb // G, s // R
    grid = (nbi, nsi)
    row = lambda b, i: (b * nsi + i, 0)
    const = lambda b, i: (0, 0)
    mod_spec = pl.BlockSpec((G, 1, d), lambda b, i: (b, 0, 0))
    out_bf = jax.ShapeDtypeStruct((n, WIDTH), BF16)
    out_f32 = jax.ShapeDtypeStruct((n, WIDTH), F32)
    tile = pl.BlockSpec((tm, WIDTH), row)
    state_shape = jax.ShapeDtypeStruct((n, N_HEADS, HEAD_DIM), F32)
    state_spec = pl.BlockSpec((tm, N_HEADS, HEAD_DIM), lambda b, i: (b * nsi + i, 0, 0))
    if band_last_only:
        assert G == 1 and R == BAND_REACH
        band_shape = jax.ShapeDtypeStruct((nb, BAND_REACH, WIDTH), F32)
        band_spec = pl.BlockSpec((None, BAND_REACH, WIDTH), lambda b, i: (b, 0, 0))
        v_shape = jax.ShapeDtypeStruct((nb, nsi, WIDTH, tm), BF16)
        v_spec = pl.BlockSpec((None, None, WIDTH, tm), lambda b, i: (b, i, 0, 0))
    else:
        band_shape, band_spec = out_f32, tile
        v_shape, v_spec = out_bf, tile
    return pl.pallas_call(
        functools.partial(_proj_body, band_last_only=band_last_only),
        grid=grid,
        in_specs=[pl.BlockSpec((G, R, d), lambda b, i: (b, i, 0)), mod_spec, mod_spec,
                  pl.BlockSpec((1, d), const), pl.BlockSpec(w_all.shape, const), pl.BlockSpec(bd.shape, const),
                  pl.BlockSpec((1, WIDTH), const), pl.BlockSpec((1, WIDTH), const),
                  pl.BlockSpec((1, WIDTH), const), pl.BlockSpec((1, WIDTH), const),
                  pl.BlockSpec((1, LANES), const)],
        out_specs=[tile, tile, v_spec, state_spec, state_spec, pl.BlockSpec((tm, LANES), row), tile, tile, v_spec,
                   band_spec, band_spec],
        out_shape=[out_bf, out_bf, v_shape, state_shape, state_shape, jax.ShapeDtypeStruct((n, LANES), F32),
                   out_bf, out_bf, v_shape, band_shape, band_shape],
        compiler_params=_cparams(2),
        name="proj",
    )(x, shift, scale, g1, w_all, bd, gqf, gkf, gqb, gkb, bf_row)


def _scan_body(lf_ref, cum_ref, cumt_ref):
    lf = lf_ref[...]
    s = lf.shape[0]
    hi, mid, lo = _split3(lf)
    rr = lax.broadcasted_iota(jnp.int32, (s, s), 0)
    cc = lax.broadcasted_iota(jnp.int32, (s, s), 1)
    tri = jnp.where(cc <= rr, 1.0, 0.0).astype(BF16)
    cum2 = (_dot(tri, hi) + _dot(tri, mid) + _dot(tri, lo)) * LOG2E
    cum_ref[...] = cum2
    cumt_ref[...] = cum2.T


def _scan(lf):
    s, _ = lf.shape
    return pl.pallas_call(
        _scan_body,
        out_shape=[jax.ShapeDtypeStruct((s, LANES), F32), jax.ShapeDtypeStruct((LANES, s), F32)],
        compiler_params=pltpu.CompilerParams(vmem_limit_bytes=VMEM_LIMIT),
        name="scan",
    )(lf)


AUG_PIECES = 3
FOX_UNDERFLOW = 160.0
FOX_NORM_SLACK = 1.02
FOX_BOUND_SLACK = 2.0


def _scan_t_body(lf_ref, place_ref, ct_ref, ka_ref, carry_ref):
    @pl.when(pl.program_id(1) == 0)
    def _():
        carry_ref[...] = jnp.zeros_like(carry_ref)

    lf = lf_ref[...]
    ts = lf.shape[0]
    lane = lax.broadcasted_iota(jnp.int32, lf.shape, 1)
    lf = jnp.where(lane < N_HEADS, lf, 0.0)
    hi, mid, lo = _split3(lf)
    rr = lax.broadcasted_iota(jnp.int32, (ts, ts), 0)
    cc = lax.broadcasted_iota(jnp.int32, (ts, ts), 1)
    tri = jnp.where(cc <= rr, 1.0, 0.0).astype(BF16)
    cum = _dot(tri, hi) + _dot(tri, mid) + _dot(tri, lo) + carry_ref[0:1, :]
    carry_ref[...] = jnp.broadcast_to(cum[ts - 1:ts, :], carry_ref.shape)
    cum2 = cum * LOG2E
    ct_ref[...] = cum2.T[0:N_HEADS, :]
    pieces = _split3(cum2 - cum2[0:1, :])
    ka = _dot(pieces[0], place_ref[0]) + _dot(pieces[1], place_ref[1]) + _dot(pieces[2], place_ref[2])
    ka_ref[...] = ka.astype(BF16)


def _aug_placement():
    h = jnp.arange(LANES)[:, None]
    col = jnp.arange(WIDTH)[None, :]
    mats = []
    for x in range(AUG_PIECES):
        tgt = PAIR * (h // 2) + AUG_PIECES * (h % 2) + x
        mats.append(jnp.where((h < N_HEADS) & (col == tgt), 1.0, 0.0))
    return jnp.stack(mats).astype(BF16)


def _scan_t(lf, *, T):
    b, s, _ = lf.shape
    place = _aug_placement()
    return pl.pallas_call(
        _scan_t_body,
        grid=(b, s // T),
        in_specs=[pl.BlockSpec((None, T, LANES), lambda bi, i: (bi, i, 0)),
                  pl.BlockSpec(place.shape, lambda bi, i: (0, 0, 0))],
        out_specs=[pl.BlockSpec((None, None, N_HEADS, T), lambda bi, i: (bi, i, 0, 0)),
                   pl.BlockSpec((None, T, WIDTH), lambda bi, i: (bi, i, 0))],
        out_shape=[jax.ShapeDtypeStruct((b, s // T, N_HEADS, T), F32),
                   jax.ShapeDtypeStruct((b, s, WIDTH), BF16)],
        scratch_shapes=[pltpu.VMEM((8, LANES), F32)],
        compiler_params=_cparams(2),
        name="scan_t",
    )(lf, place)


def _foxt_body(q_ref, k_ref, ka_ref, vt_ref, ct_ref, o_ref, kn_ref, *, T):
    p = pl.program_id(1)
    i = pl.program_id(2)
    q = q_ref[...]
    lane = lax.broadcasted_iota(jnp.int32, (T, PAIR), 1)
    halves = []
    for par in range(2):
        qm = jnp.where((lane >= HEAD_DIM) == (par == 1), q, jnp.zeros_like(q))
        lo_lane = AUG_PIECES * par
        qa = jnp.where((lane >= lo_lane) & (lane < lo_lane + AUG_PIECES), -1.0, 0.0).astype(BF16)
        halves.append(jnp.concatenate([qm, qa], axis=1))
    qcat = jnp.concatenate(halves, axis=0)
    h_even = 2 * p
    cq = jnp.concatenate([ct_ref[i, pl.ds(h_even, 1), :], ct_ref[i, pl.ds(h_even + 1, 1), :]], axis=1)
    ones_rows = jnp.ones((16, T), BF16)
    krow = lax.broadcasted_iota(jnp.int32, (T, 2 * T), 0)
    qcol = lax.broadcasted_iota(jnp.int32, (T, 2 * T), 1) % T

    def step(j, carry, masked):
        m, acc_e, acc_o = carry
        k0 = pl.multiple_of(j * T, T)
        kcat = jnp.concatenate([k_ref[pl.ds(k0, T), :], ka_ref[pl.ds(k0, T), :]], axis=1)
        st = _dot_nt(kcat, qcat)
        c0 = jnp.concatenate([jnp.broadcast_to(ct_ref[j, pl.ds(h_even, 1), :][:, 0:1], (1, T)),
                              jnp.broadcast_to(ct_ref[j, pl.ds(h_even + 1, 1), :][:, 0:1], (1, T))], axis=1)
        rb = cq - c0
        if masked:
            st = jnp.where(krow <= qcol, st, NEG_INF)
        m_new = jnp.maximum(m, jnp.max(st, axis=0, keepdims=True) + rb)
        alpha = jnp.exp2(m - m_new)
        pt = jnp.exp2(st + (rb - m_new)).astype(BF16)
        vt = vt_ref[j]
        pv_e = _dot(jnp.concatenate([vt[0:HEAD_DIM], ones_rows], axis=0), pt[:, 0:T])
        pv_o = _dot(jnp.concatenate([vt[HEAD_DIM:PAIR], ones_rows], axis=0), pt[:, T:2 * T])
        return m_new, alpha[:, 0:T] * acc_e + pv_e, alpha[:, T:2 * T] * acc_o + pv_o

    @pl.when(i == 0)
    def _():
        ones = jnp.ones((PAIR, LANES), BF16)
        kmax = jnp.zeros((1, LANES), F32)
        for c in range(k_ref.shape[0] // T):
            kc = k_ref[c * T:(c + 1) * T, :].astype(F32)
            kmax = jnp.maximum(kmax, jnp.max(_dot((kc * kc).astype(BF16), ones), axis=0, keepdims=True))
        kn_ref[...] = jnp.broadcast_to(kmax, kn_ref.shape)

    rows = HEAD_DIM + 16
    init = (jnp.full((1, 2 * T), NEG_INF, F32), jnp.zeros((rows, T), F32), jnp.zeros((rows, T), F32))
    carry = step(i, init, True)

    qf = q.astype(F32)
    qsq = qf * qf
    kn2 = kn_ref[0:1, 0:1] * FOX_NORM_SLACK
    need = jnp.zeros((1, 1), jnp.int32)
    blk = lax.broadcasted_iota(jnp.int32, (ct_ref.shape[0], 1, 1), 0)
    for par in range(2):
        head_lanes = (lane >= HEAD_DIM) == (par == 1)
        qn2 = jnp.max(jnp.sum(jnp.where(head_lanes, qsq, 0.0), axis=1, keepdims=True), axis=0, keepdims=True)
        reach = jnp.sqrt(qn2 * kn2) + FOX_BOUND_SLACK
        m_min = jnp.min(carry[0][:, par * T:(par + 1) * T], axis=1, keepdims=True)
        cq_first = cq[:, par * T:par * T + 1]
        ck_end = ct_ref[:, pl.ds(h_even + par, 1), :][:, :, T - 1:T]
        live = (reach + cq_first - m_min)[None, :, :] - ck_end > -FOX_UNDERFLOW
        count = jnp.sum(jnp.where(live & (blk < i), 1, 0), axis=0)
        need = jnp.maximum(need, count)
    n_keep = need[0, 0]

    n_pairs = n_keep // 2
    carry = lax.fori_loop(0, n_pairs, lambda u, c: step(i - 2 - 2 * u, step(i - 1 - 2 * u, c, False), False), carry)
    carry = lax.fori_loop(2 * n_pairs, n_keep, lambda u, c: step(i - 1 - u, c, False), carry)
    _, acc_e, acc_o = carry
    o_t = jnp.concatenate([acc_e[0:HEAD_DIM] / acc_e[HEAD_DIM:HEAD_DIM + 1],
                           acc_o[0:HEAD_DIM] / acc_o[HEAD_DIM:HEAD_DIM + 1]], axis=0)
    o_ref[...] = o_t.T.astype(o_ref.dtype)


def _foxt(q, k, ka, vt, ct, *, T):
    b, s, _ = q.shape
    nt = s // T
    return pl.pallas_call(
        functools.partial(_foxt_body, T=T),
        grid=(b, N_PAIRS, nt),
        in_specs=[pl.BlockSpec((None, T, PAIR), lambda bi, p, i: (bi, i, p)),
                  pl.BlockSpec((None, s, PAIR), lambda bi, p, i: (bi, 0, p)),
                  pl.BlockSpec((None, s, PAIR), lambda bi, p, i: (bi, 0, p)),
                  pl.BlockSpec((None, nt, PAIR, T), lambda bi, p, i: (bi, 0, p, 0)),
                  pl.BlockSpec((None, nt, N_HEADS, T), lambda bi, p, i: (bi, 0, 0, 0))],
        out_specs=pl.BlockSpec((None, T, PAIR), lambda bi, p, i: (bi, i, p)),
        out_shape=jax.ShapeDtypeStruct((b, s, WIDTH), BF16),
        scratch_shapes=[pltpu.VMEM((8, LANES), F32)],
        compiler_params=_cparams(3),
        name="foxt",
    )(q, k, ka, vt, ct)


BAND_TQ = 256
BAND_KBLK = 512
BAND_STEP = 1024


def _bandt_body(q_ref, *refs):
    n_blk = 1 + BAND_STEP // BAND_KBLK
    k_refs, v_refs, (bias_ref, o_ref) = refs[:n_blk], refs[n_blk:2 * n_blk], refs[2 * n_blk:]
    i = pl.program_id(2)
    w = BAND_REACH + BAND_TQ
    k2 = jnp.concatenate([r[...] for r in k_refs], axis=0)
    vt2 = jnp.concatenate([r[...] for r in v_refs], axis=1)
    lane = lax.broadcasted_iota(jnp.int32, (BAND_TQ, PAIR), 1)
    ones_rows = jnp.ones((16, w), BF16)
    krow = lax.broadcasted_iota(jnp.int32, (w, 2 * BAND_TQ), 0)
    bias = bias_ref[...]
    for sub in range(BAND_STEP // BAND_TQ):
        r0 = sub * BAND_TQ
        q = q_ref[r0:r0 + BAND_TQ, :]
        qcat = jnp.concatenate([jnp.where(lane < HEAD_DIM, q, jnp.zeros_like(q)),
                                jnp.where(lane >= HEAD_DIM, q, jnp.zeros_like(q))], axis=0)
        st = _dot_nt(k2[r0:r0 + w], qcat) + bias
        st = jnp.where(krow >= BAND_KBLK - i * BAND_STEP - r0, st, NEG_INF)
        m = jnp.max(st, axis=0, keepdims=True)
        pt = jnp.exp2(st - m).astype(BF16)
        vwin = vt2[:, r0:r0 + w]
        outs = []
        for par in range(2):
            vcat = jnp.concatenate([vwin[par * HEAD_DIM:(par + 1) * HEAD_DIM], ones_rows], axis=0)
            pv = _dot(vcat, pt[:, par * BAND_TQ:(par + 1) * BAND_TQ])
            outs.append(pv[0:HEAD_DIM] / pv[HEAD_DIM:HEAD_DIM + 1])
        o_ref[r0:r0 + BAND_TQ, :] = jnp.concatenate(outs, axis=0).T.astype(o_ref.dtype)


def _bandt(q, k, vt, bias_t):
    b, s, _ = q.shape
    per = BAND_STEP // BAND_KBLK
    n_blk = 1 + per
    blk = lambda d: (lambda i: jnp.maximum(per * i + d, 0))
    offs = [blk(d) for d in range(-1, per)]
    k_specs = [pl.BlockSpec((None, BAND_KBLK, PAIR), lambda bi, p, i, f=f: (bi, f(i), p)) for f in offs]
    v_specs = [pl.BlockSpec((None, None, PAIR, BAND_KBLK), lambda bi, p, i, f=f: (bi, f(i), p, 0)) for f in offs]
    return pl.pallas_call(
        _bandt_body,
        grid=(b, N_PAIRS, s // BAND_STEP),
        in_specs=[pl.BlockSpec((None, BAND_STEP, PAIR), lambda bi, p, i: (bi, i, p))] + k_specs + v_specs
                 + [pl.BlockSpec((None,) + bias_t.shape[1:], lambda bi, p, i: (p, 0, 0))],
        out_specs=pl.BlockSpec((None, BAND_STEP, PAIR), lambda bi, p, i: (bi, i, p)),
        out_shape=jax.ShapeDtypeStruct((b, s, WIDTH), BF16),
        compiler_params=_cparams(3),
        name="bandt",
    )(q, *([k] * n_blk), *([vt] * n_blk), bias_t)


def _band_bias_tile_t(rel_bias):
    tile = _band_bias_tile(rel_bias, BAND_TQ, BAND_REACH + BAND_TQ)
    t = jnp.swapaxes(tile, 1, 2)
    return jnp.concatenate([t[0::2], t[1::2]], axis=2)


def _band_bias_tile(rel_bias, tq, w):
    span = w + tq - 1
    period = span + 1
    v = np.arange(period)
    d = np.where(v < w, v, v - period)
    table_idx = np.clip(BAND_REACH - d, -REL_CLIP, REL_CLIP) + REL_CLIP
    table = rel_bias[:, table_idx] * LOG2E
    n_h = rel_bias.shape[0]
    vals = jnp.tile(table, (1, tq))[:, :tq * span].reshape(n_h, tq, span)[:, :, :w]
    r = np.arange(tq)[:, None]
    c = np.arange(w)[None, :]
    in_band = (c // CHUNK >= r // CHUNK) & (c // CHUNK <= r // CHUNK + BAND_REACH // CHUNK)
    return jnp.where(jnp.asarray(in_band)[None], vals, NEG_INF).astype(F32)


def _sample_attn_body(q_ref, kc_ref, vc_ref, kn_ref, vn_ref, *rest, fox):
    p = pl.program_id(1)
    t = q_ref.shape[0]
    past = kc_ref.shape[1]
    lane = lax.broadcasted_iota(jnp.int32, (t, PAIR), 1)
    q = q_ref[...]
    qs = jnp.concatenate([jnp.where(lane < HEAD_DIM, q, jnp.zeros_like(q)),
                          jnp.where(lane >= HEAD_DIM, q, jnp.zeros_like(q))], axis=0)
    kc = kc_ref[...].astype(BF16)
    vc = vc_ref[...].astype(BF16)
    s_c = _dot(qs, kc)
    s_n = _dot_nt(qs, kn_ref[...])
    row = lax.broadcasted_iota(jnp.int32, (t, t), 0)
    col = lax.broadcasted_iota(jnp.int32, (t, t), 1)
    if fox:
        cq_ref, ck_ref, o_ref = rest
    else:
        bias_ref, o_ref = rest
    pcs, pns = [], []
    for par in range(2):
        h = 2 * p + par
        sc = s_c[par * t:(par + 1) * t]
        sn = s_n[par * t:(par + 1) * t]
        if fox:
            cq_col = jnp.sum(jnp.where(lane == h, cq_ref[...], 0.0), axis=1, keepdims=True)
            ck = ck_ref[pl.ds(h, 1), :]
            uc = sc + (cq_col - ck[:, 0:past])
            un = jnp.where(col <= row, sn + (cq_col - ck[:, past:past + t]), NEG_INF)
        else:
            bias = bias_ref[h]
            uc = sc + bias[:, 0:past]
            un = sn + bias[:, past:past + t]
        m = jnp.maximum(jnp.max(uc, axis=1, keepdims=True), jnp.max(un, axis=1, keepdims=True))
        pc = jnp.exp2(uc - m)
        pn = jnp.exp2(un - m)
        inv = 1.0 / (jnp.sum(pc, axis=1, keepdims=True) + jnp.sum(pn, axis=1, keepdims=True))
        pcs.append((pc * inv).astype(BF16))
        pns.append((pn * inv).astype(BF16))
    o2 = _dot_nt(jnp.concatenate(pcs, axis=0), vc) + _dot(jnp.concatenate(pns, axis=0), vn_ref[...])
    o_ref[...] = jnp.where(lane < HEAD_DIM, o2[0:t], o2[t:2 * t]).astype(o_ref.dtype)


def _sample_attn(q, kc_t, vc_t, k_new, v_new, *extra, fox):
    b, t, _ = q.shape
    past = kc_t.shape[3]
    pair_rows = pl.BlockSpec((None, t, PAIR), lambda bi, p: (bi, 0, p))
    cache = pl.BlockSpec((None, None, PAIR, past), lambda bi, p: (bi, p, 0, 0))
    if fox:
        cq, ck = extra
        extra_specs = [pl.BlockSpec((None, t, LANES), lambda bi, p: (bi, 0, 0)),
                       pl.BlockSpec((None, None, N_HEADS, ck.shape[3]), lambda bi, p: (bi, 0, 0, 0))]
    else:
        extra_specs = [pl.BlockSpec(extra[0].shape, lambda bi, p: (0, 0, 0))]
    return pl.pallas_call(
        functools.partial(_sample_attn_body, fox=fox),
        grid=(b, N_PAIRS),
        in_specs=[pair_rows, cache, cache, pair_rows, pair_rows] + extra_specs,
        out_specs=pair_rows,
        out_shape=jax.ShapeDtypeStruct((b, t, WIDTH), BF16),
        compiler_params=_cparams(2),
        name="sample_fox" if fox else "sample_band",
    )(q, kc_t, vc_t, k_new, v_new, *extra)


def _first_index(is_max, idx, axis, big):
    return jnp.min(jnp.where(is_max, idx, big), axis=axis, keepdims=True)


def _route(scores, choice):
    t = scores.shape[1]
    c3 = choice.reshape(N_GROUPS, GROUP_SIZE, t)
    j_idx = lax.broadcasted_iota(jnp.int32, c3.shape, 1)
    top1 = jnp.max(c3, axis=1, keepdims=True)
    first = _first_index(c3 == top1, j_idx, 1, GROUP_SIZE)
    top2 = jnp.max(jnp.where(j_idx == first, -jnp.inf, c3), axis=1, keepdims=True)
    gscore = (top1 + top2).reshape(N_GROUPS, t)

    g_idx = lax.broadcasted_iota(jnp.int32, gscore.shape, 0)
    gsel = jnp.zeros(gscore.shape, F32)
    work = gscore
    for _ in range(TOPK_GROUPS):
        gm = jnp.max(work, axis=0, keepdims=True)
        pick = g_idx == _first_index(work == gm, g_idx, 0, N_GROUPS)
        gsel = jnp.where(pick, 1.0, gsel)
        work = jnp.where(pick, -jnp.inf, work)

    emask = jnp.broadcast_to(gsel.reshape(N_GROUPS, 1, t), c3.shape) > 0.0
    work = jnp.where(emask, c3, NEG_INF)
    e_idx = lax.broadcasted_iota(jnp.int32, c3.shape, 0) * GROUP_SIZE + j_idx
    esel = jnp.zeros(c3.shape, F32)
    for _ in range(TOP_K):
        em = jnp.max(jnp.max(work, axis=1, keepdims=True), axis=0, keepdims=True)
        cand = jnp.where(work == em, e_idx, N_EXPERTS)
        first = jnp.min(jnp.min(cand, axis=1, keepdims=True), axis=0, keepdims=True)
        pick = e_idx == first
        esel = jnp.where(pick, 1.0, esel)
        work = jnp.where(pick, -jnp.inf, work)

    w = esel * scores.reshape(c3.shape)
    denom = jnp.sum(jnp.sum(w, axis=1, keepdims=True), axis=0, keepdims=True)
    return (w / denom * ROUTED_SCALE).reshape(N_EXPERTS, t)


def _merge_body(of_ref, ob_ref, x_ref, gate_ref, sh_ref, sc_ref, ogf_ref, ogb_ref, wo_ref, g2_ref,
                wrh_ref, wrl_ref, br_ref, x1_ref, h2_ref, comb_ref, combt_ref, cnt_ref):
    def group_norm(t_ref, gain_ref):
        t = t_ref[...].astype(F32)
        ms = jnp.mean(t * t, axis=-1, keepdims=True)
        return (t * lax.rsqrt(ms + EPS) * gain_ref[...]).astype(BF16)

    y = _dot(group_norm(of_ref, ogf_ref), wo_ref[0:WIDTH, :]) + _dot(group_norm(ob_ref, ogb_ref), wo_ref[WIDTH:, :])
    x = x_ref[...]
    g, r, d = x.shape
    x1 = x + gate_ref[...] * y.reshape(g, r, d)
    x1_ref[...] = x1
    ms = jnp.mean(x1 * x1, axis=-1, keepdims=True)
    h2 = (x1 * lax.rsqrt(ms + EPS) * g2_ref[...] * (1.0 + sc_ref[...]) + sh_ref[...]).reshape(g * r, d)
    h_hi, h_lo = _split2(h2)
    h2_ref[...] = h_hi
    logits = _dot_nt(wrh_ref[...], h_hi) + _dot_nt(wrh_ref[...], h_lo) + _dot_nt(wrl_ref[...], h_hi)
    scores = jax.nn.sigmoid(logits)
    t = scores.shape[1]
    bias = jnp.concatenate([br_ref[...]] * (t // LANES), axis=1)
    comb = _route(scores, scores + bias)
    comb_pad = jnp.concatenate([comb, jnp.zeros((LANES - N_EXPERTS, t), F32)], axis=0)
    comb_ref[...] = comb_pad.T
    combt_ref[...] = comb
    picked = jnp.where(comb > 0.0, 1.0, 0.0)
    for sub in range(cnt_ref.shape[0]):
        cnt = jnp.sum(picked[:, sub * MOE_TM:(sub + 1) * MOE_TM], axis=1, keepdims=True)
        cnt_ref[sub] = jnp.broadcast_to(cnt, (N_EXPERTS, LANES))


def _merge(of, ob, x, gate, shift, scale, ogf, ogb, wo, g2, wr_hi, wr_lo, br, *, G, R):
    nb, s, d = x.shape
    n = nb * s
    tm = G * R
    nbi, nsi = nb // G, s // R
    row = lambda b, i: (b * nsi + i, 0)
    const = lambda b, i: (0, 0)
    mod_spec = pl.BlockSpec((G, 1, d), lambda b, i: (b, 0, 0))
    x_spec = pl.BlockSpec((G, R, d), lambda b, i: (b, i, 0))
    return pl.pallas_call(
        _merge_body,
        grid=(nbi, nsi),
        in_specs=[pl.BlockSpec((tm, WIDTH), row), pl.BlockSpec((tm, WIDTH), row), x_spec,
                  mod_spec, mod_spec, mod_spec,
                  pl.BlockSpec((1, WIDTH), const), pl.BlockSpec((1, WIDTH), const),
                  pl.BlockSpec(wo.shape, const), pl.BlockSpec((1, d), const),
                  pl.BlockSpec(wr_hi.shape, const), pl.BlockSpec(wr_lo.shape, const),
                  pl.BlockSpec(br.shape, const)],
        out_specs=[x_spec, pl.BlockSpec((tm, d), row), pl.BlockSpec((tm, LANES), row),
                   pl.BlockSpec((N_EXPERTS, tm), lambda b, i: (0, b * nsi + i)),
                   pl.BlockSpec((tm // MOE_TM, N_EXPERTS, LANES), lambda b, i: (b * nsi + i, 0, 0))],
        out_shape=[jax.ShapeDtypeStruct((nb, s, d), F32), jax.ShapeDtypeStruct((n, d), BF16),
                   jax.ShapeDtypeStruct((n, LANES), F32), jax.ShapeDtypeStruct((N_EXPERTS, n), F32),
                   jax.ShapeDtypeStruct((n // MOE_TM, N_EXPERTS, LANES), F32)],
        compiler_params=_cparams(2),
        name="merge",
    )(of, ob, x, gate, shift, scale, ogf, ogb, wo, g2, wr_hi, wr_lo, br)


def _silu(g):
    return g * jax.nn.sigmoid(g)


def _moe_body(h_ref, comb_ref, x1_ref, gate_ref, wg_ref, wu_ref, wd_ref, sg_ref, su_ref, sd_ref, y_ref, acc_ref):
    e = pl.program_id(2)
    hb = h_ref[...]

    @pl.when(e == 0)
    def _():
        a = _silu(_dot(hb, sg_ref[...])) * _dot(hb, su_ref[...])
        acc_ref[...] = _dot(a.astype(BF16), sd_ref[...])

    comb = comb_ref[...]
    lane = lax.broadcasted_iota(jnp.int32, comb.shape, 1)
    c_e = jnp.sum(jnp.where(lane == e, comb, 0.0), axis=1, keepdims=True)
    a = _silu(_dot(hb, wg_ref[...])) * _dot(hb, wu_ref[...]) * c_e
    acc_ref[...] += _dot(a.astype(BF16), wd_ref[...])

    @pl.when(e == pl.num_programs(2) - 1)
    def _():
        x1 = x1_ref[...]
        g, r, d = x1.shape
        y_ref[...] = x1 + gate_ref[...] * acc_ref[...].reshape(g, r, d)


def _moe(h2, comb, x1, gate, wg, wu, wd, sg, su, sd, *, G, R):
    nb, s, d = x1.shape
    tm = G * R
    nbi, nsi = nb // G, s // R
    ff = wg.shape[2]
    row = lambda b, i, e: (b * nsi + i, 0)
    const = lambda b, i, e: (0, 0)
    x_spec = pl.BlockSpec((G, R, d), lambda b, i, e: (b, i, 0))
    return pl.pallas_call(
        _moe_body,
        grid=(nbi, nsi, N_EXPERTS),
        in_specs=[pl.BlockSpec((tm, d), row), pl.BlockSpec((tm, LANES), row), x_spec,
                  pl.BlockSpec((G, 1, d), lambda b, i, e: (b, 0, 0)),
                  pl.BlockSpec((None, d, ff), lambda b, i, e: (e, 0, 0)),
                  pl.BlockSpec((None, d, ff), lambda b, i, e: (e, 0, 0)),
                  pl.BlockSpec((None, ff, d), lambda b, i, e: (e, 0, 0)),
                  pl.BlockSpec(sg.shape, const), pl.BlockSpec(su.shape, const), pl.BlockSpec(sd.shape, const)],
        out_specs=x_spec,
        out_shape=jax.ShapeDtypeStruct((nb, s, d), F32),
        scratch_shapes=[pltpu.VMEM((tm, d), F32)],
        compiler_params=_cparams(3),
        name="moe",
    )(h2, comb, x1, gate, wg, wu, wd, sg, su, sd)


MOE_TM = 256
MOE_CH = 16
MOE_SLOTS = TOP_K * MOE_TM + N_EXPERTS * MOE_CH
MOE_NCHUNK = MOE_SLOTS // MOE_CH
MOE_PIECE = 512
MOE_NPIECE = MOE_SLOTS // MOE_PIECE
MOE_CPP = MOE_PIECE // MOE_CH
MOE_RB = 1024
TAU_RADIX = 64.0
assert MOE_SLOTS % MOE_PIECE == 0


def _moe_plan(cnt, n_tiles):
    pc = (cnt + MOE_CH - 1) // MOE_CH * MOE_CH
    off = jnp.cumsum(pc, axis=1) - pc
    end = off + pc
    n_used = (jnp.sum(pc, axis=1) // MOE_CH).astype(jnp.int32)
    tot = jnp.sum(pc, axis=0)
    reg = (tot + MOE_RB - 1) // MOE_RB * MOE_RB
    reg_end = jnp.cumsum(reg)
    reg_start = reg_end - reg
    dest_base = reg_start[None, :] + jnp.cumsum(pc, axis=0) - pc
    chunk_row = jnp.arange(MOE_NCHUNK, dtype=jnp.int32)[None, :] * MOE_CH
    in_group = (chunk_row[:, :, None] >= off[:, None, :]) & (chunk_row[:, :, None] < end[:, None, :])
    used = jnp.any(in_group, axis=2)
    cdest = chunk_row + jnp.sum(jnp.where(in_group, (dest_base - off)[:, None, :], 0), axis=2)
    worst_rows = TOP_K * MOE_TM * n_tiles + n_tiles * N_EXPERTS * (MOE_CH - 1) + N_EXPERTS * (MOE_RB - MOE_CH)
    r_max = -(-worst_rows // MOE_RB)
    parity = (jnp.arange(n_tiles, dtype=jnp.int32) % 2)[:, None]
    cdest_d = jnp.where(used, cdest, r_max * MOE_RB + parity * MOE_SLOTS + chunk_row).astype(jnp.int32)
    cdest_c = jnp.where(used, cdest, chunk_row).astype(jnp.int32)
    n_active = (reg_end[-1] // MOE_RB).astype(jnp.int32).reshape(1)
    tile_row = jnp.arange(r_max, dtype=jnp.int32) * MOE_RB
    tile_expert = jnp.minimum(jnp.sum((tile_row[:, None] >= reg_end[None, :]).astype(jnp.int32), axis=1),
                              N_EXPERTS - 1).astype(jnp.int32)
    in_region = (tile_row[:, None] >= reg_start[None, :]) & (tile_row[:, None] < reg_end[None, :])
    rows_end = jnp.sum(jnp.where(in_region, (reg_start + tot)[None, :], 0), axis=1)
    valid = jnp.clip(rows_end - tile_row, 0, MOE_RB).astype(jnp.int32)
    f = lambda a: a.astype(F32)
    zeros64 = jnp.zeros((n_tiles, N_EXPERTS), F32)
    row2 = lambda a: jnp.broadcast_to(jnp.concatenate([f(a), f(a)], axis=1)[:, None, :], (n_tiles, 8, LANES))
    col = lambda a: jnp.broadcast_to(f(a)[:, :, None], (n_tiles, N_EXPERTS, LANES))
    col128 = lambda a: jnp.broadcast_to(jnp.concatenate([f(a), zeros64], axis=1)[:, :, None], (n_tiles, LANES, LANES))
    row1 = lambda a: jnp.broadcast_to(jnp.concatenate([f(a), zeros64], axis=1)[:, None, :], (n_tiles, 8, LANES))
    return dict(n_used=n_used, cdest_d=cdest_d.reshape(n_tiles, 1, MOE_NCHUNK),
                cdest_c=cdest_c.reshape(n_tiles, 1, MOE_NCHUNK), r_max=r_max, n_active=n_active,
                tile_expert=tile_expert, valid=valid,
                off_row2=row2(off), end_row2=row2(end), off_col=col(off),
                off_row1=row1(off), off_col128=col128(off), end_col128=col128(end))


def _tau_pieces(sel, tau):
    tau = jnp.where(sel, tau, -1.0)
    hi = jnp.floor(tau * (1.0 / TAU_RADIX)) * TAU_RADIX
    return hi.astype(BF16), (tau - hi).astype(BF16)


def _dispatch_body(nused_ref, cdest_ref, h_ref, combt_ref, offcol_ref, offrow_ref, endrow_ref, sorted_ref,
                   buf_ref, sem):
    t = pl.program_id(0)
    n_used = nused_ref[t]
    tm = h_ref.shape[0]
    sel = combt_ref[...] > 0.0
    rr = lax.broadcasted_iota(jnp.int32, (tm, tm), 0)
    cc = lax.broadcasted_iota(jnp.int32, (tm, tm), 1)
    upper = jnp.where(rr < cc, 1.0, 0.0).astype(BF16)
    rank = _dot(jnp.where(sel, 1.0, 0.0).astype(BF16), upper)
    cols = jnp.concatenate([offcol_ref[...]] * (tm // LANES), axis=1)
    tau_hi, tau_lo = _tau_pieces(sel, cols + rank)
    taucat = jnp.concatenate([tau_hi, tau_lo], axis=0)
    off_row = offrow_ref[0:1, :]
    end_row = endrow_ref[0:1, :]
    hb = h_ref[...]

    slot = t % 2
    buf = buf_ref.at[slot]
    buf_prev = buf_ref.at[1 - slot]

    def start_piece(piece):
        chunks = range(piece * MOE_CPP, (piece + 1) * MOE_CPP)
        dests = [cdest_ref[0, c] for c in chunks]
        for c, row in zip(chunks, dests):
            dst = sorted_ref.at[pl.ds(pl.multiple_of(row, MOE_CH), MOE_CH), :]
            pltpu.make_async_copy(buf.at[pl.ds(c * MOE_CH, MOE_CH), :], dst, sem.at[slot]).start()

    def wait_piece(piece, which_buf, which_sem):
        rows = pl.ds(piece * MOE_PIECE, MOE_PIECE)
        pltpu.make_async_copy(which_buf.at[rows, :], sorted_ref.at[rows, :], which_sem).wait()

    for piece in range(MOE_NPIECE):
        @pl.when(piece * MOE_CPP < n_used)
        def _():
            if piece > 0:
                start_piece(piece - 1)
            base = piece * MOE_PIECE
            s_col = (base + lax.broadcasted_iota(jnp.int32, (MOE_PIECE, LANES), 0)).astype(F32)
            onehot = jnp.where((s_col >= off_row) & (s_col < end_row), 1.0, 0.0).astype(BF16)
            q = _dot(onehot, taucat)
            s_mat = (base + lax.broadcasted_iota(jnp.int32, (MOE_PIECE, tm), 0)).astype(F32)
            g = jnp.where(q == s_mat, 1.0, 0.0).astype(BF16)
            buf[pl.ds(base, MOE_PIECE), :] = _dot(g, hb).astype(BF16)

    last = (n_used - 1) // MOE_CPP
    for piece in range(MOE_NPIECE):
        @pl.when(piece == last)
        def _():
            start_piece(piece)

    n_prev = nused_ref[jnp.maximum(t - 1, 0)]
    for piece in range(MOE_NPIECE):
        @pl.when((t > 0) & (piece * MOE_CPP < n_prev))
        def _():
            wait_piece(piece, buf_prev, sem.at[1 - slot])

    for piece in range(MOE_NPIECE):
        @pl.when((t == pl.num_programs(0) - 1) & (piece * MOE_CPP < n_used))
        def _():
            wait_piece(piece, buf, sem.at[slot])


def _dispatch(h2, combt, plan, n_tiles):
    n, d = h2.shape
    r_total = plan["r_max"] * MOE_RB + 2 * MOE_SLOTS
    grid_spec = pltpu.PrefetchScalarGridSpec(
        num_scalar_prefetch=1,
        grid=(n_tiles,),
        in_specs=[pl.BlockSpec((None, 1, MOE_NCHUNK), lambda t, nu: (t, 0, 0), memory_space=pltpu.SMEM),
                  pl.BlockSpec((MOE_TM, d), lambda t, nu: (t, 0)),
                  pl.BlockSpec((N_EXPERTS, MOE_TM), lambda t, nu: (0, t)),
                  pl.BlockSpec((None, N_EXPERTS, LANES), lambda t, nu: (t, 0, 0)),
                  pl.BlockSpec((None, 8, LANES), lambda t, nu: (t, 0, 0)),
                  pl.BlockSpec((None, 8, LANES), lambda t, nu: (t, 0, 0))],
        out_specs=pl.BlockSpec(memory_space=pl.ANY),
        scratch_shapes=[pltpu.VMEM((2, MOE_SLOTS, d), BF16), pltpu.SemaphoreType.DMA((2,))],
    )
    return pl.pallas_call(
        _dispatch_body,
        grid_spec=grid_spec,
        out_shape=jax.ShapeDtypeStruct((r_total, d), BF16),
        compiler_params=_cparams(1),
        name="moe_dispatch",
    )(plan["n_used"], plan["cdest_d"], h2, combt, plan["off_col"], plan["off_row2"], plan["end_row2"])


def _ffn_body(texp_ref, nact_ref, valid_ref, x_ref, wg_ref, wu_ref, wd_ref, o_ref):
    r = pl.program_id(0)

    @pl.when(r < nact_ref[0])
    def _():
        x = x_ref[...]
        rows = lax.broadcasted_iota(jnp.int32, x.shape, 0)
        x = jnp.where(rows < valid_ref[r], x, jnp.zeros_like(x))
        a = _silu(_dot(x, wg_ref[...])) * _dot(x, wu_ref[...])
        o_ref[...] = _dot(a.astype(BF16), wd_ref[...]).astype(o_ref.dtype)


def _ffn(xs, wg, wu, wd, plan):
    r_total, d = xs.shape
    ff = wg.shape[2]
    last = lambda r, te, na, va: (jnp.minimum(r, na[0] - 1), 0)
    wmap = lambda r, te, na, va: (te[r], 0, 0)
    grid_spec = pltpu.PrefetchScalarGridSpec(
        num_scalar_prefetch=3,
        grid=(plan["r_max"],),
        in_specs=[pl.BlockSpec((MOE_RB, d), last),
                  pl.BlockSpec((None, d, ff), wmap), pl.BlockSpec((None, d, ff), wmap),
                  pl.BlockSpec((None, ff, d), wmap)],
        out_specs=pl.BlockSpec((MOE_RB, d), last),
    )
    return pl.pallas_call(
        _ffn_body,
        grid_spec=grid_spec,
        out_shape=jax.ShapeDtypeStruct((r_total, d), BF16),
        compiler_params=_cparams(1),
        name="moe_ffn",
    )(plan["tile_expert"], plan["n_active"], plan["valid"], xs, wg, wu, wd)


def _combine_body(nused_ref, cdest_ref, cnext_ref, h_ref, comb_ref, x1_ref, gate_ref, offrow_ref, offcol_ref,
                  endcol_ref, sg_ref, su_ref, sd_ref, ys_ref, y_ref, buf_ref, sem):
    t = pl.program_id(0)
    n_tiles = pl.num_programs(0)
    n_used = nused_ref[t]
    tm = h_ref.shape[0]
    slot = t % 2
    buf = buf_ref.at[slot]

    def fetch_tile(table_ref, count, which):
        for piece in range(MOE_NPIECE):
            @pl.when(piece * MOE_CPP < count)
            def _():
                chunks = range(piece * MOE_CPP, (piece + 1) * MOE_CPP)
                srcs = [table_ref[0, c] for c in chunks]
                for c, row in zip(chunks, srcs):
                    src = ys_ref.at[pl.ds(pl.multiple_of(row, MOE_CH), MOE_CH), :]
                    dst = buf_ref.at[which, pl.ds(c * MOE_CH, MOE_CH), :]
                    pltpu.make_async_copy(src, dst, sem.at[which, piece]).start()

    def wait_piece(piece):
        rows = pl.ds(piece * MOE_PIECE, MOE_PIECE)
        pltpu.make_async_copy(ys_ref.at[rows, :], buf.at[rows, :], sem.at[slot, piece]).wait()

    @pl.when(t == 0)
    def _():
        buf_ref[...] = jnp.zeros_like(buf_ref)
        fetch_tile(cdest_ref, n_used, 0)

    @pl.when(t + 1 < n_tiles)
    def _():
        fetch_tile(cnext_ref, nused_ref[jnp.minimum(t + 1, n_tiles - 1)], 1 - slot)

    hb = h_ref[...]
    acc = _dot((_silu(_dot(hb, sg_ref[...])) * _dot(hb, su_ref[...])).astype(BF16), sd_ref[...])
    comb = comb_ref[...]
    sel = comb > 0.0
    rr = lax.broadcasted_iota(jnp.int32, (tm, tm), 0)
    cc = lax.broadcasted_iota(jnp.int32, (tm, tm), 1)
    lower = jnp.where(cc < rr, 1.0, 0.0).astype(BF16)
    rank = _dot(lower, jnp.where(sel, 1.0, 0.0).astype(BF16))
    tau_hi, tau_lo = _tau_pieces(sel, offrow_ref[0:1, :] + rank)
    taucat = jnp.concatenate([tau_hi, tau_lo], axis=1)
    lhs = jnp.concatenate([taucat, jnp.concatenate([comb.astype(BF16), jnp.zeros((tm, LANES), BF16)], axis=1)],
                          axis=0)
    off_col = offcol_ref[...]
    end_col = endcol_ref[...]
    s_row = lax.broadcasted_iota(jnp.int32, (LANES, MOE_SLOTS), 1).astype(F32)
    off_b = jnp.concatenate([off_col] * (MOE_SLOTS // LANES), axis=1)
    end_b = jnp.concatenate([end_col] * (MOE_SLOTS // LANES), axis=1)
    onehot = jnp.where((s_row >= off_b) & (s_row < end_b), 1.0, 0.0).astype(BF16)
    qw = _dot(lhs, jnp.concatenate([onehot, onehot], axis=0))
    s_mat = lax.broadcasted_iota(jnp.int32, (tm, MOE_SLOTS), 1).astype(F32)
    gw = jnp.where(qw[0:tm] == s_mat, qw[tm:2 * tm], 0.0).astype(BF16)

    for piece in range(MOE_NPIECE):
        @pl.when(piece * MOE_CPP < n_used)
        def _():
            wait_piece(piece)

    y_ref[...] = x1_ref[...] + gate_ref[...] * (acc + _dot(gw, buf[...]))


def _combine(h2, comb, x1, gate, ys, sg, su, sd, plan, n_tiles):
    nb, s, d = x1.shape
    n = nb * s
    per_b = s // MOE_TM
    x1f = x1.reshape(n, d)
    const2 = lambda t, nu: (0, 0)
    grid_spec = pltpu.PrefetchScalarGridSpec(
        num_scalar_prefetch=1,
        grid=(n_tiles,),
        in_specs=[pl.BlockSpec((None, 1, MOE_NCHUNK), lambda t, nu: (t, 0, 0), memory_space=pltpu.SMEM),
                  pl.BlockSpec((None, 1, MOE_NCHUNK), lambda t, nu: (jnp.minimum(t + 1, n_tiles - 1), 0, 0),
                               memory_space=pltpu.SMEM),
                  pl.BlockSpec((MOE_TM, d), lambda t, nu: (t, 0)),
                  pl.BlockSpec((MOE_TM, LANES), lambda t, nu: (t, 0)),
                  pl.BlockSpec((MOE_TM, d), lambda t, nu: (t, 0)),
                  pl.BlockSpec((None, 1, d), lambda t, nu: (t // per_b, 0, 0)),
                  pl.BlockSpec((None, 8, LANES), lambda t, nu: (t, 0, 0)),
                  pl.BlockSpec((None, LANES, LANES), lambda t, nu: (t, 0, 0)),
                  pl.BlockSpec((None, LANES, LANES), lambda t, nu: (t, 0, 0)),
                  pl.BlockSpec(sg.shape, const2), pl.BlockSpec(su.shape, const2), pl.BlockSpec(sd.shape, const2),
                  pl.BlockSpec(memory_space=pl.ANY)],
        out_specs=pl.BlockSpec((MOE_TM, d), lambda t, nu: (t, 0)),
        scratch_shapes=[pltpu.VMEM((2, MOE_SLOTS, d), BF16),
                        pltpu.SemaphoreType.DMA((2, MOE_NPIECE))],
    )
    y = pl.pallas_call(
        _combine_body,
        grid_spec=grid_spec,
        out_shape=jax.ShapeDtypeStruct((n, d), F32),
        compiler_params=_cparams(1),
        name="moe_combine",
    )(plan["n_used"], plan["cdest_c"], plan["cdest_c"], h2, comb, x1f, gate, plan["off_row1"], plan["off_col128"],
      plan["end_col128"],
      sg, su, sd, ys)
    return y.reshape(nb, s, d)


def _moe_sparse(h2, comb, combt, cnt, x1, gate, wg, wu, wd, sg, su, sd):
    n = h2.shape[0]
    n_tiles = n // MOE_TM
    plan = _moe_plan(cnt[:, :, 0].astype(jnp.int32), n_tiles)
    xs = _dispatch(h2, combt, plan, n_tiles)
    ys = _ffn(xs, wg, wu, wd, plan)
    return _combine(h2, comb, x1, gate, ys, sg, su, sd, plan, n_tiles)


def _tile_heads(g, mult=1.0):
    return (jnp.tile(g.astype(F32), N_HEADS) * mult).reshape(1, WIDTH)


def kernel(x_prompt, x_sample, cache_fox_k, cache_fox_v, cache_fox_logf, cache_band_k, cache_band_v, c_prompt, c_sample, w_ada, b_ada, norm1_g, norm2_g, w_in, b_forget, g_q_fox, g_k_fox, g_q_band, g_k_band, rel_bias, out_g_fox, out_g_band, w_out, w_router, b_router, w_gate, w_up, w_down, ws_gate, ws_up, ws_down):
    depth = w_ada.shape[0]
    assert depth == 1
    bsz, seq, d = x_prompt.shape
    dbs, dseq, _ = x_sample.shape
    past = cache_fox_k.shape[2]
    n_cache = cache_band_k.shape[2]
    assert n_cache == BAND_REACH and dseq == CHUNK and seq % BAND_REACH == 0

    wi = w_in[0]
    cols = [wi[:, 0:512], wi[:, 512:1024], wi[:, 1024:1536], wi[:, 1544:2056], wi[:, 2056:2568], wi[:, 2568:3080],
            wi[:, 1536:1544], jnp.zeros((d, LANES - N_HEADS), F32)]
    w_all = jnp.concatenate(cols, axis=1).astype(BF16)
    hd = jnp.arange(WIDTH) // HEAD_DIM
    bd = jnp.where(hd[:, None] == hd[None, :], 1.0 / HEAD_DIM, 0.0).astype(BF16)
    qscale = ATTN_SCALE * LOG2E
    gqf, gkf = _tile_heads(g_q_fox[0], qscale), _tile_heads(g_k_fox[0])
    gqb, gkb = _tile_heads(g_q_band[0], qscale), _tile_heads(g_k_band[0])
    bf_row = jnp.concatenate([b_forget[0], jnp.zeros((LANES - N_HEADS,), F32)]).reshape(1, LANES)
    g1 = norm1_g[0].reshape(1, d)
    g2 = norm2_g[0].reshape(1, d)
    ogf = out_g_fox[0].reshape(1, WIDTH)
    ogb = out_g_band[0].reshape(1, WIDTH)
    wo = w_out[0].astype(BF16)
    wr_t = w_router[0].T
    wr_hi = wr_t.astype(BF16)
    wr_lo = (wr_t - wr_hi.astype(F32)).astype(BF16)
    br = jnp.broadcast_to(b_router[0].reshape(N_EXPERTS, 1), (N_EXPERTS, LANES)).astype(F32)
    wg, wu, wd = w_gate[0].astype(BF16), w_up[0].astype(BF16), w_down[0].astype(BF16)
    sg, su, sd = ws_gate[0].astype(BF16), ws_up[0].astype(BF16), ws_down[0].astype(BF16)

    n_c = bsz + dbs
    rows = -(-n_c // 8) * 8
    c_all = jnp.concatenate([c_prompt, c_sample, jnp.zeros((rows - n_c, d), F32)], axis=0)
    mod = _ada(c_all, w_ada[0], b_ada[0].reshape(1, -1))

    def mods(lo, hi):
        return [mod[lo:hi, j * d:(j + 1) * d].reshape(hi - lo, 1, d) for j in range(6)]

    shift1_p, scale1_p, gate1_p, shift2_p, scale2_p, gate2_p = mods(0, bsz)
    shift1_s, scale1_s, gate1_s, shift2_s, scale2_s, gate2_s = mods(bsz, n_c)

    TM = BAND_REACH
    (qf, kf, vft, kf32, vf32, lf, qb, kb, vbt, kb32, vb32) = _proj(
        x_prompt, shift1_p, scale1_p, g1, w_all, bd, gqf, gkf, gqb, gkb, bf_row, G=1, R=TM, band_last_only=True)
    r3 = lambda a: a.reshape(bsz, seq, a.shape[-1])
    ct, ka = _scan_t(r3(lf), T=TM)
    of = _foxt(r3(qf), r3(kf), ka, vft, ct, T=TM)
    assert TM == BAND_KBLK == BAND_REACH and seq % BAND_STEP == 0
    ob = _bandt(r3(qb), r3(kb), vbt, _band_bias_tile_t(rel_bias[0]))
    assert TM % MOE_TM == 0
    x1_p, h2_p, comb_p, combt_p, cnt_p = _merge(of.reshape(-1, WIDTH), ob.reshape(-1, WIDTH), x_prompt, gate1_p,
                                                shift2_p, scale2_p, ogf, ogb, wo, g2, wr_hi, wr_lo, br, G=1,
                                                R=MERGE_ROWS if seq % MERGE_ROWS == 0 else TM)
    y_p = _moe_sparse(h2_p, comb_p, combt_p, cnt_p, x1_p, gate2_p, wg, wu, wd, sg, su, sd)

    GS = 8
    (qf_s, kf_s, vf_s, kf32_s, vf32_s, lf_s, qb_s, kb_s, vb_s, kb32_s, vb32_s) = _proj(
        x_sample, shift1_s, scale1_s, g1, w_all, bd, gqf, gkf, gqb, gkb, bf_row, G=GS, R=dseq,
        band_last_only=False)
    s3 = lambda a: a.reshape(dbs, dseq, a.shape[-1])
    sk = past + dseq
    skp = -(-sk // LANES) * LANES
    pad_k = skp - sk
    n_seq = dbs * N_HEADS
    assert n_seq <= LANES
    lf_seq = jnp.concatenate([cache_fox_logf[0], s3(lf_s)[:, :, :N_HEADS]], axis=1)
    lf_seq = jnp.swapaxes(lf_seq, 0, 1).reshape(sk, n_seq)
    lf_seq = jnp.pad(lf_seq, ((0, pad_k), (0, LANES - n_seq)))
    cum_col, cum_row = _scan(lf_seq)
    cq_s = jnp.swapaxes(cum_col[past:past + dseq, :n_seq].reshape(dseq, dbs, N_HEADS), 0, 1)
    cum_s = jnp.pad(cq_s, ((0, 0), (0, 0), (0, LANES - N_HEADS)))
    cumt_s = cum_row[:n_seq].reshape(dbs, 1, N_HEADS, skp)

    def cache_t(c):
        return jnp.transpose(c, (0, 2, 3, 1)).reshape(dbs, N_PAIRS, PAIR, c.shape[1])

    of_s = _sample_attn(s3(qf_s), cache_t(cache_fox_k[0]), cache_t(cache_fox_v[0]), s3(kf_s), s3(vf_s),
                        cum_s, cumt_s, fox=True)
    bias_s = _band_bias_tile(rel_bias[0], dseq, BAND_REACH + LANES)
    ob_s = _sample_attn(s3(qb_s), cache_t(cache_band_k[0]), cache_t(cache_band_v[0]), s3(kb_s), s3(vb_s),
                        bias_s, fox=False)
    x1_s, h2_s, comb_s, _, _ = _merge(of_s.reshape(-1, WIDTH), ob_s.reshape(-1, WIDTH), x_sample, gate1_s, shift2_s,
                                      scale2_s, ogf, ogb, wo, g2, wr_hi, wr_lo, br, G=GS, R=dseq)
    y_s = _moe(h2_s, comb_s, x1_s, gate2_s, wg, wu, wd, sg, su, sd, G=dbs, R=dseq)

    hshape = (N_HEADS, HEAD_DIM)
    new_bk_s = jnp.concatenate([cache_band_k[0], s3(kb32_s).reshape(dbs, dseq, *hshape)], axis=1)[:, -n_cache:]
    new_bv_s = jnp.concatenate([cache_band_v[0], s3(vb32_s).reshape(dbs, dseq, *hshape)], axis=1)[:, -n_cache:]
    return (y_p, y_s,
            kf32.reshape(1, bsz, seq, *hshape), vf32.reshape(1, bsz, seq, *hshape),
            lf[:, :N_HEADS].reshape(1, bsz, seq, N_HEADS),
            kb32.reshape(1, bsz, BAND_REACH, *hshape), vb32.reshape(1, bsz, BAND_REACH, *hshape),
            kf32_s.reshape(1, dbs, dseq, *hshape), vf32_s.reshape(1, dbs, dseq, *hshape),
            lf_s[:, :N_HEADS].reshape(1, dbs, dseq, N_HEADS),
            new_bk_s[None], new_bv_s[None])
```
